```python
import jax, jax.numpy as jnp
from jax import lax
import numpy as np

D_MODEL = 1024
BATCH = 8
SEQ = 4096
DEPTH = 2

N_META = 16
EPS = 1e-6
D_FF = 4 * D_MODEL
A_HEADS = 4
A_DQK = D_MODEL // 8
A_DV = D_MODEL // 4
A_CHUNK = 64
B_WIDTH = D_MODEL
B_BLOCKS = 8
B_BLOCK = B_WIDTH // B_BLOCKS
CONV_W = 4
LRU_C = 8.0
C_HEADS = 8
C_Q_LORA = 3 * D_MODEL // 8
C_KV_LORA = D_MODEL // 4
C_NOPE = 64
C_ROPE = 32
C_V = 64
ROPE_THETA = 10000.0
D_HEADS = 8
D_HD = 64
ATT_BLOCK = 128

N_EVEN = (DEPTH + 1) // 2
N_ODD = DEPTH // 2
EVEN_SIZES = (A_HEADS * A_DQK, A_HEADS * A_DQK, A_HEADS * A_DV, A_HEADS * A_DV, A_HEADS, A_HEADS, B_WIDTH, B_WIDTH)
ODD_SIZES = (C_Q_LORA, C_KV_LORA, C_ROPE, D_HEADS * D_HD, D_HEADS * D_HD, D_HEADS * D_HD, D_HEADS)
P_EVEN = sum(EVEN_SIZES)
P_ODD = sum(ODD_SIZES)

kernel_name = "hybrid_mlstm_rglru_mla_fox_trunk"


def rmsnorm(x, g):
    xf = x.astype(jnp.float32)
    y = xf * lax.rsqrt(jnp.mean(xf * xf, axis=-1, keepdims=True) + EPS) * g.astype(jnp.float32)
    return y.astype(x.dtype)


def split_cols(z, sizes):
    out, start = [], 0
    for s in sizes:
        out.append(z[..., start:start + s])
        start += s
    return out


def pad_time(a, n, value=0.0):
    widths = [(0, 0), (0, 0), (n, 0)] + [(0, 0)] * (a.ndim - 3)
    return jnp.pad(a, widths, constant_values=value)


def to_heads(a, n_heads, d):
    b, t, _ = a.shape
    return a.reshape(b, t, n_heads, d).transpose(0, 2, 1, 3)


def from_heads(a):
    b, h, t, d = a.shape
    return a.transpose(0, 2, 1, 3).reshape(b, t, h * d)


def rope(x, pos):
    half = x.shape[-1] // 2
    freqs = ROPE_THETA ** (-jnp.arange(half, dtype=jnp.float32) / half)
    ang = pos[:, None, :, None].astype(jnp.float32) * freqs
    cos, sin = jnp.cos(ang), jnp.sin(ang)
    xf = x.astype(jnp.float32)
    x1, x2 = xf[..., :half], xf[..., half:]
    return jnp.concatenate([x1 * cos - x2 * sin, x1 * sin + x2 * cos], axis=-1).astype(x.dtype)


def mlstm_chunkwise(q, k, v, logi, logf):
    bn, nh, tp, dqk = q.shape
    dv = v.shape[-1]
    nc = tp // A_CHUNK

    def chunks(a):
        a = a.astype(jnp.float32).reshape(bn, nh, nc, A_CHUNK, *a.shape[3:])
        return jnp.moveaxis(a, 2, 0)

    causal = jnp.tril(jnp.ones((A_CHUNK, A_CHUNK), dtype=bool))

    def step(carry, inp):
        c_state, n_state, m_state = carry
        qj, kj, vj, ij, fj = inp
        b = jnp.cumsum(fj, axis=-1)
        dmat = jnp.where(causal, b[..., :, None] - b[..., None, :] + ij[..., None, :], -jnp.inf)
        inter = b + m_state[..., None]
        m_row = jnp.maximum(inter, jnp.max(dmat, axis=-1))
        w_intra = jnp.exp(dmat - m_row[..., None])
        w_inter = jnp.exp(inter - m_row)
        s = jnp.einsum('bhtd,bhsd->bhts', qj, kj) * w_intra
        num = (w_inter[..., None] * jnp.einsum('bhtd,bhde->bhte', qj, c_state)
               + jnp.einsum('bhts,bhse->bhte', s, vj))
        den = w_inter * jnp.einsum('bhtd,bhd->bht', qj, n_state) + jnp.sum(s, axis=-1)
        h = num / jnp.maximum(jnp.abs(den), jnp.exp(-m_row))[..., None]
        g = b[..., -1]
        dk = g[..., None] - b + ij
        m_new = jnp.maximum(g + m_state, jnp.max(dk, axis=-1))
        wk = jnp.exp(dk - m_new[..., None])
        sc = jnp.exp(g + m_state - m_new)
        c_new = sc[..., None, None] * c_state + jnp.einsum('bhs,bhsd,bhse->bhde', wk, kj, vj)
        n_new = sc[..., None] * n_state + jnp.einsum('bhs,bhsd->bhd', wk, kj)
        return (c_new, n_new, m_new), h

    init = (jnp.zeros((bn, nh, dqk, dv), jnp.float32),
            jnp.zeros((bn, nh, dqk), jnp.float32),
            jnp.zeros((bn, nh), jnp.float32))
    _, hs = lax.scan(step, init, (chunks(q), chunks(k), chunks(v), chunks(logi), chunks(logf)))
    return jnp.moveaxis(hs, 0, 2).reshape(bn, nh, tp, dv)


def causal_depthwise_conv(x, w, b):
    k, c = w.shape
    y = lax.conv_general_dilated(x, w[:, None, :], window_strides=(1,), padding=[(k - 1, 0)],
                                 dimension_numbers=('NWC', 'WIO', 'NWC'), feature_group_count=c)
    return y + b


def rglru(x, w_a, b_a, w_x, b_x, lam):
    bn, t, w = x.shape
    xb = x.reshape(bn, t, B_BLOCKS, B_BLOCK)
    r = jax.nn.sigmoid(jnp.einsum('btgi,gij->btgj', xb, w_a).reshape(bn, t, w) + b_a)
    i = jax.nn.sigmoid(jnp.einsum('btgi,gij->btgj', xb, w_x).reshape(bn, t, w) + b_x)
    log_a = -LRU_C * r.astype(jnp.float32) * jax.nn.softplus(-lam.astype(jnp.float32))
    a = jnp.exp(log_a)
    u = jnp.sqrt(-jnp.expm1(2.0 * log_a)) * (i * x).astype(jnp.float32)

    def combine(left, right):
        a1, u1 = left
        a2, u2 = right
        return a1 * a2, a2 * u1 + u2

    _, h = lax.associative_scan(combine, (a, u), axis=1)
    return h.astype(x.dtype)


def blocked_causal_attention(q, k, v, scale, n_pad, fcum=None):
    bn, nh, tp, _ = q.shape
    nb = tp // ATT_BLOCK
    kpos = jnp.arange(tp)
    kvalid = kpos >= n_pad

    def one_block(i):
        start = i * ATT_BLOCK
        qb = lax.dynamic_slice_in_dim(q, start, ATT_BLOCK, axis=2)
        qpos = start + jnp.arange(ATT_BLOCK)
        logits = jnp.einsum('bhqd,bhkd->bhqk', qb, k, preferred_element_type=jnp.float32) * scale
        if fcum is not None:
            fq = lax.dynamic_slice_in_dim(fcum, start, ATT_BLOCK, axis=2)
            logits = logits + (fq[..., :, None] - fcum[..., None, :])
        mask = (kpos[None, :] <= qpos[:, None]) & (kvalid[None, :] | (kpos[None, :] == qpos[:, None]))
        p = jax.nn.softmax(jnp.where(mask, logits, -jnp.inf), axis=-1)
        return jnp.einsum('bhqk,bhkd->bhqd', p.astype(v.dtype), v)

    out = lax.map(one_block, jnp.arange(nb))
    return jnp.moveaxis(out, 0, 2).reshape(bn, nh, tp, v.shape[-1])


def even_mixer(h, ln, w_in, b_if, a_norm_g, conv_w, conv_b, w_ra, b_ra, w_rx, b_rx, lam, w_out):
    bn, t, _ = h.shape
    z = rmsnorm(h, ln) @ w_in
    q, k, v, o, ig, fg, xb, gb = split_cols(z, EVEN_SIZES)
    q = to_heads(q, A_HEADS, A_DQK)
    k = to_heads(k, A_HEADS, A_DQK) * (A_DQK ** -0.5)
    v = to_heads(v, A_HEADS, A_DV)
    gates = (jnp.concatenate([ig, fg], axis=-1) + b_if).astype(jnp.float32)
    logi = gates[..., :A_HEADS].transpose(0, 2, 1)
    logf = jax.nn.log_sigmoid(gates[..., A_HEADS:]).transpose(0, 2, 1)
    pad = A_CHUNK - N_META
    ha = mlstm_chunkwise(pad_time(q, pad), pad_time(k, pad), pad_time(v, pad),
                         pad_time(logi, pad, -jnp.inf), pad_time(logf, pad))[:, :, pad:]
    ha = rmsnorm(ha.astype(h.dtype), a_norm_g) * jax.nn.sigmoid(to_heads(o, A_HEADS, A_DV))
    xc = causal_depthwise_conv(xb, conv_w, conv_b)
    hb = rglru(xc, w_ra, b_ra, w_rx, b_rx, lam) * jax.nn.gelu(gb)
    return jnp.concatenate([from_heads(ha), hb], axis=-1) @ w_out


def odd_mixer(h, pos_full, ln, w_in, b_f, g_qa, g_kva, w_uq, w_ukv, g_qn, g_qr, g_kn, g_kr,
              g_fq, g_fk, w_out):
    bn, t, _ = h.shape
    z = rmsnorm(h, ln) @ w_in
    cq, ckv, kr, fq, fk, fv, ff = split_cols(z, ODD_SIZES)
    pad = ATT_BLOCK - N_META
    q = to_heads(rmsnorm(cq, g_qa) @ w_uq, C_HEADS, C_NOPE + C_ROPE)
    kv = to_heads(rmsnorm(ckv, g_kva) @ w_ukv, C_HEADS, C_NOPE + C_V)
    q_nope = rmsnorm(q[..., :C_NOPE], g_qn)
    q_rope = rope(rmsnorm(q[..., C_NOPE:], g_qr), pos_full)
    k_nope = rmsnorm(kv[..., :C_NOPE], g_kn)
    k_rope = rope(rmsnorm(kr, g_kr)[:, None], pos_full)
    q_mla = jnp.concatenate([q_nope, q_rope], axis=-1)
    k_mla = jnp.concatenate([k_nope, jnp.broadcast_to(k_rope, (bn, C_HEADS, t, C_ROPE))], axis=-1)
    v_mla = kv[..., C_NOPE:]
    hc = blocked_causal_attention(pad_time(q_mla, pad), pad_time(k_mla, pad), pad_time(v_mla, pad),
                                  (C_NOPE + C_ROPE) ** -0.5, pad)[:, :, pad:]
    qf = rmsnorm(to_heads(fq, D_HEADS, D_HD), g_fq)
    kf = rmsnorm(to_heads(fk, D_HEADS, D_HD), g_fk)
    vf = to_heads(fv, D_HEADS, D_HD)
    logf = jax.nn.log_sigmoid((ff + b_f).astype(jnp.float32)).transpose(0, 2, 1)
    fcum = jnp.cumsum(pad_time(logf, pad), axis=-1)
    hd = blocked_causal_attention(pad_time(qf, pad), pad_time(kf, pad), pad_time(vf, pad),
                                  D_HD ** -0.5, pad, fcum)[:, :, pad:]
    return jnp.concatenate([from_heads(hc), from_heads(hd)], axis=-1) @ w_out


def sqrelu_mlp(h, ln, w1, w2):
    return jnp.square(jax.nn.relu(rmsnorm(h, ln) @ w1)) @ w2


def _fwd_setup_inputs(seed: int = 0) -> dict:
    key = jax.random.key(seed)
    ks = iter(jax.random.split(key, 48))
    nrm = lambda shape, s: jax.random.normal(next(ks), shape, jnp.float32) * s
    gain = lambda shape: 1.0 + nrm(shape, 0.02)
    ne, no = N_EVEN, N_ODD
    x = jax.random.normal(next(ks), (BATCH, SEQ, D_MODEL), jnp.float32)
    positions = (jnp.arange(SEQ, dtype=jnp.int32)[None, :]
                 + jax.random.randint(next(ks), (BATCH, 1), 0, 1024, dtype=jnp.int32))
    meta = nrm((N_META, D_MODEL), 1.0)
    ev_ln = gain((ne, D_MODEL))
    ev_w_in = nrm((ne, D_MODEL, P_EVEN), D_MODEL ** -0.5)
    ev_b_if = jnp.concatenate([nrm((ne, A_HEADS), 0.1),
                               3.0 + 3.0 * jax.random.uniform(next(ks), (ne, A_HEADS))], axis=-1)
    ev_a_norm = gain((ne, A_DV))
    ev_conv_w = nrm((ne, CONV_W, B_WIDTH), CONV_W ** -0.5)
    ev_conv_b = nrm((ne, B_WIDTH), 0.01)
    ev_w_ra = nrm((ne, B_BLOCKS, B_BLOCK, B_BLOCK), B_BLOCK ** -0.5)
    ev_b_ra = nrm((ne, B_WIDTH), 0.01)
    ev_w_rx = nrm((ne, B_BLOCKS, B_BLOCK, B_BLOCK), B_BLOCK ** -0.5)
    ev_b_rx = nrm((ne, B_WIDTH), 0.01)
    a0 = jax.random.uniform(next(ks), (ne, B_WIDTH), jnp.float32, 0.9, 0.999)
    s0 = a0 ** (1.0 / LRU_C)
    ev_lam = jnp.log(s0) - jnp.log1p(-s0)
    ev_w_out = nrm((ne, A_HEADS * A_DV + B_WIDTH, D_MODEL), (A_HEADS * A_DV + B_WIDTH) ** -0.5)
    od_ln = gain((no, D_MODEL))
    od_w_in = nrm((no, D_MODEL, P_ODD), D_MODEL ** -0.5)
    od_b_f = 2.0 + nrm((no, D_HEADS), 0.5)
    od_g_qa = gain((no, C_Q_LORA))
    od_g_kva = gain((no, C_KV_LORA))
    od_w_uq = nrm((no, C_Q_LORA, C_HEADS * (C_NOPE + C_ROPE)), C_Q_LORA ** -0.5)
    od_w_ukv = nrm((no, C_KV_LORA, C_HEADS * (C_NOPE + C_V)), C_KV_LORA ** -0.5)
    od_g_qn = gain((no, C_NOPE))
    od_g_qr = gain((no, C_ROPE))
    od_g_kn = gain((no, C_NOPE))
    od_g_kr = gain((no, C_ROPE))
    od_g_fq = gain((no, D_HD))
    od_g_fk = gain((no, D_HD))
    od_w_out = nrm((no, C_HEADS * C_V + D_HEADS * D_HD, D_MODEL), (C_HEADS * C_V + D_HEADS * D_HD) ** -0.5)
    mlp_ln = gain((DEPTH, D_MODEL))
    w_ff1 = nrm((DEPTH, D_MODEL, D_FF), D_MODEL ** -0.5)
    w_ff2 = nrm((DEPTH, D_FF, D_MODEL), D_FF ** -0.5)
    return {"x": x, "positions": positions, "meta": meta,
            "ev_ln": ev_ln, "ev_w_in": ev_w_in, "ev_b_if": ev_b_if, "ev_a_norm": ev_a_norm,
            "ev_conv_w": ev_conv_w, "ev_conv_b": ev_conv_b, "ev_w_ra": ev_w_ra, "ev_b_ra": ev_b_ra,
            "ev_w_rx": ev_w_rx, "ev_b_rx": ev_b_rx, "ev_lam": ev_lam, "ev_w_out": ev_w_out,
            "od_ln": od_ln, "od_w_in": od_w_in, "od_b_f": od_b_f, "od_g_qa": od_g_qa,
            "od_g_kva": od_g_kva, "od_w_uq": od_w_uq, "od_w_ukv": od_w_ukv, "od_g_qn": od_g_qn,
            "od_g_qr": od_g_qr, "od_g_kn": od_g_kn, "od_g_kr": od_g_kr, "od_g_fq": od_g_fq,
            "od_g_fk": od_g_fk, "od_w_out": od_w_out,
            "mlp_ln": mlp_ln, "w_ff1": w_ff1, "w_ff2": w_ff2}


def _fwd_reference(x, positions, meta,
              ev_ln, ev_w_in, ev_b_if, ev_a_norm, ev_conv_w, ev_conv_b, ev_w_ra, ev_b_ra,
              ev_w_rx, ev_b_rx, ev_lam, ev_w_out,
              od_ln, od_w_in, od_b_f, od_g_qa, od_g_kva, od_w_uq, od_w_ukv, od_g_qn, od_g_qr,
              od_g_kn, od_g_kr, od_g_fq, od_g_fk, od_w_out,
              mlp_ln, w_ff1, w_ff2):
    bn = x.shape[0]
    h = jnp.concatenate([jnp.broadcast_to(meta.astype(x.dtype), (bn, N_META, D_MODEL)), x], axis=1)
    meta_pos = jnp.broadcast_to(jnp.arange(N_META, dtype=jnp.int32)[None, :], (bn, N_META))
    pos_full = jnp.concatenate([meta_pos, positions + N_META], axis=1)
    for layer in range(DEPTH):
        if layer % 2 == 0:
            e = layer // 2
            h = h + even_mixer(h, ev_ln[e], ev_w_in[e], ev_b_if[e], ev_a_norm[e], ev_conv_w[e],
                               ev_conv_b[e], ev_w_ra[e], ev_b_ra[e], ev_w_rx[e], ev_b_rx[e],
                               ev_lam[e], ev_w_out[e])
        else:
            o = layer // 2
            h = h + odd_mixer(h, pos_full, od_ln[o], od_w_in[o], od_b_f[o], od_g_qa[o], od_g_kva[o],
                              od_w_uq[o], od_w_ukv[o], od_g_qn[o], od_g_qr[o], od_g_kn[o],
                              od_g_kr[o], od_g_fq[o], od_g_fk[o], od_w_out[o])
        h = h + sqrelu_mlp(h, mlp_ln[layer], w_ff1[layer], w_ff2[layer])
    return h[:, N_META:]


import jax as _jax
import jax.numpy as _jnp

TWIN_FORMAT = 'train_step'
FWD_PARAMS = ['x', 'positions', 'meta', 'ev_ln', 'ev_w_in', 'ev_b_if', 'ev_a_norm', 'ev_conv_w', 'ev_conv_b', 'ev_w_ra', 'ev_b_ra', 'ev_w_rx', 'ev_b_rx', 'ev_lam', 'ev_w_out', 'od_ln', 'od_w_in', 'od_b_f', 'od_g_qa', 'od_g_kva', 'od_w_uq', 'od_w_ukv', 'od_g_qn', 'od_g_qr', 'od_g_kn', 'od_g_kr', 'od_g_fq', 'od_g_fk', 'od_w_out', 'mlp_ln', 'w_ff1', 'w_ff2']
TWIN_WEIGHTS = ['meta', 'ev_ln', 'ev_w_in', 'ev_b_if', 'ev_a_norm', 'ev_conv_w', 'ev_conv_b', 'ev_w_ra', 'ev_b_ra', 'ev_w_rx', 'ev_b_rx', 'ev_lam', 'ev_w_out', 'od_ln', 'od_w_in', 'od_b_f', 'od_g_qa', 'od_g_kva', 'od_w_uq', 'od_w_ukv', 'od_g_qn', 'od_g_qr', 'od_g_kn', 'od_g_kr', 'od_g_fq', 'od_g_fk', 'od_w_out', 'mlp_ln', 'w_ff1', 'w_ff2']
TWIN_DIFF_INPUT = 'x'
TWIN_INPUTS = ['x', 'positions', 'meta', 'ev_ln', 'ev_w_in', 'ev_b_if', 'ev_a_norm', 'ev_conv_w', 'ev_conv_b', 'ev_w_ra', 'ev_b_ra', 'ev_w_rx', 'ev_b_rx', 'ev_lam', 'ev_w_out', 'od_ln', 'od_w_in', 'od_b_f', 'od_g_qa', 'od_g_kva', 'od_w_uq', 'od_w_ukv', 'od_g_qn', 'od_g_qr', 'od_g_kn', 'od_g_kr', 'od_g_fq', 'od_g_fk', 'od_w_out', 'mlp_ln', 'w_ff1', 'w_ff2', 'loss_target', 'm_meta', 'm_ev_ln', 'm_ev_w_in', 'm_ev_b_if', 'm_ev_a_norm', 'm_ev_conv_w', 'm_ev_conv_b', 'm_ev_w_ra', 'm_ev_b_ra', 'm_ev_w_rx', 'm_ev_b_rx', 'm_ev_lam', 'm_ev_w_out', 'm_od_ln', 'm_od_w_in', 'm_od_b_f', 'm_od_g_qa', 'm_od_g_kva', 'm_od_w_uq', 'm_od_w_ukv', 'm_od_g_qn', 'm_od_g_qr', 'm_od_g_kn', 'm_od_g_kr', 'm_od_g_fq', 'm_od_g_fk', 'm_od_w_out', 'm_mlp_ln', 'm_w_ff1', 'm_w_ff2', 'v_meta', 'v_ev_ln', 'v_ev_w_in', 'v_ev_b_if', 'v_ev_a_norm', 'v_ev_conv_w', 'v_ev_conv_b', 'v_ev_w_ra', 'v_ev_b_ra', 'v_ev_w_rx', 'v_ev_b_rx', 'v_ev_lam', 'v_ev_w_out', 'v_od_ln', 'v_od_w_in', 'v_od_b_f', 'v_od_g_qa', 'v_od_g_kva', 'v_od_w_uq', 'v_od_w_ukv', 'v_od_g_qn', 'v_od_g_qr', 'v_od_g_kn', 'v_od_g_kr', 'v_od_g_fq', 'v_od_g_fk', 'v_od_w_out', 'v_mlp_ln', 'v_w_ff1', 'v_w_ff2']
TWIN_OUTPUTS = ['loss', 'grad_x', 'grad_meta', 'grad_ev_ln', 'grad_ev_w_in', 'grad_ev_b_if', 'grad_ev_a_norm', 'grad_ev_conv_w', 'grad_ev_conv_b', 'grad_ev_w_ra', 'grad_ev_b_ra', 'grad_ev_w_rx', 'grad_ev_b_rx', 'grad_ev_lam', 'grad_ev_w_out', 'grad_od_ln', 'grad_od_w_in', 'grad_od_b_f', 'grad_od_g_qa', 'grad_od_g_kva', 'grad_od_w_uq', 'grad_od_w_ukv', 'grad_od_g_qn', 'grad_od_g_qr', 'grad_od_g_kn', 'grad_od_g_kr', 'grad_od_g_fq', 'grad_od_g_fk', 'grad_od_w_out', 'grad_mlp_ln', 'grad_w_ff1', 'grad_w_ff2', 'delta_meta', 'delta_ev_ln', 'delta_ev_w_in', 'delta_ev_b_if', 'delta_ev_a_norm', 'delta_ev_conv_w', 'delta_ev_conv_b', 'delta_ev_w_ra', 'delta_ev_b_ra', 'delta_ev_w_rx', 'delta_ev_b_rx', 'delta_ev_lam', 'delta_ev_w_out', 'delta_od_ln', 'delta_od_w_in', 'delta_od_b_f', 'delta_od_g_qa', 'delta_od_g_kva', 'delta_od_w_uq', 'delta_od_w_ukv', 'delta_od_g_qn', 'delta_od_g_qr', 'delta_od_g_kn', 'delta_od_g_kr', 'delta_od_g_fq', 'delta_od_g_fk', 'delta_od_w_out', 'delta_mlp_ln', 'delta_w_ff1', 'delta_w_ff2', 'new_m_meta', 'new_m_ev_ln', 'new_m_ev_w_in', 'new_m_ev_b_if', 'new_m_ev_a_norm', 'new_m_ev_conv_w', 'new_m_ev_conv_b', 'new_m_ev_w_ra', 'new_m_ev_b_ra', 'new_m_ev_w_rx', 'new_m_ev_b_rx', 'new_m_ev_lam', 'new_m_ev_w_out', 'new_m_od_ln', 'new_m_od_w_in', 'new_m_od_b_f', 'new_m_od_g_qa', 'new_m_od_g_kva', 'new_m_od_w_uq', 'new_m_od_w_ukv', 'new_m_od_g_qn', 'new_m_od_g_qr', 'new_m_od_g_kn', 'new_m_od_g_kr', 'new_m_od_g_fq', 'new_m_od_g_fk', 'new_m_od_w_out', 'new_m_mlp_ln', 'new_m_w_ff1', 'new_m_w_ff2', 'new_v_meta', 'new_v_ev_ln', 'new_v_ev_w_in', 'new_v_ev_b_if', 'new_v_ev_a_norm', 'new_v_ev_conv_w', 'new_v_ev_conv_b', 'new_v_ev_w_ra', 'new_v_ev_b_ra', 'new_v_ev_w_rx', 'new_v_ev_b_rx', 'new_v_ev_lam', 'new_v_ev_w_out', 'new_v_od_ln', 'new_v_od_w_in', 'new_v_od_b_f', 'new_v_od_g_qa', 'new_v_od_g_kva', 'new_v_od_w_uq', 'new_v_od_w_ukv', 'new_v_od_g_qn', 'new_v_od_g_qr', 'new_v_od_g_kn', 'new_v_od_g_kr', 'new_v_od_g_fq', 'new_v_od_g_fk', 'new_v_od_w_out', 'new_v_mlp_ln', 'new_v_w_ff1', 'new_v_w_ff2']
TWIN_LEAF_KINDS = {'loss': 'loss', 'grad_x': 'grad_x', 'grad_meta': 'grad_w', 'grad_ev_ln': 'grad_w', 'grad_ev_w_in': 'grad_w', 'grad_ev_b_if': 'grad_w', 'grad_ev_a_norm': 'grad_w', 'grad_ev_conv_w': 'grad_w', 'grad_ev_conv_b': 'grad_w', 'grad_ev_w_ra': 'grad_w', 'grad_ev_b_ra': 'grad_w', 'grad_ev_w_rx': 'grad_w', 'grad_ev_b_rx': 'grad_w', 'grad_ev_lam': 'grad_w', 'grad_ev_w_out': 'grad_w', 'grad_od_ln': 'grad_w', 'grad_od_w_in': 'grad_w', 'grad_od_b_f': 'grad_w', 'grad_od_g_qa': 'grad_w', 'grad_od_g_kva': 'grad_w', 'grad_od_w_uq': 'grad_w', 'grad_od_w_ukv': 'grad_w', 'grad_od_g_qn': 'grad_w', 'grad_od_g_qr': 'grad_w', 'grad_od_g_kn': 'grad_w', 'grad_od_g_kr': 'grad_w', 'grad_od_g_fq': 'grad_w', 'grad_od_g_fk': 'grad_w', 'grad_od_w_out': 'grad_w', 'grad_mlp_ln': 'grad_w', 'grad_w_ff1': 'grad_w', 'grad_w_ff2': 'grad_w', 'delta_meta': 'delta_w', 'delta_ev_ln': 'delta_w', 'delta_ev_w_in': 'delta_w', 'delta_ev_b_if': 'delta_w', 'delta_ev_a_norm': 'delta_w', 'delta_ev_conv_w': 'delta_w', 'delta_ev_conv_b': 'delta_w', 'delta_ev_w_ra': 'delta_w', 'delta_ev_b_ra': 'delta_w', 'delta_ev_w_rx': 'delta_w', 'delta_ev_b_rx': 'delta_w', 'delta_ev_lam': 'delta_w', 'delta_ev_w_out': 'delta_w', 'delta_od_ln': 'delta_w', 'delta_od_w_in': 'delta_w', 'delta_od_b_f': 'delta_w', 'delta_od_g_qa': 'delta_w', 'delta_od_g_kva': 'delta_w', 'delta_od_w_uq': 'delta_w', 'delta_od_w_ukv': 'delta_w', 'delta_od_g_qn': 'delta_w', 'delta_od_g_qr': 'delta_w', 'delta_od_g_kn': 'delta_w', 'delta_od_g_kr': 'delta_w', 'delta_od_g_fq': 'delta_w', 'delta_od_g_fk': 'delta_w', 'delta_od_w_out': 'delta_w', 'delta_mlp_ln': 'delta_w', 'delta_w_ff1': 'delta_w', 'delta_w_ff2': 'delta_w', 'new_m_meta': 'new_m', 'new_m_ev_ln': 'new_m', 'new_m_ev_w_in': 'new_m', 'new_m_ev_b_if': 'new_m', 'new_m_ev_a_norm': 'new_m', 'new_m_ev_conv_w': 'new_m', 'new_m_ev_conv_b': 'new_m', 'new_m_ev_w_ra': 'new_m', 'new_m_ev_b_ra': 'new_m', 'new_m_ev_w_rx': 'new_m', 'new_m_ev_b_rx': 'new_m', 'new_m_ev_lam': 'new_m', 'new_m_ev_w_out': 'new_m', 'new_m_od_ln': 'new_m', 'new_m_od_w_in': 'new_m', 'new_m_od_b_f': 'new_m', 'new_m_od_g_qa': 'new_m', 'new_m_od_g_kva': 'new_m', 'new_m_od_w_uq': 'new_m', 'new_m_od_w_ukv': 'new_m', 'new_m_od_g_qn': 'new_m', 'new_m_od_g_qr': 'new_m', 'new_m_od_g_kn': 'new_m', 'new_m_od_g_kr': 'new_m', 'new_m_od_g_fq': 'new_m', 'new_m_od_g_fk': 'new_m', 'new_m_od_w_out': 'new_m', 'new_m_mlp_ln': 'new_m', 'new_m_w_ff1': 'new_m', 'new_m_w_ff2': 'new_m', 'new_v_meta': 'new_v', 'new_v_ev_ln': 'new_v', 'new_v_ev_w_in': 'new_v', 'new_v_ev_b_if': 'new_v', 'new_v_ev_a_norm': 'new_v', 'new_v_ev_conv_w': 'new_v', 'new_v_ev_conv_b': 'new_v', 'new_v_ev_w_ra': 'new_v', 'new_v_ev_b_ra': 'new_v', 'new_v_ev_w_rx': 'new_v', 'new_v_ev_b_rx': 'new_v', 'new_v_ev_lam': 'new_v', 'new_v_ev_w_out': 'new_v', 'new_v_od_ln': 'new_v', 'new_v_od_w_in': 'new_v', 'new_v_od_b_f': 'new_v', 'new_v_od_g_qa': 'new_v', 'new_v_od_g_kva': 'new_v', 'new_v_od_w_uq': 'new_v', 'new_v_od_w_ukv': 'new_v', 'new_v_od_g_qn': 'new_v', 'new_v_od_g_qr': 'new_v', 'new_v_od_g_kn': 'new_v', 'new_v_od_g_kr': 'new_v', 'new_v_od_g_fq': 'new_v', 'new_v_od_g_fk': 'new_v', 'new_v_od_w_out': 'new_v', 'new_v_mlp_ln': 'new_v', 'new_v_w_ff1': 'new_v', 'new_v_w_ff2': 'new_v'}


def _forward(args):
    return _fwd_reference(*[args[k] for k in FWD_PARAMS])


def _output_shape():
    out = _jax.eval_shape(lambda: _forward(_fwd_setup_inputs(0)))
    return out.shape, out.dtype

N_MICROBATCH = 1
ADAM_LR = 0.001
ADAM_B1 = 0.9
ADAM_B2 = 0.999
ADAM_EPS = 1e-08
ADAM_WD = 0.01
ADAM_STEP = 10
PER_EXAMPLE_BATCH_AXIS = {'x': 0, 'positions': 0, 'loss_target': 0}
SHARED_INPUTS = []
_WEIGHT_DTYPES = {'meta': _jnp.float32, 'ev_ln': _jnp.float32, 'ev_w_in': _jnp.float32, 'ev_b_if': _jnp.float32, 'ev_a_norm': _jnp.float32, 'ev_conv_w': _jnp.float32, 'ev_conv_b': _jnp.float32, 'ev_w_ra': _jnp.float32, 'ev_b_ra': _jnp.float32, 'ev_w_rx': _jnp.float32, 'ev_b_rx': _jnp.float32, 'ev_lam': _jnp.float32, 'ev_w_out': _jnp.float32, 'od_ln': _jnp.float32, 'od_w_in': _jnp.float32, 'od_b_f': _jnp.float32, 'od_g_qa': _jnp.float32, 'od_g_kva': _jnp.float32, 'od_w_uq': _jnp.float32, 'od_w_ukv': _jnp.float32, 'od_g_qn': _jnp.float32, 'od_g_qr': _jnp.float32, 'od_g_kn': _jnp.float32, 'od_g_kr': _jnp.float32, 'od_g_fq': _jnp.float32, 'od_g_fk': _jnp.float32, 'od_w_out': _jnp.float32, 'mlp_ln': _jnp.float32, 'w_ff1': _jnp.float32, 'w_ff2': _jnp.float32}
MOMENT_SCALE = {'meta': 1.276330e-01, 'ev_ln': 5.586169e+00, 'ev_w_in': 5.904990e-01, 'ev_b_if': 2.422864e+00, 'ev_a_norm': 1.712141e+01, 'ev_conv_w': 3.491925e+00, 'ev_conv_b': 4.949050e+01, 'ev_w_ra': 1.347175e+00, 'ev_b_ra': 8.223126e-01, 'ev_w_rx': 2.477094e+00, 'ev_b_rx': 1.320794e+00, 'ev_lam': 1.261398e+00, 'ev_w_out': 2.899729e+00, 'od_ln': 1.335527e+01, 'od_w_in': 8.428009e+00, 'od_b_f': 9.099876e+01, 'od_g_qa': 5.569987e-01, 'od_g_kva': 2.006777e+01, 'od_w_uq': 4.037976e-01, 'od_w_ukv': 8.122869e+00, 'od_g_qn': 1.630160e+00, 'od_g_qr': 6.170094e-01, 'od_g_kn': 1.616657e+00, 'od_g_kr': 6.469641e-01, 'od_g_fq': 1.216397e+01, 'od_g_fk': 1.211816e+01, 'od_w_out': 1.069596e+01, 'mlp_ln': 1.008037e+02, 'w_ff1': 4.707915e+00, 'w_ff2': 1.973047e+01}


def _to_microbatches(a, axis):
    t = _jnp.moveaxis(a, axis, 0)
    t = t.reshape((N_MICROBATCH, t.shape[0] // N_MICROBATCH) + t.shape[1:])
    return _jnp.moveaxis(t, 1, axis + 1)


def setup_inputs(seed: int = 0) -> dict:
    inp = _fwd_setup_inputs(seed)
    key = _jax.random.fold_in(_jax.random.key(seed), 7919)
    shape, _ = _output_shape()
    out = dict(inp)
    out["loss_target"] = _jax.random.normal(_jax.random.fold_in(key, 0), shape, _jnp.float32)
    for i, name in enumerate(TWIN_WEIGHTS):
        w = inp[name].astype(_jnp.float32)
        if MOMENT_SCALE is None:
            s = _jnp.sqrt(_jnp.mean(_jnp.square(w)) + 1e-30)
        else:
            s = MOMENT_SCALE[name]
        km, kv = _jax.random.split(_jax.random.fold_in(key, i + 1))
        out[name] = w
        out["m_" + name] = s * _jax.random.normal(km, w.shape, _jnp.float32)
        out["v_" + name] = (s * s) * _jax.random.uniform(kv, w.shape, _jnp.float32, 0.5, 1.5)
    if N_MICROBATCH > 1:
        for name, axis in PER_EXAMPLE_BATCH_AXIS.items():
            out[name] = _to_microbatches(out[name], axis)
    return {'x': out['x'], 'positions': out['positions'], 'meta': out['meta'], 'ev_ln': out['ev_ln'], 'ev_w_in': out['ev_w_in'], 'ev_b_if': out['ev_b_if'], 'ev_a_norm': out['ev_a_norm'], 'ev_conv_w': out['ev_conv_w'], 'ev_conv_b': out['ev_conv_b'], 'ev_w_ra': out['ev_w_ra'], 'ev_b_ra': out['ev_b_ra'], 'ev_w_rx': out['ev_w_rx'], 'ev_b_rx': out['ev_b_rx'], 'ev_lam': out['ev_lam'], 'ev_w_out': out['ev_w_out'], 'od_ln': out['od_ln'], 'od_w_in': out['od_w_in'], 'od_b_f': out['od_b_f'], 'od_g_qa': out['od_g_qa'], 'od_g_kva': out['od_g_kva'], 'od_w_uq': out['od_w_uq'], 'od_w_ukv': out['od_w_ukv'], 'od_g_qn': out['od_g_qn'], 'od_g_qr': out['od_g_qr'], 'od_g_kn': out['od_g_kn'], 'od_g_kr': out['od_g_kr'], 'od_g_fq': out['od_g_fq'], 'od_g_fk': out['od_g_fk'], 'od_w_out': out['od_w_out'], 'mlp_ln': out['mlp_ln'], 'w_ff1': out['w_ff1'], 'w_ff2': out['w_ff2'], 'loss_target': out['loss_target'], 'm_meta': out['m_meta'], 'm_ev_ln': out['m_ev_ln'], 'm_ev_w_in': out['m_ev_w_in'], 'm_ev_b_if': out['m_ev_b_if'], 'm_ev_a_norm': out['m_ev_a_norm'], 'm_ev_conv_w': out['m_ev_conv_w'], 'm_ev_conv_b': out['m_ev_conv_b'], 'm_ev_w_ra': out['m_ev_w_ra'], 'm_ev_b_ra': out['m_ev_b_ra'], 'm_ev_w_rx': out['m_ev_w_rx'], 'm_ev_b_rx': out['m_ev_b_rx'], 'm_ev_lam': out['m_ev_lam'], 'm_ev_w_out': out['m_ev_w_out'], 'm_od_ln': out['m_od_ln'], 'm_od_w_in': out['m_od_w_in'], 'm_od_b_f': out['m_od_b_f'], 'm_od_g_qa': out['m_od_g_qa'], 'm_od_g_kva': out['m_od_g_kva'], 'm_od_w_uq': out['m_od_w_uq'], 'm_od_w_ukv': out['m_od_w_ukv'], 'm_od_g_qn': out['m_od_g_qn'], 'm_od_g_qr': out['m_od_g_qr'], 'm_od_g_kn': out['m_od_g_kn'], 'm_od_g_kr': out['m_od_g_kr'], 'm_od_g_fq': out['m_od_g_fq'], 'm_od_g_fk': out['m_od_g_fk'], 'm_od_w_out': out['m_od_w_out'], 'm_mlp_ln': out['m_mlp_ln'], 'm_w_ff1': out['m_w_ff1'], 'm_w_ff2': out['m_w_ff2'], 'v_meta': out['v_meta'], 'v_ev_ln': out['v_ev_ln'], 'v_ev_w_in': out['v_ev_w_in'], 'v_ev_b_if': out['v_ev_b_if'], 'v_ev_a_norm': out['v_ev_a_norm'], 'v_ev_conv_w': out['v_ev_conv_w'], 'v_ev_conv_b': out['v_ev_conv_b'], 'v_ev_w_ra': out['v_ev_w_ra'], 'v_ev_b_ra': out['v_ev_b_ra'], 'v_ev_w_rx': out['v_ev_w_rx'], 'v_ev_b_rx': out['v_ev_b_rx'], 'v_ev_lam': out['v_ev_lam'], 'v_ev_w_out': out['v_ev_w_out'], 'v_od_ln': out['v_od_ln'], 'v_od_w_in': out['v_od_w_in'], 'v_od_b_f': out['v_od_b_f'], 'v_od_g_qa': out['v_od_g_qa'], 'v_od_g_kva': out['v_od_g_kva'], 'v_od_w_uq': out['v_od_w_uq'], 'v_od_w_ukv': out['v_od_w_ukv'], 'v_od_g_qn': out['v_od_g_qn'], 'v_od_g_qr': out['v_od_g_qr'], 'v_od_g_kn': out['v_od_g_kn'], 'v_od_g_kr': out['v_od_g_kr'], 'v_od_g_fq': out['v_od_g_fq'], 'v_od_g_fk': out['v_od_g_fk'], 'v_od_w_out': out['v_od_w_out'], 'v_mlp_ln': out['v_mlp_ln'], 'v_w_ff1': out['v_w_ff1'], 'v_w_ff2': out['v_w_ff2']}


def _loss(weights, diff, rest, loss_target):
    with _jax.named_scope("forward"):
        args = {**rest, TWIN_DIFF_INPUT: diff, **{k: w.astype(_WEIGHT_DTYPES[k]) for k, w in weights.items()}}
        y = _forward(args)
    with _jax.named_scope("loss_head"):
        err = _jnp.square(y.astype(_jnp.float32) - loss_target)
        return 0.5 * _jnp.sum(_jnp.mean(err, axis=-1)) if err.ndim else 0.5 * err


def _adamw(w, g, m, v):
    m = ADAM_B1 * m + (1.0 - ADAM_B1) * g
    v = ADAM_B2 * v + (1.0 - ADAM_B2) * _jnp.square(g)
    m_hat = m / (1.0 - ADAM_B1 ** ADAM_STEP)
    v_hat = v / (1.0 - ADAM_B2 ** ADAM_STEP)
    delta = -ADAM_LR * (m_hat / (_jnp.sqrt(v_hat) + ADAM_EPS) + ADAM_WD * w)
    return delta, m, v


def reference(x, positions, meta, ev_ln, ev_w_in, ev_b_if, ev_a_norm, ev_conv_w, ev_conv_b, ev_w_ra, ev_b_ra, ev_w_rx, ev_b_rx, ev_lam, ev_w_out, od_ln, od_w_in, od_b_f, od_g_qa, od_g_kva, od_w_uq, od_w_ukv, od_g_qn, od_g_qr, od_g_kn, od_g_kr, od_g_fq, od_g_fk, od_w_out, mlp_ln, w_ff1, w_ff2, loss_target, m_meta, m_ev_ln, m_ev_w_in, m_ev_b_if, m_ev_a_norm, m_ev_conv_w, m_ev_conv_b, m_ev_w_ra, m_ev_b_ra, m_ev_w_rx, m_ev_b_rx, m_ev_lam, m_ev_w_out, m_od_ln, m_od_w_in, m_od_b_f, m_od_g_qa, m_od_g_kva, m_od_w_uq, m_od_w_ukv, m_od_g_qn, m_od_g_qr, m_od_g_kn, m_od_g_kr, m_od_g_fq, m_od_g_fk, m_od_w_out, m_mlp_ln, m_w_ff1, m_w_ff2, v_meta, v_ev_ln, v_ev_w_in, v_ev_b_if, v_ev_a_norm, v_ev_conv_w, v_ev_conv_b, v_ev_w_ra, v_ev_b_ra, v_ev_w_rx, v_ev_b_rx, v_ev_lam, v_ev_w_out, v_od_ln, v_od_w_in, v_od_b_f, v_od_g_qa, v_od_g_kva, v_od_w_uq, v_od_w_ukv, v_od_g_qn, v_od_g_qr, v_od_g_kn, v_od_g_kr, v_od_g_fq, v_od_g_fk, v_od_w_out, v_mlp_ln, v_w_ff1, v_w_ff2):
    given = dict(x=x, positions=positions, meta=meta, ev_ln=ev_ln, ev_w_in=ev_w_in, ev_b_if=ev_b_if, ev_a_norm=ev_a_norm, ev_conv_w=ev_conv_w, ev_conv_b=ev_conv_b, ev_w_ra=ev_w_ra, ev_b_ra=ev_b_ra, ev_w_rx=ev_w_rx, ev_b_rx=ev_b_rx, ev_lam=ev_lam, ev_w_out=ev_w_out, od_ln=od_ln, od_w_in=od_w_in, od_b_f=od_b_f, od_g_qa=od_g_qa, od_g_kva=od_g_kva, od_w_uq=od_w_uq, od_w_ukv=od_w_ukv, od_g_qn=od_g_qn, od_g_qr=od_g_qr, od_g_kn=od_g_kn, od_g_kr=od_g_kr, od_g_fq=od_g_fq, od_g_fk=od_g_fk, od_w_out=od_w_out, mlp_ln=mlp_ln, w_ff1=w_ff1, w_ff2=w_ff2, loss_target=loss_target, m_meta=m_meta, m_ev_ln=m_ev_ln, m_ev_w_in=m_ev_w_in, m_ev_b_if=m_ev_b_if, m_ev_a_norm=m_ev_a_norm, m_ev_conv_w=m_ev_conv_w, m_ev_conv_b=m_ev_conv_b, m_ev_w_ra=m_ev_w_ra, m_ev_b_ra=m_ev_b_ra, m_ev_w_rx=m_ev_w_rx, m_ev_b_rx=m_ev_b_rx, m_ev_lam=m_ev_lam, m_ev_w_out=m_ev_w_out, m_od_ln=m_od_ln, m_od_w_in=m_od_w_in, m_od_b_f=m_od_b_f, m_od_g_qa=m_od_g_qa, m_od_g_kva=m_od_g_kva, m_od_w_uq=m_od_w_uq, m_od_w_ukv=m_od_w_ukv, m_od_g_qn=m_od_g_qn, m_od_g_qr=m_od_g_qr, m_od_g_kn=m_od_g_kn, m_od_g_kr=m_od_g_kr, m_od_g_fq=m_od_g_fq, m_od_g_fk=m_od_g_fk, m_od_w_out=m_od_w_out, m_mlp_ln=m_mlp_ln, m_w_ff1=m_w_ff1, m_w_ff2=m_w_ff2, v_meta=v_meta, v_ev_ln=v_ev_ln, v_ev_w_in=v_ev_w_in, v_ev_b_if=v_ev_b_if, v_ev_a_norm=v_ev_a_norm, v_ev_conv_w=v_ev_conv_w, v_ev_conv_b=v_ev_conv_b, v_ev_w_ra=v_ev_w_ra, v_ev_b_ra=v_ev_b_ra, v_ev_w_rx=v_ev_w_rx, v_ev_b_rx=v_ev_b_rx, v_ev_lam=v_ev_lam, v_ev_w_out=v_ev_w_out, v_od_ln=v_od_ln, v_od_w_in=v_od_w_in, v_od_b_f=v_od_b_f, v_od_g_qa=v_od_g_qa, v_od_g_kva=v_od_g_kva, v_od_w_uq=v_od_w_uq, v_od_w_ukv=v_od_w_ukv, v_od_g_qn=v_od_g_qn, v_od_g_qr=v_od_g_qr, v_od_g_kn=v_od_g_kn, v_od_g_kr=v_od_g_kr, v_od_g_fq=v_od_g_fq, v_od_g_fk=v_od_g_fk, v_od_w_out=v_od_w_out, v_mlp_ln=v_mlp_ln, v_w_ff1=v_w_ff1, v_w_ff2=v_w_ff2)
    weights = {n: given[n] for n in TWIN_WEIGHTS}
    shared = {n: given[n] for n in SHARED_INPUTS}
    per_example = {n: given[n] for n in ['x', 'positions']}
    grad_fn = _jax.value_and_grad(_loss, argnums=(0, 1))

    def one_microbatch(ex, loss_target):
        ex = dict(ex)
        diff = ex.pop(TWIN_DIFF_INPUT)
        return grad_fn(weights, diff, {**shared, **ex}, loss_target)

    if N_MICROBATCH == 1:
        loss, (grad_w, grad_x) = one_microbatch(per_example, given["loss_target"])
    else:
        def body(carry, xs):
            loss_sum, grad_sum = carry
            l_k, (gw_k, gx_k) = one_microbatch(xs[0], xs[1])
            with _jax.named_scope("update"):
                return (loss_sum + l_k, _jax.tree.map(_jnp.add, grad_sum, gw_k)), gx_k

        init = (_jnp.zeros((), _jnp.float32), _jax.tree.map(_jnp.zeros_like, weights))
        (loss, grad_w), grad_x = _jax.lax.scan(body, init, (per_example, given["loss_target"]))
    with _jax.named_scope("update"):
        delta_w, new_m, new_v = {}, {}, {}
        for n in TWIN_WEIGHTS:
            delta_w[n], new_m[n], new_v[n] = _adamw(weights[n], grad_w[n], given["m_" + n], given["v_" + n])
    return (loss, grad_x, *[grad_w[n] for n in TWIN_WEIGHTS], *[delta_w[n] for n in TWIN_WEIGHTS],
            *[new_m[n] for n in TWIN_WEIGHTS], *[new_v[n] for n in TWIN_WEIGHTS])
```

```python
import functools

import jax
import jax.numpy as jnp
from jax import lax
from jax.experimental import pallas as pl
from jax.experimental.pallas import tpu as pltpu

F32 = jnp.float32
BF16 = jnp.bfloat16

D_MODEL = 1024
N_META = 16
PAD = 112
EPS = 1e-6
NEG = -1e30
A_HEADS, A_DQK, A_DV, A_CHUNK = 4, 128, 256, 64
B_BLOCKS, B_BLOCK, CONV_W, LRU_C = 8, 128, 4, 8.0
C_HEADS, C_Q_LORA, C_KV_LORA, C_NOPE, C_ROPE, C_V = 8, 384, 256, 64, 32, 64
ROPE_THETA = 10000.0
D_HEADS, D_HD = 8, 64
HP = 128
ZE = 5376
ZO = 4096
N_DEV = 8
ADAM_LR, ADAM_B1, ADAM_B2, ADAM_EPS, ADAM_WD, ADAM_STEP = 0.001, 0.9, 0.999, 1e-08, 0.01, 10
VMEM_LIMIT = 56 * 1024 * 1024
SCAN_BLOCK = 128


def _pick(n, prefs):
    for p in prefs:
        if n % p == 0:
            return p
    return n


def _cparams(dims):
    return pltpu.CompilerParams(dimension_semantics=dims, vmem_limit_bytes=VMEM_LIMIT)


def _full_spec(shape):
    nd = len(shape)
    return pl.BlockSpec(shape, lambda *_: (0,) * nd)


def _mm(a, b, mode, name, out_dtype=F32, add=None):
    if mode == "nn":
        (m, k), n = a.shape, b.shape[1]
    elif mode == "nt":
        (m, k), n = a.shape, b.shape[0]
    else:
        (k, m), n = a.shape, b.shape[1]
    tm = _pick(m, (1408, 1024, 768, 512, 384, 256, 128))
    tn = _pick(n, (1024, 768, 512, 384, 256, 128))
    tk = _pick(k, (1408, 1024, 768, 512, 384, 256, 128))
    nk = k // tk
    if mode == "nn":
        a_spec = pl.BlockSpec((tm, tk), lambda i, j, q: (i, q))
        b_spec = pl.BlockSpec((tk, tn), lambda i, j, q: (q, j))
        dn = (((1,), (0,)), ((), ()))
    elif mode == "nt":
        a_spec = pl.BlockSpec((tm, tk), lambda i, j, q: (i, q))
        b_spec = pl.BlockSpec((tn, tk), lambda i, j, q: (j, q))
        dn = (((1,), (1,)), ((), ()))
    else:
        a_spec = pl.BlockSpec((tk, tm), lambda i, j, q: (q, i))
        b_spec = pl.BlockSpec((tk, tn), lambda i, j, q: (q, j))
        dn = (((0,), (0,)), ((), ()))
    o_spec = pl.BlockSpec((tm, tn), lambda i, j, q: (i, j))
    has_add = add is not None

    def body(*refs):
        if has_add:
            a_ref, b_ref, add_ref, o_ref, acc = refs
        else:
            a_ref, b_ref, o_ref, acc = refs
        q = pl.program_id(2)

        @pl.when(q == 0)
        def _():
            acc[...] = jnp.zeros_like(acc)

        acc[...] += lax.dot_general(a_ref[...].astype(BF16), b_ref[...].astype(BF16), dn,
                                    preferred_element_type=F32)

        @pl.when(q == nk - 1)
        def _():
            r = acc[...]
            if has_add:
                r = r + add_ref[...]
            o_ref[...] = r.astype(o_ref.dtype)

    ins = [a, b] + ([add] if has_add else [])
    in_specs = [a_spec, b_spec] + ([o_spec] if has_add else [])
    return pl.pallas_call(
        body, name=name, grid=(m // tm, n // tn, nk), in_specs=in_specs, out_specs=o_spec,
        out_shape=jax.ShapeDtypeStruct((m, n), out_dtype),
        scratch_shapes=[pltpu.VMEM((tm, tn), F32)],
        compiler_params=_cparams(("parallel", "parallel", "arbitrary")),
    )(*ins)


def _row_specs(rows, tm):
    return [pl.BlockSpec((tm, w), functools.partial(lambda cb, i: (i, cb), cb)) for (_, w, cb) in rows]


def _row_fwd(name, f, rows, params, outs, tm):
    tp = rows[0][0].shape[0]
    nr, npar = len(rows), len(params)

    def body(*refs):
        i = pl.program_id(0)
        rv = [r[...] for r in refs[:nr]]
        pv = [r[...] for r in refs[nr:nr + npar]]
        res = f(i, rv, pv)
        for o_ref, r in zip(refs[nr + npar:], res):
            o_ref[...] = r.astype(o_ref.dtype)

    res = pl.pallas_call(
        body, name=name, grid=(tp // tm,),
        in_specs=_row_specs(rows, tm) + [_full_spec(p.shape) for p in params],
        out_specs=[pl.BlockSpec((tm, w), lambda i: (i, 0)) for (w, _) in outs],
        out_shape=[jax.ShapeDtypeStruct((tp, w), dt) for (w, dt) in outs],
        compiler_params=_cparams(("parallel",)),
    )(*[r[0] for r in rows], *params)
    return list(res)


def _row_bwd(name, f, rows, params, douts, tm, diff, add=None):
    tp = rows[0][0].shape[0]
    nr, npar, nd = len(rows), len(params), len(douts)
    didx = [k for k in range(nr) if diff[k]]
    has_add = add is not None

    def body(*refs):
        i = pl.program_id(0)
        rv = [r[...] for r in refs[:nr]]
        pv = [r[...] for r in refs[nr:nr + npar]]
        dv = [r[...] for r in refs[nr + npar:nr + npar + nd]]
        pos = nr + npar + nd
        add_ref = refs[pos] if has_add else None
        pos += 1 if has_add else 0
        dr_refs = refs[pos:pos + len(didx)]
        dp_refs = refs[pos + len(didx):]

        def g(drv, pvs):
            full = list(rv)
            for k, val in zip(didx, drv):
                full[k] = val
            return tuple(f(i, full, list(pvs)))

        _, vjp = jax.vjp(g, [rv[k] for k in didx], pv)
        d_r, d_p = vjp(tuple(dv))
        for n_, (ref, val) in enumerate(zip(dr_refs, d_r)):
            if has_add and n_ == 0:
                val = val + add_ref[...]
            ref[...] = val

        @pl.when(i == 0)
        def _():
            for ref in dp_refs:
                ref[...] = jnp.zeros_like(ref)

        for ref, val in zip(dp_refs, d_p):
            ref[...] += val

    in_specs = (_row_specs(rows, tm) + [_full_spec(p.shape) for p in params] + _row_specs(douts, tm))
    ins = [r[0] for r in rows] + list(params) + [d[0] for d in douts]
    if has_add:
        in_specs.append(pl.BlockSpec((tm, rows[didx[0]][1]), lambda i: (i, 0)))
        ins.append(add)
    out_specs = ([pl.BlockSpec((tm, rows[k][1]), lambda i: (i, 0)) for k in didx]
                 + [_full_spec(p.shape) for p in params])
    out_shape = ([jax.ShapeDtypeStruct((tp, rows[k][1]), F32) for k in didx]
                 + [jax.ShapeDtypeStruct(p.shape, F32) for p in params])
    res = pl.pallas_call(
        body, name=name, grid=(tp // tm,), in_specs=in_specs, out_specs=out_specs, out_shape=out_shape,
        compiler_params=_cparams(("arbitrary",)),
    )(*ins)
    return list(res[:len(didx)]), list(res[len(didx):])


def _rowmask(i, tm):
    return (i * tm + lax.broadcasted_iota(jnp.int32, (tm, 1), 0)) >= PAD


def _lane(n=HP):
    return lax.broadcasted_iota(jnp.int32, (1, n), 1)


def _softplus(x):
    return jnp.maximum(x, 0.0) + jnp.log(1.0 + jnp.exp(-jnp.abs(x)))


def _log_sigmoid(x):
    return -_softplus(-x)


def _sigmoid(x):
    return 1.0 / (1.0 + jnp.exp(-x))


@functools.partial(jax.custom_vjp, nondiff_argnums=(1,))
def _lroll(x, s):
    return pltpu.roll(x, s % HP, 1)


def _lroll_fwd(x, s):
    return _lroll(x, s), None


def _lroll_bwd(s, _, g):
    return (pltpu.roll(g, (-s) % HP, 1),)


_lroll.defvjp(_lroll_fwd, _lroll_bwd)


def _f_norm(i, rv, pv):
    (h,), (g,) = rv, pv
    return [h * lax.rsqrt(jnp.mean(h * h, axis=-1, keepdims=True) + EPS) * g]


def _f_gate0(i, rv, pv):
    (misc,), (b,) = rv, pv
    tm = misc.shape[0]
    x = misc + b
    lane, ok = _lane(), _rowmask(i, tm)
    li = jnp.where(ok, x, NEG)
    lf = jnp.where(ok, _log_sigmoid(x), 0.0)
    return [jnp.where(lane < A_HEADS, li, jnp.where(lane < 2 * A_HEADS, lf, 0.0))]


def _f_aout(i, rv, pv):
    (ha, o), (g,) = rv, pv
    outs = []
    for h in range(A_HEADS):
        x = ha[:, h * A_DV:(h + 1) * A_DV]
        outs.append(x * lax.rsqrt(jnp.mean(x * x, axis=-1, keepdims=True) + EPS) * g)
    return [jnp.concatenate(outs, axis=1) * _sigmoid(o)]


def _f_gates(i, rv, pv):
    (xc,), (w_ra, b_ra, w_rx, b_rx, lam) = rv, pv
    tm = xc.shape[0]
    ra, rx = [], []
    for g in range(B_BLOCKS):
        xg = xc[:, g * B_BLOCK:(g + 1) * B_BLOCK].astype(BF16)
        ra.append(jnp.dot(xg, w_ra[g].astype(BF16), preferred_element_type=F32))
        rx.append(jnp.dot(xg, w_rx[g].astype(BF16), preferred_element_type=F32))
    r = _sigmoid(jnp.concatenate(ra, axis=1) + b_ra)
    ig = _sigmoid(jnp.concatenate(rx, axis=1) + b_rx)
    log_a = -LRU_C * r * _softplus(-lam)
    a = jnp.exp(log_a)
    u = jnp.sqrt(1.0 - jnp.exp(2.0 * log_a)) * (ig * xc)
    return [a, jnp.where(_rowmask(i, tm), u, 0.0)]


def _f_bout(i, rv, pv):
    hs, gb = rv
    gelu = 0.5 * gb * (1.0 + jnp.tanh(0.7978845608028654 * (gb + 0.044715 * gb * gb * gb)))
    return [hs * gelu]


def _f_relu2(i, rv, pv):
    (p,) = rv
    r = jnp.maximum(p, 0.0)
    return [r * r]


def _f_cnorm(i, rv, pv):
    (cq, ckv), (gq, gkv) = rv, pv
    return [cq * lax.rsqrt(jnp.mean(cq * cq, axis=-1, keepdims=True) + EPS) * gq,
            ckv * lax.rsqrt(jnp.mean(ckv * ckv, axis=-1, keepdims=True) + EPS) * gkv]


def _rope128(x, cos, sin):
    lane = _lane()
    rot = jnp.where((lane >= C_NOPE) & (lane < C_NOPE + C_ROPE // 2), -_lroll(x, -(C_ROPE // 2)),
                    jnp.where((lane >= C_NOPE + C_ROPE // 2) & (lane < C_NOPE + C_ROPE), _lroll(x, C_ROPE // 2), 0.0))
    return x * cos + rot * sin


def _f_mlaprep(i, rv, pv):
    (q_, kk_, misc, cos, sin), (gq, gk) = rv, pv
    lane = _lane()
    m_n = lane < C_NOPE
    m_r = (lane >= C_NOPE) & (lane < C_NOPE + C_ROPE)

    def norm2(x, g):
        x2 = x * x
        sn = jnp.sum(jnp.where(m_n, x2, 0.0), axis=-1, keepdims=True) * (1.0 / C_NOPE)
        sr = jnp.sum(jnp.where(m_r, x2, 0.0), axis=-1, keepdims=True) * (1.0 / C_ROPE)
        scale = jnp.where(m_n, lax.rsqrt(sn + EPS), jnp.where(m_r, lax.rsqrt(sr + EPS), 0.0))
        return x * scale * g

    kr = _rope128(norm2(jnp.where(m_r, misc, 0.0), gk), cos, sin)
    qs, ks = [], []
    for h in range(C_HEADS):
        qs.append(_rope128(norm2(q_[:, h * HP:(h + 1) * HP], gq), cos, sin))
        ks.append(norm2(jnp.where(m_n, kk_[:, h * HP:(h + 1) * HP], 0.0), gk) + kr)
    return [jnp.concatenate(qs, axis=1), jnp.concatenate(ks, axis=1)]


def _f_foxprep(i, rv, pv):
    (fq, fk, misc), (gq, gk, bf) = rv, pv
    tm = fq.shape[0]
    lane = _lane()

    def hnorm(x, g):
        outs = []
        for h in range(D_HEADS):
            xh = x[:, h * HP:(h + 1) * HP]
            ss = jnp.sum(xh * xh, axis=-1, keepdims=True) * (1.0 / D_HD)
            outs.append(xh * lax.rsqrt(ss + EPS) * g)
        return jnp.concatenate(outs, axis=1)

    lf = jnp.where(_rowmask(i, tm) & (lane >= 96) & (lane < 96 + D_HEADS), _log_sigmoid(misc + bf), 0.0)
    return [hnorm(fq, gq), hnorm(fk, gk), lf]


def _mlstm_chunk(c, n, m, q, k, v, li, lf):
    ln = q.shape[0]
    r = lax.broadcasted_iota(jnp.int32, (ln, ln), 0)
    cc = lax.broadcasted_iota(jnp.int32, (ln, ln), 1)
    causal = cc <= r
    eye = cc == r
    li_row = jnp.sum(jnp.where(eye, li, 0.0), axis=0, keepdims=True)
    b_col = jnp.sum(jnp.where(causal, jnp.sum(jnp.where(eye, lf, 0.0), axis=0, keepdims=True), 0.0),
                    axis=1, keepdims=True)
    b_row = jnp.sum(jnp.where(r <= cc, lf, 0.0), axis=0, keepdims=True)
    k = k * (A_DQK ** -0.5)
    qb, kb, vb = q.astype(BF16), k.astype(BF16), v.astype(BF16)
    dmat = jnp.where(causal, b_col - b_row + li_row, NEG)
    inter = b_col + m
    m_row = jnp.maximum(inter, jnp.max(dmat, axis=1, keepdims=True))
    w_intra = jnp.exp(dmat - m_row)
    w_inter = jnp.exp(inter - m_row)
    s = lax.dot_general(qb, kb, (((1,), (1,)), ((), ())), preferred_element_type=F32) * w_intra
    num = (w_inter * jnp.dot(qb, c.astype(BF16), preferred_element_type=F32)
           + jnp.dot(s.astype(BF16), vb, preferred_element_type=F32))
    den = w_inter * jnp.sum(q * n, axis=1, keepdims=True) + jnp.sum(s, axis=1, keepdims=True)
    h = num / jnp.maximum(jnp.abs(den), jnp.exp(-m_row))
    g = jnp.sum(lf, axis=0, keepdims=True)
    dk = g - b_col + li
    m_new = jnp.maximum(g + m, jnp.max(dk, axis=0, keepdims=True))
    wk = jnp.exp(dk - m_new)
    sc = jnp.exp(g + m - m_new)
    kw = wk * k
    c_new = sc * c + lax.dot_general(kw.astype(BF16), vb, (((0,), (0,)), ((), ())), preferred_element_type=F32)
    n_new = sc * n + jnp.sum(kw, axis=0, keepdims=True)
    return c_new, n_new, m_new, h


def _mlstm_fwd(z, li, lf):
    tp = z.shape[0]
    nc = tp // A_CHUNK
    ln = A_CHUNK

    def body(q_ref, k_ref, v_ref, li_ref, lf_ref, h_ref, cs_ref, ns_ref, ms_ref, c_s, n_s, m_s):
        @pl.when(pl.program_id(1) == 0)
        def _():
            c_s[...] = jnp.zeros_like(c_s)
            n_s[...] = jnp.zeros_like(n_s)
            m_s[...] = jnp.zeros_like(m_s)

        c, n, m = c_s[...], n_s[...], m_s[...]
        cs_ref[0, 0] = c
        ns_ref[0, 0] = n
        ms_ref[0, 0] = m
        c2, n2, m2, h = _mlstm_chunk(c, n, m, q_ref[...], k_ref[...], v_ref[...], li_ref[0], lf_ref[0])
        c_s[...] = c2
        n_s[...] = n2
        m_s[...] = m2
        h_ref[...] = h

    g_spec = pl.BlockSpec((1, ln, 1), lambda h, j: (h, j, 0))
    return pl.pallas_call(
        body, name="mlstm_fwd", grid=(A_HEADS, nc),
        in_specs=[pl.BlockSpec((ln, A_DQK), lambda h, j: (j, h)),
                  pl.BlockSpec((ln, A_DQK), lambda h, j: (j, A_HEADS + h)),
                  pl.BlockSpec((ln, A_DV), lambda h, j: (j, A_HEADS + h)),
                  g_spec, g_spec],
        out_specs=[pl.BlockSpec((ln, A_DV), lambda h, j: (j, h)),
                   pl.BlockSpec((1, 1, A_DQK, A_DV), lambda h, j: (h, j, 0, 0)),
                   pl.BlockSpec((1, 1, 1, A_DQK), lambda h, j: (h, j, 0, 0)),
                   pl.BlockSpec((1, 1, 1, 1), lambda h, j: (h, j, 0, 0))],
        out_shape=[jax.ShapeDtypeStruct((tp, A_HEADS * A_DV), F32),
                   jax.ShapeDtypeStruct((A_HEADS, nc, A_DQK, A_DV), F32),
                   jax.ShapeDtypeStruct((A_HEADS, nc, 1, A_DQK), F32),
                   jax.ShapeDtypeStruct((A_HEADS, nc, 1, 1), F32)],
        scratch_shapes=[pltpu.VMEM((A_DQK, A_DV), F32), pltpu.VMEM((1, A_DQK), F32), pltpu.VMEM((1, 1), F32)],
        compiler_params=_cparams(("parallel", "arbitrary")),
    )(z, z, z, li, lf)


def _mlstm_bwd(z, li, lf, cs, ns, ms, dh):
    tp = z.shape[0]
    nc = tp // A_CHUNK
    ln = A_CHUNK

    def body(q_ref, k_ref, v_ref, li_ref, lf_ref, cs_ref, ns_ref, ms_ref, dh_ref,
             dq_ref, dk_ref, dv_ref, dli_ref, dlf_ref, dc_s, dn_s, dm_s):
        @pl.when(pl.program_id(1) == 0)
        def _():
            dc_s[...] = jnp.zeros_like(dc_s)
            dn_s[...] = jnp.zeros_like(dn_s)
            dm_s[...] = jnp.zeros_like(dm_s)

        _, vjp = jax.vjp(_mlstm_chunk, cs_ref[0, 0], ns_ref[0, 0], ms_ref[0, 0], q_ref[...], k_ref[...],
                         v_ref[...], li_ref[0], lf_ref[0])
        dc, dn, dm, dq, dk, dv, dli, dlf = vjp((dc_s[...], dn_s[...], dm_s[...], dh_ref[...]))
        dc_s[...] = dc
        dn_s[...] = dn
        dm_s[...] = dm
        dq_ref[...] = dq
        dk_ref[...] = dk
        dv_ref[...] = dv
        dli_ref[0] = dli
        dlf_ref[0] = dlf

    def rj(j):
        return nc - 1 - j

    g_spec = pl.BlockSpec((1, ln, 1), lambda h, j: (h, rj(j), 0))
    return pl.pallas_call(
        body, name="mlstm_bwd", grid=(A_HEADS, nc),
        in_specs=[pl.BlockSpec((ln, A_DQK), lambda h, j: (rj(j), h)),
                  pl.BlockSpec((ln, A_DQK), lambda h, j: (rj(j), A_HEADS + h)),
                  pl.BlockSpec((ln, A_DV), lambda h, j: (rj(j), A_HEADS + h)),
                  g_spec, g_spec,
                  pl.BlockSpec((1, 1, A_DQK, A_DV), lambda h, j: (h, rj(j), 0, 0)),
                  pl.BlockSpec((1, 1, 1, A_DQK), lambda h, j: (h, rj(j), 0, 0)),
                  pl.BlockSpec((1, 1, 1, 1), lambda h, j: (h, rj(j), 0, 0)),
                  pl.BlockSpec((ln, A_DV), lambda h, j: (rj(j), h))],
        out_specs=[pl.BlockSpec((ln, A_DQK), lambda h, j: (rj(j), h)),
                   pl.BlockSpec((ln, A_DQK), lambda h, j: (rj(j), h)),
                   pl.BlockSpec((ln, A_DV), lambda h, j: (rj(j), h)),
                   g_spec, g_spec],
        out_shape=[jax.ShapeDtypeStruct((tp, A_HEADS * A_DQK), F32),
                   jax.ShapeDtypeStruct((tp, A_HEADS * A_DQK), F32),
                   jax.ShapeDtypeStruct((tp, A_HEADS * A_DV), F32),
                   jax.ShapeDtypeStruct((A_HEADS, tp, 1), F32),
                   jax.ShapeDtypeStruct((A_HEADS, tp, 1), F32)],
        scratch_shapes=[pltpu.VMEM((A_DQK, A_DV), F32), pltpu.VMEM((1, A_DQK), F32), pltpu.VMEM((1, 1), F32)],
        compiler_params=_cparams(("parallel", "arbitrary")),
    )(z, z, z, li, lf, cs, ns, ms, dh)


def _shift_down(x, s, row):
    return x if s == 0 else jnp.where(row >= s, pltpu.roll(x, s, 0), 0.0)


def _shift_up(x, s, row):
    n = x.shape[0]
    return x if s == 0 else jnp.where(row < n - s, pltpu.roll(x, n - s, 0), 0.0)


def _conv_fwd(z, xcb, w, b):
    tp, c = z.shape[0], w.shape[1]
    ct = 256

    def body(x_ref, w_ref, b_ref, o_ref):
        x = x_ref[...]
        row = lax.broadcasted_iota(jnp.int32, (tp, 1), 0)
        acc = jnp.zeros_like(x) + b_ref[...]
        for k in range(CONV_W):
            acc = acc + w_ref[k:k + 1, :] * _shift_down(x, CONV_W - 1 - k, row)
        o_ref[...] = acc

    return pl.pallas_call(
        body, name="conv_fwd", grid=(c // ct,),
        in_specs=[pl.BlockSpec((tp, ct), lambda j: (0, xcb + j)), pl.BlockSpec((CONV_W, ct), lambda j: (0, j)),
                  pl.BlockSpec((1, ct), lambda j: (0, j))],
        out_specs=pl.BlockSpec((tp, ct), lambda j: (0, j)),
        out_shape=jax.ShapeDtypeStruct((tp, c), F32),
        compiler_params=_cparams(("parallel",)),
    )(z, w, b)


def _conv_bwd(z, xcb, w, dxc):
    tp, c = z.shape[0], w.shape[1]
    ct = 256

    def body(x_ref, w_ref, d_ref, dx_ref, dw_ref, db_ref):
        x, d = x_ref[...], d_ref[...]
        row = lax.broadcasted_iota(jnp.int32, (tp, 1), 0)
        acc = jnp.zeros_like(x)
        for k in range(CONV_W):
            s = CONV_W - 1 - k
            acc = acc + w_ref[k:k + 1, :] * _shift_up(d, s, row)
            dw_ref[k:k + 1, :] = jnp.sum(d * _shift_down(x, s, row), axis=0, keepdims=True)
        dx_ref[...] = acc
        db_ref[...] = jnp.sum(d, axis=0, keepdims=True)

    return pl.pallas_call(
        body, name="conv_bwd", grid=(c // ct,),
        in_specs=[pl.BlockSpec((tp, ct), lambda j: (0, xcb + j)), pl.BlockSpec((CONV_W, ct), lambda j: (0, j)),
                  pl.BlockSpec((tp, ct), lambda j: (0, j))],
        out_specs=[pl.BlockSpec((tp, ct), lambda j: (0, j)), pl.BlockSpec((CONV_W, ct), lambda j: (0, j)),
                   pl.BlockSpec((1, ct), lambda j: (0, j))],
        out_shape=[jax.ShapeDtypeStruct((tp, c), F32), jax.ShapeDtypeStruct((CONV_W, c), F32),
                   jax.ShapeDtypeStruct((1, c), F32)],
        compiler_params=_cparams(("parallel",)),
    )(z, w, dxc)


def _scan_fwd(a, u, name):
    tp, c = a.shape
    ct = _pick(c, (256, 128))
    lb = SCAN_BLOCK
    nb = tp // lb

    def body(a_ref, u_ref, h_ref, hp_ref):
        row = lax.broadcasted_iota(jnp.int32, (lb, 1), 0)

        def blk(j, carry):
            r0 = pl.multiple_of(j * lb, lb)
            aa, uu = a_ref[pl.ds(r0, lb), :], u_ref[pl.ds(r0, lb), :]
            s = 1
            while s < lb:
                mk = row >= s
                uu = jnp.where(mk, aa * pltpu.roll(uu, s, 0) + uu, uu)
                aa = jnp.where(mk, aa * pltpu.roll(aa, s, 0), aa)
                s *= 2
            hh = uu + aa * carry
            h_ref[pl.ds(r0, lb), :] = hh
            hp_ref[pl.ds(r0, lb), :] = jnp.where(row >= 1, pltpu.roll(hh, 1, 0), carry)
            return hh[lb - 1:lb, :]

        lax.fori_loop(0, nb, blk, jnp.zeros((1, ct), F32))

    spec = pl.BlockSpec((tp, ct), lambda j: (0, j))
    return pl.pallas_call(
        body, name=name, grid=(c // ct,), in_specs=[spec, spec], out_specs=[spec, spec],
        out_shape=[jax.ShapeDtypeStruct((tp, c), F32)] * 2,
        compiler_params=_cparams(("parallel",)),
    )(a, u)


def _scan_bwd(a, hprev, dh, name):
    tp, c = a.shape
    ct = _pick(c, (256, 128))
    lb = SCAN_BLOCK
    nb = tp // lb

    def body(a_ref, hp_ref, dh_ref, du_ref, da_ref):
        row = lax.broadcasted_iota(jnp.int32, (lb, 1), 0)

        def blk(jj, carry):
            g_next, a_next = carry
            r0 = pl.multiple_of((nb - 1 - jj) * lb, lb)
            a_blk = a_ref[pl.ds(r0, lb), :]
            aa = jnp.where(row < lb - 1, pltpu.roll(a_blk, lb - 1, 0), a_next)
            gg = dh_ref[pl.ds(r0, lb), :]
            s = 1
            while s < lb:
                mk = row < lb - s
                gg = jnp.where(mk, aa * pltpu.roll(gg, lb - s, 0) + gg, gg)
                aa = jnp.where(mk, aa * pltpu.roll(aa, lb - s, 0), aa)
                s *= 2
            gg = gg + aa * g_next
            du_ref[pl.ds(r0, lb), :] = gg
            da_ref[pl.ds(r0, lb), :] = gg * hp_ref[pl.ds(r0, lb), :]
            return gg[0:1, :], a_blk[0:1, :]

        lax.fori_loop(0, nb, blk, (jnp.zeros((1, ct), F32), jnp.zeros((1, ct), F32)))

    spec = pl.BlockSpec((tp, ct), lambda j: (0, j))
    return pl.pallas_call(
        body, name=name, grid=(c // ct,), in_specs=[spec, spec, spec], out_specs=[spec, spec],
        out_shape=[jax.ShapeDtypeStruct((tp, c), F32)] * 2,
        compiler_params=_cparams(("parallel",)),
    )(a, hprev, dh)


def _attn_mask(i, tq, tp):
    t = i * tq + lax.broadcasted_iota(jnp.int32, (tq, 1), 0)
    s = lax.broadcasted_iota(jnp.int32, (1, tp), 1)
    return (s <= t) & ((s >= PAD) | (s == t))


def _attn_fwd(q, k, v, vcb, scale, bias, name):
    tp = q.shape[0]
    nh = q.shape[1] // HP
    tq = _pick(tp, (384, 256, 128))
    has_bias = bias is not None

    def body(*refs):
        if has_bias:
            q_ref, k_ref, v_ref, bq_ref, bk_ref, o_ref, lse_ref = refs
        else:
            q_ref, k_ref, v_ref, o_ref, lse_ref = refs
        i = pl.program_id(1)
        s = lax.dot_general(q_ref[...].astype(BF16), k_ref[...].astype(BF16), (((1,), (1,)), ((), ())),
                            preferred_element_type=F32) * scale
        if has_bias:
            s = s + (bq_ref[0] - bk_ref[0])
        s = jnp.where(_attn_mask(i, tq, tp), s, NEG)
        m = jnp.max(s, axis=1, keepdims=True)
        p = jnp.exp(s - m)
        l = jnp.sum(p, axis=1, keepdims=True)
        o = jnp.dot(p.astype(BF16), v_ref[...].astype(BF16), preferred_element_type=F32)
        o_ref[...] = o / l
        lse_ref[0] = m + jnp.log(l)

    in_specs = [pl.BlockSpec((tq, HP), lambda h, i: (i, h)), pl.BlockSpec((tp, HP), lambda h, i: (0, h)),
                pl.BlockSpec((tp, HP), lambda h, i: (0, vcb + h))]
    ins = [q, k, v]
    if has_bias:
        in_specs += [pl.BlockSpec((1, tq, 1), lambda h, i: (h, i, 0)), pl.BlockSpec((1, 1, tp), lambda h, i: (h, 0, 0))]
        ins += list(bias)
    return pl.pallas_call(
        body, name=name, grid=(nh, tp // tq), in_specs=in_specs,
        out_specs=[pl.BlockSpec((tq, HP), lambda h, i: (i, h)), pl.BlockSpec((1, tq, 1), lambda h, i: (h, i, 0))],
        out_shape=[jax.ShapeDtypeStruct((tp, nh * HP), F32), jax.ShapeDtypeStruct((nh, tp, 1), F32)],
        compiler_params=_cparams(("parallel", "parallel")),
    )(*ins)


def _attn_bwd(q, k, v, vcb, o, lse, do, docb, scale, bias, name):
    tp = q.shape[0]
    nh = q.shape[1] // HP
    tq = _pick(tp, (384, 256, 128))
    has_bias = bias is not None

    def body(*refs):
        if has_bias:
            (q_ref, k_ref, v_ref, o_ref, lse_ref, do_ref, bq_ref, bk_ref,
             dq_ref, dk_ref, dv_ref, dbq_ref, dbk_ref) = refs
        else:
            q_ref, k_ref, v_ref, o_ref, lse_ref, do_ref, dq_ref, dk_ref, dv_ref = refs
        i = pl.program_id(1)

        @pl.when(i == 0)
        def _():
            dk_ref[...] = jnp.zeros_like(dk_ref)
            dv_ref[...] = jnp.zeros_like(dv_ref)
            if has_bias:
                dbk_ref[...] = jnp.zeros_like(dbk_ref)

        qb, kb, vb = q_ref[...].astype(BF16), k_ref[...].astype(BF16), v_ref[...].astype(BF16)
        do_ = do_ref[...]
        dob = do_.astype(BF16)
        s = lax.dot_general(qb, kb, (((1,), (1,)), ((), ())), preferred_element_type=F32) * scale
        if has_bias:
            s = s + (bq_ref[0] - bk_ref[0])
        s = jnp.where(_attn_mask(i, tq, tp), s, NEG)
        p = jnp.exp(s - lse_ref[0])
        dp = lax.dot_general(dob, vb, (((1,), (1,)), ((), ())), preferred_element_type=F32)
        delta = jnp.sum(do_ * o_ref[...], axis=1, keepdims=True)
        ds = p * (dp - delta)
        dsb = ds.astype(BF16)
        dq_ref[...] = jnp.dot(dsb, kb, preferred_element_type=F32) * scale
        dk_ref[...] += lax.dot_general(dsb, qb, (((0,), (0,)), ((), ())), preferred_element_type=F32) * scale
        dv_ref[...] += lax.dot_general(p.astype(BF16), dob, (((0,), (0,)), ((), ())), preferred_element_type=F32)
        if has_bias:
            dbq_ref[0] = jnp.sum(ds, axis=1, keepdims=True)
            dbk_ref[0] -= jnp.sum(ds, axis=0, keepdims=True)

    blk_q = pl.BlockSpec((tq, HP), lambda h, i: (i, h))
    blk_k = pl.BlockSpec((tp, HP), lambda h, i: (0, h))
    in_specs = [blk_q, blk_k, pl.BlockSpec((tp, HP), lambda h, i: (0, vcb + h)), blk_q,
                pl.BlockSpec((1, tq, 1), lambda h, i: (h, i, 0)), pl.BlockSpec((tq, HP), lambda h, i: (i, docb + h))]
    ins = [q, k, v, o, lse, do]
    out_specs = [blk_q, blk_k, blk_k]
    out_shape = [jax.ShapeDtypeStruct((tp, nh * HP), F32)] * 3
    if has_bias:
        in_specs += [pl.BlockSpec((1, tq, 1), lambda h, i: (h, i, 0)), pl.BlockSpec((1, 1, tp), lambda h, i: (h, 0, 0))]
        ins += list(bias)
        out_specs += [pl.BlockSpec((1, tq, 1), lambda h, i: (h, i, 0)), pl.BlockSpec((1, 1, tp), lambda h, i: (h, 0, 0))]
        out_shape += [jax.ShapeDtypeStruct((nh, tp, 1), F32), jax.ShapeDtypeStruct((nh, 1, tp), F32)]
    return pl.pallas_call(
        body, name=name, grid=(nh, tp // tq), in_specs=in_specs, out_specs=out_specs, out_shape=out_shape,
        compiler_params=_cparams(("parallel", "arbitrary")),
    )(*ins)


def _loss_head(h, tgt):
    tp, d = h.shape
    tm = 128
    first = (PAD + N_META) // tm

    def body(h_ref, t_ref, l_ref, d_ref):
        i = pl.program_id(0)

        @pl.when(i == 0)
        def _():
            l_ref[...] = jnp.zeros_like(l_ref)

        live = i >= first
        err = jnp.where(live, h_ref[...] - t_ref[...], 0.0)
        d_ref[...] = err * (1.0 / d)
        l_ref[...] += (0.5 / d) * jnp.sum(err * err)

    return pl.pallas_call(
        body, name="loss_head", grid=(tp // tm,),
        in_specs=[pl.BlockSpec((tm, d), lambda i: (i, 0)), pl.BlockSpec((tm, d), lambda i: (i, 0))],
        out_specs=[_full_spec((8, 128)), pl.BlockSpec((tm, d), lambda i: (i, 0))],
        out_shape=[jax.ShapeDtypeStruct((8, 128), F32), jax.ShapeDtypeStruct((tp, d), F32)],
        compiler_params=_cparams(("arbitrary",)),
    )(h, tgt)


def _rope_tables(pos_rows):
    half = C_ROPE // 2
    freqs = ROPE_THETA ** (-jnp.arange(half, dtype=F32) / half)
    ang = pos_rows[:, None].astype(F32) * freqs
    cos, sin = jnp.cos(ang), jnp.sin(ang)
    tp = pos_rows.shape[0]
    one, zero = jnp.ones((tp, C_NOPE), F32), jnp.zeros((tp, C_NOPE), F32)
    tail1, tail0 = jnp.ones((tp, HP - C_NOPE - C_ROPE), F32), jnp.zeros((tp, HP - C_NOPE - C_ROPE), F32)
    return (jnp.concatenate([one, cos, cos, tail1], axis=1), jnp.concatenate([zero, sin, sin, tail0], axis=1))


def _heads_to_cols(x, lo, n):
    t = x[:, lo:lo + n].T
    return t[:, :, None], t[:, None, :]


def _local_step(x, positions, tgt, w):
    s_len = x.shape[0]
    tp = PAD + N_META + s_len
    tm = _pick(tp, (384, 256, 128))
    front = PAD + N_META
    h0 = jnp.concatenate([jnp.zeros((PAD, D_MODEL), F32), w["meta"], x], axis=0)
    tgt_p = jnp.concatenate([jnp.zeros((front, D_MODEL), F32), tgt], axis=0)
    pos_rows = jnp.concatenate([jnp.zeros((PAD,), jnp.int32), jnp.arange(N_META, dtype=jnp.int32),
                                positions + N_META])
    cos, sin = _rope_tables(pos_rows)
    g = {}

    r_h0 = [(h0, D_MODEL, 0)]
    (xn0,) = _row_fwd("norm0_f", _f_norm, r_h0, [w["ev_ln"]], [(D_MODEL, BF16)], tm)
    z0 = _mm(xn0, w["ev_w_in"], "nn", "mm_z0")
    r_misc0 = [(z0, HP, 5120 // HP)]
    (g0,) = _row_fwd("gate0_f", _f_gate0, r_misc0, [w["ev_b_if"]], [(HP, F32)], tm)
    li, _ = _heads_to_cols(g0, 0, A_HEADS)
    lf, _ = _heads_to_cols(g0, A_HEADS, A_HEADS)
    h_a, cs, ns, ms = _mlstm_fwd(z0, li, lf)
    r_aout = [(h_a, 1024, 0), (z0, 1024, 2)]
    (ha,) = _row_fwd("aout_f", _f_aout, r_aout, [w["ev_a_norm"]], [(1024, BF16)], tm)
    xc = _conv_fwd(z0, 3072 // 256, w["ev_conv_w"], w["ev_conv_b"])
    p_gates = [w["ev_w_ra"], w["ev_b_ra"], w["ev_w_rx"], w["ev_b_rx"], w["ev_lam"]]
    a_g, u_g = _row_fwd("gates_f", _f_gates, [(xc, 1024, 0)], p_gates, [(1024, F32), (1024, F32)], tm)
    hs, hprev = _scan_fwd(a_g, u_g, "lru_f")
    r_bout = [(hs, 1024, 0), (z0, 1024, 4)]
    (hb,) = _row_fwd("bout_f", _f_bout, r_bout, [], [(1024, BF16)], tm)
    hab = jnp.concatenate([ha, hb], axis=1)
    h1 = _mm(hab, w["ev_w_out"], "nn", "mm_h1", add=h0)
    (xn1,) = _row_fwd("norm1_f", _f_norm, [(h1, D_MODEL, 0)], [w["mlp_ln0"]], [(D_MODEL, BF16)], tm)
    p0 = _mm(xn1, w["w_ff1_0"], "nn", "mm_p0")
    (act0,) = _row_fwd("relu0_f", _f_relu2, [(p0, 4096, 0)], [], [(4096, BF16)], tm)
    h2 = _mm(act0, w["w_ff2_0"], "nn", "mm_h2", add=h1)
    (xn2,) = _row_fwd("norm2_f", _f_norm, [(h2, D_MODEL, 0)], [w["od_ln"]], [(D_MODEL, BF16)], tm)
    z1 = _mm(xn2, w["od_w_in"], "nn", "mm_z1")
    r_c = [(z1, C_Q_LORA, 3072 // C_Q_LORA), (z1, C_KV_LORA, 3584 // C_KV_LORA)]
    cqn, ckvn = _row_fwd("cnorm_f", _f_cnorm, r_c, [w["od_g_qa"], w["od_g_kva"]],
                         [(C_Q_LORA, BF16), (C_KV_LORA, BF16)], tm)
    q_ = _mm(cqn, w["od_w_uq"], "nn", "mm_q")
    kv_ = _mm(ckvn, w["od_w_ukv"], "nn", "mm_kv")
    r_mla = [(q_, 1024, 0), (kv_, 1024, 0), (z1, HP, 3840 // HP), (cos, HP, 0), (sin, HP, 0)]
    p_mla = [w["gq_full"], w["gk_full"]]
    qm, km = _row_fwd("mla_f", _f_mlaprep, r_mla, p_mla, [(1024, BF16), (1024, BF16)], tm)
    sc_c = (C_NOPE + C_ROPE) ** -0.5
    hc, lse_c = _attn_fwd(qm, km, kv_, C_HEADS, sc_c, None, "mla_attn_f")
    r_fox = [(z1, 1024, 0), (z1, 1024, 1), (z1, HP, 3840 // HP)]
    p_fox = [w["gfq_full"], w["gfk_full"], w["bf_full"]]
    qf, kf, lfx = _row_fwd("fox_f", _f_foxprep, r_fox, p_fox, [(1024, BF16), (1024, BF16), (HP, F32)], tm)
    ones = jnp.ones((tp, HP), F32)
    fcum, fprev = _scan_fwd(ones, lfx, "fcum_f")
    bias = _heads_to_cols(fcum, 96, D_HEADS)
    sc_d = D_HD ** -0.5
    hd, lse_d = _attn_fwd(qf, kf, z1, 2048 // HP, sc_d, bias, "fox_attn_f")
    hcd = jnp.concatenate([hc, hd], axis=1)
    h3 = _mm(hcd, w["od_w_out"], "nn", "mm_h3", add=h2)
    (xn3,) = _row_fwd("norm3_f", _f_norm, [(h3, D_MODEL, 0)], [w["mlp_ln1"]], [(D_MODEL, BF16)], tm)
    p1 = _mm(xn3, w["w_ff1_1"], "nn", "mm_p1")
    (act1,) = _row_fwd("relu1_f", _f_relu2, [(p1, 4096, 0)], [], [(4096, BF16)], tm)
    h4 = _mm(act1, w["w_ff2_1"], "nn", "mm_h4", add=h3)
    lpart, dh4 = _loss_head(h4, tgt_p)
    loss = lpart[0, 0]

    def mlp_bwd(tag, dh_out, h_in, xn, p, act, ln, w1, w2):
        dact = _mm(dh_out, w2, "nt", f"mm_dact{tag}")
        g_w2 = _mm(act, dh_out, "tn", f"mm_dw2_{tag}")
        (dp,), _ = _row_bwd(f"relu{tag}_b", _f_relu2, [(p, 4096, 0)], [], [(dact, 4096, 0)], tm, [True])
        g_w1 = _mm(xn, dp, "tn", f"mm_dw1_{tag}")
        dxn = _mm(dp, w1, "nt", f"mm_dxn{tag}")
        (dh_in,), (g_ln,) = _row_bwd(f"normm{tag}_b", _f_norm, [(h_in, D_MODEL, 0)], [ln], [(dxn, D_MODEL, 0)], tm,
                                     [True], add=dh_out)
        return dh_in, g_ln, g_w1, g_w2

    dh3, g["mlp_ln1"], g["w_ff1_1"], g["w_ff2_1"] = mlp_bwd("1", dh4, h3, xn3, p1, act1, w["mlp_ln1"],
                                                            w["w_ff1_1"], w["w_ff2_1"])
    dhcd = _mm(dh3, w["od_w_out"], "nt", "mm_dhcd")
    g["od_w_out"] = _mm(hcd, dh3, "tn", "mm_dwout1")
    dqf, dkf, dvf, dbq, dbk = _attn_bwd(qf, kf, z1, 2048 // HP, hd, lse_d, dhcd, D_HEADS, sc_d, bias, "fox_attn_b")
    dfc = dbq[:, :, 0].T + dbk[:, 0, :].T
    dfcum = jnp.concatenate([jnp.zeros((tp, 96), F32), dfc, jnp.zeros((tp, HP - 96 - D_HEADS), F32)], axis=1)
    dlfx, _ = _scan_bwd(ones, fprev, dfcum, "fcum_b")
    (dfq, dfk, dmisc_f), (g["gfq_full"], g["gfk_full"], g["bf_full"]) = _row_bwd(
        "fox_b", _f_foxprep, r_fox, p_fox, [(dqf, 1024, 0), (dkf, 1024, 0), (dlfx, HP, 0)], tm, [True, True, True])
    dqm, dkm, dvm = _attn_bwd(qm, km, kv_, C_HEADS, hc, lse_c, dhcd, 0, sc_c, None, "mla_attn_b")
    (dq_, dkk_, dmisc_m), (g["gq_full"], g["gk_full"]) = _row_bwd(
        "mla_b", _f_mlaprep, r_mla, p_mla, [(dqm, 1024, 0), (dkm, 1024, 0)], tm, [True, True, True, False, False])
    dkv_ = jnp.concatenate([dkk_, dvm], axis=1)
    dckvn = _mm(dkv_, w["od_w_ukv"], "nt", "mm_dckvn")
    g["od_w_ukv"] = _mm(ckvn, dkv_, "tn", "mm_dwukv")
    dcqn = _mm(dq_, w["od_w_uq"], "nt", "mm_dcqn")
    g["od_w_uq"] = _mm(cqn, dq_, "tn", "mm_dwuq")
    (dcq, dckv), (g["od_g_qa"], g["od_g_kva"]) = _row_bwd(
        "cnorm_b", _f_cnorm, r_c, [w["od_g_qa"], w["od_g_kva"]],
        [(dcqn, C_Q_LORA, 0), (dckvn, C_KV_LORA, 0)], tm, [True, True])
    zpad = jnp.zeros((tp, HP), F32)
    dz1 = jnp.concatenate([dfq, dfk, dvf, dcq, zpad, dckv, dmisc_f + dmisc_m, zpad], axis=1)
    g["od_w_in"] = _mm(xn2, dz1, "tn", "mm_dwin1")
    dxn2 = _mm(dz1, w["od_w_in"], "nt", "mm_dxn2")
    (dh2,), (g["od_ln"],) = _row_bwd("norm2_b", _f_norm, [(h2, D_MODEL, 0)], [w["od_ln"]], [(dxn2, D_MODEL, 0)], tm,
                                     [True], add=dh3)
    dh1, g["mlp_ln0"], g["w_ff1_0"], g["w_ff2_0"] = mlp_bwd("0", dh2, h1, xn1, p0, act0, w["mlp_ln0"],
                                                            w["w_ff1_0"], w["w_ff2_0"])
    dhab = _mm(dh1, w["ev_w_out"], "nt", "mm_dhab")
    g["ev_w_out"] = _mm(hab, dh1, "tn", "mm_dwout0")
    (dhs, dgb), _ = _row_bwd("bout_b", _f_bout, r_bout, [], [(dhab, 1024, 1)], tm, [True, True])
    du_g, da_g = _scan_bwd(a_g, hprev, dhs, "lru_b")
    (dxc,), (g["ev_w_ra"], g["ev_b_ra"], g["ev_w_rx"], g["ev_b_rx"], g["ev_lam"]) = _row_bwd(
        "gates_b", _f_gates, [(xc, 1024, 0)], p_gates, [(da_g, 1024, 0), (du_g, 1024, 0)], tm, [True])
    dxb, g["ev_conv_w"], g["ev_conv_b"] = _conv_bwd(z0, 3072 // 256, w["ev_conv_w"], dxc)
    (dh_a, do_), (g["ev_a_norm"],) = _row_bwd("aout_b", _f_aout, r_aout, [w["ev_a_norm"]], [(dhab, 1024, 0)], tm,
                                              [True, True])
    dq, dk, dv, dli, dlf = _mlstm_bwd(z0, li, lf, cs, ns, ms, dh_a)
    dg0 = jnp.concatenate([dli[:, :, 0].T, dlf[:, :, 0].T, jnp.zeros((tp, HP - 2 * A_HEADS), F32)], axis=1)
    (dmisc0,), (g["ev_b_if"],) = _row_bwd("gate0_b", _f_gate0, r_misc0, [w["ev_b_if"]], [(dg0, HP, 0)], tm, [True])
    dz0 = jnp.concatenate([dq, dk, dv, do_, dxb, dgb, dmisc0, zpad], axis=1)
    g["ev_w_in"] = _mm(xn0, dz0, "tn", "mm_dwin0")
    dxn0 = _mm(dz0, w["ev_w_in"], "nt", "mm_dxn0")
    (dh0,), (g["ev_ln"],) = _row_bwd("norm0_b", _f_norm, r_h0, [w["ev_ln"]], [(dxn0, D_MODEL, 0)], tm, [True], add=dh1)
    g["meta"] = dh0[PAD:front]
    return loss, dh0[front:], g


def _pad_last(a, n):
    return jnp.pad(a, [(0, 0)] * (a.ndim - 1) + [(0, n - a.shape[-1])])


def _pad_heads(a, nh, d):
    return _pad_last(a.reshape(a.shape[:-1] + (nh, d)), HP).reshape(a.shape[:-1] + (nh * HP,))


def _unpad_heads(a, nh, d):
    return a.reshape(a.shape[:-1] + (nh, HP))[..., :d].reshape(a.shape[:-1] + (nh * d,))


def _prep_weights(p):
    w = {}
    w["meta"] = p["meta"]
    for k in ("ev_ln", "ev_a_norm", "ev_conv_b", "ev_b_ra", "ev_b_rx", "ev_lam", "od_ln", "od_g_qa", "od_g_kva"):
        w[k] = p[k].reshape(1, -1)
    wi = p["ev_w_in"][0]
    w["ev_w_in"] = _pad_last(jnp.concatenate([wi[:, :3072], wi[:, 3080:5128], wi[:, 3072:3080]], axis=1), ZE)
    w["ev_b_if"] = _pad_last(p["ev_b_if"].reshape(1, -1), HP)
    w["ev_conv_w"] = p["ev_conv_w"][0]
    w["ev_w_ra"] = p["ev_w_ra"][0]
    w["ev_w_rx"] = p["ev_w_rx"][0]
    w["ev_w_out"] = p["ev_w_out"][0]
    wo = p["od_w_in"][0]
    z = lambda n: jnp.zeros((wo.shape[0], n), wo.dtype)
    w["od_w_in"] = jnp.concatenate(
        [_pad_heads(wo[:, 672:1184], D_HEADS, D_HD), _pad_heads(wo[:, 1184:1696], D_HEADS, D_HD),
         _pad_heads(wo[:, 1696:2208], D_HEADS, D_HD), wo[:, 0:384], z(128), wo[:, 384:640],
         z(64), wo[:, 640:672], wo[:, 2208:2216], z(24), z(128)], axis=1)
    w["od_w_uq"] = _pad_heads(p["od_w_uq"][0], C_HEADS, C_NOPE + C_ROPE)
    wkv = p["od_w_ukv"][0].reshape(C_KV_LORA, C_HEADS, C_NOPE + C_V)
    w["od_w_ukv"] = jnp.concatenate([_pad_last(wkv[:, :, :C_NOPE], HP).reshape(C_KV_LORA, -1),
                                     _pad_last(wkv[:, :, C_NOPE:], HP).reshape(C_KV_LORA, -1)], axis=1)
    f1 = lambda a: a.reshape(1, -1)
    w["gq_full"] = _pad_last(jnp.concatenate([f1(p["od_g_qn"]), f1(p["od_g_qr"])], axis=1), HP)
    w["gk_full"] = _pad_last(jnp.concatenate([f1(p["od_g_kn"]), f1(p["od_g_kr"])], axis=1), HP)
    w["gfq_full"] = _pad_last(f1(p["od_g_fq"]), HP)
    w["gfk_full"] = _pad_last(f1(p["od_g_fk"]), HP)
    w["bf_full"] = _pad_last(jnp.concatenate([jnp.zeros((1, 96), F32), f1(p["od_b_f"])], axis=1), HP)
    wout = p["od_w_out"][0]
    w["od_w_out"] = jnp.pad(wout.reshape(2 * C_HEADS, C_V, D_MODEL), ((0, 0), (0, HP - C_V), (0, 0))).reshape(-1, D_MODEL)
    for l in (0, 1):
        w[f"mlp_ln{l}"] = p["mlp_ln"][l:l + 1]
        w[f"w_ff1_{l}"] = p["w_ff1"][l]
        w[f"w_ff2_{l}"] = p["w_ff2"][l]
    return w


def _unprep_grads(g):
    o = {}
    o["meta"] = g["meta"]
    for k in ("ev_ln", "ev_a_norm", "ev_conv_b", "ev_b_ra", "ev_b_rx", "ev_lam", "od_ln", "od_g_qa", "od_g_kva"):
        o[k] = g[k].reshape(1, -1)
    gi = g["ev_w_in"]
    o["ev_w_in"] = jnp.concatenate([gi[:, :3072], gi[:, 5120:5128], gi[:, 3072:5120]], axis=1)[None]
    o["ev_b_if"] = g["ev_b_if"][:, :2 * A_HEADS]
    o["ev_conv_w"] = g["ev_conv_w"][None]
    o["ev_w_ra"] = g["ev_w_ra"][None]
    o["ev_w_rx"] = g["ev_w_rx"][None]
    o["ev_w_out"] = g["ev_w_out"][None]
    go = g["od_w_in"]
    o["od_w_in"] = jnp.concatenate(
        [go[:, 3072:3456], go[:, 3584:3840], go[:, 3904:3936], _unpad_heads(go[:, 0:1024], D_HEADS, D_HD),
         _unpad_heads(go[:, 1024:2048], D_HEADS, D_HD), _unpad_heads(go[:, 2048:3072], D_HEADS, D_HD),
         go[:, 3936:3944]], axis=1)[None]
    o["od_w_uq"] = _unpad_heads(g["od_w_uq"], C_HEADS, C_NOPE + C_ROPE)[None]
    gkv = g["od_w_ukv"]
    gk = gkv[:, :C_HEADS * HP].reshape(C_KV_LORA, C_HEADS, HP)[:, :, :C_NOPE]
    gv = gkv[:, C_HEADS * HP:].reshape(C_KV_LORA, C_HEADS, HP)[:, :, :C_V]
    o["od_w_ukv"] = jnp.concatenate([gk, gv], axis=2).reshape(1, C_KV_LORA, -1)
    o["od_g_qn"] = g["gq_full"][:, :C_NOPE]
    o["od_g_qr"] = g["gq_full"][:, C_NOPE:C_NOPE + C_ROPE]
    o["od_g_kn"] = g["gk_full"][:, :C_NOPE]
    o["od_g_kr"] = g["gk_full"][:, C_NOPE:C_NOPE + C_ROPE]
    o["od_g_fq"] = g["gfq_full"][:, :D_HD]
    o["od_g_fk"] = g["gfk_full"][:, :D_HD]
    o["od_b_f"] = g["bf_full"][:, 96:96 + D_HEADS]
    o["od_w_out"] = g["od_w_out"].reshape(2 * C_HEADS, HP, D_MODEL)[:, :C_V].reshape(1, -1, D_MODEL)
    o["mlp_ln"] = jnp.concatenate([g["mlp_ln0"], g["mlp_ln1"]], axis=0)
    o["w_ff1"] = jnp.stack([g["w_ff1_0"], g["w_ff1_1"]])
    o["w_ff2"] = jnp.stack([g["w_ff2_0"], g["w_ff2_1"]])
    return o


def _me_and_peers():
    mx, my, mc = lax.axis_index("x"), lax.axis_index("y"), lax.axis_index("c")
    peers = []
    for k in range(1, N_DEV):
        px, py, pc = mx ^ ((k >> 2) & 1), my ^ ((k >> 1) & 1), mc ^ (k & 1)
        peers.append(((px, py, pc), 4 * px + 2 * py + pc))
    return 4 * mx + 2 * my + mc, peers


def _exchange(x, name, scatter):
    out_shape = (N_DEV,) + x.shape[-2:]

    def body(x_ref, o_ref, send_sems, recv_sems, local_sem):
        me, peers = _me_and_peers()
        src_me = x_ref.at[me] if scatter else x_ref
        local = pltpu.make_async_copy(src_me, o_ref.at[me], local_sem)
        local.start()
        copies = []
        for k, (peer, pid) in enumerate(peers):
            cp = pltpu.make_async_remote_copy(
                src_ref=x_ref.at[pid] if scatter else x_ref, dst_ref=o_ref.at[me],
                send_sem=send_sems.at[k], recv_sem=recv_sems.at[k],
                device_id=peer, device_id_type=pl.DeviceIdType.MESH)
            cp.start()
            copies.append(cp)
        for cp in copies:
            cp.wait_send()
        for cp in copies:
            cp.wait_recv()
        local.wait()

    return pl.pallas_call(
        body, name=name, out_shape=jax.ShapeDtypeStruct(out_shape, x.dtype),
        in_specs=[pl.BlockSpec(memory_space=pl.ANY)], out_specs=pl.BlockSpec(memory_space=pl.ANY),
        scratch_shapes=[pltpu.SemaphoreType.DMA((N_DEV - 1,)), pltpu.SemaphoreType.DMA((N_DEV - 1,)),
                        pltpu.SemaphoreType.DMA(())],
    )(x)


def _adamw(parts, w, m, v, name):
    r = w.shape[0]
    tr = _pick(r, (128, 64, 32, 16, 8))
    c1 = 1.0 / (1.0 - ADAM_B1 ** ADAM_STEP)
    c2 = 1.0 / (1.0 - ADAM_B2 ** ADAM_STEP)

    def body(p_ref, w_ref, m_ref, v_ref, g_ref, d_ref, mo_ref, vo_ref):
        g = p_ref[0]
        for j in range(1, N_DEV):
            g = g + p_ref[j]
        m2 = ADAM_B1 * m_ref[...] + (1.0 - ADAM_B1) * g
        v2 = ADAM_B2 * v_ref[...] + (1.0 - ADAM_B2) * (g * g)
        g_ref[...] = g
        mo_ref[...] = m2
        vo_ref[...] = v2
        d_ref[...] = -ADAM_LR * ((m2 * c1) / (jnp.sqrt(v2 * c2) + ADAM_EPS) + ADAM_WD * w_ref[...])

    spec = pl.BlockSpec((tr, 1024), lambda i: (i, 0))
    return pl.pallas_call(
        body, name=name, grid=(r // tr,),
        in_specs=[pl.BlockSpec((N_DEV, tr, 1024), lambda i: (0, i, 0)), spec, spec, spec],
        out_specs=[spec] * 4, out_shape=[jax.ShapeDtypeStruct((r, 1024), F32)] * 4,
        compiler_params=_cparams(("parallel",)),
    )(parts, w, m, v)


def _rows_for(n, mult):
    return -(-n // (1024 * mult)) * mult


def _pack(arrs, mult, lead=()):
    nl = len(lead)
    flat = jnp.concatenate([a.reshape(lead + (-1,)) for a in arrs], axis=nl)
    rows = _rows_for(flat.shape[-1], mult)
    return jnp.pad(flat, [(0, 0)] * nl + [(0, rows * 1024 - flat.shape[-1])]).reshape(lead + (rows, 1024))


def _unpack(buf, shapes):
    lead = buf.shape[:-2]
    flat = buf.reshape(lead + (-1,))
    out, off = [], 0
    for s in shapes:
        n = 1
        for d_ in s:
            n *= d_
        out.append(flat[..., off:off + n].reshape(lead + tuple(s)))
        off += n
    return out


def _unshard(g8, ax):
    a = jnp.moveaxis(g8, 0, ax)
    return a.reshape(a.shape[:ax] + (N_DEV * a.shape[ax + 1],) + a.shape[ax + 2:])


def _shard8(full, ax):
    s = full.shape
    return jnp.moveaxis(full.reshape(s[:ax] + (N_DEV, s[ax] // N_DEV) + s[ax + 1:]), ax, 0)


_NAMES = ["meta", "ev_ln", "ev_w_in", "ev_b_if", "ev_a_norm", "ev_conv_w", "ev_conv_b", "ev_w_ra", "ev_b_ra",
          "ev_w_rx", "ev_b_rx", "ev_lam", "ev_w_out", "od_ln", "od_w_in", "od_b_f", "od_g_qa", "od_g_kva",
          "od_w_uq", "od_w_ukv", "od_g_qn", "od_g_qr", "od_g_kn", "od_g_kr", "od_g_fq", "od_g_fk", "od_w_out",
          "mlp_ln", "w_ff1", "w_ff2"]
_SHARD_AXIS = {"meta": 1, "ev_w_in": 2, "ev_conv_w": 2, "ev_w_out": 1, "od_ln": 1, "od_w_in": 2, "od_g_qa": 1,
               "od_g_kva": 1, "od_w_uq": 2, "od_w_ukv": 2, "od_w_out": 1, "w_ff1": 2, "w_ff2": 1}
_MATMUL_WEIGHTS = ("ev_w_in", "ev_w_out", "od_w_in", "od_w_uq", "od_w_ukv", "od_w_out", "w_ff1", "w_ff2")


def kernel(x, positions, meta, ev_ln, ev_w_in, ev_b_if, ev_a_norm, ev_conv_w, ev_conv_b, ev_w_ra, ev_b_ra, ev_w_rx, ev_b_rx, ev_lam, ev_w_out, od_ln, od_w_in, od_b_f, od_g_qa, od_g_kva, od_w_uq, od_w_ukv, od_g_qn, od_g_qr, od_g_kn, od_g_kr, od_g_fq, od_g_fk, od_w_out, mlp_ln, w_ff1, w_ff2, loss_target, m_meta, m_ev_ln, m_ev_w_in, m_ev_b_if, m_ev_a_norm, m_ev_conv_w, m_ev_conv_b, m_ev_w_ra, m_ev_b_ra, m_ev_w_rx, m_ev_b_rx, m_ev_lam, m_ev_w_out, m_od_ln, m_od_w_in, m_od_b_f, m_od_g_qa, m_od_g_kva, m_od_w_uq, m_od_w_ukv, m_od_g_qn, m_od_g_qr, m_od_g_kn, m_od_g_kr, m_od_g_fq, m_od_g_fk, m_od_w_out, m_mlp_ln, m_w_ff1, m_w_ff2, v_meta, v_ev_ln, v_ev_w_in, v_ev_b_if, v_ev_a_norm, v_ev_conv_w, v_ev_conv_b, v_ev_w_ra, v_ev_b_ra, v_ev_w_rx, v_ev_b_rx, v_ev_lam, v_ev_w_out, v_od_ln, v_od_w_in, v_od_b_f, v_od_g_qa, v_od_g_kva, v_od_w_uq, v_od_w_ukv, v_od_g_qn, v_od_g_qr, v_od_g_kn, v_od_g_kr, v_od_g_fq, v_od_g_fk, v_od_w_out, v_mlp_ln, v_w_ff1, v_w_ff2):
    given = dict(locals())
    wts = {n: given[n] for n in _NAMES}
    mom = {n: given["m_" + n] for n in _NAMES}
    var = {n: given["v_" + n] for n in _NAMES}
    sharded = [n for n in _NAMES if n in _SHARD_AXIS]
    repl = [n for n in _NAMES if n not in _SHARD_AXIS]
    big = [n for n in sharded if n in _MATMUL_WEIGHTS]
    small = [n for n in sharded if n not in _MATMUL_WEIGHTS]

    full = {n: wts[n] for n in repl}
    got = _exchange(_pack([wts[n].astype(BF16) for n in big], 16), "gather_matmul_weights", False)
    for n, a in zip(big, _unpack(got, [wts[n].shape for n in big])):
        full[n] = _unshard(a, _SHARD_AXIS[n])
    got = _exchange(_pack([wts[n] for n in small], 8), "gather_small_weights", False)
    for n, a in zip(small, _unpack(got, [wts[n].shape for n in small])):
        full[n] = _unshard(a, _SHARD_AXIS[n])

    loss, gx, g = _local_step(x[0], positions[0], loss_target[0], _prep_weights(full))
    grads = _unprep_grads(g)

    parts = _exchange(_pack([_shard8(grads[n], _SHARD_AXIS[n]) for n in sharded], 128, (N_DEV,)),
                      "scatter_grads", True)
    res_s = _adamw(parts, _pack([wts[n] for n in sharded], 128), _pack([mom[n] for n in sharded], 128),
                   _pack([var[n] for n in sharded], 128), "adamw_sharded")
    parts = _exchange(_pack([grads[n].reshape(wts[n].shape) for n in repl], 8), "gather_repl_grads", False)
    res_r = _adamw(parts, _pack([wts[n] for n in repl], 8), _pack([mom[n] for n in repl], 8),
                   _pack([var[n] for n in repl], 8), "adamw_repl")

    outs = []
    for kind in range(4):
        by_name = dict(zip(sharded, _unpack(res_s[kind], [wts[n].shape for n in sharded])))
        by_name.update(zip(repl, _unpack(res_r[kind], [wts[n].shape for n in repl])))
        outs += [by_name[n] for n in _NAMES]
    loss = lax.psum(loss, ("x", "y", "c"))
    return (loss, gx[None], *outs)
```

```python
import functools

import jax
import jax.numpy as jnp
from jax import lax
from jax.experimental import pallas as pl
from jax.experimental.pallas import tpu as pltpu

F32 = jnp.float32
BF16 = jnp.bfloat16

D_MODEL = 1024
N_META = 16
PAD = 112
EPS = 1e-6
NEG = -1e30
A_HEADS, A_DQK, A_DV, A_CHUNK = 4, 128, 256, 64
B_BLOCKS, B_BLOCK, CONV_W, LRU_C = 8, 128, 4, 8.0
C_HEADS, C_Q_LORA, C_KV_LORA, C_NOPE, C_ROPE, C_V = 8, 384, 256, 64, 32, 64
ROPE_THETA = 10000.0
D_HEADS, D_HD = 8, 64
HP = 128
ZE = 5376
ZO = 4096
N_DEV = 8
ADAM_LR, ADAM_B1, ADAM_B2, ADAM_EPS, ADAM_WD, ADAM_STEP = 0.001, 0.9, 0.999, 1e-08, 0.01, 10
VMEM_LIMIT = 56 * 1024 * 1024
SCAN_BLOCK = 128


def _pick(n, prefs):
    for p in prefs:
        if n % p == 0:
            return p
    return n


def _cparams(dims):
    return pltpu.CompilerParams(dimension_semantics=dims, vmem_limit_bytes=VMEM_LIMIT)


def _full_spec(shape):
    nd = len(shape)
    return pl.BlockSpec(shape, lambda *_: (0,) * nd)


def _mm(a, b, mode, name, out_dtype=None, add=None):
    if out_dtype is None:
        out_dtype = BF16 if mode == "tn" else F32
    if mode == "nn":
        (m, k), n = a.shape, b.shape[1]
    elif mode == "nt":
        (m, k), n = a.shape, b.shape[0]
    else:
        (k, m), n = a.shape, b.shape[1]
    tm = _pick(m, (1408, 1024, 768, 512, 384, 256, 128))
    tn = _pick(n, (1024, 768, 512, 384, 256, 128))
    tk = _pick(k, (1408, 1024, 768, 512, 384, 256, 128))
    nk = k // tk
    if mode == "nn":
        a_spec = pl.BlockSpec((tm, tk), lambda i, j, q: (i, q))
        b_spec = pl.BlockSpec((tk, tn), lambda i, j, q: (q, j))
        dn = (((1,), (0,)), ((), ()))
    elif mode == "nt":
        a_spec = pl.BlockSpec((tm, tk), lambda i, j, q: (i, q))
        b_spec = pl.BlockSpec((tn, tk), lambda i, j, q: (j, q))
        dn = (((1,), (1,)), ((), ()))
    else:
        a_spec = pl.BlockSpec((tk, tm), lambda i, j, q: (q, i))
        b_spec = pl.BlockSpec((tk, tn), lambda i, j, q: (q, j))
        dn = (((0,), (0,)), ((), ()))
    o_spec = pl.BlockSpec((tm, tn), lambda i, j, q: (i, j))
    has_add = add is not None

    def body(*refs):
        if has_add:
            a_ref, b_ref, add_ref, o_ref, acc = refs
        else:
            a_ref, b_ref, o_ref, acc = refs
        q = pl.program_id(2)

        @pl.when(q == 0)
        def _():
            acc[...] = jnp.zeros_like(acc)

        acc[...] += lax.dot_general(a_ref[...].astype(BF16), b_ref[...].astype(BF16), dn,
                                    preferred_element_type=F32)

        @pl.when(q == nk - 1)
        def _():
            r = acc[...]
            if has_add:
                r = r + add_ref[...]
            o_ref[...] = r.astype(o_ref.dtype)

    ins = [a, b] + ([add] if has_add else [])
    in_specs = [a_spec, b_spec] + ([o_spec] if has_add else [])
    return pl.pallas_call(
        body, name=name, grid=(m // tm, n // tn, nk), in_specs=in_specs, out_specs=o_spec,
        out_shape=jax.ShapeDtypeStruct((m, n), out_dtype),
        scratch_shapes=[pltpu.VMEM((tm, tn), F32)],
        compiler_params=_cparams(("parallel", "parallel", "arbitrary")),
    )(*ins)


def _row_specs(rows, tm):
    return [pl.BlockSpec((tm, w), functools.partial(lambda cb, i: (i, cb), cb)) for (_, w, cb) in rows]


def _row_fwd(name, f, rows, params, outs, tm):
    tp = rows[0][0].shape[0]
    nr, npar = len(rows), len(params)

    def body(*refs):
        i = pl.program_id(0)
        rv = [r[...] for r in refs[:nr]]
        pv = [r[...] for r in refs[nr:nr + npar]]
        res = f(i, rv, pv)
        for o_ref, r in zip(refs[nr + npar:], res):
            o_ref[...] = r.astype(o_ref.dtype)

    res = pl.pallas_call(
        body, name=name, grid=(tp // tm,),
        in_specs=_row_specs(rows, tm) + [_full_spec(p.shape) for p in params],
        out_specs=[pl.BlockSpec((tm, w), lambda i: (i, 0)) for (w, _) in outs],
        out_shape=[jax.ShapeDtypeStruct((tp, w), dt) for (w, dt) in outs],
        compiler_params=_cparams(("parallel",)),
    )(*[r[0] for r in rows], *params)
    return list(res)


def _row_bwd(name, f, rows, params, douts, tm, diff, add=None):
    tp = rows[0][0].shape[0]
    nr, npar, nd = len(rows), len(params), len(douts)
    didx = [k for k in range(nr) if diff[k]]
    has_add = add is not None

    def body(*refs):
        i = pl.program_id(0)
        rv = [r[...] for r in refs[:nr]]
        pv = [r[...] for r in refs[nr:nr + npar]]
        dv = [r[...] for r in refs[nr + npar:nr + npar + nd]]
        pos = nr + npar + nd
        add_ref = refs[pos] if has_add else None
        pos += 1 if has_add else 0
        dr_refs = refs[pos:pos + len(didx)]
        dp_refs = refs[pos + len(didx):]

        def g(drv, pvs):
            full = list(rv)
            for k, val in zip(didx, drv):
                full[k] = val
            return tuple(f(i, full, list(pvs)))

        _, vjp = jax.vjp(g, [rv[k] for k in didx], pv)
        d_r, d_p = vjp(tuple(dv))
        for n_, (ref, val) in enumerate(zip(dr_refs, d_r)):
            if has_add and n_ == 0:
                val = val + add_ref[...]
            ref[...] = val

        @pl.when(i == 0)
        def _():
            for ref in dp_refs:
                ref[...] = jnp.zeros_like(ref)

        for ref, val in zip(dp_refs, d_p):
            ref[...] += val

    in_specs = (_row_specs(rows, tm) + [_full_spec(p.shape) for p in params] + _row_specs(douts, tm))
    ins = [r[0] for r in rows] + list(params) + [d[0] for d in douts]
    if has_add:
        in_specs.append(pl.BlockSpec((tm, rows[didx[0]][1]), lambda i: (i, 0)))
        ins.append(add)
    out_specs = ([pl.BlockSpec((tm, rows[k][1]), lambda i: (i, 0)) for k in didx]
                 + [_full_spec(p.shape) for p in params])
    out_shape = ([jax.ShapeDtypeStruct((tp, rows[k][1]), F32) for k in didx]
                 + [jax.ShapeDtypeStruct(p.shape, F32) for p in params])
    res = pl.pallas_call(
        body, name=name, grid=(tp // tm,), in_specs=in_specs, out_specs=out_specs, out_shape=out_shape,
        compiler_params=_cparams(("arbitrary",)),
    )(*ins)
    return list(res[:len(didx)]), list(res[len(didx):])


def _rowmask(i, tm):
    return (i * tm + lax.broadcasted_iota(jnp.int32, (tm, 1), 0)) >= PAD


def _lane(n=HP):
    return lax.broadcasted_iota(jnp.int32, (1, n), 1)


def _softplus(x):
    return jnp.maximum(x, 0.0) + jnp.log(1.0 + jnp.exp(-jnp.abs(x)))


def _log_sigmoid(x):
    return -_softplus(-x)


def _sigmoid(x):
    return 1.0 / (1.0 + jnp.exp(-x))


@functools.partial(jax.custom_vjp, nondiff_argnums=(1,))
def _lroll(x, s):
    return pltpu.roll(x, s % HP, 1)


def _lroll_fwd(x, s):
    return _lroll(x, s), None


def _lroll_bwd(s, _, g):
    return (pltpu.roll(g, (-s) % HP, 1),)


_lroll.defvjp(_lroll_fwd, _lroll_bwd)


def _f_norm(i, rv, pv):
    (h,), (g,) = rv, pv
    return [h * lax.rsqrt(jnp.mean(h * h, axis=-1, keepdims=True) + EPS) * g]


def _f_gate0(i, rv, pv):
    (misc,), (b,) = rv, pv
    tm = misc.shape[0]
    x = misc + b
    lane, ok = _lane(), _rowmask(i, tm)
    li = jnp.where(ok, x, NEG)
    lf = jnp.where(ok, _log_sigmoid(x), 0.0)
    return [jnp.where(lane < A_HEADS, li, jnp.where(lane < 2 * A_HEADS, lf, 0.0))]


def _f_aout(i, rv, pv):
    (ha, o), (g,) = rv, pv
    outs = []
    for h in range(A_HEADS):
        x = ha[:, h * A_DV:(h + 1) * A_DV]
        outs.append(x * lax.rsqrt(jnp.mean(x * x, axis=-1, keepdims=True) + EPS) * g)
    return [jnp.concatenate(outs, axis=1) * _sigmoid(o)]


def _f_gates(i, rv, pv):
    (xc,), (w_ra, b_ra, w_rx, b_rx, lam) = rv, pv
    tm = xc.shape[0]
    ra, rx = [], []
    for g in range(B_BLOCKS):
        xg = xc[:, g * B_BLOCK:(g + 1) * B_BLOCK].astype(BF16)
        ra.append(jnp.dot(xg, w_ra[g].astype(BF16), preferred_element_type=F32))
        rx.append(jnp.dot(xg, w_rx[g].astype(BF16), preferred_element_type=F32))
    r = _sigmoid(jnp.concatenate(ra, axis=1) + b_ra)
    ig = _sigmoid(jnp.concatenate(rx, axis=1) + b_rx)
    log_a = -LRU_C * r * _softplus(-lam)
    a = jnp.exp(log_a)
    u = jnp.sqrt(1.0 - jnp.exp(2.0 * log_a)) * (ig * xc)
    return [a, jnp.where(_rowmask(i, tm), u, 0.0)]


def _f_bout(i, rv, pv):
    hs, gb = rv
    gelu = 0.5 * gb * (1.0 + jnp.tanh(0.7978845608028654 * (gb + 0.044715 * gb * gb * gb)))
    return [hs * gelu]


def _f_relu2(i, rv, pv):
    (p,) = rv
    r = jnp.maximum(p, 0.0)
    return [r * r]


def _f_cnorm(i, rv, pv):
    (cq, ckv), (gq, gkv) = rv, pv
    return [cq * lax.rsqrt(jnp.mean(cq * cq, axis=-1, keepdims=True) + EPS) * gq,
            ckv * lax.rsqrt(jnp.mean(ckv * ckv, axis=-1, keepdims=True) + EPS) * gkv]


def _rope128(x, cos, sin):
    lane = _lane()
    rot = jnp.where((lane >= C_NOPE) & (lane < C_NOPE + C_ROPE // 2), -_lroll(x, -(C_ROPE // 2)),
                    jnp.where((lane >= C_NOPE + C_ROPE // 2) & (lane < C_NOPE + C_ROPE), _lroll(x, C_ROPE // 2), 0.0))
    return x * cos + rot * sin


def _f_mlaprep(i, rv, pv):
    (q_, kk_, misc, cos, sin), (gq, gk) = rv, pv
    lane = _lane()
    m_n = lane < C_NOPE
    m_r = (lane >= C_NOPE) & (lane < C_NOPE + C_ROPE)

    def norm2(x, g):
        x2 = x * x
        sn = jnp.sum(jnp.where(m_n, x2, 0.0), axis=-1, keepdims=True) * (1.0 / C_NOPE)
        sr = jnp.sum(jnp.where(m_r, x2, 0.0), axis=-1, keepdims=True) * (1.0 / C_ROPE)
        scale = jnp.where(m_n, lax.rsqrt(sn + EPS), jnp.where(m_r, lax.rsqrt(sr + EPS), 0.0))
        return x * scale * g

    kr = _rope128(norm2(jnp.where(m_r, misc, 0.0), gk), cos, sin)
    qs, ks = [], []
    for h in range(C_HEADS):
        qs.append(_rope128(norm2(q_[:, h * HP:(h + 1) * HP], gq), cos, sin))
        ks.append(norm2(jnp.where(m_n, kk_[:, h * HP:(h + 1) * HP], 0.0), gk) + kr)
    return [jnp.concatenate(qs, axis=1), jnp.concatenate(ks, axis=1)]


def _f_foxprep(i, rv, pv):
    (fq, fk, misc), (gq, gk, bf) = rv, pv
    tm = fq.shape[0]
    lane = _lane()

    def hnorm(x, g):
        outs = []
        for h in range(D_HEADS):
            xh = x[:, h * HP:(h + 1) * HP]
            ss = jnp.sum(xh * xh, axis=-1, keepdims=True) * (1.0 / D_HD)
            outs.append(xh * lax.rsqrt(ss + EPS) * g)
        return jnp.concatenate(outs, axis=1)

    lf = jnp.where(_rowmask(i, tm) & (lane >= 96) & (lane < 96 + D_HEADS), _log_sigmoid(misc + bf), 0.0)
    return [hnorm(fq, gq), hnorm(fk, gk), lf]


def _mlstm_chunk(c, n, m, q, k, v, li, lf):
    ln = q.shape[0]
    r = lax.broadcasted_iota(jnp.int32, (ln, ln), 0)
    cc = lax.broadcasted_iota(jnp.int32, (ln, ln), 1)
    causal = cc <= r
    eye = cc == r
    li_row = jnp.sum(jnp.where(eye, li, 0.0), axis=0, keepdims=True)
    b_col = jnp.sum(jnp.where(causal, jnp.sum(jnp.where(eye, lf, 0.0), axis=0, keepdims=True), 0.0),
                    axis=1, keepdims=True)
    b_row = jnp.sum(jnp.where(r <= cc, lf, 0.0), axis=0, keepdims=True)
    k = k * (A_DQK ** -0.5)
    qb, kb, vb = q.astype(BF16), k.astype(BF16), v.astype(BF16)
    dmat = jnp.where(causal, b_col - b_row + li_row, NEG)
    inter = b_col + m
    m_row = jnp.maximum(inter, jnp.max(dmat, axis=1, keepdims=True))
    w_intra = jnp.exp(dmat - m_row)
    w_inter = jnp.exp(inter - m_row)
    s = lax.dot_general(qb, kb, (((1,), (1,)), ((), ())), preferred_element_type=F32) * w_intra
    num = (w_inter * jnp.dot(qb, c.astype(BF16), preferred_element_type=F32)
           + jnp.dot(s.astype(BF16), vb, preferred_element_type=F32))
    den = w_inter * jnp.sum(q * n, axis=1, keepdims=True) + jnp.sum(s, axis=1, keepdims=True)
    h = num / jnp.maximum(jnp.abs(den), jnp.exp(-m_row))
    g = jnp.sum(lf, axis=0, keepdims=True)
    dk = g - b_col + li
    m_new = jnp.maximum(g + m, jnp.max(dk, axis=0, keepdims=True))
    wk = jnp.exp(dk - m_new)
    sc = jnp.exp(g + m - m_new)
    kw = wk * k
    c_new = sc * c + lax.dot_general(kw.astype(BF16), vb, (((0,), (0,)), ((), ())), preferred_element_type=F32)
    n_new = sc * n + jnp.sum(kw, axis=0, keepdims=True)
    return c_new, n_new, m_new, h


def _mlstm_fwd(z, li, lf):
    tp = z.shape[0]
    nc = tp // A_CHUNK
    ln = A_CHUNK

    def body(q_ref, k_ref, v_ref, li_ref, lf_ref, h_ref, cs_ref, ns_ref, ms_ref, c_s, n_s, m_s):
        @pl.when(pl.program_id(1) == 0)
        def _():
            c_s[...] = jnp.zeros_like(c_s)
            n_s[...] = jnp.zeros_like(n_s)
            m_s[...] = jnp.zeros_like(m_s)

        c, n, m = c_s[...], n_s[...], m_s[...]
        cs_ref[0, 0] = c
        ns_ref[0, 0] = n
        ms_ref[0, 0] = m
        c2, n2, m2, h = _mlstm_chunk(c, n, m, q_ref[...], k_ref[...], v_ref[...], li_ref[0], lf_ref[0])
        c_s[...] = c2
        n_s[...] = n2
        m_s[...] = m2
        h_ref[...] = h

    g_spec = pl.BlockSpec((1, ln, 1), lambda h, j: (h, j, 0))
    return pl.pallas_call(
        body, name="mlstm_fwd", grid=(A_HEADS, nc),
        in_specs=[pl.BlockSpec((ln, A_DQK), lambda h, j: (j, h)),
                  pl.BlockSpec((ln, A_DQK), lambda h, j: (j, A_HEADS + h)),
                  pl.BlockSpec((ln, A_DV), lambda h, j: (j, A_HEADS + h)),
                  g_spec, g_spec],
        out_specs=[pl.BlockSpec((ln, A_DV), lambda h, j: (j, h)),
                   pl.BlockSpec((1, 1, A_DQK, A_DV), lambda h, j: (h, j, 0, 0)),
                   pl.BlockSpec((1, 1, 1, A_DQK), lambda h, j: (h, j, 0, 0)),
                   pl.BlockSpec((1, 1, 1, 1), lambda h, j: (h, j, 0, 0))],
        out_shape=[jax.ShapeDtypeStruct((tp, A_HEADS * A_DV), F32),
                   jax.ShapeDtypeStruct((A_HEADS, nc, A_DQK, A_DV), F32),
                   jax.ShapeDtypeStruct((A_HEADS, nc, 1, A_DQK), F32),
                   jax.ShapeDtypeStruct((A_HEADS, nc, 1, 1), F32)],
        scratch_shapes=[pltpu.VMEM((A_DQK, A_DV), F32), pltpu.VMEM((1, A_DQK), F32), pltpu.VMEM((1, 1), F32)],
        compiler_params=_cparams(("parallel", "arbitrary")),
    )(z, z, z, li, lf)


def _mlstm_bwd(z, li, lf, cs, ns, ms, dh):
    tp = z.shape[0]
    nc = tp // A_CHUNK
    ln = A_CHUNK

    def body(q_ref, k_ref, v_ref, li_ref, lf_ref, cs_ref, ns_ref, ms_ref, dh_ref,
             dq_ref, dk_ref, dv_ref, dli_ref, dlf_ref, dc_s, dn_s, dm_s):
        @pl.when(pl.program_id(1) == 0)
        def _():
            dc_s[...] = jnp.zeros_like(dc_s)
            dn_s[...] = jnp.zeros_like(dn_s)
            dm_s[...] = jnp.zeros_like(dm_s)

        _, vjp = jax.vjp(_mlstm_chunk, cs_ref[0, 0], ns_ref[0, 0], ms_ref[0, 0], q_ref[...], k_ref[...],
                         v_ref[...], li_ref[0], lf_ref[0])
        dc, dn, dm, dq, dk, dv, dli, dlf = vjp((dc_s[...], dn_s[...], dm_s[...], dh_ref[...]))
        dc_s[...] = dc
        dn_s[...] = dn
        dm_s[...] = dm
        dq_ref[...] = dq
        dk_ref[...] = dk
        dv_ref[...] = dv
        dli_ref[0] = dli
        dlf_ref[0] = dlf

    def rj(j):
        return nc - 1 - j

    g_spec = pl.BlockSpec((1, ln, 1), lambda h, j: (h, rj(j), 0))
    return pl.pallas_call(
        body, name="mlstm_bwd", grid=(A_HEADS, nc),
        in_specs=[pl.BlockSpec((ln, A_DQK), lambda h, j: (rj(j), h)),
                  pl.BlockSpec((ln, A_DQK), lambda h, j: (rj(j), A_HEADS + h)),
                  pl.BlockSpec((ln, A_DV), lambda h, j: (rj(j), A_HEADS + h)),
                  g_spec, g_spec,
                  pl.BlockSpec((1, 1, A_DQK, A_DV), lambda h, j: (h, rj(j), 0, 0)),
                  pl.BlockSpec((1, 1, 1, A_DQK), lambda h, j: (h, rj(j), 0, 0)),
                  pl.BlockSpec((1, 1, 1, 1), lambda h, j: (h, rj(j), 0, 0)),
                  pl.BlockSpec((ln, A_DV), lambda h, j: (rj(j), h))],
        out_specs=[pl.BlockSpec((ln, A_DQK), lambda h, j: (rj(j), h)),
                   pl.BlockSpec((ln, A_DQK), lambda h, j: (rj(j), h)),
                   pl.BlockSpec((ln, A_DV), lambda h, j: (rj(j), h)),
                   g_spec, g_spec],
        out_shape=[jax.ShapeDtypeStruct((tp, A_HEADS * A_DQK), F32),
                   jax.ShapeDtypeStruct((tp, A_HEADS * A_DQK), F32),
                   jax.ShapeDtypeStruct((tp, A_HEADS * A_DV), F32),
                   jax.ShapeDtypeStruct((A_HEADS, tp, 1), F32),
                   jax.ShapeDtypeStruct((A_HEADS, tp, 1), F32)],
        scratch_shapes=[pltpu.VMEM((A_DQK, A_DV), F32), pltpu.VMEM((1, A_DQK), F32), pltpu.VMEM((1, 1), F32)],
        compiler_params=_cparams(("parallel", "arbitrary")),
    )(z, z, z, li, lf, cs, ns, ms, dh)


def _shift_down(x, s, row):
    return x if s == 0 else jnp.where(row >= s, pltpu.roll(x, s, 0), 0.0)


def _shift_up(x, s, row):
    n = x.shape[0]
    return x if s == 0 else jnp.where(row < n - s, pltpu.roll(x, n - s, 0), 0.0)


def _conv_fwd(z, xcb, w, b):
    tp, c = z.shape[0], w.shape[1]
    ct = 256

    def body(x_ref, w_ref, b_ref, o_ref):
        x = x_ref[...]
        row = lax.broadcasted_iota(jnp.int32, (tp, 1), 0)
        acc = jnp.zeros_like(x) + b_ref[...]
        for k in range(CONV_W):
            acc = acc + w_ref[k:k + 1, :] * _shift_down(x, CONV_W - 1 - k, row)
        o_ref[...] = acc

    return pl.pallas_call(
        body, name="conv_fwd", grid=(c // ct,),
        in_specs=[pl.BlockSpec((tp, ct), lambda j: (0, xcb + j)), pl.BlockSpec((CONV_W, ct), lambda j: (0, j)),
                  pl.BlockSpec((1, ct), lambda j: (0, j))],
        out_specs=pl.BlockSpec((tp, ct), lambda j: (0, j)),
        out_shape=jax.ShapeDtypeStruct((tp, c), F32),
        compiler_params=_cparams(("parallel",)),
    )(z, w, b)


def _conv_bwd(z, xcb, w, dxc):
    tp, c = z.shape[0], w.shape[1]
    ct = 256

    def body(x_ref, w_ref, d_ref, dx_ref, dw_ref, db_ref):
        x, d = x_ref[...], d_ref[...]
        row = lax.broadcasted_iota(jnp.int32, (tp, 1), 0)
        acc = jnp.zeros_like(x)
        for k in range(CONV_W):
            s = CONV_W - 1 - k
            acc = acc + w_ref[k:k + 1, :] * _shift_up(d, s, row)
            dw_ref[k:k + 1, :] = jnp.sum(d * _shift_down(x, s, row), axis=0, keepdims=True)
        dx_ref[...] = acc
        db_ref[...] = jnp.sum(d, axis=0, keepdims=True)

    return pl.pallas_call(
        body, name="conv_bwd", grid=(c // ct,),
        in_specs=[pl.BlockSpec((tp, ct), lambda j: (0, xcb + j)), pl.BlockSpec((CONV_W, ct), lambda j: (0, j)),
                  pl.BlockSpec((tp, ct), lambda j: (0, j))],
        out_specs=[pl.BlockSpec((tp, ct), lambda j: (0, j)), pl.BlockSpec((CONV_W, ct), lambda j: (0, j)),
                   pl.BlockSpec((1, ct), lambda j: (0, j))],
        out_shape=[jax.ShapeDtypeStruct((tp, c), F32), jax.ShapeDtypeStruct((CONV_W, c), F32),
                   jax.ShapeDtypeStruct((1, c), F32)],
        compiler_params=_cparams(("parallel",)),
    )(z, w, dxc)


def _scan_fwd(a, u, name):
    tp, c = a.shape
    ct = _pick(c, (256, 128))
    lb = SCAN_BLOCK
    nb = tp // lb

    def body(a_ref, u_ref, h_ref, hp_ref):
        row = lax.broadcasted_iota(jnp.int32, (lb, 1), 0)

        def blk(j, carry):
            r0 = pl.multiple_of(j * lb, lb)
            aa, uu = a_ref[pl.ds(r0, lb), :], u_ref[pl.ds(r0, lb), :]
            s = 1
            while s < lb:
                mk = row >= s
                uu = jnp.where(mk, aa * pltpu.roll(uu, s, 0) + uu, uu)
                aa = jnp.where(mk, aa * pltpu.roll(aa, s, 0), aa)
                s *= 2
            hh = uu + aa * carry
            h_ref[pl.ds(r0, lb), :] = hh
            hp_ref[pl.ds(r0, lb), :] = jnp.where(row >= 1, pltpu.roll(hh, 1, 0), carry)
            return hh[lb - 1:lb, :]

        lax.fori_loop(0, nb, blk, jnp.zeros((1, ct), F32))

    spec = pl.BlockSpec((tp, ct), lambda j: (0, j))
    return pl.pallas_call(
        body, name=name, grid=(c // ct,), in_specs=[spec, spec], out_specs=[spec, spec],
        out_shape=[jax.ShapeDtypeStruct((tp, c), F32)] * 2,
        compiler_params=_cparams(("parallel",)),
    )(a, u)


def _scan_bwd(a, hprev, dh, name):
    tp, c = a.shape
    ct = _pick(c, (256, 128))
    lb = SCAN_BLOCK
    nb = tp // lb

    def body(a_ref, hp_ref, dh_ref, du_ref, da_ref):
        row = lax.broadcasted_iota(jnp.int32, (lb, 1), 0)

        def blk(jj, carry):
            g_next, a_next = carry
            r0 = pl.multiple_of((nb - 1 - jj) * lb, lb)
            a_blk = a_ref[pl.ds(r0, lb), :]
            aa = jnp.where(row < lb - 1, pltpu.roll(a_blk, lb - 1, 0), a_next)
            gg = dh_ref[pl.ds(r0, lb), :]
            s = 1
            while s < lb:
                mk = row < lb - s
                gg = jnp.where(mk, aa * pltpu.roll(gg, lb - s, 0) + gg, gg)
                aa = jnp.where(mk, aa * pltpu.roll(aa, lb - s, 0), aa)
                s *= 2
            gg = gg + aa * g_next
            du_ref[pl.ds(r0, lb), :] = gg
            da_ref[pl.ds(r0, lb), :] = gg * hp_ref[pl.ds(r0, lb), :]
            return gg[0:1, :], a_blk[0:1, :]

        lax.fori_loop(0, nb, blk, (jnp.zeros((1, ct), F32), jnp.zeros((1, ct), F32)))

    spec = pl.BlockSpec((tp, ct), lambda j: (0, j))
    return pl.pallas_call(
        body, name=name, grid=(c // ct,), in_specs=[spec, spec, spec], out_specs=[spec, spec],
        out_shape=[jax.ShapeDtypeStruct((tp, c), F32)] * 2,
        compiler_params=_cparams(("parallel",)),
    )(a, hprev, dh)


def _attn_valid(i, j, tq):
    t = i * tq + lax.broadcasted_iota(jnp.int32, (tq, 1), 0)
    s = j * tq + lax.broadcasted_iota(jnp.int32, (1, tq), 1)
    return (s <= t) & ((s >= PAD) | (s == t))


def _attn_fwd(q, k, v, vcb, scale, bias, name):
    tp = q.shape[0]
    nh = q.shape[1] // HP
    tq = _pick(tp, (384, 256, 128))
    has_bias = bias is not None

    def body(*refs):
        if has_bias:
            q_ref, k_ref, v_ref, bq_ref, bk_ref, o_ref, lse_ref = refs
        else:
            q_ref, k_ref, v_ref, o_ref, lse_ref = refs
        i = pl.program_id(1)
        qb = q_ref[...].astype(BF16)

        def step(j, carry):
            m, l, acc = carry
            r0 = pl.multiple_of(j * tq, tq)
            kb = k_ref[pl.ds(r0, tq), :].astype(BF16)
            vb = v_ref[pl.ds(r0, tq), :].astype(BF16)
            s = lax.dot_general(qb, kb, (((1,), (1,)), ((), ())), preferred_element_type=F32) * scale
            if has_bias:
                s = s + (bq_ref[0] - bk_ref[0, j])
            s = jnp.where(_attn_valid(i, j, tq), s, NEG)
            m2 = jnp.maximum(m, jnp.max(s, axis=1, keepdims=True))
            alpha = jnp.exp(m - m2)
            p = jnp.exp(s - m2)
            l2 = alpha * l + jnp.sum(p, axis=1, keepdims=True)
            acc2 = alpha * acc + jnp.dot(p.astype(BF16), vb, preferred_element_type=F32)
            return m2, l2, acc2

        init = (jnp.full((tq, 1), NEG, F32), jnp.zeros((tq, 1), F32), jnp.zeros((tq, HP), F32))
        m, l, acc = lax.fori_loop(0, i + 1, step, init)
        o_ref[...] = acc / l
        lse_ref[0] = m + jnp.log(l)

    in_specs = [pl.BlockSpec((tq, HP), lambda h, i: (i, h)), pl.BlockSpec((tp, HP), lambda h, i: (0, h)),
                pl.BlockSpec((tp, HP), lambda h, i: (0, vcb + h))]
    ins = [q, k, v]
    if has_bias:
        in_specs += [pl.BlockSpec((1, tq, 1), lambda h, i: (h, i, 0)),
                     pl.BlockSpec((1, tp // tq, 1, tq), lambda h, i: (h, 0, 0, 0))]
        ins += list(bias)
    return pl.pallas_call(
        body, name=name, grid=(nh, tp // tq), in_specs=in_specs,
        out_specs=[pl.BlockSpec((tq, HP), lambda h, i: (i, h)), pl.BlockSpec((1, tq, 1), lambda h, i: (h, i, 0))],
        out_shape=[jax.ShapeDtypeStruct((tp, nh * HP), F32), jax.ShapeDtypeStruct((nh, tp, 1), F32)],
        compiler_params=_cparams(("parallel", "parallel")),
    )(*ins)


def _attn_bwd(q, k, v, vcb, o, lse, do, docb, scale, bias, name):
    tp = q.shape[0]
    nh = q.shape[1] // HP
    tq = _pick(tp, (384, 256, 128))
    has_bias = bias is not None

    def body(*refs):
        if has_bias:
            (q_ref, k_ref, v_ref, o_ref, lse_ref, do_ref, bq_ref, bk_ref,
             dq_ref, dk_ref, dv_ref, dbq_ref, dbk_ref) = refs
        else:
            q_ref, k_ref, v_ref, o_ref, lse_ref, do_ref, dq_ref, dk_ref, dv_ref = refs
        i = pl.program_id(1)

        @pl.when(i == 0)
        def _():
            dk_ref[...] = jnp.zeros_like(dk_ref)
            dv_ref[...] = jnp.zeros_like(dv_ref)
            if has_bias:
                dbk_ref[...] = jnp.zeros_like(dbk_ref)

        qb = q_ref[...].astype(BF16)
        do_ = do_ref[...]
        dob = do_.astype(BF16)
        lse_ = lse_ref[0]
        delta = jnp.sum(do_ * o_ref[...], axis=1, keepdims=True)

        def step(j, carry):
            dq, dbq = carry
            r0 = pl.multiple_of(j * tq, tq)
            kb = k_ref[pl.ds(r0, tq), :].astype(BF16)
            vb = v_ref[pl.ds(r0, tq), :].astype(BF16)
            s = lax.dot_general(qb, kb, (((1,), (1,)), ((), ())), preferred_element_type=F32) * scale
            if has_bias:
                s = s + (bq_ref[0] - bk_ref[0, j])
            s = jnp.where(_attn_valid(i, j, tq), s, NEG)
            p = jnp.exp(s - lse_)
            dp = lax.dot_general(dob, vb, (((1,), (1,)), ((), ())), preferred_element_type=F32)
            ds = p * (dp - delta)
            dsb = ds.astype(BF16)
            dk_ref[pl.ds(r0, tq), :] += lax.dot_general(dsb, qb, (((0,), (0,)), ((), ())),
                                                        preferred_element_type=F32) * scale
            dv_ref[pl.ds(r0, tq), :] += lax.dot_general(p.astype(BF16), dob, (((0,), (0,)), ((), ())),
                                                        preferred_element_type=F32)
            if has_bias:
                dbk_ref[0, j] -= jnp.sum(ds, axis=0, keepdims=True)
                dbq = dbq + jnp.sum(ds, axis=1, keepdims=True)
            return dq + jnp.dot(dsb, kb, preferred_element_type=F32), dbq

        dq, dbq = lax.fori_loop(0, i + 1, step, (jnp.zeros((tq, HP), F32), jnp.zeros((tq, 1), F32)))
        dq_ref[...] = dq * scale
        if has_bias:
            dbq_ref[0] = dbq

    blk_q = pl.BlockSpec((tq, HP), lambda h, i: (i, h))
    blk_k = pl.BlockSpec((tp, HP), lambda h, i: (0, h))
    in_specs = [blk_q, blk_k, pl.BlockSpec((tp, HP), lambda h, i: (0, vcb + h)), blk_q,
                pl.BlockSpec((1, tq, 1), lambda h, i: (h, i, 0)), pl.BlockSpec((tq, HP), lambda h, i: (i, docb + h))]
    ins = [q, k, v, o, lse, do]
    out_specs = [blk_q, blk_k, blk_k]
    out_shape = [jax.ShapeDtypeStruct((tp, nh * HP), F32)] * 3
    if has_bias:
        bk_spec = pl.BlockSpec((1, tp // tq, 1, tq), lambda h, i: (h, 0, 0, 0))
        in_specs += [pl.BlockSpec((1, tq, 1), lambda h, i: (h, i, 0)), bk_spec]
        ins += list(bias)
        out_specs += [pl.BlockSpec((1, tq, 1), lambda h, i: (h, i, 0)), bk_spec]
        out_shape += [jax.ShapeDtypeStruct((nh, tp, 1), F32), jax.ShapeDtypeStruct((nh, tp // tq, 1, tq), F32)]
    return pl.pallas_call(
        body, name=name, grid=(nh, tp // tq), in_specs=in_specs, out_specs=out_specs, out_shape=out_shape,
        compiler_params=_cparams(("parallel", "arbitrary")),
    )(*ins)


def _loss_head(h, tgt):
    tp, d = h.shape
    tm = 128
    first = (PAD + N_META) // tm

    def body(h_ref, t_ref, l_ref, d_ref):
        i = pl.program_id(0)

        @pl.when(i == 0)
        def _():
            l_ref[...] = jnp.zeros_like(l_ref)

        live = i >= first
        err = jnp.where(live, h_ref[...] - t_ref[...], 0.0)
        d_ref[...] = err * (1.0 / d)
        l_ref[...] += (0.5 / d) * jnp.sum(err * err)

    return pl.pallas_call(
        body, name="loss_head", grid=(tp // tm,),
        in_specs=[pl.BlockSpec((tm, d), lambda i: (i, 0)), pl.BlockSpec((tm, d), lambda i: (i, 0))],
        out_specs=[_full_spec((8, 128)), pl.BlockSpec((tm, d), lambda i: (i, 0))],
        out_shape=[jax.ShapeDtypeStruct((8, 128), F32), jax.ShapeDtypeStruct((tp, d), F32)],
        compiler_params=_cparams(("arbitrary",)),
    )(h, tgt)


def _rope_tables(pos_rows):
    half = C_ROPE // 2
    freqs = ROPE_THETA ** (-jnp.arange(half, dtype=F32) / half)
    ang = pos_rows[:, None].astype(F32) * freqs
    cos, sin = jnp.cos(ang), jnp.sin(ang)
    tp = pos_rows.shape[0]
    one, zero = jnp.ones((tp, C_NOPE), F32), jnp.zeros((tp, C_NOPE), F32)
    tail1, tail0 = jnp.ones((tp, HP - C_NOPE - C_ROPE), F32), jnp.zeros((tp, HP - C_NOPE - C_ROPE), F32)
    return (jnp.concatenate([one, cos, cos, tail1], axis=1), jnp.concatenate([zero, sin, sin, tail0], axis=1))


def _heads_to_cols(x, lo, n):
    t = x[:, lo:lo + n].T
    return t[:, :, None], t[:, None, :]


def _local_step(x, positions, tgt, w):
    s_len = x.shape[0]
    tp = PAD + N_META + s_len
    tm = _pick(tp, (384, 256, 128))
    front = PAD + N_META
    h0 = jnp.concatenate([jnp.zeros((PAD, D_MODEL), F32), w["meta"], x], axis=0)
    tgt_p = jnp.concatenate([jnp.zeros((front, D_MODEL), F32), tgt], axis=0)
    pos_rows = jnp.concatenate([jnp.zeros((PAD,), jnp.int32), jnp.arange(N_META, dtype=jnp.int32),
                                positions + N_META])
    cos, sin = _rope_tables(pos_rows)
    g = {}

    r_h0 = [(h0, D_MODEL, 0)]
    (xn0,) = _row_fwd("norm0_f", _f_norm, r_h0, [w["ev_ln"]], [(D_MODEL, BF16)], tm)
    z0 = _mm(xn0, w["ev_w_in"], "nn", "mm_z0")
    r_misc0 = [(z0, HP, 5120 // HP)]
    (g0,) = _row_fwd("gate0_f", _f_gate0, r_misc0, [w["ev_b_if"]], [(HP, F32)], tm)
    li, _ = _heads_to_cols(g0, 0, A_HEADS)
    lf, _ = _heads_to_cols(g0, A_HEADS, A_HEADS)
    h_a, cs, ns, ms = _mlstm_fwd(z0, li, lf)
    r_aout = [(h_a, 1024, 0), (z0, 1024, 2)]
    (ha,) = _row_fwd("aout_f", _f_aout, r_aout, [w["ev_a_norm"]], [(1024, BF16)], tm)
    xc = _conv_fwd(z0, 3072 // 256, w["ev_conv_w"], w["ev_conv_b"])
    p_gates = [w["ev_w_ra"], w["ev_b_ra"], w["ev_w_rx"], w["ev_b_rx"], w["ev_lam"]]
    a_g, u_g = _row_fwd("gates_f", _f_gates, [(xc, 1024, 0)], p_gates, [(1024, F32), (1024, F32)], tm)
    hs, hprev = _scan_fwd(a_g, u_g, "lru_f")
    r_bout = [(hs, 1024, 0), (z0, 1024, 4)]
    (hb,) = _row_fwd("bout_f", _f_bout, r_bout, [], [(1024, BF16)], tm)
    hab = jnp.concatenate([ha, hb], axis=1)
    h1 = _mm(hab, w["ev_w_out"], "nn", "mm_h1", add=h0)
    (xn1,) = _row_fwd("norm1_f", _f_norm, [(h1, D_MODEL, 0)], [w["mlp_ln0"]], [(D_MODEL, BF16)], tm)
    p0 = _mm(xn1, w["w_ff1_0"], "nn", "mm_p0")
    (act0,) = _row_fwd("relu0_f", _f_relu2, [(p0, 4096, 0)], [], [(4096, BF16)], tm)
    h2 = _mm(act0, w["w_ff2_0"], "nn", "mm_h2", add=h1)
    (xn2,) = _row_fwd("norm2_f", _f_norm, [(h2, D_MODEL, 0)], [w["od_ln"]], [(D_MODEL, BF16)], tm)
    z1 = _mm(xn2, w["od_w_in"], "nn", "mm_z1")
    r_c = [(z1, C_Q_LORA, 3072 // C_Q_LORA), (z1, C_KV_LORA, 3584 // C_KV_LORA)]
    cqn, ckvn = _row_fwd("cnorm_f", _f_cnorm, r_c, [w["od_g_qa"], w["od_g_kva"]],
                         [(C_Q_LORA, BF16), (C_KV_LORA, BF16)], tm)
    q_ = _mm(cqn, w["od_w_uq"], "nn", "mm_q")
    kv_ = _mm(ckvn, w["od_w_ukv"], "nn", "mm_kv")
    r_mla = [(q_, 1024, 0), (kv_, 1024, 0), (z1, HP, 3840 // HP), (cos, HP, 0), (sin, HP, 0)]
    p_mla = [w["gq_full"], w["gk_full"]]
    qm, km = _row_fwd("mla_f", _f_mlaprep, r_mla, p_mla, [(1024, BF16), (1024, BF16)], tm)
    sc_c = (C_NOPE + C_ROPE) ** -0.5
    hc, lse_c = _attn_fwd(qm, km, kv_, C_HEADS, sc_c, None, "mla_attn_f")
    r_fox = [(z1, 1024, 0), (z1, 1024, 1), (z1, HP, 3840 // HP)]
    p_fox = [w["gfq_full"], w["gfk_full"], w["bf_full"]]
    qf, kf, lfx = _row_fwd("fox_f", _f_foxprep, r_fox, p_fox, [(1024, BF16), (1024, BF16), (HP, F32)], tm)
    ones = jnp.ones((tp, HP), F32)
    fcum, fprev = _scan_fwd(ones, lfx, "fcum_f")
    bq, bk = _heads_to_cols(fcum, 96, D_HEADS)
    bias = (bq, bk.reshape(D_HEADS, tp // tm, 1, tm))
    sc_d = D_HD ** -0.5
    hd, lse_d = _attn_fwd(qf, kf, z1, 2048 // HP, sc_d, bias, "fox_attn_f")
    hcd = jnp.concatenate([hc, hd], axis=1)
    h3 = _mm(hcd, w["od_w_out"], "nn", "mm_h3", add=h2)
    (xn3,) = _row_fwd("norm3_f", _f_norm, [(h3, D_MODEL, 0)], [w["mlp_ln1"]], [(D_MODEL, BF16)], tm)
    p1 = _mm(xn3, w["w_ff1_1"], "nn", "mm_p1")
    (act1,) = _row_fwd("relu1_f", _f_relu2, [(p1, 4096, 0)], [], [(4096, BF16)], tm)
    h4 = _mm(act1, w["w_ff2_1"], "nn", "mm_h4", add=h3)
    lpart, dh4 = _loss_head(h4, tgt_p)
    loss = lpart[0, 0]

    def mlp_bwd(tag, dh_out, h_in, xn, p, act, ln, w1, w2):
        dact = _mm(dh_out, w2, "nt", f"mm_dact{tag}")
        g_w2 = _mm(act, dh_out, "tn", f"mm_dw2_{tag}")
        (dp,), _ = _row_bwd(f"relu{tag}_b", _f_relu2, [(p, 4096, 0)], [], [(dact, 4096, 0)], tm, [True])
        g_w1 = _mm(xn, dp, "tn", f"mm_dw1_{tag}")
        dxn = _mm(dp, w1, "nt", f"mm_dxn{tag}")
        (dh_in,), (g_ln,) = _row_bwd(f"normm{tag}_b", _f_norm, [(h_in, D_MODEL, 0)], [ln], [(dxn, D_MODEL, 0)], tm,
                                     [True], add=dh_out)
        return dh_in, g_ln, g_w1, g_w2

    dh3, g["mlp_ln1"], g["w_ff1_1"], g["w_ff2_1"] = mlp_bwd("1", dh4, h3, xn3, p1, act1, w["mlp_ln1"],
                                                            w["w_ff1_1"], w["w_ff2_1"])
    dhcd = _mm(dh3, w["od_w_out"], "nt", "mm_dhcd")
    g["od_w_out"] = _mm(hcd, dh3, "tn", "mm_dwout1")
    dqf, dkf, dvf, dbq, dbk = _attn_bwd(qf, kf, z1, 2048 // HP, hd, lse_d, dhcd, D_HEADS, sc_d, bias, "fox_attn_b")
    dfc = dbq[:, :, 0].T + dbk.reshape(D_HEADS, tp).T
    dfcum = jnp.concatenate([jnp.zeros((tp, 96), F32), dfc, jnp.zeros((tp, HP - 96 - D_HEADS), F32)], axis=1)
    dlfx, _ = _scan_bwd(ones, fprev, dfcum, "fcum_b")
    (dfq, dfk, dmisc_f), (g["gfq_full"], g["gfk_full"], g["bf_full"]) = _row_bwd(
        "fox_b", _f_foxprep, r_fox, p_fox, [(dqf, 1024, 0), (dkf, 1024, 0), (dlfx, HP, 0)], tm, [True, True, True])
    dqm, dkm, dvm = _attn_bwd(qm, km, kv_, C_HEADS, hc, lse_c, dhcd, 0, sc_c, None, "mla_attn_b")
    (dq_, dkk_, dmisc_m), (g["gq_full"], g["gk_full"]) = _row_bwd(
        "mla_b", _f_mlaprep, r_mla, p_mla, [(dqm, 1024, 0), (dkm, 1024, 0)], tm, [True, True, True, False, False])
    dkv_ = jnp.concatenate([dkk_, dvm], axis=1)
    dckvn = _mm(dkv_, w["od_w_ukv"], "nt", "mm_dckvn")
    g["od_w_ukv"] = _mm(ckvn, dkv_, "tn", "mm_dwukv")
    dcqn = _mm(dq_, w["od_w_uq"], "nt", "mm_dcqn")
    g["od_w_uq"] = _mm(cqn, dq_, "tn", "mm_dwuq")
    (dcq, dckv), (g["od_g_qa"], g["od_g_kva"]) = _row_bwd(
        "cnorm_b", _f_cnorm, r_c, [w["od_g_qa"], w["od_g_kva"]],
        [(dcqn, C_Q_LORA, 0), (dckvn, C_KV_LORA, 0)], tm, [True, True])
    zpad = jnp.zeros((tp, HP), F32)
    dz1 = jnp.concatenate([dfq, dfk, dvf, dcq, zpad, dckv, dmisc_f + dmisc_m, zpad], axis=1)
    g["od_w_in"] = _mm(xn2, dz1, "tn", "mm_dwin1")
    dxn2 = _mm(dz1, w["od_w_in"], "nt", "mm_dxn2")
    (dh2,), (g["od_ln"],) = _row_bwd("norm2_b", _f_norm, [(h2, D_MODEL, 0)], [w["od_ln"]], [(dxn2, D_MODEL, 0)], tm,
                                     [True], add=dh3)
    dh1, g["mlp_ln0"], g["w_ff1_0"], g["w_ff2_0"] = mlp_bwd("0", dh2, h1, xn1, p0, act0, w["mlp_ln0"],
                                                            w["w_ff1_0"], w["w_ff2_0"])
    dhab = _mm(dh1, w["ev_w_out"], "nt", "mm_dhab")
    g["ev_w_out"] = _mm(hab, dh1, "tn", "mm_dwout0")
    (dhs, dgb), _ = _row_bwd("bout_b", _f_bout, r_bout, [], [(dhab, 1024, 1)], tm, [True, True])
    du_g, da_g = _scan_bwd(a_g, hprev, dhs, "lru_b")
    (dxc,), (g["ev_w_ra"], g["ev_b_ra"], g["ev_w_rx"], g["ev_b_rx"], g["ev_lam"]) = _row_bwd(
        "gates_b", _f_gates, [(xc, 1024, 0)], p_gates, [(da_g, 1024, 0), (du_g, 1024, 0)], tm, [True])
    dxb, g["ev_conv_w"], g["ev_conv_b"] = _conv_bwd(z0, 3072 // 256, w["ev_conv_w"], dxc)
    (dh_a, do_), (g["ev_a_norm"],) = _row_bwd("aout_b", _f_aout, r_aout, [w["ev_a_norm"]], [(dhab, 1024, 0)], tm,
                                              [True, True])
    dq, dk, dv, dli, dlf = _mlstm_bwd(z0, li, lf, cs, ns, ms, dh_a)
    dg0 = jnp.concatenate([dli[:, :, 0].T, dlf[:, :, 0].T, jnp.zeros((tp, HP - 2 * A_HEADS), F32)], axis=1)
    (dmisc0,), (g["ev_b_if"],) = _row_bwd("gate0_b", _f_gate0, r_misc0, [w["ev_b_if"]], [(dg0, HP, 0)], tm, [True])
    dz0 = jnp.concatenate([dq, dk, dv, do_, dxb, dgb, dmisc0, zpad], axis=1)
    g["ev_w_in"] = _mm(xn0, dz0, "tn", "mm_dwin0")
    dxn0 = _mm(dz0, w["ev_w_in"], "nt", "mm_dxn0")
    (dh0,), (g["ev_ln"],) = _row_bwd("norm0_b", _f_norm, r_h0, [w["ev_ln"]], [(dxn0, D_MODEL, 0)], tm, [True], add=dh1)
    g["meta"] = dh0[PAD:front]
    return loss, dh0[front:], g


def _pad_last(a, n):
    return jnp.pad(a, [(0, 0)] * (a.ndim - 1) + [(0, n - a.shape[-1])])


def _pad_heads(a, nh, d):
    return _pad_last(a.reshape(a.shape[:-1] + (nh, d)), HP).reshape(a.shape[:-1] + (nh * HP,))


def _unpad_heads(a, nh, d):
    return a.reshape(a.shape[:-1] + (nh, HP))[..., :d].reshape(a.shape[:-1] + (nh * d,))


def _prep_weights(p):
    w = {}
    w["meta"] = p["meta"]
    for k in ("ev_ln", "ev_a_norm", "ev_conv_b", "ev_b_ra", "ev_b_rx", "ev_lam", "od_ln", "od_g_qa", "od_g_kva"):
        w[k] = p[k].reshape(1, -1)
    wi = p["ev_w_in"][0]
    w["ev_w_in"] = _pad_last(jnp.concatenate([wi[:, :3072], wi[:, 3080:5128], wi[:, 3072:3080]], axis=1), ZE)
    w["ev_b_if"] = _pad_last(p["ev_b_if"].reshape(1, -1), HP)
    w["ev_conv_w"] = p["ev_conv_w"][0]
    w["ev_w_ra"] = p["ev_w_ra"][0]
    w["ev_w_rx"] = p["ev_w_rx"][0]
    w["ev_w_out"] = p["ev_w_out"][0]
    wo = p["od_w_in"][0]
    z = lambda n: jnp.zeros((wo.shape[0], n), wo.dtype)
    w["od_w_in"] = jnp.concatenate(
        [_pad_heads(wo[:, 672:1184], D_HEADS, D_HD), _pad_heads(wo[:, 1184:1696], D_HEADS, D_HD),
         _pad_heads(wo[:, 1696:2208], D_HEADS, D_HD), wo[:, 0:384], z(128), wo[:, 384:640],
         z(64), wo[:, 640:672], wo[:, 2208:2216], z(24), z(128)], axis=1)
    w["od_w_uq"] = _pad_heads(p["od_w_uq"][0], C_HEADS, C_NOPE + C_ROPE)
    wkv = p["od_w_ukv"][0].reshape(C_KV_LORA, C_HEADS, C_NOPE + C_V)
    w["od_w_ukv"] = jnp.concatenate([_pad_last(wkv[:, :, :C_NOPE], HP).reshape(C_KV_LORA, -1),
                                     _pad_last(wkv[:, :, C_NOPE:], HP).reshape(C_KV_LORA, -1)], axis=1)
    f1 = lambda a: a.reshape(1, -1)
    w["gq_full"] = _pad_last(jnp.concatenate([f1(p["od_g_qn"]), f1(p["od_g_qr"])], axis=1), HP)
    w["gk_full"] = _pad_last(jnp.concatenate([f1(p["od_g_kn"]), f1(p["od_g_kr"])], axis=1), HP)
    w["gfq_full"] = _pad_last(f1(p["od_g_fq"]), HP)
    w["gfk_full"] = _pad_last(f1(p["od_g_fk"]), HP)
    w["bf_full"] = _pad_last(jnp.concatenate([jnp.zeros((1, 96), F32), f1(p["od_b_f"])], axis=1), HP)
    wout = p["od_w_out"][0]
    w["od_w_out"] = jnp.pad(wout.reshape(2 * C_HEADS, C_V, D_MODEL), ((0, 0), (0, HP - C_V), (0, 0))).reshape(-1, D_MODEL)
    for l in (0, 1):
        w[f"mlp_ln{l}"] = p["mlp_ln"][l:l + 1]
        w[f"w_ff1_{l}"] = p["w_ff1"][l]
        w[f"w_ff2_{l}"] = p["w_ff2"][l]
    return w


def _unprep_grads(g):
    o = {}
    o["meta"] = g["meta"]
    for k in ("ev_ln", "ev_a_norm", "ev_conv_b", "ev_b_ra", "ev_b_rx", "ev_lam", "od_ln", "od_g_qa", "od_g_kva"):
        o[k] = g[k].reshape(1, -1)
    gi = g["ev_w_in"]
    o["ev_w_in"] = jnp.concatenate([gi[:, :3072], gi[:, 5120:5128], gi[:, 3072:5120]], axis=1)[None]
    o["ev_b_if"] = g["ev_b_if"][:, :2 * A_HEADS]
    o["ev_conv_w"] = g["ev_conv_w"][None]
    o["ev_w_ra"] = g["ev_w_ra"][None]
    o["ev_w_rx"] = g["ev_w_rx"][None]
    o["ev_w_out"] = g["ev_w_out"][None]
    go = g["od_w_in"]
    o["od_w_in"] = jnp.concatenate(
        [go[:, 3072:3456], go[:, 3584:3840], go[:, 3904:3936], _unpad_heads(go[:, 0:1024], D_HEADS, D_HD),
         _unpad_heads(go[:, 1024:2048], D_HEADS, D_HD), _unpad_heads(go[:, 2048:3072], D_HEADS, D_HD),
         go[:, 3936:3944]], axis=1)[None]
    o["od_w_uq"] = _unpad_heads(g["od_w_uq"], C_HEADS, C_NOPE + C_ROPE)[None]
    gkv = g["od_w_ukv"]
    gk = gkv[:, :C_HEADS * HP].reshape(C_KV_LORA, C_HEADS, HP)[:, :, :C_NOPE]
    gv = gkv[:, C_HEADS * HP:].reshape(C_KV_LORA, C_HEADS, HP)[:, :, :C_V]
    o["od_w_ukv"] = jnp.concatenate([gk, gv], axis=2).reshape(1, C_KV_LORA, -1)
    o["od_g_qn"] = g["gq_full"][:, :C_NOPE]
    o["od_g_qr"] = g["gq_full"][:, C_NOPE:C_NOPE + C_ROPE]
    o["od_g_kn"] = g["gk_full"][:, :C_NOPE]
    o["od_g_kr"] = g["gk_full"][:, C_NOPE:C_NOPE + C_ROPE]
    o["od_g_fq"] = g["gfq_full"][:, :D_HD]
    o["od_g_fk"] = g["gfk_full"][:, :D_HD]
    o["od_b_f"] = g["bf_full"][:, 96:96 + D_HEADS]
    o["od_w_out"] = g["od_w_out"].reshape(2 * C_HEADS, HP, D_MODEL)[:, :C_V].reshape(1, -1, D_MODEL)
    o["mlp_ln"] = jnp.concatenate([g["mlp_ln0"], g["mlp_ln1"]], axis=0)
    o["w_ff1"] = jnp.stack([g["w_ff1_0"], g["w_ff1_1"]])
    o["w_ff2"] = jnp.stack([g["w_ff2_0"], g["w_ff2_1"]])
    return o


def _me_and_peers():
    mx, my, mc = lax.axis_index("x"), lax.axis_index("y"), lax.axis_index("c")
    peers = []
    for k in range(1, N_DEV):
        px, py, pc = mx ^ ((k >> 2) & 1), my ^ ((k >> 1) & 1), mc ^ (k & 1)
        peers.append(((px, py, pc), 4 * px + 2 * py + pc))
    return 4 * mx + 2 * my + mc, peers


def _exchange(ops, name):
    n_ops = len(ops)
    flags = [s for _, s in ops]
    n_peer = N_DEV - 1

    def body(*refs):
        x_refs, o_refs = refs[:n_ops], refs[n_ops:2 * n_ops]
        send_sems, recv_sems, local_sems = refs[2 * n_ops:]
        me, peers = _me_and_peers()
        local = []
        for a in range(n_ops):
            src = x_refs[a].at[me] if flags[a] else x_refs[a]
            local.append(pltpu.make_async_copy(src, o_refs[a].at[me], local_sems.at[a]))
            local[-1].start()
        copies = []
        for k, (peer, pid) in enumerate(peers):
            for a in range(n_ops):
                cp = pltpu.make_async_remote_copy(
                    src_ref=x_refs[a].at[pid] if flags[a] else x_refs[a], dst_ref=o_refs[a].at[me],
                    send_sem=send_sems.at[a * n_peer + k], recv_sem=recv_sems.at[a * n_peer + k],
                    device_id=peer, device_id_type=pl.DeviceIdType.MESH)
                cp.start()
                copies.append(cp)
        for cp in copies:
            cp.wait_send()
        for cp in copies:
            cp.wait_recv()
        for cp in local:
            cp.wait()

    any_spec = pl.BlockSpec(memory_space=pl.ANY)
    return pl.pallas_call(
        body, name=name,
        out_shape=[jax.ShapeDtypeStruct((N_DEV,) + x.shape[-2:], x.dtype) for x, _ in ops],
        in_specs=[any_spec] * n_ops, out_specs=[any_spec] * n_ops,
        scratch_shapes=[pltpu.SemaphoreType.DMA((n_ops * n_peer,)), pltpu.SemaphoreType.DMA((n_ops * n_peer,)),
                        pltpu.SemaphoreType.DMA((n_ops,))],
    )(*[x for x, _ in ops])


def _adamw(parts, w, m, v, name):
    r, c = w.shape
    tr = r
    for cand in (512, 256, 128, 64, 32, 16):
        if r % cand == 0 and N_DEV * cand * c * 4 <= 4 * 1024 * 1024:
            tr = cand
            break
    c1 = 1.0 / (1.0 - ADAM_B1 ** ADAM_STEP)
    c2 = 1.0 / (1.0 - ADAM_B2 ** ADAM_STEP)

    def body(p_ref, w_ref, m_ref, v_ref, g_ref, d_ref, mo_ref, vo_ref):
        g = p_ref[0].astype(F32)
        for j in range(1, N_DEV):
            g = g + p_ref[j].astype(F32)
        m2 = ADAM_B1 * m_ref[...] + (1.0 - ADAM_B1) * g
        v2 = ADAM_B2 * v_ref[...] + (1.0 - ADAM_B2) * (g * g)
        g_ref[...] = g
        mo_ref[...] = m2
        vo_ref[...] = v2
        d_ref[...] = -ADAM_LR * ((m2 * c1) / (jnp.sqrt(v2 * c2) + ADAM_EPS) + ADAM_WD * w_ref[...])

    spec = pl.BlockSpec((tr, c), lambda i: (i, 0))
    return pl.pallas_call(
        body, name=name, grid=(r // tr,),
        in_specs=[pl.BlockSpec((N_DEV, tr, c), lambda i: (0, i, 0)), spec, spec, spec],
        out_specs=[spec] * 4, out_shape=[jax.ShapeDtypeStruct((r, c), F32)] * 4,
        compiler_params=_cparams(("parallel",)),
    )(parts, w, m, v)


def _group2d(arrs, c, lead=()):
    return jnp.concatenate([a.reshape(lead + (-1, c)) for a in arrs], axis=len(lead))


def _ungroup(buf, shapes, c):
    lead = buf.shape[:-2]
    out, off = [], 0
    for s in shapes:
        n = 1
        for d_ in s:
            n *= d_
        out.append(buf[..., off:off + n // c, :].reshape(lead + tuple(s)))
        off += n // c
    return out


def _rows_for(n, mult):
    return -(-n // (1024 * mult)) * mult


def _pack(arrs, mult, lead=()):
    nl = len(lead)
    flat = jnp.concatenate([a.reshape(lead + (-1,)) for a in arrs], axis=nl)
    rows = _rows_for(flat.shape[-1], mult)
    return jnp.pad(flat, [(0, 0)] * nl + [(0, rows * 1024 - flat.shape[-1])]).reshape(lead + (rows, 1024))


def _unpack(buf, shapes):
    lead = buf.shape[:-2]
    flat = buf.reshape(lead + (-1,))
    out, off = [], 0
    for s in shapes:
        n = 1
        for d_ in s:
            n *= d_
        out.append(flat[..., off:off + n].reshape(lead + tuple(s)))
        off += n
    return out


def _unshard(g8, ax):
    a = jnp.moveaxis(g8, 0, ax)
    return a.reshape(a.shape[:ax] + (N_DEV * a.shape[ax + 1],) + a.shape[ax + 2:])


def _shard8(full, ax):
    s = full.shape
    return jnp.moveaxis(full.reshape(s[:ax] + (N_DEV, s[ax] // N_DEV) + s[ax + 1:]), ax, 0)


_NAMES = ["meta", "ev_ln", "ev_w_in", "ev_b_if", "ev_a_norm", "ev_conv_w", "ev_conv_b", "ev_w_ra", "ev_b_ra",
          "ev_w_rx", "ev_b_rx", "ev_lam", "ev_w_out", "od_ln", "od_w_in", "od_b_f", "od_g_qa", "od_g_kva",
          "od_w_uq", "od_w_ukv", "od_g_qn", "od_g_qr", "od_g_kn", "od_g_kr", "od_g_fq", "od_g_fk", "od_w_out",
          "mlp_ln", "w_ff1", "w_ff2"]
_SHARD_AXIS = {"meta": 1, "ev_w_in": 2, "ev_conv_w": 2, "ev_w_out": 1, "od_ln": 1, "od_w_in": 2, "od_g_qa": 1,
               "od_g_kva": 1, "od_w_uq": 2, "od_w_ukv": 2, "od_w_out": 1, "w_ff1": 2, "w_ff2": 1}
_MATMUL_WEIGHTS = ("ev_w_in", "ev_w_out", "od_w_in", "od_w_uq", "od_w_ukv", "od_w_out", "w_ff1", "w_ff2")
_GROUPS = ((("ev_w_out", "od_w_out", "w_ff2"), 1024), (("w_ff1",), 512), (("ev_w_in",), 641), (("od_w_in",), 277),
           (("od_w_uq",), 96), (("od_w_ukv",), 128))
_BIG_REPL = ("ev_w_ra", "ev_w_rx")


def kernel(x, positions, meta, ev_ln, ev_w_in, ev_b_if, ev_a_norm, ev_conv_w, ev_conv_b, ev_w_ra, ev_b_ra, ev_w_rx, ev_b_rx, ev_lam, ev_w_out, od_ln, od_w_in, od_b_f, od_g_qa, od_g_kva, od_w_uq, od_w_ukv, od_g_qn, od_g_qr, od_g_kn, od_g_kr, od_g_fq, od_g_fk, od_w_out, mlp_ln, w_ff1, w_ff2, loss_target, m_meta, m_ev_ln, m_ev_w_in, m_ev_b_if, m_ev_a_norm, m_ev_conv_w, m_ev_conv_b, m_ev_w_ra, m_ev_b_ra, m_ev_w_rx, m_ev_b_rx, m_ev_lam, m_ev_w_out, m_od_ln, m_od_w_in, m_od_b_f, m_od_g_qa, m_od_g_kva, m_od_w_uq, m_od_w_ukv, m_od_g_qn, m_od_g_qr, m_od_g_kn, m_od_g_kr, m_od_g_fq, m_od_g_fk, m_od_w_out, m_mlp_ln, m_w_ff1, m_w_ff2, v_meta, v_ev_ln, v_ev_w_in, v_ev_b_if, v_ev_a_norm, v_ev_conv_w, v_ev_conv_b, v_ev_w_ra, v_ev_b_ra, v_ev_w_rx, v_ev_b_rx, v_ev_lam, v_ev_w_out, v_od_ln, v_od_w_in, v_od_b_f, v_od_g_qa, v_od_g_kva, v_od_w_uq, v_od_w_ukv, v_od_g_qn, v_od_g_qr, v_od_g_kn, v_od_g_kr, v_od_g_fq, v_od_g_fk, v_od_w_out, v_mlp_ln, v_w_ff1, v_w_ff2):
    given = dict(locals())
    wts = {n: given[n] for n in _NAMES}
    mom = {n: given["m_" + n] for n in _NAMES}
    var = {n: given["v_" + n] for n in _NAMES}
    small_sh = [n for n in _NAMES if n in _SHARD_AXIS and n not in _MATMUL_WEIGHTS]
    small_rp = [n for n in _NAMES if n not in _SHARD_AXIS and n not in _BIG_REPL]
    shp = {n: wts[n].shape for n in _NAMES}

    ops = [(_group2d([wts[n].astype(BF16) for n in names], c), False) for names, c in _GROUPS]
    ops.append((_pack([wts[n] for n in small_sh], 8), False))
    got = _exchange(ops, "gather_weights")
    full = {n: wts[n] for n in _NAMES if n not in _SHARD_AXIS}
    for (names, c), buf in zip(_GROUPS, got):
        for n, a in zip(names, _ungroup(buf, [shp[n] for n in names], c)):
            full[n] = _unshard(a, _SHARD_AXIS[n])
    for n, a in zip(small_sh, _unpack(got[-1], [shp[n] for n in small_sh])):
        full[n] = _unshard(a, _SHARD_AXIS[n])

    loss, gx, g = _local_step(x[0], positions[0], loss_target[0], _prep_weights(full))
    grads = _unprep_grads(g)

    ops = [(_group2d([_shard8(grads[n], _SHARD_AXIS[n]).astype(BF16) for n in names], c, (N_DEV,)), True)
           for names, c in _GROUPS]
    ops.append((_pack([_shard8(grads[n], _SHARD_AXIS[n]) for n in small_sh], 8, (N_DEV,)), True))
    ops += [(grads[n].reshape(-1, B_BLOCK), False) for n in _BIG_REPL]
    ops.append((_pack([grads[n].reshape(shp[n]) for n in small_rp], 8), False))
    parts = _exchange(ops, "exchange_grads")

    res = {}
    for k, (names, c) in enumerate(_GROUPS):
        r4 = _adamw(parts[k], *[_group2d([d[n] for n in names], c) for d in (wts, mom, var)], f"adamw_g{k}")
        for n, *four in zip(names, *[_ungroup(r, [shp[n] for n in names], c) for r in r4]):
            res[n] = four
    k = len(_GROUPS)
    r4 = _adamw(parts[k], *[_pack([d[n] for n in small_sh], 8) for d in (wts, mom, var)], "adamw_small_sharded")
    for n, *four in zip(small_sh, *[_unpack(r, [shp[n] for n in small_sh]) for r in r4]):
        res[n] = four
    for j, n in enumerate(_BIG_REPL):
        r4 = _adamw(parts[k + 1 + j], *[d[n].reshape(-1, B_BLOCK) for d in (wts, mom, var)], f"adamw_{n}")
        res[n] = [r.reshape(shp[n]) for r in r4]
    r4 = _adamw(parts[-1], *[_pack([d[n] for n in small_rp], 8) for d in (wts, mom, var)], "adamw_small_repl")
    for n, *four in zip(small_rp, *[_unpack(r, [shp[n] for n in small_rp]) for r in r4]):
        res[n] = four

    outs = [res[n][kind] for kind in range(4) for n in _NAMES]
    loss = lax.psum(loss, ("x", "y", "c"))
    return (loss, gx[None], *outs)
```

```python
import functools

import jax
import jax.numpy as jnp
from jax import lax
from jax.experimental import pallas as pl
from jax.experimental.pallas import tpu as pltpu

F32 = jnp.float32
BF16 = jnp.bfloat16

D_MODEL = 1024
N_META = 16
PAD = 112
EPS = 1e-6
NEG = -1e30
A_HEADS, A_DQK, A_DV, A_CHUNK = 4, 128, 256, 64
B_BLOCKS, B_BLOCK, CONV_W, LRU_C = 8, 128, 4, 8.0
C_HEADS, C_Q_LORA, C_KV_LORA, C_NOPE, C_ROPE, C_V = 8, 384, 256, 64, 32, 64
ROPE_THETA = 10000.0
D_HEADS, D_HD = 8, 64
HP = 128
ZE = 5376
ZO = 4096
N_DEV = 8
ADAM_LR, ADAM_B1, ADAM_B2, ADAM_EPS, ADAM_WD, ADAM_STEP = 0.001, 0.9, 0.999, 1e-08, 0.01, 10
VMEM_LIMIT = 56 * 1024 * 1024
SCAN_BLOCK = 128


def _pick(n, prefs):
    for p in prefs:
        if n % p == 0:
            return p
    return n


def _cparams(dims):
    return pltpu.CompilerParams(dimension_semantics=dims, vmem_limit_bytes=VMEM_LIMIT)


def _full_spec(shape):
    nd = len(shape)
    return pl.BlockSpec(shape, lambda *_: (0,) * nd)


def _me_and_peers():
    mx, my, mc = lax.axis_index("x"), lax.axis_index("y"), lax.axis_index("c")
    peers = []
    for k in range(1, N_DEV):
        px, py, pc = mx ^ ((k >> 2) & 1), my ^ ((k >> 1) & 1), mc ^ (k & 1)
        peers.append(((px, py, pc), 4 * px + 2 * py + pc))
    return 4 * mx + 2 * my + mc, peers


def _comm_copies(x_refs, o_refs, flags, send_sems, recv_sems, local_sems):
    n_ops, n_peer = len(flags), N_DEV - 1
    me, peers = _me_and_peers()
    local = [pltpu.make_async_copy(x_refs[a].at[me] if flags[a] else x_refs[a], o_refs[a].at[me], local_sems.at[a])
             for a in range(n_ops)]
    remote = []
    for k, (peer, pid) in enumerate(peers):
        for a in range(n_ops):
            remote.append(pltpu.make_async_remote_copy(
                src_ref=x_refs[a].at[pid] if flags[a] else x_refs[a], dst_ref=o_refs[a].at[me],
                send_sem=send_sems.at[a * n_peer + k], recv_sem=recv_sems.at[a * n_peer + k],
                device_id=peer, device_id_type=pl.DeviceIdType.MESH))
    return local, remote


def _comm_wait(local, remote):
    for cp in remote:
        cp.wait_send()
    for cp in remote:
        cp.wait_recv()
    for cp in local:
        cp.wait()


def _comm_shapes(comm):
    n = len(comm)
    out_shape = [jax.ShapeDtypeStruct((N_DEV,) + x.shape[-2:], x.dtype) for x, _ in comm]
    sems = [pltpu.SemaphoreType.DMA((n * (N_DEV - 1),)), pltpu.SemaphoreType.DMA((n * (N_DEV - 1),)),
            pltpu.SemaphoreType.DMA((n,))]
    return out_shape, sems


def _pcall(body, name, grid, in_specs, out_specs, out_shape, scratch_shapes, dims, ins, comm=None):
    if not comm:
        return pl.pallas_call(body, name=name, grid=grid, in_specs=in_specs, out_specs=out_specs,
                              out_shape=out_shape, scratch_shapes=scratch_shapes,
                              compiler_params=_cparams(dims))(*ins)
    n_in, n_out, n = len(in_specs), len(out_specs), len(comm)
    flags = [s for _, s in comm]
    c_shape, c_sems = _comm_shapes(comm)

    def riding(*refs):
        cx = refs[n_in:n_in + n]
        co = refs[n_in + n + n_out:n_in + 2 * n + n_out]
        rest = refs[n_in + 2 * n + n_out:]
        sems = rest[len(rest) - 3:]
        ids = [pl.program_id(a) for a in range(len(grid))]
        first = functools.reduce(jnp.logical_and, [i == 0 for i in ids])
        last = functools.reduce(jnp.logical_and, [i == g - 1 for i, g in zip(ids, grid)])

        @pl.when(first)
        def _():
            local, remote = _comm_copies(cx, co, flags, *sems)
            for cp in local + remote:
                cp.start()

        body(*refs[:n_in], *refs[n_in + n:n_in + n + n_out], *rest[:len(rest) - 3])

        @pl.when(last)
        def _():
            _comm_wait(*_comm_copies(cx, co, flags, *sems))

    any_spec = pl.BlockSpec(memory_space=pl.ANY)
    res = pl.pallas_call(
        riding, name=name, grid=grid, in_specs=list(in_specs) + [any_spec] * n,
        out_specs=list(out_specs) + [any_spec] * n, out_shape=list(out_shape) + c_shape,
        scratch_shapes=list(scratch_shapes) + c_sems,
        compiler_params=_cparams(("arbitrary",) * len(grid)))(*ins, *[x for x, _ in comm])
    return list(res[:n_out]), list(res[n_out:])


def _mm(a, b, mode, name, out_dtype=None, add=None, comm=None):
    if out_dtype is None:
        out_dtype = BF16 if mode == "tn" else F32
    if mode == "nn":
        (m, k), n = a.shape, b.shape[1]
    elif mode == "nt":
        (m, k), n = a.shape, b.shape[0]
    else:
        (k, m), n = a.shape, b.shape[1]
    tm = _pick(m, (1408, 1024, 768, 512, 384, 256, 128))
    tn = _pick(n, (1024, 768, 512, 384, 256, 128))
    tk = _pick(k, (1408, 1024, 768, 512, 384, 256, 128))
    nk = k // tk
    if mode == "nn":
        a_spec = pl.BlockSpec((tm, tk), lambda i, j, q: (i, q))
        b_spec = pl.BlockSpec((tk, tn), lambda i, j, q: (q, j))
        dn = (((1,), (0,)), ((), ()))
    elif mode == "nt":
        a_spec = pl.BlockSpec((tm, tk), lambda i, j, q: (i, q))
        b_spec = pl.BlockSpec((tn, tk), lambda i, j, q: (j, q))
        dn = (((1,), (1,)), ((), ()))
    else:
        a_spec = pl.BlockSpec((tk, tm), lambda i, j, q: (q, i))
        b_spec = pl.BlockSpec((tk, tn), lambda i, j, q: (q, j))
        dn = (((0,), (0,)), ((), ()))
    o_spec = pl.BlockSpec((tm, tn), lambda i, j, q: (i, j))
    has_add = add is not None

    def body(*refs):
        if has_add:
            a_ref, b_ref, add_ref, o_ref, acc = refs
        else:
            a_ref, b_ref, o_ref, acc = refs
        q = pl.program_id(2)

        @pl.when(q == 0)
        def _():
            acc[...] = jnp.zeros_like(acc)

        acc[...] += lax.dot_general(a_ref[...].astype(BF16), b_ref[...].astype(BF16), dn,
                                    preferred_element_type=F32)

        @pl.when(q == nk - 1)
        def _():
            r = acc[...]
            if has_add:
                r = r + add_ref[...]
            o_ref[...] = r.astype(o_ref.dtype)

    ins = [a, b] + ([add] if has_add else [])
    in_specs = [a_spec, b_spec] + ([o_spec] if has_add else [])
    res = _pcall(body, name, (m // tm, n // tn, nk), in_specs, [o_spec], [jax.ShapeDtypeStruct((m, n), out_dtype)],
                 [pltpu.VMEM((tm, tn), F32)], ("parallel", "parallel", "arbitrary"), ins, comm)
    return (res[0][0], res[1]) if comm else res[0]


def _row_specs(rows, tm):
    return [pl.BlockSpec((tm, w), functools.partial(lambda cb, i: (i, cb), cb)) for (_, w, cb) in rows]


def _row_fwd(name, f, rows, params, outs, tm):
    tp = rows[0][0].shape[0]
    nr, npar = len(rows), len(params)

    def body(*refs):
        i = pl.program_id(0)
        rv = [r[...] for r in refs[:nr]]
        pv = [r[...] for r in refs[nr:nr + npar]]
        res = f(i, rv, pv)
        for o_ref, r in zip(refs[nr + npar:], res):
            o_ref[...] = r.astype(o_ref.dtype)

    res = pl.pallas_call(
        body, name=name, grid=(tp // tm,),
        in_specs=_row_specs(rows, tm) + [_full_spec(p.shape) for p in params],
        out_specs=[pl.BlockSpec((tm, w), lambda i: (i, 0)) for (w, _) in outs],
        out_shape=[jax.ShapeDtypeStruct((tp, w), dt) for (w, dt) in outs],
        compiler_params=_cparams(("parallel",)),
    )(*[r[0] for r in rows], *params)
    return list(res)


def _row_bwd(name, f, rows, params, douts, tm, diff, add=None):
    tp = rows[0][0].shape[0]
    nr, npar, nd = len(rows), len(params), len(douts)
    didx = [k for k in range(nr) if diff[k]]
    has_add = add is not None

    def body(*refs):
        i = pl.program_id(0)
        rv = [r[...] for r in refs[:nr]]
        pv = [r[...] for r in refs[nr:nr + npar]]
        dv = [r[...] for r in refs[nr + npar:nr + npar + nd]]
        pos = nr + npar + nd
        add_ref = refs[pos] if has_add else None
        pos += 1 if has_add else 0
        dr_refs = refs[pos:pos + len(didx)]
        dp_refs = refs[pos + len(didx):]

        def g(drv, pvs):
            full = list(rv)
            for k, val in zip(didx, drv):
                full[k] = val
            return tuple(f(i, full, list(pvs)))

        _, vjp = jax.vjp(g, [rv[k] for k in didx], pv)
        d_r, d_p = vjp(tuple(dv))
        for n_, (ref, val) in enumerate(zip(dr_refs, d_r)):
            if has_add and n_ == 0:
                val = val + add_ref[...]
            ref[...] = val

        @pl.when(i == 0)
        def _():
            for ref in dp_refs:
                ref[...] = jnp.zeros_like(ref)

        for ref, val in zip(dp_refs, d_p):
            ref[...] += val

    in_specs = (_row_specs(rows, tm) + [_full_spec(p.shape) for p in params] + _row_specs(douts, tm))
    ins = [r[0] for r in rows] + list(params) + [d[0] for d in douts]
    if has_add:
        in_specs.append(pl.BlockSpec((tm, rows[didx[0]][1]), lambda i: (i, 0)))
        ins.append(add)
    out_specs = ([pl.BlockSpec((tm, rows[k][1]), lambda i: (i, 0)) for k in didx]
                 + [_full_spec(p.shape) for p in params])
    out_shape = ([jax.ShapeDtypeStruct((tp, rows[k][1]), F32) for k in didx]
                 + [jax.ShapeDtypeStruct(p.shape, F32) for p in params])
    res = pl.pallas_call(
        body, name=name, grid=(tp // tm,), in_specs=in_specs, out_specs=out_specs, out_shape=out_shape,
        compiler_params=_cparams(("arbitrary",)),
    )(*ins)
    return list(res[:len(didx)]), list(res[len(didx):])


def _rowmask(i, tm):
    return (i * tm + lax.broadcasted_iota(jnp.int32, (tm, 1), 0)) >= PAD


def _lane(n=HP):
    return lax.broadcasted_iota(jnp.int32, (1, n), 1)


def _softplus(x):
    return jnp.maximum(x, 0.0) + jnp.log(1.0 + jnp.exp(-jnp.abs(x)))


def _log_sigmoid(x):
    return -_softplus(-x)


def _sigmoid(x):
    return 1.0 / (1.0 + jnp.exp(-x))


@functools.partial(jax.custom_vjp, nondiff_argnums=(1,))
def _lroll(x, s):
    return pltpu.roll(x, s % HP, 1)


def _lroll_fwd(x, s):
    return _lroll(x, s), None


def _lroll_bwd(s, _, g):
    return (pltpu.roll(g, (-s) % HP, 1),)


_lroll.defvjp(_lroll_fwd, _lroll_bwd)


def _f_norm(i, rv, pv):
    (h,), (g,) = rv, pv
    return [h * lax.rsqrt(jnp.mean(h * h, axis=-1, keepdims=True) + EPS) * g]


def _f_gate0(i, rv, pv):
    (misc,), (b,) = rv, pv
    tm = misc.shape[0]
    x = misc + b
    lane, ok = _lane(), _rowmask(i, tm)
    li = jnp.where(ok, x, NEG)
    lf = jnp.where(ok, _log_sigmoid(x), 0.0)
    return [jnp.where(lane < A_HEADS, li, jnp.where(lane < 2 * A_HEADS, lf, 0.0))]


def _f_aout(i, rv, pv):
    (ha, o), (g,) = rv, pv
    outs = []
    for h in range(A_HEADS):
        x = ha[:, h * A_DV:(h + 1) * A_DV]
        outs.append(x * lax.rsqrt(jnp.mean(x * x, axis=-1, keepdims=True) + EPS) * g)
    return [jnp.concatenate(outs, axis=1) * _sigmoid(o)]


def _f_gates(i, rv, pv):
    (xc,), (w_ra, b_ra, w_rx, b_rx, lam) = rv, pv
    tm = xc.shape[0]
    ra, rx = [], []
    for g in range(B_BLOCKS):
        xg = xc[:, g * B_BLOCK:(g + 1) * B_BLOCK].astype(BF16)
        ra.append(jnp.dot(xg, w_ra[g].astype(BF16), preferred_element_type=F32))
        rx.append(jnp.dot(xg, w_rx[g].astype(BF16), preferred_element_type=F32))
    r = _sigmoid(jnp.concatenate(ra, axis=1) + b_ra)
    ig = _sigmoid(jnp.concatenate(rx, axis=1) + b_rx)
    log_a = -LRU_C * r * _softplus(-lam)
    a = jnp.exp(log_a)
    u = jnp.sqrt(1.0 - jnp.exp(2.0 * log_a)) * (ig * xc)
    return [a, jnp.where(_rowmask(i, tm), u, 0.0)]


def _f_bout(i, rv, pv):
    hs, gb = rv
    gelu = 0.5 * gb * (1.0 + jnp.tanh(0.7978845608028654 * (gb + 0.044715 * gb * gb * gb)))
    return [hs * gelu]


def _f_relu2(i, rv, pv):
    (p,) = rv
    r = jnp.maximum(p, 0.0)
    return [r * r]


def _f_cnorm(i, rv, pv):
    (cq, ckv), (gq, gkv) = rv, pv
    return [cq * lax.rsqrt(jnp.mean(cq * cq, axis=-1, keepdims=True) + EPS) * gq,
            ckv * lax.rsqrt(jnp.mean(ckv * ckv, axis=-1, keepdims=True) + EPS) * gkv]


def _rope128(x, cos, sin):
    lane = _lane()
    rot = jnp.where((lane >= C_NOPE) & (lane < C_NOPE + C_ROPE // 2), -_lroll(x, -(C_ROPE // 2)),
                    jnp.where((lane >= C_NOPE + C_ROPE // 2) & (lane < C_NOPE + C_ROPE), _lroll(x, C_ROPE // 2), 0.0))
    return x * cos + rot * sin


def _f_mlaprep(i, rv, pv):
    (q_, kk_, misc, cos, sin), (gq, gk) = rv, pv
    lane = _lane()
    m_n = lane < C_NOPE
    m_r = (lane >= C_NOPE) & (lane < C_NOPE + C_ROPE)

    def norm2(x, g):
        x2 = x * x
        sn = jnp.sum(jnp.where(m_n, x2, 0.0), axis=-1, keepdims=True) * (1.0 / C_NOPE)
        sr = jnp.sum(jnp.where(m_r, x2, 0.0), axis=-1, keepdims=True) * (1.0 / C_ROPE)
        scale = jnp.where(m_n, lax.rsqrt(sn + EPS), jnp.where(m_r, lax.rsqrt(sr + EPS), 0.0))
        return x * scale * g

    kr = _rope128(norm2(jnp.where(m_r, misc, 0.0), gk), cos, sin)
    qs, ks = [], []
    for h in range(C_HEADS):
        qs.append(_rope128(norm2(q_[:, h * HP:(h + 1) * HP], gq), cos, sin))
        ks.append(norm2(jnp.where(m_n, kk_[:, h * HP:(h + 1) * HP], 0.0), gk) + kr)
    return [jnp.concatenate(qs, axis=1), jnp.concatenate(ks, axis=1)]


def _f_foxprep(i, rv, pv):
    (fq, fk, misc), (gq, gk, bf) = rv, pv
    tm = fq.shape[0]
    lane = _lane()

    def hnorm(x, g):
        outs = []
        for h in range(D_HEADS):
            xh = x[:, h * HP:(h + 1) * HP]
            ss = jnp.sum(xh * xh, axis=-1, keepdims=True) * (1.0 / D_HD)
            outs.append(xh * lax.rsqrt(ss + EPS) * g)
        return jnp.concatenate(outs, axis=1)

    lf = jnp.where(_rowmask(i, tm) & (lane >= 96) & (lane < 96 + D_HEADS), _log_sigmoid(misc + bf), 0.0)
    return [hnorm(fq, gq), hnorm(fk, gk), lf]


def _mlstm_chunk(c, n, m, q, k, v, li, lf):
    ln = q.shape[0]
    r = lax.broadcasted_iota(jnp.int32, (ln, ln), 0)
    cc = lax.broadcasted_iota(jnp.int32, (ln, ln), 1)
    causal = cc <= r
    eye = cc == r
    li_row = jnp.sum(jnp.where(eye, li, 0.0), axis=0, keepdims=True)
    b_col = jnp.sum(jnp.where(causal, jnp.sum(jnp.where(eye, lf, 0.0), axis=0, keepdims=True), 0.0),
                    axis=1, keepdims=True)
    b_row = jnp.sum(jnp.where(r <= cc, lf, 0.0), axis=0, keepdims=True)
    k = k * (A_DQK ** -0.5)
    qb, kb, vb = q.astype(BF16), k.astype(BF16), v.astype(BF16)
    dmat = jnp.where(causal, b_col - b_row + li_row, NEG)
    inter = b_col + m
    m_row = jnp.maximum(inter, jnp.max(dmat, axis=1, keepdims=True))
    w_intra = jnp.exp(dmat - m_row)
    w_inter = jnp.exp(inter - m_row)
    s = lax.dot_general(qb, kb, (((1,), (1,)), ((), ())), preferred_element_type=F32) * w_intra
    num = (w_inter * jnp.dot(qb, c.astype(BF16), preferred_element_type=F32)
           + jnp.dot(s.astype(BF16), vb, preferred_element_type=F32))
    den = w_inter * jnp.sum(q * n, axis=1, keepdims=True) + jnp.sum(s, axis=1, keepdims=True)
    h = num / jnp.maximum(jnp.abs(den), jnp.exp(-m_row))
    g = jnp.sum(lf, axis=0, keepdims=True)
    dk = g - b_col + li
    m_new = jnp.maximum(g + m, jnp.max(dk, axis=0, keepdims=True))
    wk = jnp.exp(dk - m_new)
    sc = jnp.exp(g + m - m_new)
    kw = wk * k
    c_new = sc * c + lax.dot_general(kw.astype(BF16), vb, (((0,), (0,)), ((), ())), preferred_element_type=F32)
    n_new = sc * n + jnp.sum(kw, axis=0, keepdims=True)
    return c_new, n_new, m_new, h


def _mlstm_fwd(z, li, lf, comm=None):
    tp = z.shape[0]
    nc = tp // A_CHUNK
    ln = A_CHUNK

    def body(q_ref, k_ref, v_ref, li_ref, lf_ref, h_ref, cs_ref, ns_ref, ms_ref, c_s, n_s, m_s):
        @pl.when(pl.program_id(1) == 0)
        def _():
            c_s[...] = jnp.zeros_like(c_s)
            n_s[...] = jnp.zeros_like(n_s)
            m_s[...] = jnp.zeros_like(m_s)

        c, n, m = c_s[...], n_s[...], m_s[...]
        cs_ref[0, 0] = c
        ns_ref[0, 0] = n
        ms_ref[0, 0] = m
        c2, n2, m2, h = _mlstm_chunk(c, n, m, q_ref[...], k_ref[...], v_ref[...], li_ref[0], lf_ref[0])
        c_s[...] = c2
        n_s[...] = n2
        m_s[...] = m2
        h_ref[...] = h

    g_spec = pl.BlockSpec((1, ln, 1), lambda h, j: (h, j, 0))
    return _pcall(
        body, "mlstm_fwd", (A_HEADS, nc),
        [pl.BlockSpec((ln, A_DQK), lambda h, j: (j, h)),
         pl.BlockSpec((ln, A_DQK), lambda h, j: (j, A_HEADS + h)),
         pl.BlockSpec((ln, A_DV), lambda h, j: (j, A_HEADS + h)),
         g_spec, g_spec],
        [pl.BlockSpec((ln, A_DV), lambda h, j: (j, h)),
         pl.BlockSpec((1, 1, A_DQK, A_DV), lambda h, j: (h, j, 0, 0)),
         pl.BlockSpec((1, 1, 1, A_DQK), lambda h, j: (h, j, 0, 0)),
         pl.BlockSpec((1, 1, 1, 1), lambda h, j: (h, j, 0, 0))],
        [jax.ShapeDtypeStruct((tp, A_HEADS * A_DV), F32),
         jax.ShapeDtypeStruct((A_HEADS, nc, A_DQK, A_DV), F32),
         jax.ShapeDtypeStruct((A_HEADS, nc, 1, A_DQK), F32),
         jax.ShapeDtypeStruct((A_HEADS, nc, 1, 1), F32)],
        [pltpu.VMEM((A_DQK, A_DV), F32), pltpu.VMEM((1, A_DQK), F32), pltpu.VMEM((1, 1), F32)],
        ("parallel", "arbitrary"), (z, z, z, li, lf), comm)


def _mlstm_bwd(z, li, lf, cs, ns, ms, dh, comm=None):
    tp = z.shape[0]
    nc = tp // A_CHUNK
    ln = A_CHUNK

    def body(q_ref, k_ref, v_ref, li_ref, lf_ref, cs_ref, ns_ref, ms_ref, dh_ref,
             dq_ref, dk_ref, dv_ref, dli_ref, dlf_ref, dc_s, dn_s, dm_s):
        @pl.when(pl.program_id(1) == 0)
        def _():
            dc_s[...] = jnp.zeros_like(dc_s)
            dn_s[...] = jnp.zeros_like(dn_s)
            dm_s[...] = jnp.zeros_like(dm_s)

        _, vjp = jax.vjp(_mlstm_chunk, cs_ref[0, 0], ns_ref[0, 0], ms_ref[0, 0], q_ref[...], k_ref[...],
                         v_ref[...], li_ref[0], lf_ref[0])
        dc, dn, dm, dq, dk, dv, dli, dlf = vjp((dc_s[...], dn_s[...], dm_s[...], dh_ref[...]))
        dc_s[...] = dc
        dn_s[...] = dn
        dm_s[...] = dm
        dq_ref[...] = dq
        dk_ref[...] = dk
        dv_ref[...] = dv
        dli_ref[0] = dli
        dlf_ref[0] = dlf

    def rj(j):
        return nc - 1 - j

    g_spec = pl.BlockSpec((1, ln, 1), lambda h, j: (h, rj(j), 0))
    return _pcall(
        body, "mlstm_bwd", (A_HEADS, nc),
        [pl.BlockSpec((ln, A_DQK), lambda h, j: (rj(j), h)),
         pl.BlockSpec((ln, A_DQK), lambda h, j: (rj(j), A_HEADS + h)),
         pl.BlockSpec((ln, A_DV), lambda h, j: (rj(j), A_HEADS + h)),
         g_spec, g_spec,
         pl.BlockSpec((1, 1, A_DQK, A_DV), lambda h, j: (h, rj(j), 0, 0)),
         pl.BlockSpec((1, 1, 1, A_DQK), lambda h, j: (h, rj(j), 0, 0)),
         pl.BlockSpec((1, 1, 1, 1), lambda h, j: (h, rj(j), 0, 0)),
         pl.BlockSpec((ln, A_DV), lambda h, j: (rj(j), h))],
        [pl.BlockSpec((ln, A_DQK), lambda h, j: (rj(j), h)),
         pl.BlockSpec((ln, A_DQK), lambda h, j: (rj(j), h)),
         pl.BlockSpec((ln, A_DV), lambda h, j: (rj(j), h)),
         g_spec, g_spec],
        [jax.ShapeDtypeStruct((tp, A_HEADS * A_DQK), F32),
         jax.ShapeDtypeStruct((tp, A_HEADS * A_DQK), F32),
         jax.ShapeDtypeStruct((tp, A_HEADS * A_DV), F32),
         jax.ShapeDtypeStruct((A_HEADS, tp, 1), F32),
         jax.ShapeDtypeStruct((A_HEADS, tp, 1), F32)],
        [pltpu.VMEM((A_DQK, A_DV), F32), pltpu.VMEM((1, A_DQK), F32), pltpu.VMEM((1, 1), F32)],
        ("parallel", "arbitrary"), (z, z, z, li, lf, cs, ns, ms, dh), comm)


def _shift_down(x, s, row):
    return x if s == 0 else jnp.where(row >= s, pltpu.roll(x, s, 0), 0.0)


def _shift_up(x, s, row):
    n = x.shape[0]
    return x if s == 0 else jnp.where(row < n - s, pltpu.roll(x, n - s, 0), 0.0)


def _conv_fwd(z, xcb, w, b):
    tp, c = z.shape[0], w.shape[1]
    ct = 256

    def body(x_ref, w_ref, b_ref, o_ref):
        x = x_ref[...]
        row = lax.broadcasted_iota(jnp.int32, (tp, 1), 0)
        acc = jnp.zeros_like(x) + b_ref[...]
        for k in range(CONV_W):
            acc = acc + w_ref[k:k + 1, :] * _shift_down(x, CONV_W - 1 - k, row)
        o_ref[...] = acc

    return pl.pallas_call(
        body, name="conv_fwd", grid=(c // ct,),
        in_specs=[pl.BlockSpec((tp, ct), lambda j: (0, xcb + j)), pl.BlockSpec((CONV_W, ct), lambda j: (0, j)),
                  pl.BlockSpec((1, ct), lambda j: (0, j))],
        out_specs=pl.BlockSpec((tp, ct), lambda j: (0, j)),
        out_shape=jax.ShapeDtypeStruct((tp, c), F32),
        compiler_params=_cparams(("parallel",)),
    )(z, w, b)


def _conv_bwd(z, xcb, w, dxc):
    tp, c = z.shape[0], w.shape[1]
    ct = 256

    def body(x_ref, w_ref, d_ref, dx_ref, dw_ref, db_ref):
        x, d = x_ref[...], d_ref[...]
        row = lax.broadcasted_iota(jnp.int32, (tp, 1), 0)
        acc = jnp.zeros_like(x)
        for k in range(CONV_W):
            s = CONV_W - 1 - k
            acc = acc + w_ref[k:k + 1, :] * _shift_up(d, s, row)
            dw_ref[k:k + 1, :] = jnp.sum(d * _shift_down(x, s, row), axis=0, keepdims=True)
        dx_ref[...] = acc
        db_ref[...] = jnp.sum(d, axis=0, keepdims=True)

    return pl.pallas_call(
        body, name="conv_bwd", grid=(c // ct,),
        in_specs=[pl.BlockSpec((tp, ct), lambda j: (0, xcb + j)), pl.BlockSpec((CONV_W, ct), lambda j: (0, j)),
                  pl.BlockSpec((tp, ct), lambda j: (0, j))],
        out_specs=[pl.BlockSpec((tp, ct), lambda j: (0, j)), pl.BlockSpec((CONV_W, ct), lambda j: (0, j)),
                   pl.BlockSpec((1, ct), lambda j: (0, j))],
        out_shape=[jax.ShapeDtypeStruct((tp, c), F32), jax.ShapeDtypeStruct((CONV_W, c), F32),
                   jax.ShapeDtypeStruct((1, c), F32)],
        compiler_params=_cparams(("parallel",)),
    )(z, w, dxc)


def _scan_fwd(a, u, name):
    tp, c = a.shape
    ct = _pick(c, (256, 128))
    lb = SCAN_BLOCK
    nb = tp // lb

    def body(a_ref, u_ref, h_ref, hp_ref):
        row = lax.broadcasted_iota(jnp.int32, (lb, 1), 0)

        def blk(j, carry):
            r0 = pl.multiple_of(j * lb, lb)
            aa, uu = a_ref[pl.ds(r0, lb), :], u_ref[pl.ds(r0, lb), :]
            s = 1
            while s < lb:
                mk = row >= s
                uu = jnp.where(mk, aa * pltpu.roll(uu, s, 0) + uu, uu)
                aa = jnp.where(mk, aa * pltpu.roll(aa, s, 0), aa)
                s *= 2
            hh = uu + aa * carry
            h_ref[pl.ds(r0, lb), :] = hh
            hp_ref[pl.ds(r0, lb), :] = jnp.where(row >= 1, pltpu.roll(hh, 1, 0), carry)
            return hh[lb - 1:lb, :]

        lax.fori_loop(0, nb, blk, jnp.zeros((1, ct), F32))

    spec = pl.BlockSpec((tp, ct), lambda j: (0, j))
    return pl.pallas_call(
        body, name=name, grid=(c // ct,), in_specs=[spec, spec], out_specs=[spec, spec],
        out_shape=[jax.ShapeDtypeStruct((tp, c), F32)] * 2,
        compiler_params=_cparams(("parallel",)),
    )(a, u)


def _scan_bwd(a, hprev, dh, name):
    tp, c = a.shape
    ct = _pick(c, (256, 128))
    lb = SCAN_BLOCK
    nb = tp // lb

    def body(a_ref, hp_ref, dh_ref, du_ref, da_ref):
        row = lax.broadcasted_iota(jnp.int32, (lb, 1), 0)

        def blk(jj, carry):
            g_next, a_next = carry
            r0 = pl.multiple_of((nb - 1 - jj) * lb, lb)
            a_blk = a_ref[pl.ds(r0, lb), :]
            aa = jnp.where(row < lb - 1, pltpu.roll(a_blk, lb - 1, 0), a_next)
            gg = dh_ref[pl.ds(r0, lb), :]
            s = 1
            while s < lb:
                mk = row < lb - s
                gg = jnp.where(mk, aa * pltpu.roll(gg, lb - s, 0) + gg, gg)
                aa = jnp.where(mk, aa * pltpu.roll(aa, lb - s, 0), aa)
                s *= 2
            gg = gg + aa * g_next
            du_ref[pl.ds(r0, lb), :] = gg
            da_ref[pl.ds(r0, lb), :] = gg * hp_ref[pl.ds(r0, lb), :]
            return gg[0:1, :], a_blk[0:1, :]

        lax.fori_loop(0, nb, blk, (jnp.zeros((1, ct), F32), jnp.zeros((1, ct), F32)))

    spec = pl.BlockSpec((tp, ct), lambda j: (0, j))
    return pl.pallas_call(
        body, name=name, grid=(c // ct,), in_specs=[spec, spec, spec], out_specs=[spec, spec],
        out_shape=[jax.ShapeDtypeStruct((tp, c), F32)] * 2,
        compiler_params=_cparams(("parallel",)),
    )(a, hprev, dh)


LOG2E = 1.4426950408889634
LN2 = 0.6931471805599453


def _diag_valid(i, tq):
    r = lax.broadcasted_iota(jnp.int32, (tq, tq), 0)
    c = lax.broadcasted_iota(jnp.int32, (tq, tq), 1)
    return (c <= r) & ((i * tq + c >= PAD) | (c == r))


def _key_terms(tp, tq, bk=None):
    pad_neg = jnp.where(jnp.arange(tp) < PAD, NEG, 0.0).astype(F32).reshape(1, tp // tq, 1, tq)
    if bk is None:
        return pad_neg, jnp.zeros_like(pad_neg)
    kd = -bk.reshape(bk.shape[0], tp // tq, 1, tq)
    return kd + pad_neg, kd


def _attn_fwd(q, k, v, vcb, scale, bq, kterms, name, comm=None):
    tp = q.shape[0]
    nh = q.shape[1] // HP
    tq = _pick(tp, (384, 256, 128))
    has_bq = bq is not None
    c1 = scale * LOG2E
    per_head = kterms[0].shape[0] > 1

    def body(*refs):
        if has_bq:
            q_ref, k_ref, v_ref, kb_ref, kd_ref, bq_ref, o_ref, lse_ref = refs
            rb = bq_ref[0] * LOG2E
        else:
            q_ref, k_ref, v_ref, kb_ref, kd_ref, o_ref, lse_ref = refs
        i = pl.program_id(1)
        qb = q_ref[...].astype(BF16)

        def tile(j, carry, diag):
            m, l, acc = carry
            r0 = pl.multiple_of(j * tq, tq)
            kb = k_ref[pl.ds(r0, tq), :].astype(BF16)
            vb = v_ref[pl.ds(r0, tq), :].astype(BF16)
            kt = (kd_ref if diag else kb_ref)[0, j] * LOG2E
            x = lax.dot_general(qb, kb, (((1,), (1,)), ((), ())), preferred_element_type=F32) * c1 + kt
            if has_bq:
                x = x + rb
            if diag:
                x = jnp.where(_diag_valid(i, tq), x, NEG)
            m2 = jnp.maximum(m, jnp.max(x, axis=1, keepdims=True))
            alpha = jnp.exp2(m - m2)
            p = jnp.exp2(x - m2)
            l2 = alpha * l + jnp.sum(p, axis=1, keepdims=True)
            acc2 = alpha * acc + jnp.dot(p.astype(BF16), vb, preferred_element_type=F32)
            return m2, l2, acc2

        init = (jnp.full((tq, 1), NEG, F32), jnp.zeros((tq, 1), F32), jnp.zeros((tq, HP), F32))
        m, l, acc = tile(i, lax.fori_loop(0, i, lambda j, c: tile(j, c, False), init), True)
        o_ref[...] = acc / l
        lse_ref[0] = m * LN2 + jnp.log(l)

    kt_spec = pl.BlockSpec((1, tp // tq, 1, tq), (lambda h, i: (h, 0, 0, 0)) if per_head else (lambda h, i: (0, 0, 0, 0)))
    in_specs = [pl.BlockSpec((tq, HP), lambda h, i: (i, h)), pl.BlockSpec((tp, HP), lambda h, i: (0, h)),
                pl.BlockSpec((tp, HP), lambda h, i: (0, vcb + h)), kt_spec, kt_spec]
    ins = [q, k, v, *kterms]
    if has_bq:
        in_specs += [pl.BlockSpec((1, tq, 1), lambda h, i: (h, i, 0))]
        ins += [bq]
    return _pcall(
        body, name, (nh, tp // tq), in_specs,
        [pl.BlockSpec((tq, HP), lambda h, i: (i, h)), pl.BlockSpec((1, tq, 1), lambda h, i: (h, i, 0))],
        [jax.ShapeDtypeStruct((tp, nh * HP), F32), jax.ShapeDtypeStruct((nh, tp, 1), F32)],
        [], ("parallel", "parallel"), ins, comm)


def _attn_bwd(q, k, v, vcb, o, lse, do, docb, scale, bq, kterms, name, comm=None):
    tp = q.shape[0]
    nh = q.shape[1] // HP
    tq = _pick(tp, (384, 256, 128))
    has_bq = bq is not None
    c1 = scale * LOG2E
    per_head = kterms[0].shape[0] > 1

    def body(*refs):
        if has_bq:
            (q_ref, k_ref, v_ref, o_ref, lse_ref, do_ref, kb_ref, kd_ref, bq_ref,
             dq_ref, dk_ref, dv_ref, dbq_ref, dkt_ref) = refs
            rb = (bq_ref[0] - lse_ref[0]) * LOG2E
        else:
            q_ref, k_ref, v_ref, o_ref, lse_ref, do_ref, kb_ref, kd_ref, dq_ref, dk_ref, dv_ref = refs
            rb = lse_ref[0] * (-LOG2E)
        i = pl.program_id(1)

        @pl.when(i == 0)
        def _():
            dk_ref[...] = jnp.zeros_like(dk_ref)
            dv_ref[...] = jnp.zeros_like(dv_ref)
            if has_bq:
                dkt_ref[...] = jnp.zeros_like(dkt_ref)

        qb = q_ref[...].astype(BF16)
        do_ = do_ref[...]
        dob = do_.astype(BF16)
        delta = jnp.sum(do_ * o_ref[...], axis=1, keepdims=True)

        def tile(j, carry, diag):
            dq, dbq = carry
            r0 = pl.multiple_of(j * tq, tq)
            kb = k_ref[pl.ds(r0, tq), :].astype(BF16)
            vb = v_ref[pl.ds(r0, tq), :].astype(BF16)
            kt = (kd_ref if diag else kb_ref)[0, j] * LOG2E
            x = lax.dot_general(qb, kb, (((1,), (1,)), ((), ())), preferred_element_type=F32) * c1 + kt + rb
            if diag:
                x = jnp.where(_diag_valid(i, tq), x, NEG)
            p = jnp.exp2(x)
            dp = lax.dot_general(dob, vb, (((1,), (1,)), ((), ())), preferred_element_type=F32)
            ds = p * (dp - delta)
            dsb = ds.astype(BF16)
            dk_ref[pl.ds(r0, tq), :] += lax.dot_general(dsb, qb, (((0,), (0,)), ((), ())),
                                                        preferred_element_type=F32) * scale
            dv_ref[pl.ds(r0, tq), :] += lax.dot_general(p.astype(BF16), dob, (((0,), (0,)), ((), ())),
                                                        preferred_element_type=F32)
            if has_bq:
                dkt_ref[0, j] += jnp.sum(ds, axis=0, keepdims=True)
                dbq = dbq + jnp.sum(ds, axis=1, keepdims=True)
            return dq + jnp.dot(dsb, kb, preferred_element_type=F32), dbq

        init = (jnp.zeros((tq, HP), F32), jnp.zeros((tq, 1), F32))
        dq, dbq = tile(i, lax.fori_loop(0, i, lambda j, c: tile(j, c, False), init), True)
        dq_ref[...] = dq * scale
        if has_bq:
            dbq_ref[0] = dbq

    blk_q = pl.BlockSpec((tq, HP), lambda h, i: (i, h))
    blk_k = pl.BlockSpec((tp, HP), lambda h, i: (0, h))
    kt_spec = pl.BlockSpec((1, tp // tq, 1, tq), (lambda h, i: (h, 0, 0, 0)) if per_head else (lambda h, i: (0, 0, 0, 0)))
    in_specs = [blk_q, blk_k, pl.BlockSpec((tp, HP), lambda h, i: (0, vcb + h)), blk_q,
                pl.BlockSpec((1, tq, 1), lambda h, i: (h, i, 0)), pl.BlockSpec((tq, HP), lambda h, i: (i, docb + h)),
                kt_spec, kt_spec]
    ins = [q, k, v, o, lse, do, *kterms]
    out_specs = [blk_q, blk_k, blk_k]
    out_shape = [jax.ShapeDtypeStruct((tp, nh * HP), F32)] * 3
    if has_bq:
        in_specs += [pl.BlockSpec((1, tq, 1), lambda h, i: (h, i, 0))]
        ins += [bq]
        out_specs += [pl.BlockSpec((1, tq, 1), lambda h, i: (h, i, 0)), kt_spec]
        out_shape += [jax.ShapeDtypeStruct((nh, tp, 1), F32), jax.ShapeDtypeStruct((nh, tp // tq, 1, tq), F32)]
    return _pcall(body, name, (nh, tp // tq), in_specs, out_specs, out_shape, [], ("parallel", "arbitrary"), ins, comm)


def _loss_head(h, tgt):
    tp, d = h.shape
    tm = 128
    first = (PAD + N_META) // tm

    def body(h_ref, t_ref, l_ref, d_ref):
        i = pl.program_id(0)

        @pl.when(i == 0)
        def _():
            l_ref[...] = jnp.zeros_like(l_ref)

        live = i >= first
        err = jnp.where(live, h_ref[...] - t_ref[...], 0.0)
        d_ref[...] = err * (1.0 / d)
        l_ref[...] += (0.5 / d) * jnp.sum(err * err)

    return pl.pallas_call(
        body, name="loss_head", grid=(tp // tm,),
        in_specs=[pl.BlockSpec((tm, d), lambda i: (i, 0)), pl.BlockSpec((tm, d), lambda i: (i, 0))],
        out_specs=[_full_spec((8, 128)), pl.BlockSpec((tm, d), lambda i: (i, 0))],
        out_shape=[jax.ShapeDtypeStruct((8, 128), F32), jax.ShapeDtypeStruct((tp, d), F32)],
        compiler_params=_cparams(("arbitrary",)),
    )(h, tgt)


def _rope_tables(pos_rows):
    half = C_ROPE // 2
    freqs = ROPE_THETA ** (-jnp.arange(half, dtype=F32) / half)
    ang = pos_rows[:, None].astype(F32) * freqs
    cos, sin = jnp.cos(ang), jnp.sin(ang)
    tp = pos_rows.shape[0]
    one, zero = jnp.ones((tp, C_NOPE), F32), jnp.zeros((tp, C_NOPE), F32)
    tail1, tail0 = jnp.ones((tp, HP - C_NOPE - C_ROPE), F32), jnp.zeros((tp, HP - C_NOPE - C_ROPE), F32)
    return (jnp.concatenate([one, cos, cos, tail1], axis=1), jnp.concatenate([zero, sin, sin, tail0], axis=1))


def _heads_to_cols(x, lo, n):
    t = x[:, lo:lo + n].T
    return t[:, :, None], t[:, None, :]


def _local_step(x, positions, tgt, w, plan=None):
    def hosted(tag, fn, *args):
        ops = plan.ride(tag, g) if plan is not None else None
        if not ops:
            return fn(*args)
        res, got = fn(*args, comm=ops)
        plan.arrived(tag, got, w)
        return res

    s_len = x.shape[0]
    tp = PAD + N_META + s_len
    tm = _pick(tp, (384, 256, 128))
    front = PAD + N_META
    h0 = jnp.concatenate([jnp.zeros((PAD, D_MODEL), F32), w["meta"], x], axis=0)
    tgt_p = jnp.concatenate([jnp.zeros((front, D_MODEL), F32), tgt], axis=0)
    pos_rows = jnp.concatenate([jnp.zeros((PAD,), jnp.int32), jnp.arange(N_META, dtype=jnp.int32),
                                positions + N_META])
    cos, sin = _rope_tables(pos_rows)
    g = {}

    r_h0 = [(h0, D_MODEL, 0)]
    (xn0,) = _row_fwd("norm0_f", _f_norm, r_h0, [w["ev_ln"]], [(D_MODEL, BF16)], tm)
    z0 = hosted("mm_z0", _mm, xn0, w["ev_w_in"], "nn", "mm_z0")
    r_misc0 = [(z0, HP, 5120 // HP)]
    (g0,) = _row_fwd("gate0_f", _f_gate0, r_misc0, [w["ev_b_if"]], [(HP, F32)], tm)
    li, _ = _heads_to_cols(g0, 0, A_HEADS)
    lf, _ = _heads_to_cols(g0, A_HEADS, A_HEADS)
    h_a, cs, ns, ms = hosted("mlstm_fwd", _mlstm_fwd, z0, li, lf)
    r_aout = [(h_a, 1024, 0), (z0, 1024, 2)]
    (ha,) = _row_fwd("aout_f", _f_aout, r_aout, [w["ev_a_norm"]], [(1024, BF16)], tm)
    xc = _conv_fwd(z0, 3072 // 256, w["ev_conv_w"], w["ev_conv_b"])
    p_gates = [w["ev_w_ra"], w["ev_b_ra"], w["ev_w_rx"], w["ev_b_rx"], w["ev_lam"]]
    a_g, u_g = _row_fwd("gates_f", _f_gates, [(xc, 1024, 0)], p_gates, [(1024, F32), (1024, F32)], tm)
    hs, hprev = _scan_fwd(a_g, u_g, "lru_f")
    r_bout = [(hs, 1024, 0), (z0, 1024, 4)]
    (hb,) = _row_fwd("bout_f", _f_bout, r_bout, [], [(1024, BF16)], tm)
    hab = jnp.concatenate([ha, hb], axis=1)
    h1 = _mm(hab, w["ev_w_out"], "nn", "mm_h1", add=h0)
    (xn1,) = _row_fwd("norm1_f", _f_norm, [(h1, D_MODEL, 0)], [w["mlp_ln0"]], [(D_MODEL, BF16)], tm)
    p0 = _mm(xn1, w["w_ff1_0"], "nn", "mm_p0")
    (act0,) = _row_fwd("relu0_f", _f_relu2, [(p0, 4096, 0)], [], [(4096, BF16)], tm)
    h2 = _mm(act0, w["w_ff2_0"], "nn", "mm_h2", add=h1)
    (xn2,) = _row_fwd("norm2_f", _f_norm, [(h2, D_MODEL, 0)], [w["od_ln"]], [(D_MODEL, BF16)], tm)
    z1 = _mm(xn2, w["od_w_in"], "nn", "mm_z1")
    r_c = [(z1, C_Q_LORA, 3072 // C_Q_LORA), (z1, C_KV_LORA, 3584 // C_KV_LORA)]
    cqn, ckvn = _row_fwd("cnorm_f", _f_cnorm, r_c, [w["od_g_qa"], w["od_g_kva"]],
                         [(C_Q_LORA, BF16), (C_KV_LORA, BF16)], tm)
    q_ = _mm(cqn, w["od_w_uq"], "nn", "mm_q")
    kv_ = _mm(ckvn, w["od_w_ukv"], "nn", "mm_kv")
    r_mla = [(q_, 1024, 0), (kv_, 1024, 0), (z1, HP, 3840 // HP), (cos, HP, 0), (sin, HP, 0)]
    p_mla = [w["gq_full"], w["gk_full"]]
    qm, km = _row_fwd("mla_f", _f_mlaprep, r_mla, p_mla, [(1024, BF16), (1024, BF16)], tm)
    sc_c = (C_NOPE + C_ROPE) ** -0.5
    kt_c = _key_terms(tp, tm)
    hc, lse_c = hosted("mla_attn_f", _attn_fwd, qm, km, kv_, C_HEADS, sc_c, None, kt_c, "mla_attn_f")
    r_fox = [(z1, 1024, 0), (z1, 1024, 1), (z1, HP, 3840 // HP)]
    p_fox = [w["gfq_full"], w["gfk_full"], w["bf_full"]]
    qf, kf, lfx = _row_fwd("fox_f", _f_foxprep, r_fox, p_fox, [(1024, BF16), (1024, BF16), (HP, F32)], tm)
    ones = jnp.ones((tp, HP), F32)
    fcum, fprev = _scan_fwd(ones, lfx, "fcum_f")
    bq, bk = _heads_to_cols(fcum, 96, D_HEADS)
    kt_d = _key_terms(tp, tm, bk)
    sc_d = D_HD ** -0.5
    hd, lse_d = _attn_fwd(qf, kf, z1, 2048 // HP, sc_d, bq, kt_d, "fox_attn_f")
    hcd = jnp.concatenate([hc, hd], axis=1)
    h3 = _mm(hcd, w["od_w_out"], "nn", "mm_h3", add=h2)
    (xn3,) = _row_fwd("norm3_f", _f_norm, [(h3, D_MODEL, 0)], [w["mlp_ln1"]], [(D_MODEL, BF16)], tm)
    p1 = _mm(xn3, w["w_ff1_1"], "nn", "mm_p1")
    (act1,) = _row_fwd("relu1_f", _f_relu2, [(p1, 4096, 0)], [], [(4096, BF16)], tm)
    h4 = _mm(act1, w["w_ff2_1"], "nn", "mm_h4", add=h3)
    lpart, dh4 = _loss_head(h4, tgt_p)
    loss = lpart[0, 0]

    def mlp_bwd(tag, dh_out, h_in, xn, p, act, ln, w1, w2):
        dact = _mm(dh_out, w2, "nt", f"mm_dact{tag}")
        g_w2 = _mm(act, dh_out, "tn", f"mm_dw2_{tag}")
        (dp,), _ = _row_bwd(f"relu{tag}_b", _f_relu2, [(p, 4096, 0)], [], [(dact, 4096, 0)], tm, [True])
        g_w1 = _mm(xn, dp, "tn", f"mm_dw1_{tag}")
        dxn = _mm(dp, w1, "nt", f"mm_dxn{tag}")
        (dh_in,), (g_ln,) = _row_bwd(f"normm{tag}_b", _f_norm, [(h_in, D_MODEL, 0)], [ln], [(dxn, D_MODEL, 0)], tm,
                                     [True], add=dh_out)
        return dh_in, g_ln, g_w1, g_w2

    dh3, g["mlp_ln1"], g["w_ff1_1"], g["w_ff2_1"] = mlp_bwd("1", dh4, h3, xn3, p1, act1, w["mlp_ln1"],
                                                            w["w_ff1_1"], w["w_ff2_1"])
    dhcd = _mm(dh3, w["od_w_out"], "nt", "mm_dhcd")
    g["od_w_out"] = _mm(hcd, dh3, "tn", "mm_dwout1")
    dqf, dkf, dvf, dbq, dkt = hosted("fox_attn_b", _attn_bwd, qf, kf, z1, 2048 // HP, hd, lse_d, dhcd, D_HEADS, sc_d,
                                     bq, kt_d, "fox_attn_b")
    dfc = dbq[:, :, 0].T - dkt.reshape(D_HEADS, tp).T
    dfcum = jnp.concatenate([jnp.zeros((tp, 96), F32), dfc, jnp.zeros((tp, HP - 96 - D_HEADS), F32)], axis=1)
    dlfx, _ = _scan_bwd(ones, fprev, dfcum, "fcum_b")
    (dfq, dfk, dmisc_f), (g["gfq_full"], g["gfk_full"], g["bf_full"]) = _row_bwd(
        "fox_b", _f_foxprep, r_fox, p_fox, [(dqf, 1024, 0), (dkf, 1024, 0), (dlfx, HP, 0)], tm, [True, True, True])
    dqm, dkm, dvm = _attn_bwd(qm, km, kv_, C_HEADS, hc, lse_c, dhcd, 0, sc_c, None, kt_c, "mla_attn_b")
    (dq_, dkk_, dmisc_m), (g["gq_full"], g["gk_full"]) = _row_bwd(
        "mla_b", _f_mlaprep, r_mla, p_mla, [(dqm, 1024, 0), (dkm, 1024, 0)], tm, [True, True, True, False, False])
    dkv_ = jnp.concatenate([dkk_, dvm], axis=1)
    dckvn = _mm(dkv_, w["od_w_ukv"], "nt", "mm_dckvn")
    g["od_w_ukv"] = _mm(ckvn, dkv_, "tn", "mm_dwukv")
    dcqn = _mm(dq_, w["od_w_uq"], "nt", "mm_dcqn")
    g["od_w_uq"] = _mm(cqn, dq_, "tn", "mm_dwuq")
    (dcq, dckv), (g["od_g_qa"], g["od_g_kva"]) = _row_bwd(
        "cnorm_b", _f_cnorm, r_c, [w["od_g_qa"], w["od_g_kva"]],
        [(dcqn, C_Q_LORA, 0), (dckvn, C_KV_LORA, 0)], tm, [True, True])
    zpad = jnp.zeros((tp, HP), F32)
    dz1 = jnp.concatenate([dfq, dfk, dvf, dcq, zpad, dckv, dmisc_f + dmisc_m, zpad], axis=1)
    g["od_w_in"] = _mm(xn2, dz1, "tn", "mm_dwin1")
    dxn2 = _mm(dz1, w["od_w_in"], "nt", "mm_dxn2")
    (dh2,), (g["od_ln"],) = _row_bwd("norm2_b", _f_norm, [(h2, D_MODEL, 0)], [w["od_ln"]], [(dxn2, D_MODEL, 0)], tm,
                                     [True], add=dh3)
    dh1, g["mlp_ln0"], g["w_ff1_0"], g["w_ff2_0"] = mlp_bwd("0", dh2, h1, xn1, p0, act0, w["mlp_ln0"],
                                                            w["w_ff1_0"], w["w_ff2_0"])
    dhab = _mm(dh1, w["ev_w_out"], "nt", "mm_dhab")
    g["ev_w_out"] = _mm(hab, dh1, "tn", "mm_dwout0")
    (dhs, dgb), _ = _row_bwd("bout_b", _f_bout, r_bout, [], [(dhab, 1024, 1)], tm, [True, True])
    du_g, da_g = _scan_bwd(a_g, hprev, dhs, "lru_b")
    (dxc,), (g["ev_w_ra"], g["ev_b_ra"], g["ev_w_rx"], g["ev_b_rx"], g["ev_lam"]) = _row_bwd(
        "gates_b", _f_gates, [(xc, 1024, 0)], p_gates, [(da_g, 1024, 0), (du_g, 1024, 0)], tm, [True])
    dxb, g["ev_conv_w"], g["ev_conv_b"] = _conv_bwd(z0, 3072 // 256, w["ev_conv_w"], dxc)
    (dh_a, do_), (g["ev_a_norm"],) = _row_bwd("aout_b", _f_aout, r_aout, [w["ev_a_norm"]], [(dhab, 1024, 0)], tm,
                                              [True, True])
    dq, dk, dv, dli, dlf = hosted("mlstm_bwd", _mlstm_bwd, z0, li, lf, cs, ns, ms, dh_a)
    dg0 = jnp.concatenate([dli[:, :, 0].T, dlf[:, :, 0].T, jnp.zeros((tp, HP - 2 * A_HEADS), F32)], axis=1)
    (dmisc0,), (g["ev_b_if"],) = _row_bwd("gate0_b", _f_gate0, r_misc0, [w["ev_b_if"]], [(dg0, HP, 0)], tm, [True])
    dz0 = jnp.concatenate([dq, dk, dv, do_, dxb, dgb, dmisc0, zpad], axis=1)
    g["ev_w_in"] = _mm(xn0, dz0, "tn", "mm_dwin0")
    dxn0 = _mm(dz0, w["ev_w_in"], "nt", "mm_dxn0")
    (dh0,), (g["ev_ln"],) = _row_bwd("norm0_b", _f_norm, r_h0, [w["ev_ln"]], [(dxn0, D_MODEL, 0)], tm, [True], add=dh1)
    g["meta"] = dh0[PAD:front]
    return loss, dh0[front:], g


def _pad_last(a, n):
    return jnp.pad(a, [(0, 0)] * (a.ndim - 1) + [(0, n - a.shape[-1])])


def _pad_heads(a, nh, d):
    return _pad_last(a.reshape(a.shape[:-1] + (nh, d)), HP).reshape(a.shape[:-1] + (nh * HP,))


def _unpad_heads(a, nh, d):
    return a.reshape(a.shape[:-1] + (nh, HP))[..., :d].reshape(a.shape[:-1] + (nh * d,))


_MM_UNITS = ("ev_w_in", "ev_w_out", "od_w_in", "od_w_uq", "od_w_ukv", "od_w_out",
             "w_ff1_0", "w_ff2_0", "w_ff1_1", "w_ff2_1")


def _mw_pad(name, a):
    if name == "ev_w_in":
        return _pad_last(jnp.concatenate([a[:, :3072], a[:, 3080:5128], a[:, 3072:3080]], axis=1), ZE)
    if name == "od_w_in":
        z = lambda n: jnp.zeros((a.shape[0], n), a.dtype)
        return jnp.concatenate(
            [_pad_heads(a[:, 672:1184], D_HEADS, D_HD), _pad_heads(a[:, 1184:1696], D_HEADS, D_HD),
             _pad_heads(a[:, 1696:2208], D_HEADS, D_HD), a[:, 0:384], z(128), a[:, 384:640],
             z(64), a[:, 640:672], a[:, 2208:2216], z(24), z(128)], axis=1)
    if name == "od_w_uq":
        return _pad_heads(a, C_HEADS, C_NOPE + C_ROPE)
    if name == "od_w_ukv":
        wkv = a.reshape(C_KV_LORA, C_HEADS, C_NOPE + C_V)
        return jnp.concatenate([_pad_last(wkv[:, :, :C_NOPE], HP).reshape(C_KV_LORA, -1),
                                _pad_last(wkv[:, :, C_NOPE:], HP).reshape(C_KV_LORA, -1)], axis=1)
    if name == "od_w_out":
        return jnp.pad(a.reshape(2 * C_HEADS, C_V, D_MODEL), ((0, 0), (0, HP - C_V), (0, 0))).reshape(-1, D_MODEL)
    return a


def _mw_unpad(name, g):
    if name == "ev_w_in":
        return jnp.concatenate([g[:, :3072], g[:, 5120:5128], g[:, 3072:5120]], axis=1)
    if name == "od_w_in":
        return jnp.concatenate(
            [g[:, 3072:3456], g[:, 3584:3840], g[:, 3904:3936], _unpad_heads(g[:, 0:1024], D_HEADS, D_HD),
             _unpad_heads(g[:, 1024:2048], D_HEADS, D_HD), _unpad_heads(g[:, 2048:3072], D_HEADS, D_HD),
             g[:, 3936:3944]], axis=1)
    if name == "od_w_uq":
        return _unpad_heads(g, C_HEADS, C_NOPE + C_ROPE)
    if name == "od_w_ukv":
        gk = g[:, :C_HEADS * HP].reshape(C_KV_LORA, C_HEADS, HP)[:, :, :C_NOPE]
        gv = g[:, C_HEADS * HP:].reshape(C_KV_LORA, C_HEADS, HP)[:, :, :C_V]
        return jnp.concatenate([gk, gv], axis=2).reshape(C_KV_LORA, -1)
    if name == "od_w_out":
        return g.reshape(2 * C_HEADS, HP, D_MODEL)[:, :C_V].reshape(-1, D_MODEL)
    return g


def _prep_weights(p):
    w = {}
    w["meta"] = p["meta"]
    for k in ("ev_ln", "ev_a_norm", "ev_conv_b", "ev_b_ra", "ev_b_rx", "ev_lam", "od_ln", "od_g_qa", "od_g_kva"):
        w[k] = p[k].reshape(1, -1)
    w["ev_b_if"] = _pad_last(p["ev_b_if"].reshape(1, -1), HP)
    w["ev_conv_w"] = p["ev_conv_w"][0]
    w["ev_w_ra"] = p["ev_w_ra"][0]
    w["ev_w_rx"] = p["ev_w_rx"][0]
    f1 = lambda a: a.reshape(1, -1)
    w["gq_full"] = _pad_last(jnp.concatenate([f1(p["od_g_qn"]), f1(p["od_g_qr"])], axis=1), HP)
    w["gk_full"] = _pad_last(jnp.concatenate([f1(p["od_g_kn"]), f1(p["od_g_kr"])], axis=1), HP)
    w["gfq_full"] = _pad_last(f1(p["od_g_fq"]), HP)
    w["gfk_full"] = _pad_last(f1(p["od_g_fk"]), HP)
    w["bf_full"] = _pad_last(jnp.concatenate([jnp.zeros((1, 96), F32), f1(p["od_b_f"])], axis=1), HP)
    for l in (0, 1):
        w[f"mlp_ln{l}"] = p["mlp_ln"][l:l + 1]
    for n in _MM_UNITS:
        if n in p:
            w[n] = _mw_pad(n, p[n])
    return w


def _unprep_grads(g):
    o = {}
    o["meta"] = g["meta"]
    for k in ("ev_ln", "ev_a_norm", "ev_conv_b", "ev_b_ra", "ev_b_rx", "ev_lam", "od_ln", "od_g_qa", "od_g_kva"):
        o[k] = g[k].reshape(1, -1)
    o["ev_b_if"] = g["ev_b_if"][:, :2 * A_HEADS]
    o["ev_conv_w"] = g["ev_conv_w"][None]
    o["ev_w_ra"] = g["ev_w_ra"][None]
    o["ev_w_rx"] = g["ev_w_rx"][None]
    o["od_g_qn"] = g["gq_full"][:, :C_NOPE]
    o["od_g_qr"] = g["gq_full"][:, C_NOPE:C_NOPE + C_ROPE]
    o["od_g_kn"] = g["gk_full"][:, :C_NOPE]
    o["od_g_kr"] = g["gk_full"][:, C_NOPE:C_NOPE + C_ROPE]
    o["od_g_fq"] = g["gfq_full"][:, :D_HD]
    o["od_g_fk"] = g["gfk_full"][:, :D_HD]
    o["od_b_f"] = g["bf_full"][:, 96:96 + D_HEADS]
    o["mlp_ln"] = jnp.concatenate([g["mlp_ln0"], g["mlp_ln1"]], axis=0)
    return o


def _exchange(ops, name):
    n_ops = len(ops)
    flags = [s for _, s in ops]

    def body(*refs):
        local, remote = _comm_copies(refs[:n_ops], refs[n_ops:2 * n_ops], flags, *refs[2 * n_ops:])
        for cp in local + remote:
            cp.start()
        _comm_wait(local, remote)

    any_spec = pl.BlockSpec(memory_space=pl.ANY)
    c_shape, c_sems = _comm_shapes(ops)
    return pl.pallas_call(body, name=name, out_shape=c_shape, in_specs=[any_spec] * n_ops,
                          out_specs=[any_spec] * n_ops, scratch_shapes=c_sems)(*[x for x, _ in ops])


def _adamw(parts, w, m, v, name):
    r, c = w.shape
    tr = r
    for cand in (512, 256, 128, 64, 32, 16):
        if r % cand == 0 and N_DEV * cand * c * 4 <= 4 * 1024 * 1024:
            tr = cand
            break
    c1 = 1.0 / (1.0 - ADAM_B1 ** ADAM_STEP)
    c2 = 1.0 / (1.0 - ADAM_B2 ** ADAM_STEP)

    def body(p_ref, w_ref, m_ref, v_ref, g_ref, d_ref, mo_ref, vo_ref):
        g = p_ref[0].astype(F32)
        for j in range(1, N_DEV):
            g = g + p_ref[j].astype(F32)
        m2 = ADAM_B1 * m_ref[...] + (1.0 - ADAM_B1) * g
        v2 = ADAM_B2 * v_ref[...] + (1.0 - ADAM_B2) * (g * g)
        g_ref[...] = g
        mo_ref[...] = m2
        vo_ref[...] = v2
        d_ref[...] = -ADAM_LR * ((m2 * c1) / (jnp.sqrt(v2 * c2) + ADAM_EPS) + ADAM_WD * w_ref[...])

    spec = pl.BlockSpec((tr, c), lambda i: (i, 0))
    return pl.pallas_call(
        body, name=name, grid=(r // tr,),
        in_specs=[pl.BlockSpec((N_DEV, tr, c), lambda i: (0, i, 0)), spec, spec, spec],
        out_specs=[spec] * 4, out_shape=[jax.ShapeDtypeStruct((r, c), F32)] * 4,
        compiler_params=_cparams(("parallel",)),
    )(parts, w, m, v)


def _rows_for(n, mult):
    return -(-n // (1024 * mult)) * mult


def _pack(arrs, mult, lead=()):
    nl = len(lead)
    flat = jnp.concatenate([a.reshape(lead + (-1,)) for a in arrs], axis=nl)
    rows = _rows_for(flat.shape[-1], mult)
    return jnp.pad(flat, [(0, 0)] * nl + [(0, rows * 1024 - flat.shape[-1])]).reshape(lead + (rows, 1024))


def _unpack(buf, shapes):
    lead = buf.shape[:-2]
    flat = buf.reshape(lead + (-1,))
    out, off = [], 0
    for s in shapes:
        n = 1
        for d_ in s:
            n *= d_
        out.append(flat[..., off:off + n].reshape(lead + tuple(s)))
        off += n
    return out


def _unshard(g8, ax):
    a = jnp.moveaxis(g8, 0, ax)
    return a.reshape(a.shape[:ax] + (N_DEV * a.shape[ax + 1],) + a.shape[ax + 2:])


def _shard8(full, ax):
    s = full.shape
    return jnp.moveaxis(full.reshape(s[:ax] + (N_DEV, s[ax] // N_DEV) + s[ax + 1:]), ax, 0)


_NAMES = ["meta", "ev_ln", "ev_w_in", "ev_b_if", "ev_a_norm", "ev_conv_w", "ev_conv_b", "ev_w_ra", "ev_b_ra",
          "ev_w_rx", "ev_b_rx", "ev_lam", "ev_w_out", "od_ln", "od_w_in", "od_b_f", "od_g_qa", "od_g_kva",
          "od_w_uq", "od_w_ukv", "od_g_qn", "od_g_qr", "od_g_kn", "od_g_kr", "od_g_fq", "od_g_fk", "od_w_out",
          "mlp_ln", "w_ff1", "w_ff2"]
_SHARD_AXIS = {"meta": 1, "ev_w_in": 2, "ev_conv_w": 2, "ev_w_out": 1, "od_ln": 1, "od_w_in": 2, "od_g_qa": 1,
               "od_g_kva": 1, "od_w_uq": 2, "od_w_ukv": 2, "od_w_out": 1, "w_ff1": 2, "w_ff2": 1}
_MATMUL_WEIGHTS = ("ev_w_in", "ev_w_out", "od_w_in", "od_w_uq", "od_w_ukv", "od_w_out", "w_ff1", "w_ff2")
_BIG_REPL = ("ev_w_ra", "ev_w_rx")
_COL_SHARDED = ("ev_w_in", "od_w_in", "od_w_uq", "od_w_ukv", "w_ff1_0", "w_ff1_1")
_GATHER_ON = {"mm_z0": ("od_w_in", "od_w_uq", "od_w_ukv"), "mlstm_fwd": ("ev_w_out", "w_ff1_0", "w_ff2_0"),
              "mla_attn_f": ("od_w_out", "w_ff1_1", "w_ff2_1")}
_SCATTER_ON = {"fox_attn_b": ("w_ff2_1", "w_ff1_1", "od_w_out"),
               "mlstm_bwd": ("od_w_ukv", "od_w_uq", "od_w_in", "w_ff2_0", "w_ff1_0", "ev_w_out")}
_REPL_ON = "mlstm_bwd"


def _unit_of(d, n):
    return d[n[:-2]][int(n[-1])] if n.startswith("w_ff") else d[n][0]


def _unit_full(n, g8):
    return jnp.transpose(g8, (1, 0, 2)).reshape(g8.shape[1], -1) if n in _COL_SHARDED else g8.reshape(-1, g8.shape[2])


def _unit_slots(n, full, r, c):
    return full.reshape(r, N_DEV, c).transpose(1, 0, 2) if n in _COL_SHARDED else full.reshape(N_DEV, r, c)


class _Plan:
    def __init__(self, shards):
        self.shards = shards
        self.parts = {}

    def slots(self, n, g):
        r, c = self.shards[n].shape
        return _unit_slots(n, _mw_unpad(n, g[n]), r, c).astype(BF16)

    def ride(self, tag, g):
        if tag in _GATHER_ON:
            return [(self.shards[n].astype(BF16), False) for n in _GATHER_ON[tag]]
        ops = [(self.slots(n, g), True) for n in _SCATTER_ON[tag]]
        if tag == _REPL_ON:
            ops += [(g[n].reshape(-1, B_BLOCK), False) for n in _BIG_REPL]
        return ops

    def arrived(self, tag, got, w):
        if tag in _GATHER_ON:
            for n, g8 in zip(_GATHER_ON[tag], got):
                w[n] = _mw_pad(n, _unit_full(n, g8))
        else:
            self.parts.update(zip(_SCATTER_ON[tag] + (_BIG_REPL if tag == _REPL_ON else ()), got))


def kernel(x, positions, meta, ev_ln, ev_w_in, ev_b_if, ev_a_norm, ev_conv_w, ev_conv_b, ev_w_ra, ev_b_ra, ev_w_rx, ev_b_rx, ev_lam, ev_w_out, od_ln, od_w_in, od_b_f, od_g_qa, od_g_kva, od_w_uq, od_w_ukv, od_g_qn, od_g_qr, od_g_kn, od_g_kr, od_g_fq, od_g_fk, od_w_out, mlp_ln, w_ff1, w_ff2, loss_target, m_meta, m_ev_ln, m_ev_w_in, m_ev_b_if, m_ev_a_norm, m_ev_conv_w, m_ev_conv_b, m_ev_w_ra, m_ev_b_ra, m_ev_w_rx, m_ev_b_rx, m_ev_lam, m_ev_w_out, m_od_ln, m_od_w_in, m_od_b_f, m_od_g_qa, m_od_g_kva, m_od_w_uq, m_od_w_ukv, m_od_g_qn, m_od_g_qr, m_od_g_kn, m_od_g_kr, m_od_g_fq, m_od_g_fk, m_od_w_out, m_mlp_ln, m_w_ff1, m_w_ff2, v_meta, v_ev_ln, v_ev_w_in, v_ev_b_if, v_ev_a_norm, v_ev_conv_w, v_ev_conv_b, v_ev_w_ra, v_ev_b_ra, v_ev_w_rx, v_ev_b_rx, v_ev_lam, v_ev_w_out, v_od_ln, v_od_w_in, v_od_b_f, v_od_g_qa, v_od_g_kva, v_od_w_uq, v_od_w_ukv, v_od_g_qn, v_od_g_qr, v_od_g_kn, v_od_g_kr, v_od_g_fq, v_od_g_fk, v_od_w_out, v_mlp_ln, v_w_ff1, v_w_ff2):
    given = dict(locals())
    wts = {n: given[n] for n in _NAMES}
    mom = {n: given["m_" + n] for n in _NAMES}
    var = {n: given["v_" + n] for n in _NAMES}
    small_sh = [n for n in _NAMES if n in _SHARD_AXIS and n not in _MATMUL_WEIGHTS]
    small_rp = [n for n in _NAMES if n not in _SHARD_AXIS and n not in _BIG_REPL]
    shp = {n: wts[n].shape for n in _NAMES}
    plan = _Plan({n: _unit_of(wts, n) for n in _MM_UNITS})

    got = _exchange([(plan.shards["ev_w_in"].astype(BF16), False), (_pack([wts[n] for n in small_sh], 8), False)],
                    "gather_first")
    p = {n: wts[n] for n in _NAMES if n not in _SHARD_AXIS}
    for n, a in zip(small_sh, _unpack(got[1], [shp[n] for n in small_sh])):
        p[n] = _unshard(a, _SHARD_AXIS[n])
    p["ev_w_in"] = _unit_full("ev_w_in", got[0])

    loss, gx, g = _local_step(x[0], positions[0], loss_target[0], _prep_weights(p), plan)
    grads = _unprep_grads(g)

    parts = _exchange([(plan.slots("ev_w_in", g), True),
                       (_pack([_shard8(grads[n], _SHARD_AXIS[n]) for n in small_sh], 8, (N_DEV,)), True),
                       (_pack([grads[n].reshape(shp[n]) for n in small_rp], 8), False)], "exchange_last")
    plan.parts["ev_w_in"] = parts[0]

    res = {}
    unit_res = {n: _adamw(plan.parts[n], *[_unit_of(d, n) for d in (wts, mom, var)], f"adamw_{n}")
                for n in _MM_UNITS}
    for n in _MATMUL_WEIGHTS:
        if n.startswith("w_ff"):
            res[n] = [jnp.stack([unit_res[n + "_0"][k], unit_res[n + "_1"][k]]) for k in range(4)]
        else:
            res[n] = [r[None] for r in unit_res[n]]
    r4 = _adamw(parts[1], *[_pack([d[n] for n in small_sh], 8) for d in (wts, mom, var)], "adamw_small_sharded")
    for n, *four in zip(small_sh, *[_unpack(r, [shp[n] for n in small_sh]) for r in r4]):
        res[n] = four
    for n in _BIG_REPL:
        r4 = _adamw(plan.parts[n], *[d[n].reshape(-1, B_BLOCK) for d in (wts, mom, var)], f"adamw_{n}")
        res[n] = [r.reshape(shp[n]) for r in r4]
    r4 = _adamw(parts[2], *[_pack([d[n] for n in small_rp], 8) for d in (wts, mom, var)], "adamw_small_repl")
    for n, *four in zip(small_rp, *[_unpack(r, [shp[n] for n in small_rp]) for r in r4]):
        res[n] = four

    outs = [res[n][kind] for kind in range(4) for n in _NAMES]
    loss = lax.psum(loss, ("x", "y", "c"))
    return (loss, gx[None], *outs)
```

```python
import functools

import jax
import jax.numpy as jnp
from jax import lax
from jax.experimental import pallas as pl
from jax.experimental.pallas import tpu as pltpu

F32 = jnp.float32
BF16 = jnp.bfloat16

D_MODEL = 1024
N_META = 16
PAD = 112
EPS = 1e-6
NEG = -1e30
A_HEADS, A_DQK, A_DV, A_CHUNK = 4, 128, 256, 64
B_BLOCKS, B_BLOCK, CONV_W, LRU_C = 8, 128, 4, 8.0
C_HEADS, C_Q_LORA, C_KV_LORA, C_NOPE, C_ROPE, C_V = 8, 384, 256, 64, 32, 64
ROPE_THETA = 10000.0
D_HEADS, D_HD = 8, 64
HP = 128
ZE = 5376
ZO = 4096
N_DEV = 8
ADAM_LR, ADAM_B1, ADAM_B2, ADAM_EPS, ADAM_WD, ADAM_STEP = 0.001, 0.9, 0.999, 1e-08, 0.01, 10
VMEM_LIMIT = 56 * 1024 * 1024
SCAN_BLOCK = 128


def _pick(n, prefs):
    for p in prefs:
        if n % p == 0:
            return p
    return n


def _cparams(dims):
    return pltpu.CompilerParams(dimension_semantics=dims, vmem_limit_bytes=VMEM_LIMIT)


def _full_spec(shape):
    nd = len(shape)
    return pl.BlockSpec(shape, lambda *_: (0,) * nd)


def _me_and_peers():
    mx, my, mc = lax.axis_index("x"), lax.axis_index("y"), lax.axis_index("c")
    peers = []
    for k in range(1, N_DEV):
        px, py, pc = mx ^ ((k >> 2) & 1), my ^ ((k >> 1) & 1), mc ^ (k & 1)
        peers.append(((px, py, pc), 4 * px + 2 * py + pc))
    return 4 * mx + 2 * my + mc, peers


def _comm_copies(x_refs, o_refs, flags, send_sems, recv_sems, local_sems):
    n_ops, n_peer = len(flags), N_DEV - 1
    me, peers = _me_and_peers()
    local = [pltpu.make_async_copy(x_refs[a].at[me] if flags[a] else x_refs[a], o_refs[a].at[me], local_sems.at[a])
             for a in range(n_ops)]
    remote = []
    for k, (peer, pid) in enumerate(peers):
        for a in range(n_ops):
            remote.append(pltpu.make_async_remote_copy(
                src_ref=x_refs[a].at[pid] if flags[a] else x_refs[a], dst_ref=o_refs[a].at[me],
                send_sem=send_sems.at[a * n_peer + k], recv_sem=recv_sems.at[a * n_peer + k],
                device_id=peer, device_id_type=pl.DeviceIdType.MESH))
    return local, remote


def _comm_wait(local, remote):
    for cp in remote:
        cp.wait_send()
    for cp in remote:
        cp.wait_recv()
    for cp in local:
        cp.wait()


def _comm_shapes(comm):
    n = len(comm)
    out_shape = [jax.ShapeDtypeStruct((N_DEV,) + x.shape[-2:], x.dtype) for x, _ in comm]
    sems = [pltpu.SemaphoreType.DMA((n * (N_DEV - 1),)), pltpu.SemaphoreType.DMA((n * (N_DEV - 1),)),
            pltpu.SemaphoreType.DMA((n,))]
    return out_shape, sems


def _pcall(body, name, grid, in_specs, out_specs, out_shape, scratch_shapes, dims, ins, comm=None):
    if not comm:
        return pl.pallas_call(body, name=name, grid=grid, in_specs=in_specs, out_specs=out_specs,
                              out_shape=out_shape, scratch_shapes=scratch_shapes,
                              compiler_params=_cparams(dims))(*ins)
    n_in, n_out, n = len(in_specs), len(out_specs), len(comm)
    flags = [s for _, s in comm]
    c_shape, c_sems = _comm_shapes(comm)

    def riding(*refs):
        cx = refs[n_in:n_in + n]
        co = refs[n_in + n + n_out:n_in + 2 * n + n_out]
        rest = refs[n_in + 2 * n + n_out:]
        sems = rest[len(rest) - 3:]
        ids = [pl.program_id(a) for a in range(len(grid))]
        first = functools.reduce(jnp.logical_and, [i == 0 for i in ids])
        last = functools.reduce(jnp.logical_and, [i == g - 1 for i, g in zip(ids, grid)])

        @pl.when(first)
        def _():
            local, remote = _comm_copies(cx, co, flags, *sems)
            for cp in local + remote:
                cp.start()

        body(*refs[:n_in], *refs[n_in + n:n_in + n + n_out], *rest[:len(rest) - 3])

        @pl.when(last)
        def _():
            _comm_wait(*_comm_copies(cx, co, flags, *sems))

    any_spec = pl.BlockSpec(memory_space=pl.ANY)
    res = pl.pallas_call(
        riding, name=name, grid=grid, in_specs=list(in_specs) + [any_spec] * n,
        out_specs=list(out_specs) + [any_spec] * n, out_shape=list(out_shape) + c_shape,
        scratch_shapes=list(scratch_shapes) + c_sems,
        compiler_params=_cparams(("arbitrary",) * len(grid)))(*ins, *[x for x, _ in comm])
    return list(res[:n_out]), list(res[n_out:])


def _mm(a, b, mode, name, out_dtype=None, add=None, relu2=False, relu2_of=None, comm=None):
    if out_dtype is None:
        out_dtype = BF16 if (mode == "tn" or relu2_of is not None) else F32
    if relu2_of is not None:
        add = relu2_of
    if mode == "nn":
        (m, k), n = a.shape, b.shape[1]
    elif mode == "nt":
        (m, k), n = a.shape, b.shape[0]
    else:
        (k, m), n = a.shape, b.shape[1]
    tm = _pick(m, (1408, 1024, 768, 512, 384, 256, 128))
    tn = _pick(n, (1024, 768, 512, 384, 256, 128))
    tk = _pick(k, (1408, 1024, 768, 512, 384, 256, 128))
    nk = k // tk
    if mode == "nn":
        a_spec = pl.BlockSpec((tm, tk), lambda i, j, q: (i, q))
        b_spec = pl.BlockSpec((tk, tn), lambda i, j, q: (q, j))
        dn = (((1,), (0,)), ((), ()))
    elif mode == "nt":
        a_spec = pl.BlockSpec((tm, tk), lambda i, j, q: (i, q))
        b_spec = pl.BlockSpec((tn, tk), lambda i, j, q: (j, q))
        dn = (((1,), (1,)), ((), ()))
    else:
        a_spec = pl.BlockSpec((tk, tm), lambda i, j, q: (q, i))
        b_spec = pl.BlockSpec((tk, tn), lambda i, j, q: (q, j))
        dn = (((0,), (0,)), ((), ()))
    o_spec = pl.BlockSpec((tm, tn), lambda i, j, q: (i, j))
    has_add = add is not None

    def body(*refs):
        a_ref, b_ref = refs[:2]
        add_ref = refs[2] if has_add else None
        o_refs, acc = refs[2 + has_add:-1], refs[-1]
        q = pl.program_id(2)

        @pl.when(q == 0)
        def _():
            acc[...] = jnp.zeros_like(acc)

        acc[...] += lax.dot_general(a_ref[...].astype(BF16), b_ref[...].astype(BF16), dn,
                                    preferred_element_type=F32)

        @pl.when(q == nk - 1)
        def _():
            r = acc[...]
            if relu2_of is not None:
                r = r * (2.0 * jnp.maximum(add_ref[...], 0.0))
            elif has_add:
                r = r + add_ref[...]
            o_refs[0][...] = r.astype(o_refs[0].dtype)
            if relu2:
                pos = jnp.maximum(r, 0.0)
                o_refs[1][...] = (pos * pos).astype(o_refs[1].dtype)

    ins = [a, b] + ([add] if has_add else [])
    in_specs = [a_spec, b_spec] + ([o_spec] if has_add else [])
    out_shape = [jax.ShapeDtypeStruct((m, n), out_dtype)] + ([jax.ShapeDtypeStruct((m, n), BF16)] if relu2 else [])
    res = _pcall(body, name, (m // tm, n // tn, nk), in_specs, [o_spec] * len(out_shape), out_shape,
                 [pltpu.VMEM((tm, tn), F32)], ("parallel", "parallel", "arbitrary"), ins, comm)
    outs = res[0] if comm else res
    outs = tuple(outs) if relu2 else outs[0]
    return (outs, res[1]) if comm else outs


def _row_specs(rows, tm):
    return [pl.BlockSpec((tm, w), functools.partial(lambda cb, i: (i, cb), cb)) for (_, w, cb) in rows]


def _row_fwd(name, f, rows, params, outs, tm):
    tp = rows[0][0].shape[0]
    nr, npar = len(rows), len(params)

    def body(*refs):
        i = pl.program_id(0)
        rv = [r[...] for r in refs[:nr]]
        pv = [r[...] for r in refs[nr:nr + npar]]
        res = f(i, rv, pv)
        for o_ref, r in zip(refs[nr + npar:], res):
            o_ref[...] = r.astype(o_ref.dtype)

    res = pl.pallas_call(
        body, name=name, grid=(tp // tm,),
        in_specs=_row_specs(rows, tm) + [_full_spec(p.shape) for p in params],
        out_specs=[pl.BlockSpec((tm, w), lambda i: (i, 0)) for (w, _) in outs],
        out_shape=[jax.ShapeDtypeStruct((tp, w), dt) for (w, dt) in outs],
        compiler_params=_cparams(("parallel",)),
    )(*[r[0] for r in rows], *params)
    return list(res)


def _row_bwd(name, f, rows, params, douts, tm, diff, add=None, out_dtypes=None):
    tp = rows[0][0].shape[0]
    nr, npar, nd = len(rows), len(params), len(douts)
    didx = [k for k in range(nr) if diff[k]]
    has_add = add is not None

    def body(*refs):
        i = pl.program_id(0)
        rv = [r[...] for r in refs[:nr]]
        pv = [r[...] for r in refs[nr:nr + npar]]
        dv = [r[...] for r in refs[nr + npar:nr + npar + nd]]
        pos = nr + npar + nd
        add_ref = refs[pos] if has_add else None
        pos += 1 if has_add else 0
        dr_refs = refs[pos:pos + len(didx)]
        dp_refs = refs[pos + len(didx):]

        def g(drv, pvs):
            full = list(rv)
            for k, val in zip(didx, drv):
                full[k] = val
            return tuple(f(i, full, list(pvs)))

        _, vjp = jax.vjp(g, [rv[k] for k in didx], pv)
        d_r, d_p = vjp(tuple(dv))
        for n_, (ref, val) in enumerate(zip(dr_refs, d_r)):
            if has_add and n_ == 0:
                val = val + add_ref[...]
            ref[...] = val.astype(ref.dtype)

        @pl.when(i == 0)
        def _():
            for ref in dp_refs:
                ref[...] = jnp.zeros_like(ref)

        for ref, val in zip(dp_refs, d_p):
            ref[...] += val

    in_specs = (_row_specs(rows, tm) + [_full_spec(p.shape) for p in params] + _row_specs(douts, tm))
    ins = [r[0] for r in rows] + list(params) + [d[0] for d in douts]
    if has_add:
        in_specs.append(pl.BlockSpec((tm, rows[didx[0]][1]), lambda i: (i, 0)))
        ins.append(add)
    out_specs = ([pl.BlockSpec((tm, rows[k][1]), lambda i: (i, 0)) for k in didx]
                 + [_full_spec(p.shape) for p in params])
    out_dtypes = out_dtypes or [F32] * len(didx)
    out_shape = ([jax.ShapeDtypeStruct((tp, rows[k][1]), dt) for k, dt in zip(didx, out_dtypes)]
                 + [jax.ShapeDtypeStruct(p.shape, F32) for p in params])
    res = pl.pallas_call(
        body, name=name, grid=(tp // tm,), in_specs=in_specs, out_specs=out_specs, out_shape=out_shape,
        compiler_params=_cparams(("arbitrary",)),
    )(*ins)
    return list(res[:len(didx)]), list(res[len(didx):])


def _rowmask(i, tm):
    return (i * tm + lax.broadcasted_iota(jnp.int32, (tm, 1), 0)) >= PAD


def _lane(n=HP):
    return lax.broadcasted_iota(jnp.int32, (1, n), 1)


def _softplus(x):
    return jnp.maximum(x, 0.0) + jnp.log(1.0 + jnp.exp(-jnp.abs(x)))


def _log_sigmoid(x):
    return -_softplus(-x)


def _sigmoid(x):
    return 1.0 / (1.0 + jnp.exp(-x))


@functools.partial(jax.custom_vjp, nondiff_argnums=(1,))
def _lroll(x, s):
    return pltpu.roll(x, s % HP, 1)


def _lroll_fwd(x, s):
    return _lroll(x, s), None


def _lroll_bwd(s, _, g):
    return (pltpu.roll(g, (-s) % HP, 1),)


_lroll.defvjp(_lroll_fwd, _lroll_bwd)


def _f_norm(i, rv, pv):
    (h,), (g,) = rv, pv
    return [h * lax.rsqrt(jnp.mean(h * h, axis=-1, keepdims=True) + EPS) * g]


def _f_gate0(i, rv, pv):
    (misc,), (b,) = rv, pv
    tm = misc.shape[0]
    x = misc + b
    lane, ok = _lane(), _rowmask(i, tm)
    li = jnp.where(ok, x, NEG)
    lf = jnp.where(ok, _log_sigmoid(x), 0.0)
    return [jnp.where(lane < A_HEADS, li, jnp.where(lane < 2 * A_HEADS, lf, 0.0))]


def _f_aout(i, rv, pv):
    (ha, o), (g,) = rv, pv
    outs = []
    for h in range(A_HEADS):
        x = ha[:, h * A_DV:(h + 1) * A_DV]
        outs.append(x * lax.rsqrt(jnp.mean(x * x, axis=-1, keepdims=True) + EPS) * g)
    return [jnp.concatenate(outs, axis=1) * _sigmoid(o)]


def _f_gates(i, rv, pv):
    (xc,), (w_ra, b_ra, w_rx, b_rx, lam) = rv, pv
    tm = xc.shape[0]
    ra, rx = [], []
    for g in range(B_BLOCKS):
        xg = xc[:, g * B_BLOCK:(g + 1) * B_BLOCK].astype(BF16)
        ra.append(jnp.dot(xg, w_ra[g].astype(BF16), preferred_element_type=F32))
        rx.append(jnp.dot(xg, w_rx[g].astype(BF16), preferred_element_type=F32))
    r = _sigmoid(jnp.concatenate(ra, axis=1) + b_ra)
    ig = _sigmoid(jnp.concatenate(rx, axis=1) + b_rx)
    log_a = -LRU_C * r * _softplus(-lam)
    a = jnp.exp(log_a)
    u = jnp.sqrt(1.0 - jnp.exp(2.0 * log_a)) * (ig * xc)
    return [a, jnp.where(_rowmask(i, tm), u, 0.0)]


def _f_bout(i, rv, pv):
    hs, gb = rv
    gelu = 0.5 * gb * (1.0 + jnp.tanh(0.7978845608028654 * (gb + 0.044715 * gb * gb * gb)))
    return [hs * gelu]


def _f_cnorm(i, rv, pv):
    (cq, ckv), (gq, gkv) = rv, pv
    return [cq * lax.rsqrt(jnp.mean(cq * cq, axis=-1, keepdims=True) + EPS) * gq,
            ckv * lax.rsqrt(jnp.mean(ckv * ckv, axis=-1, keepdims=True) + EPS) * gkv]


def _rope128(x, cos, sin):
    lane = _lane()
    rot = jnp.where((lane >= C_NOPE) & (lane < C_NOPE + C_ROPE // 2), -_lroll(x, -(C_ROPE // 2)),
                    jnp.where((lane >= C_NOPE + C_ROPE // 2) & (lane < C_NOPE + C_ROPE), _lroll(x, C_ROPE // 2), 0.0))
    return x * cos + rot * sin


def _f_mlaprep(i, rv, pv):
    (q_, kk_, misc, cos, sin), (gq, gk) = rv, pv
    lane = _lane()
    m_n = lane < C_NOPE
    m_r = (lane >= C_NOPE) & (lane < C_NOPE + C_ROPE)

    def norm2(x, g):
        x2 = x * x
        sn = jnp.sum(jnp.where(m_n, x2, 0.0), axis=-1, keepdims=True) * (1.0 / C_NOPE)
        sr = jnp.sum(jnp.where(m_r, x2, 0.0), axis=-1, keepdims=True) * (1.0 / C_ROPE)
        scale = jnp.where(m_n, lax.rsqrt(sn + EPS), jnp.where(m_r, lax.rsqrt(sr + EPS), 0.0))
        return x * scale * g

    kr = _rope128(norm2(jnp.where(m_r, misc, 0.0), gk), cos, sin)
    qs, ks = [], []
    for h in range(C_HEADS):
        qs.append(_rope128(norm2(q_[:, h * HP:(h + 1) * HP], gq), cos, sin))
        ks.append(norm2(jnp.where(m_n, kk_[:, h * HP:(h + 1) * HP], 0.0), gk) + kr)
    return [jnp.concatenate(qs, axis=1), jnp.concatenate(ks, axis=1)]


def _f_foxprep(i, rv, pv):
    (fq, fk, misc), (gq, gk, bf) = rv, pv
    tm = fq.shape[0]
    lane = _lane()

    def hnorm(x, g):
        outs = []
        for h in range(D_HEADS):
            xh = x[:, h * HP:(h + 1) * HP]
            ss = jnp.sum(xh * xh, axis=-1, keepdims=True) * (1.0 / D_HD)
            outs.append(xh * lax.rsqrt(ss + EPS) * g)
        return jnp.concatenate(outs, axis=1)

    lf = jnp.where(_rowmask(i, tm) & (lane >= 96) & (lane < 96 + D_HEADS), _log_sigmoid(misc + bf), 0.0)
    return [hnorm(fq, gq), hnorm(fk, gk), lf]


def _mlstm_chunk(c, n, m, q, k, v, li, lf):
    ln = q.shape[0]
    r = lax.broadcasted_iota(jnp.int32, (ln, ln), 0)
    cc = lax.broadcasted_iota(jnp.int32, (ln, ln), 1)
    causal = cc <= r
    eye = cc == r
    li_row = jnp.sum(jnp.where(eye, li, 0.0), axis=0, keepdims=True)
    b_col = jnp.sum(jnp.where(causal, jnp.sum(jnp.where(eye, lf, 0.0), axis=0, keepdims=True), 0.0),
                    axis=1, keepdims=True)
    b_row = jnp.sum(jnp.where(r <= cc, lf, 0.0), axis=0, keepdims=True)
    k = k * (A_DQK ** -0.5)
    qb, kb, vb = q.astype(BF16), k.astype(BF16), v.astype(BF16)
    dmat = jnp.where(causal, b_col - b_row + li_row, NEG)
    inter = b_col + m
    m_row = jnp.maximum(inter, jnp.max(dmat, axis=1, keepdims=True))
    w_intra = jnp.exp(dmat - m_row)
    w_inter = jnp.exp(inter - m_row)
    s = lax.dot_general(qb, kb, (((1,), (1,)), ((), ())), preferred_element_type=F32) * w_intra
    num = (w_inter * jnp.dot(qb, c.astype(BF16), preferred_element_type=F32)
           + jnp.dot(s.astype(BF16), vb, preferred_element_type=F32))
    den = w_inter * jnp.sum(q * n, axis=1, keepdims=True) + jnp.sum(s, axis=1, keepdims=True)
    h = num / jnp.maximum(jnp.abs(den), jnp.exp(-m_row))
    g = jnp.sum(lf, axis=0, keepdims=True)
    dk = g - b_col + li
    m_new = jnp.maximum(g + m, jnp.max(dk, axis=0, keepdims=True))
    wk = jnp.exp(dk - m_new)
    sc = jnp.exp(g + m - m_new)
    kw = wk * k
    c_new = sc * c + lax.dot_general(kw.astype(BF16), vb, (((0,), (0,)), ((), ())), preferred_element_type=F32)
    n_new = sc * n + jnp.sum(kw, axis=0, keepdims=True)
    return c_new, n_new, m_new, h


def _mlstm_fwd(z, li, lf, comm=None):
    tp = z.shape[0]
    nc = tp // A_CHUNK
    ln = A_CHUNK

    def body(q_ref, k_ref, v_ref, li_ref, lf_ref, h_ref, cs_ref, ns_ref, ms_ref, c_s, n_s, m_s):
        @pl.when(pl.program_id(1) == 0)
        def _():
            c_s[...] = jnp.zeros_like(c_s)
            n_s[...] = jnp.zeros_like(n_s)
            m_s[...] = jnp.zeros_like(m_s)

        c, n, m = c_s[...], n_s[...], m_s[...]
        cs_ref[0, 0] = c
        ns_ref[0, 0] = n
        ms_ref[0, 0] = m
        c2, n2, m2, h = _mlstm_chunk(c, n, m, q_ref[...], k_ref[...], v_ref[...], li_ref[0], lf_ref[0])
        c_s[...] = c2
        n_s[...] = n2
        m_s[...] = m2
        h_ref[...] = h

    g_spec = pl.BlockSpec((1, ln, 1), lambda h, j: (h, j, 0))
    return _pcall(
        body, "mlstm_fwd", (A_HEADS, nc),
        [pl.BlockSpec((ln, A_DQK), lambda h, j: (j, h)),
         pl.BlockSpec((ln, A_DQK), lambda h, j: (j, A_HEADS + h)),
         pl.BlockSpec((ln, A_DV), lambda h, j: (j, A_HEADS + h)),
         g_spec, g_spec],
        [pl.BlockSpec((ln, A_DV), lambda h, j: (j, h)),
         pl.BlockSpec((1, 1, A_DQK, A_DV), lambda h, j: (h, j, 0, 0)),
         pl.BlockSpec((1, 1, 1, A_DQK), lambda h, j: (h, j, 0, 0)),
         pl.BlockSpec((1, 1, 1, 1), lambda h, j: (h, j, 0, 0))],
        [jax.ShapeDtypeStruct((tp, A_HEADS * A_DV), F32),
         jax.ShapeDtypeStruct((A_HEADS, nc, A_DQK, A_DV), F32),
         jax.ShapeDtypeStruct((A_HEADS, nc, 1, A_DQK), F32),
         jax.ShapeDtypeStruct((A_HEADS, nc, 1, 1), F32)],
        [pltpu.VMEM((A_DQK, A_DV), F32), pltpu.VMEM((1, A_DQK), F32), pltpu.VMEM((1, 1), F32)],
        ("parallel", "arbitrary"), (z, z, z, li, lf), comm)


def _mlstm_bwd(z, li, lf, cs, ns, ms, dh, comm=None):
    tp = z.shape[0]
    nc = tp // A_CHUNK
    ln = A_CHUNK

    def body(q_ref, k_ref, v_ref, li_ref, lf_ref, cs_ref, ns_ref, ms_ref, dh_ref,
             dq_ref, dk_ref, dv_ref, dli_ref, dlf_ref, dc_s, dn_s, dm_s):
        @pl.when(pl.program_id(1) == 0)
        def _():
            dc_s[...] = jnp.zeros_like(dc_s)
            dn_s[...] = jnp.zeros_like(dn_s)
            dm_s[...] = jnp.zeros_like(dm_s)

        _, vjp = jax.vjp(_mlstm_chunk, cs_ref[0, 0], ns_ref[0, 0], ms_ref[0, 0], q_ref[...], k_ref[...],
                         v_ref[...], li_ref[0], lf_ref[0])
        dc, dn, dm, dq, dk, dv, dli, dlf = vjp((dc_s[...], dn_s[...], dm_s[...], dh_ref[...]))
        dc_s[...] = dc
        dn_s[...] = dn
        dm_s[...] = dm
        dq_ref[...] = dq.astype(BF16)
        dk_ref[...] = dk.astype(BF16)
        dv_ref[...] = dv.astype(BF16)
        dli_ref[0] = dli
        dlf_ref[0] = dlf

    def rj(j):
        return nc - 1 - j

    g_spec = pl.BlockSpec((1, ln, 1), lambda h, j: (h, rj(j), 0))
    return _pcall(
        body, "mlstm_bwd", (A_HEADS, nc),
        [pl.BlockSpec((ln, A_DQK), lambda h, j: (rj(j), h)),
         pl.BlockSpec((ln, A_DQK), lambda h, j: (rj(j), A_HEADS + h)),
         pl.BlockSpec((ln, A_DV), lambda h, j: (rj(j), A_HEADS + h)),
         g_spec, g_spec,
         pl.BlockSpec((1, 1, A_DQK, A_DV), lambda h, j: (h, rj(j), 0, 0)),
         pl.BlockSpec((1, 1, 1, A_DQK), lambda h, j: (h, rj(j), 0, 0)),
         pl.BlockSpec((1, 1, 1, 1), lambda h, j: (h, rj(j), 0, 0)),
         pl.BlockSpec((ln, A_DV), lambda h, j: (rj(j), h))],
        [pl.BlockSpec((ln, A_DQK), lambda h, j: (rj(j), h)),
         pl.BlockSpec((ln, A_DQK), lambda h, j: (rj(j), h)),
         pl.BlockSpec((ln, A_DV), lambda h, j: (rj(j), h)),
         g_spec, g_spec],
        [jax.ShapeDtypeStruct((tp, A_HEADS * A_DQK), BF16),
         jax.ShapeDtypeStruct((tp, A_HEADS * A_DQK), BF16),
         jax.ShapeDtypeStruct((tp, A_HEADS * A_DV), BF16),
         jax.ShapeDtypeStruct((A_HEADS, tp, 1), F32),
         jax.ShapeDtypeStruct((A_HEADS, tp, 1), F32)],
        [pltpu.VMEM((A_DQK, A_DV), F32), pltpu.VMEM((1, A_DQK), F32), pltpu.VMEM((1, 1), F32)],
        ("parallel", "arbitrary"), (z, z, z, li, lf, cs, ns, ms, dh), comm)


def _shift_down(x, s, row):
    return x if s == 0 else jnp.where(row >= s, pltpu.roll(x, s, 0), 0.0)


def _shift_up(x, s, row):
    n = x.shape[0]
    return x if s == 0 else jnp.where(row < n - s, pltpu.roll(x, n - s, 0), 0.0)


def _conv_fwd(z, xcb, w, b):
    tp, c = z.shape[0], w.shape[1]
    ct = 256

    def body(x_ref, w_ref, b_ref, o_ref):
        x = x_ref[...]
        row = lax.broadcasted_iota(jnp.int32, (tp, 1), 0)
        acc = jnp.zeros_like(x) + b_ref[...]
        for k in range(CONV_W):
            acc = acc + w_ref[k:k + 1, :] * _shift_down(x, CONV_W - 1 - k, row)
        o_ref[...] = acc

    return pl.pallas_call(
        body, name="conv_fwd", grid=(c // ct,),
        in_specs=[pl.BlockSpec((tp, ct), lambda j: (0, xcb + j)), pl.BlockSpec((CONV_W, ct), lambda j: (0, j)),
                  pl.BlockSpec((1, ct), lambda j: (0, j))],
        out_specs=pl.BlockSpec((tp, ct), lambda j: (0, j)),
        out_shape=jax.ShapeDtypeStruct((tp, c), F32),
        compiler_params=_cparams(("parallel",)),
    )(z, w, b)


def _conv_bwd(z, xcb, w, dxc):
    tp, c = z.shape[0], w.shape[1]
    ct = 256

    def body(x_ref, w_ref, d_ref, dx_ref, dw_ref, db_ref):
        x, d = x_ref[...], d_ref[...]
        row = lax.broadcasted_iota(jnp.int32, (tp, 1), 0)
        acc = jnp.zeros_like(x)
        for k in range(CONV_W):
            s = CONV_W - 1 - k
            acc = acc + w_ref[k:k + 1, :] * _shift_up(d, s, row)
            dw_ref[k:k + 1, :] = jnp.sum(d * _shift_down(x, s, row), axis=0, keepdims=True)
        dx_ref[...] = acc.astype(BF16)
        db_ref[...] = jnp.sum(d, axis=0, keepdims=True)

    return pl.pallas_call(
        body, name="conv_bwd", grid=(c // ct,),
        in_specs=[pl.BlockSpec((tp, ct), lambda j: (0, xcb + j)), pl.BlockSpec((CONV_W, ct), lambda j: (0, j)),
                  pl.BlockSpec((tp, ct), lambda j: (0, j))],
        out_specs=[pl.BlockSpec((tp, ct), lambda j: (0, j)), pl.BlockSpec((CONV_W, ct), lambda j: (0, j)),
                   pl.BlockSpec((1, ct), lambda j: (0, j))],
        out_shape=[jax.ShapeDtypeStruct((tp, c), BF16), jax.ShapeDtypeStruct((CONV_W, c), F32),
                   jax.ShapeDtypeStruct((1, c), F32)],
        compiler_params=_cparams(("parallel",)),
    )(z, w, dxc)


def _scan_fwd(a, u, name):
    tp, c = a.shape
    ct = _pick(c, (256, 128))
    lb = SCAN_BLOCK
    nb = tp // lb

    def body(a_ref, u_ref, h_ref, hp_ref):
        row = lax.broadcasted_iota(jnp.int32, (lb, 1), 0)

        def blk(j, carry):
            r0 = pl.multiple_of(j * lb, lb)
            aa, uu = a_ref[pl.ds(r0, lb), :], u_ref[pl.ds(r0, lb), :]
            s = 1
            while s < lb:
                mk = row >= s
                uu = jnp.where(mk, aa * pltpu.roll(uu, s, 0) + uu, uu)
                aa = jnp.where(mk, aa * pltpu.roll(aa, s, 0), aa)
                s *= 2
            hh = uu + aa * carry
            h_ref[pl.ds(r0, lb), :] = hh
            hp_ref[pl.ds(r0, lb), :] = jnp.where(row >= 1, pltpu.roll(hh, 1, 0), carry)
            return hh[lb - 1:lb, :]

        lax.fori_loop(0, nb, blk, jnp.zeros((1, ct), F32))

    spec = pl.BlockSpec((tp, ct), lambda j: (0, j))
    return pl.pallas_call(
        body, name=name, grid=(c // ct,), in_specs=[spec, spec], out_specs=[spec, spec],
        out_shape=[jax.ShapeDtypeStruct((tp, c), F32)] * 2,
        compiler_params=_cparams(("parallel",)),
    )(a, u)


def _scan_bwd(a, hprev, dh, name):
    tp, c = a.shape
    ct = _pick(c, (256, 128))
    lb = SCAN_BLOCK
    nb = tp // lb

    def body(a_ref, hp_ref, dh_ref, du_ref, da_ref):
        row = lax.broadcasted_iota(jnp.int32, (lb, 1), 0)

        def blk(jj, carry):
            g_next, a_next = carry
            r0 = pl.multiple_of((nb - 1 - jj) * lb, lb)
            a_blk = a_ref[pl.ds(r0, lb), :]
            aa = jnp.where(row < lb - 1, pltpu.roll(a_blk, lb - 1, 0), a_next)
            gg = dh_ref[pl.ds(r0, lb), :]
            s = 1
            while s < lb:
                mk = row < lb - s
                gg = jnp.where(mk, aa * pltpu.roll(gg, lb - s, 0) + gg, gg)
                aa = jnp.where(mk, aa * pltpu.roll(aa, lb - s, 0), aa)
                s *= 2
            gg = gg + aa * g_next
            du_ref[pl.ds(r0, lb), :] = gg
            da_ref[pl.ds(r0, lb), :] = gg * hp_ref[pl.ds(r0, lb), :]
            return gg[0:1, :], a_blk[0:1, :]

        lax.fori_loop(0, nb, blk, (jnp.zeros((1, ct), F32), jnp.zeros((1, ct), F32)))

    spec = pl.BlockSpec((tp, ct), lambda j: (0, j))
    return pl.pallas_call(
        body, name=name, grid=(c // ct,), in_specs=[spec, spec, spec], out_specs=[spec, spec],
        out_shape=[jax.ShapeDtypeStruct((tp, c), F32)] * 2,
        compiler_params=_cparams(("parallel",)),
    )(a, hprev, dh)


LOG2E = 1.4426950408889634
LN2 = 0.6931471805599453


def _diag_valid(i, tq):
    r = lax.broadcasted_iota(jnp.int32, (tq, tq), 0)
    c = lax.broadcasted_iota(jnp.int32, (tq, tq), 1)
    return (c <= r) & ((i * tq + c >= PAD) | (c == r))


def _key_terms(tp, tq, bk=None):
    pad_neg = jnp.where(jnp.arange(tp) < PAD, NEG, 0.0).astype(F32).reshape(1, tp // tq, 1, tq)
    if bk is None:
        return pad_neg, jnp.zeros_like(pad_neg)
    kd = -bk.reshape(bk.shape[0], tp // tq, 1, tq)
    return kd + pad_neg, kd


def _attn_fwd(q, k, v, vcb, scale, bq, kterms, name, comm=None):
    tp = q.shape[0]
    nh = q.shape[1] // HP
    tq = _pick(tp, (384, 256, 128))
    has_bq = bq is not None
    c1 = scale * LOG2E
    per_head = kterms[0].shape[0] > 1

    def body(*refs):
        if has_bq:
            q_ref, k_ref, v_ref, kb_ref, kd_ref, bq_ref, o_ref, lse_ref = refs
            rb = bq_ref[0] * LOG2E
        else:
            q_ref, k_ref, v_ref, kb_ref, kd_ref, o_ref, lse_ref = refs
        i = pl.program_id(1)
        qb = q_ref[...].astype(BF16)

        def tile(j, carry, diag):
            m, l, acc = carry
            r0 = pl.multiple_of(j * tq, tq)
            kb = k_ref[pl.ds(r0, tq), :].astype(BF16)
            vb = v_ref[pl.ds(r0, tq), :].astype(BF16)
            kt = (kd_ref if diag else kb_ref)[0, j] * LOG2E
            x = lax.dot_general(qb, kb, (((1,), (1,)), ((), ())), preferred_element_type=F32) * c1 + kt
            if has_bq:
                x = x + rb
            if diag:
                x = jnp.where(_diag_valid(i, tq), x, NEG)
            m2 = jnp.maximum(m, jnp.max(x, axis=1, keepdims=True))
            alpha = jnp.exp2(m - m2)
            p = jnp.exp2(x - m2)
            l2 = alpha * l + jnp.sum(p, axis=1, keepdims=True)
            acc2 = alpha * acc + jnp.dot(p.astype(BF16), vb, preferred_element_type=F32)
            return m2, l2, acc2

        init = (jnp.full((tq, 1), NEG, F32), jnp.zeros((tq, 1), F32), jnp.zeros((tq, HP), F32))
        m, l, acc = tile(i, lax.fori_loop(0, i, lambda j, c: tile(j, c, False), init), True)
        o_ref[...] = acc / l
        lse_ref[0] = m * LN2 + jnp.log(l)

    kt_spec = pl.BlockSpec((1, tp // tq, 1, tq), (lambda h, i: (h, 0, 0, 0)) if per_head else (lambda h, i: (0, 0, 0, 0)))
    in_specs = [pl.BlockSpec((tq, HP), lambda h, i: (i, h)), pl.BlockSpec((tp, HP), lambda h, i: (0, h)),
                pl.BlockSpec((tp, HP), lambda h, i: (0, vcb + h)), kt_spec, kt_spec]
    ins = [q, k, v, *kterms]
    if has_bq:
        in_specs += [pl.BlockSpec((1, tq, 1), lambda h, i: (h, i, 0))]
        ins += [bq]
    return _pcall(
        body, name, (nh, tp // tq), in_specs,
        [pl.BlockSpec((tq, HP), lambda h, i: (i, h)), pl.BlockSpec((1, tq, 1), lambda h, i: (h, i, 0))],
        [jax.ShapeDtypeStruct((tp, nh * HP), F32), jax.ShapeDtypeStruct((nh, tp, 1), F32)],
        [], ("parallel", "parallel"), ins, comm)


def _attn_bwd(q, k, v, vcb, o, lse, do, docb, scale, bq, kterms, name, comm=None):
    tp = q.shape[0]
    nh = q.shape[1] // HP
    tq = _pick(tp, (384, 256, 128))
    has_bq = bq is not None
    c1 = scale * LOG2E
    per_head = kterms[0].shape[0] > 1

    def body(*refs):
        if has_bq:
            (q_ref, k_ref, v_ref, o_ref, lse_ref, do_ref, kb_ref, kd_ref, bq_ref,
             dq_ref, dk_ref, dv_ref, dbq_ref, dkt_ref) = refs
            rb = (bq_ref[0] - lse_ref[0]) * LOG2E
        else:
            q_ref, k_ref, v_ref, o_ref, lse_ref, do_ref, kb_ref, kd_ref, dq_ref, dk_ref, dv_ref = refs
            rb = lse_ref[0] * (-LOG2E)
        i = pl.program_id(1)

        @pl.when(i == 0)
        def _():
            dk_ref[...] = jnp.zeros_like(dk_ref)
            dv_ref[...] = jnp.zeros_like(dv_ref)
            if has_bq:
                dkt_ref[...] = jnp.zeros_like(dkt_ref)

        qb = q_ref[...].astype(BF16)
        do_ = do_ref[...]
        dob = do_.astype(BF16)
        delta = jnp.sum(do_ * o_ref[...], axis=1, keepdims=True)

        def tile(j, carry, diag):
            dq, dbq = carry
            r0 = pl.multiple_of(j * tq, tq)
            kb = k_ref[pl.ds(r0, tq), :].astype(BF16)
            vb = v_ref[pl.ds(r0, tq), :].astype(BF16)
            kt = (kd_ref if diag else kb_ref)[0, j] * LOG2E
            x = lax.dot_general(qb, kb, (((1,), (1,)), ((), ())), preferred_element_type=F32) * c1 + kt + rb
            if diag:
                x = jnp.where(_diag_valid(i, tq), x, NEG)
            p = jnp.exp2(x)
            dp = lax.dot_general(dob, vb, (((1,), (1,)), ((), ())), preferred_element_type=F32)
            ds = p * (dp - delta)
            dsb = ds.astype(BF16)
            dk_ref[pl.ds(r0, tq), :] += lax.dot_general(dsb, qb, (((0,), (0,)), ((), ())),
                                                        preferred_element_type=F32) * scale
            dv_ref[pl.ds(r0, tq), :] += lax.dot_general(p.astype(BF16), dob, (((0,), (0,)), ((), ())),
                                                        preferred_element_type=F32)
            if has_bq:
                dkt_ref[0, j] += jnp.sum(ds, axis=0, keepdims=True)
                dbq = dbq + jnp.sum(ds, axis=1, keepdims=True)
            return dq + jnp.dot(dsb, kb, preferred_element_type=F32), dbq

        init = (jnp.zeros((tq, HP), F32), jnp.zeros((tq, 1), F32))
        dq, dbq = tile(i, lax.fori_loop(0, i, lambda j, c: tile(j, c, False), init), True)
        dq_ref[...] = dq * scale
        if has_bq:
            dbq_ref[0] = dbq

    blk_q = pl.BlockSpec((tq, HP), lambda h, i: (i, h))
    blk_k = pl.BlockSpec((tp, HP), lambda h, i: (0, h))
    kt_spec = pl.BlockSpec((1, tp // tq, 1, tq), (lambda h, i: (h, 0, 0, 0)) if per_head else (lambda h, i: (0, 0, 0, 0)))
    in_specs = [blk_q, blk_k, pl.BlockSpec((tp, HP), lambda h, i: (0, vcb + h)), blk_q,
                pl.BlockSpec((1, tq, 1), lambda h, i: (h, i, 0)), pl.BlockSpec((tq, HP), lambda h, i: (i, docb + h)),
                kt_spec, kt_spec]
    ins = [q, k, v, o, lse, do, *kterms]
    out_specs = [blk_q, blk_k, blk_k]
    out_shape = [jax.ShapeDtypeStruct((tp, nh * HP), F32)] * 3
    if has_bq:
        in_specs += [pl.BlockSpec((1, tq, 1), lambda h, i: (h, i, 0))]
        ins += [bq]
        out_specs += [pl.BlockSpec((1, tq, 1), lambda h, i: (h, i, 0)), kt_spec]
        out_shape += [jax.ShapeDtypeStruct((nh, tp, 1), F32), jax.ShapeDtypeStruct((nh, tp // tq, 1, tq), F32)]
    return _pcall(body, name, (nh, tp // tq), in_specs, out_specs, out_shape, [], ("parallel", "arbitrary"), ins, comm)


def _loss_head(h, tgt):
    tp, d = h.shape
    tm = 128
    first = (PAD + N_META) // tm

    def body(h_ref, t_ref, l_ref, d_ref):
        i = pl.program_id(0)

        @pl.when(i == 0)
        def _():
            l_ref[...] = jnp.zeros_like(l_ref)

        live = i >= first
        err = jnp.where(live, h_ref[...] - t_ref[...], 0.0)
        d_ref[...] = err * (1.0 / d)
        l_ref[...] += (0.5 / d) * jnp.sum(err * err)

    return pl.pallas_call(
        body, name="loss_head", grid=(tp // tm,),
        in_specs=[pl.BlockSpec((tm, d), lambda i: (i, 0)), pl.BlockSpec((tm, d), lambda i: (i, 0))],
        out_specs=[_full_spec((8, 128)), pl.BlockSpec((tm, d), lambda i: (i, 0))],
        out_shape=[jax.ShapeDtypeStruct((8, 128), F32), jax.ShapeDtypeStruct((tp, d), F32)],
        compiler_params=_cparams(("arbitrary",)),
    )(h, tgt)


def _rope_tables(pos_rows):
    half = C_ROPE // 2
    freqs = ROPE_THETA ** (-jnp.arange(half, dtype=F32) / half)
    ang = pos_rows[:, None].astype(F32) * freqs
    cos, sin = jnp.cos(ang), jnp.sin(ang)
    tp = pos_rows.shape[0]
    one, zero = jnp.ones((tp, C_NOPE), F32), jnp.zeros((tp, C_NOPE), F32)
    tail1, tail0 = jnp.ones((tp, HP - C_NOPE - C_ROPE), F32), jnp.zeros((tp, HP - C_NOPE - C_ROPE), F32)
    return (jnp.concatenate([one, cos, cos, tail1], axis=1), jnp.concatenate([zero, sin, sin, tail0], axis=1))


def _heads_to_cols(x, lo, n):
    t = x[:, lo:lo + n].T
    return t[:, :, None], t[:, None, :]


def _local_step(x, positions, tgt, w, plan=None):
    def hosted(tag, fn, *args):
        ops = plan.ride(tag, g) if plan is not None else None
        if not ops:
            return fn(*args)
        res, got = fn(*args, comm=ops)
        plan.arrived(tag, got, w)
        return res

    s_len = x.shape[0]
    tp = PAD + N_META + s_len
    tm = _pick(tp, (384, 256, 128))
    front = PAD + N_META
    h0 = jnp.concatenate([jnp.zeros((PAD, D_MODEL), F32), w["meta"], x], axis=0)
    tgt_p = jnp.concatenate([jnp.zeros((front, D_MODEL), F32), tgt], axis=0)
    pos_rows = jnp.concatenate([jnp.zeros((PAD,), jnp.int32), jnp.arange(N_META, dtype=jnp.int32),
                                positions + N_META])
    cos, sin = _rope_tables(pos_rows)
    g = {}

    r_h0 = [(h0, D_MODEL, 0)]
    (xn0,) = _row_fwd("norm0_f", _f_norm, r_h0, [w["ev_ln"]], [(D_MODEL, BF16)], tm)
    z0 = hosted("mm_z0", _mm, xn0, w["ev_w_in"], "nn", "mm_z0")
    r_misc0 = [(z0, HP, 5120 // HP)]
    (g0,) = _row_fwd("gate0_f", _f_gate0, r_misc0, [w["ev_b_if"]], [(HP, F32)], tm)
    li, _ = _heads_to_cols(g0, 0, A_HEADS)
    lf, _ = _heads_to_cols(g0, A_HEADS, A_HEADS)
    h_a, cs, ns, ms = hosted("mlstm_fwd", _mlstm_fwd, z0, li, lf)
    r_aout = [(h_a, 1024, 0), (z0, 1024, 2)]
    (ha,) = _row_fwd("aout_f", _f_aout, r_aout, [w["ev_a_norm"]], [(1024, BF16)], tm)
    xc = _conv_fwd(z0, 3072 // 256, w["ev_conv_w"], w["ev_conv_b"])
    p_gates = [w["ev_w_ra"], w["ev_b_ra"], w["ev_w_rx"], w["ev_b_rx"], w["ev_lam"]]
    a_g, u_g = _row_fwd("gates_f", _f_gates, [(xc, 1024, 0)], p_gates, [(1024, F32), (1024, F32)], tm)
    hs, hprev = _scan_fwd(a_g, u_g, "lru_f")
    r_bout = [(hs, 1024, 0), (z0, 1024, 4)]
    (hb,) = _row_fwd("bout_f", _f_bout, r_bout, [], [(1024, BF16)], tm)
    hab = jnp.concatenate([ha, hb], axis=1)
    h1 = _mm(hab, w["ev_w_out"], "nn", "mm_h1", add=h0)
    (xn1,) = _row_fwd("norm1_f", _f_norm, [(h1, D_MODEL, 0)], [w["mlp_ln0"]], [(D_MODEL, BF16)], tm)
    p0, act0 = _mm(xn1, w["w_ff1_0"], "nn", "mm_p0", relu2=True)
    h2 = _mm(act0, w["w_ff2_0"], "nn", "mm_h2", add=h1)
    (xn2,) = _row_fwd("norm2_f", _f_norm, [(h2, D_MODEL, 0)], [w["od_ln"]], [(D_MODEL, BF16)], tm)
    z1 = _mm(xn2, w["od_w_in"], "nn", "mm_z1")
    r_c = [(z1, C_Q_LORA, 3072 // C_Q_LORA), (z1, C_KV_LORA, 3584 // C_KV_LORA)]
    cqn, ckvn = _row_fwd("cnorm_f", _f_cnorm, r_c, [w["od_g_qa"], w["od_g_kva"]],
                         [(C_Q_LORA, BF16), (C_KV_LORA, BF16)], tm)
    q_ = _mm(cqn, w["od_w_uq"], "nn", "mm_q")
    kv_ = _mm(ckvn, w["od_w_ukv"], "nn", "mm_kv")
    r_mla = [(q_, 1024, 0), (kv_, 1024, 0), (z1, HP, 3840 // HP), (cos, HP, 0), (sin, HP, 0)]
    p_mla = [w["gq_full"], w["gk_full"]]
    qm, km = _row_fwd("mla_f", _f_mlaprep, r_mla, p_mla, [(1024, BF16), (1024, BF16)], tm)
    sc_c = (C_NOPE + C_ROPE) ** -0.5
    kt_c = _key_terms(tp, tm)
    hc, lse_c = hosted("mla_attn_f", _attn_fwd, qm, km, kv_, C_HEADS, sc_c, None, kt_c, "mla_attn_f")
    r_fox = [(z1, 1024, 0), (z1, 1024, 1), (z1, HP, 3840 // HP)]
    p_fox = [w["gfq_full"], w["gfk_full"], w["bf_full"]]
    qf, kf, lfx = _row_fwd("fox_f", _f_foxprep, r_fox, p_fox, [(1024, BF16), (1024, BF16), (HP, F32)], tm)
    ones = jnp.ones((tp, HP), F32)
    fcum, fprev = _scan_fwd(ones, lfx, "fcum_f")
    bq, bk = _heads_to_cols(fcum, 96, D_HEADS)
    kt_d = _key_terms(tp, tm, bk)
    sc_d = D_HD ** -0.5
    hd, lse_d = _attn_fwd(qf, kf, z1, 2048 // HP, sc_d, bq, kt_d, "fox_attn_f")
    hcd = jnp.concatenate([hc, hd], axis=1)
    h3 = _mm(hcd, w["od_w_out"], "nn", "mm_h3", add=h2)
    (xn3,) = _row_fwd("norm3_f", _f_norm, [(h3, D_MODEL, 0)], [w["mlp_ln1"]], [(D_MODEL, BF16)], tm)
    p1, act1 = _mm(xn3, w["w_ff1_1"], "nn", "mm_p1", relu2=True)
    h4 = _mm(act1, w["w_ff2_1"], "nn", "mm_h4", add=h3)
    lpart, dh4 = _loss_head(h4, tgt_p)
    loss = lpart[0, 0]

    def mlp_bwd(tag, dh_out, h_in, xn, p, act, ln, w1, w2):
        dp = _mm(dh_out, w2, "nt", f"mm_dp{tag}", relu2_of=p)
        g_w2 = _mm(act, dh_out, "tn", f"mm_dw2_{tag}")
        g_w1 = _mm(xn, dp, "tn", f"mm_dw1_{tag}")
        dxn = _mm(dp, w1, "nt", f"mm_dxn{tag}")
        (dh_in,), (g_ln,) = _row_bwd(f"normm{tag}_b", _f_norm, [(h_in, D_MODEL, 0)], [ln], [(dxn, D_MODEL, 0)], tm,
                                     [True], add=dh_out)
        return dh_in, g_ln, g_w1, g_w2

    dh3, g["mlp_ln1"], g["w_ff1_1"], g["w_ff2_1"] = mlp_bwd("1", dh4, h3, xn3, p1, act1, w["mlp_ln1"],
                                                            w["w_ff1_1"], w["w_ff2_1"])
    dhcd = _mm(dh3, w["od_w_out"], "nt", "mm_dhcd")
    g["od_w_out"] = _mm(hcd, dh3, "tn", "mm_dwout1")
    dqf, dkf, dvf, dbq, dkt = hosted("fox_attn_b", _attn_bwd, qf, kf, z1, 2048 // HP, hd, lse_d, dhcd, D_HEADS, sc_d,
                                     bq, kt_d, "fox_attn_b")
    dfc = dbq[:, :, 0].T - dkt.reshape(D_HEADS, tp).T
    dfcum = jnp.concatenate([jnp.zeros((tp, 96), F32), dfc, jnp.zeros((tp, HP - 96 - D_HEADS), F32)], axis=1)
    dlfx, _ = _scan_bwd(ones, fprev, dfcum, "fcum_b")
    (dfq, dfk, dmisc_f), (g["gfq_full"], g["gfk_full"], g["bf_full"]) = _row_bwd(
        "fox_b", _f_foxprep, r_fox, p_fox, [(dqf, 1024, 0), (dkf, 1024, 0), (dlfx, HP, 0)], tm, [True, True, True],
        out_dtypes=[BF16, BF16, F32])
    dqm, dkm, dvm = _attn_bwd(qm, km, kv_, C_HEADS, hc, lse_c, dhcd, 0, sc_c, None, kt_c, "mla_attn_b")
    (dq_, dkk_, dmisc_m), (g["gq_full"], g["gk_full"]) = _row_bwd(
        "mla_b", _f_mlaprep, r_mla, p_mla, [(dqm, 1024, 0), (dkm, 1024, 0)], tm, [True, True, True, False, False],
        out_dtypes=[BF16, BF16, F32])
    dkv_ = jnp.concatenate([dkk_, dvm.astype(BF16)], axis=1)
    dckvn = _mm(dkv_, w["od_w_ukv"], "nt", "mm_dckvn")
    g["od_w_ukv"] = _mm(ckvn, dkv_, "tn", "mm_dwukv")
    dcqn = _mm(dq_, w["od_w_uq"], "nt", "mm_dcqn")
    g["od_w_uq"] = _mm(cqn, dq_, "tn", "mm_dwuq")
    (dcq, dckv), (g["od_g_qa"], g["od_g_kva"]) = _row_bwd(
        "cnorm_b", _f_cnorm, r_c, [w["od_g_qa"], w["od_g_kva"]],
        [(dcqn, C_Q_LORA, 0), (dckvn, C_KV_LORA, 0)], tm, [True, True], out_dtypes=[BF16, BF16])
    zpad = jnp.zeros((tp, HP), BF16)
    dz1 = jnp.concatenate([dfq, dfk, dvf.astype(BF16), dcq, zpad, dckv, (dmisc_f + dmisc_m).astype(BF16), zpad],
                          axis=1)
    g["od_w_in"] = _mm(xn2, dz1, "tn", "mm_dwin1")
    dxn2 = _mm(dz1, w["od_w_in"], "nt", "mm_dxn2")
    (dh2,), (g["od_ln"],) = _row_bwd("norm2_b", _f_norm, [(h2, D_MODEL, 0)], [w["od_ln"]], [(dxn2, D_MODEL, 0)], tm,
                                     [True], add=dh3)
    dh1, g["mlp_ln0"], g["w_ff1_0"], g["w_ff2_0"] = mlp_bwd("0", dh2, h1, xn1, p0, act0, w["mlp_ln0"],
                                                            w["w_ff1_0"], w["w_ff2_0"])
    dhab = _mm(dh1, w["ev_w_out"], "nt", "mm_dhab")
    g["ev_w_out"] = _mm(hab, dh1, "tn", "mm_dwout0")
    (dhs, dgb), _ = _row_bwd("bout_b", _f_bout, r_bout, [], [(dhab, 1024, 1)], tm, [True, True],
                             out_dtypes=[F32, BF16])
    du_g, da_g = _scan_bwd(a_g, hprev, dhs, "lru_b")
    (dxc,), (g["ev_w_ra"], g["ev_b_ra"], g["ev_w_rx"], g["ev_b_rx"], g["ev_lam"]) = _row_bwd(
        "gates_b", _f_gates, [(xc, 1024, 0)], p_gates, [(da_g, 1024, 0), (du_g, 1024, 0)], tm, [True])
    dxb, g["ev_conv_w"], g["ev_conv_b"] = _conv_bwd(z0, 3072 // 256, w["ev_conv_w"], dxc)
    (dh_a, do_), (g["ev_a_norm"],) = _row_bwd("aout_b", _f_aout, r_aout, [w["ev_a_norm"]], [(dhab, 1024, 0)], tm,
                                              [True, True], out_dtypes=[F32, BF16])
    dq, dk, dv, dli, dlf = hosted("mlstm_bwd", _mlstm_bwd, z0, li, lf, cs, ns, ms, dh_a)
    dg0 = jnp.concatenate([dli[:, :, 0].T, dlf[:, :, 0].T, jnp.zeros((tp, HP - 2 * A_HEADS), F32)], axis=1)
    (dmisc0,), (g["ev_b_if"],) = _row_bwd("gate0_b", _f_gate0, r_misc0, [w["ev_b_if"]], [(dg0, HP, 0)], tm, [True],
                                          out_dtypes=[BF16])
    dz0 = jnp.concatenate([dq, dk, dv, do_, dxb, dgb, dmisc0, zpad], axis=1)
    g["ev_w_in"] = _mm(xn0, dz0, "tn", "mm_dwin0")
    dxn0 = hosted("mm_dxn0", _mm, dz0, w["ev_w_in"], "nt", "mm_dxn0")
    (dh0,), (g["ev_ln"],) = _row_bwd("norm0_b", _f_norm, r_h0, [w["ev_ln"]], [(dxn0, D_MODEL, 0)], tm, [True], add=dh1)
    g["meta"] = dh0[PAD:front]
    return loss, dh0[front:], g


def _pad_last(a, n):
    return jnp.pad(a, [(0, 0)] * (a.ndim - 1) + [(0, n - a.shape[-1])])


def _pad_heads(a, nh, d):
    return _pad_last(a.reshape(a.shape[:-1] + (nh, d)), HP).reshape(a.shape[:-1] + (nh * HP,))


def _unpad_heads(a, nh, d):
    return a.reshape(a.shape[:-1] + (nh, HP))[..., :d].reshape(a.shape[:-1] + (nh * d,))


_MM_UNITS = ("ev_w_in", "ev_w_out", "od_w_in", "od_w_uq", "od_w_ukv", "od_w_out",
             "w_ff1_0", "w_ff2_0", "w_ff1_1", "w_ff2_1")


def _mw_pad(name, a):
    if name == "ev_w_in":
        return _pad_last(jnp.concatenate([a[:, :3072], a[:, 3080:5128], a[:, 3072:3080]], axis=1), ZE)
    if name == "od_w_in":
        z = lambda n: jnp.zeros((a.shape[0], n), a.dtype)
        return jnp.concatenate(
            [_pad_heads(a[:, 672:1184], D_HEADS, D_HD), _pad_heads(a[:, 1184:1696], D_HEADS, D_HD),
             _pad_heads(a[:, 1696:2208], D_HEADS, D_HD), a[:, 0:384], z(128), a[:, 384:640],
             z(64), a[:, 640:672], a[:, 2208:2216], z(24), z(128)], axis=1)
    if name == "od_w_uq":
        return _pad_heads(a, C_HEADS, C_NOPE + C_ROPE)
    if name == "od_w_ukv":
        wkv = a.reshape(C_KV_LORA, C_HEADS, C_NOPE + C_V)
        return jnp.concatenate([_pad_last(wkv[:, :, :C_NOPE], HP).reshape(C_KV_LORA, -1),
                                _pad_last(wkv[:, :, C_NOPE:], HP).reshape(C_KV_LORA, -1)], axis=1)
    if name == "od_w_out":
        return jnp.pad(a.reshape(2 * C_HEADS, C_V, D_MODEL), ((0, 0), (0, HP - C_V), (0, 0))).reshape(-1, D_MODEL)
    return a


def _mw_unpad(name, g):
    if name == "ev_w_in":
        return jnp.concatenate([g[:, :3072], g[:, 5120:5128], g[:, 3072:5120]], axis=1)
    if name == "od_w_in":
        return jnp.concatenate(
            [g[:, 3072:3456], g[:, 3584:3840], g[:, 3904:3936], _unpad_heads(g[:, 0:1024], D_HEADS, D_HD),
             _unpad_heads(g[:, 1024:2048], D_HEADS, D_HD), _unpad_heads(g[:, 2048:3072], D_HEADS, D_HD),
             g[:, 3936:3944]], axis=1)
    if name == "od_w_uq":
        return _unpad_heads(g, C_HEADS, C_NOPE + C_ROPE)
    if name == "od_w_ukv":
        gk = g[:, :C_HEADS * HP].reshape(C_KV_LORA, C_HEADS, HP)[:, :, :C_NOPE]
        gv = g[:, C_HEADS * HP:].reshape(C_KV_LORA, C_HEADS, HP)[:, :, :C_V]
        return jnp.concatenate([gk, gv], axis=2).reshape(C_KV_LORA, -1)
    if name == "od_w_out":
        return g.reshape(2 * C_HEADS, HP, D_MODEL)[:, :C_V].reshape(-1, D_MODEL)
    return g


def _prep_weights(p):
    w = {}
    w["meta"] = p["meta"]
    for k in ("ev_ln", "ev_a_norm", "ev_conv_b", "ev_b_ra", "ev_b_rx", "ev_lam", "od_ln", "od_g_qa", "od_g_kva"):
        w[k] = p[k].reshape(1, -1)
    w["ev_b_if"] = _pad_last(p["ev_b_if"].reshape(1, -1), HP)
    w["ev_conv_w"] = p["ev_conv_w"][0]
    w["ev_w_ra"] = p["ev_w_ra"][0]
    w["ev_w_rx"] = p["ev_w_rx"][0]
    f1 = lambda a: a.reshape(1, -1)
    w["gq_full"] = _pad_last(jnp.concatenate([f1(p["od_g_qn"]), f1(p["od_g_qr"])], axis=1), HP)
    w["gk_full"] = _pad_last(jnp.concatenate([f1(p["od_g_kn"]), f1(p["od_g_kr"])], axis=1), HP)
    w["gfq_full"] = _pad_last(f1(p["od_g_fq"]), HP)
    w["gfk_full"] = _pad_last(f1(p["od_g_fk"]), HP)
    w["bf_full"] = _pad_last(jnp.concatenate([jnp.zeros((1, 96), F32), f1(p["od_b_f"])], axis=1), HP)
    for l in (0, 1):
        w[f"mlp_ln{l}"] = p["mlp_ln"][l:l + 1]
    for n in _MM_UNITS:
        if n in p:
            w[n] = _mw_pad(n, p[n])
    return w


def _unprep_grads(g):
    o = {}
    o["meta"] = g["meta"]
    for k in ("ev_ln", "ev_a_norm", "ev_conv_b", "ev_b_ra", "ev_b_rx", "ev_lam", "od_ln", "od_g_qa", "od_g_kva"):
        o[k] = g[k].reshape(1, -1)
    o["ev_b_if"] = g["ev_b_if"][:, :2 * A_HEADS]
    o["ev_conv_w"] = g["ev_conv_w"][None]
    o["ev_w_ra"] = g["ev_w_ra"][None]
    o["ev_w_rx"] = g["ev_w_rx"][None]
    o["od_g_qn"] = g["gq_full"][:, :C_NOPE]
    o["od_g_qr"] = g["gq_full"][:, C_NOPE:C_NOPE + C_ROPE]
    o["od_g_kn"] = g["gk_full"][:, :C_NOPE]
    o["od_g_kr"] = g["gk_full"][:, C_NOPE:C_NOPE + C_ROPE]
    o["od_g_fq"] = g["gfq_full"][:, :D_HD]
    o["od_g_fk"] = g["gfk_full"][:, :D_HD]
    o["od_b_f"] = g["bf_full"][:, 96:96 + D_HEADS]
    o["mlp_ln"] = jnp.concatenate([g["mlp_ln0"], g["mlp_ln1"]], axis=0)
    return o


def _exchange(ops, name):
    n_ops = len(ops)
    flags = [s for _, s in ops]

    def body(*refs):
        local, remote = _comm_copies(refs[:n_ops], refs[n_ops:2 * n_ops], flags, *refs[2 * n_ops:])
        for cp in local + remote:
            cp.start()
        _comm_wait(local, remote)

    any_spec = pl.BlockSpec(memory_space=pl.ANY)
    c_shape, c_sems = _comm_shapes(ops)
    return pl.pallas_call(body, name=name, out_shape=c_shape, in_specs=[any_spec] * n_ops,
                          out_specs=[any_spec] * n_ops, scratch_shapes=c_sems)(*[x for x, _ in ops])


def _adamw(parts, w, m, v, name):
    r, c = w.shape
    tr = r
    for cand in (512, 256, 128, 64, 32, 16):
        if r % cand == 0 and N_DEV * cand * c * 4 <= 4 * 1024 * 1024:
            tr = cand
            break
    c1 = 1.0 / (1.0 - ADAM_B1 ** ADAM_STEP)
    c2 = 1.0 / (1.0 - ADAM_B2 ** ADAM_STEP)

    def body(p_ref, w_ref, m_ref, v_ref, g_ref, d_ref, mo_ref, vo_ref):
        g = p_ref[0].astype(F32)
        for j in range(1, N_DEV):
            g = g + p_ref[j].astype(F32)
        m2 = ADAM_B1 * m_ref[...] + (1.0 - ADAM_B1) * g
        v2 = ADAM_B2 * v_ref[...] + (1.0 - ADAM_B2) * (g * g)
        g_ref[...] = g
        mo_ref[...] = m2
        vo_ref[...] = v2
        d_ref[...] = -ADAM_LR * ((m2 * c1) / (jnp.sqrt(v2 * c2) + ADAM_EPS) + ADAM_WD * w_ref[...])

    spec = pl.BlockSpec((tr, c), lambda i: (i, 0))
    return pl.pallas_call(
        body, name=name, grid=(r // tr,),
        in_specs=[pl.BlockSpec((N_DEV, tr, c), lambda i: (0, i, 0)), spec, spec, spec],
        out_specs=[spec] * 4, out_shape=[jax.ShapeDtypeStruct((r, c), F32)] * 4,
        compiler_params=_cparams(("parallel",)),
    )(parts, w, m, v)


def _rows_for(n, mult):
    return -(-n // (1024 * mult)) * mult


def _pack(arrs, mult, lead=()):
    nl = len(lead)
    flat = jnp.concatenate([a.reshape(lead + (-1,)) for a in arrs], axis=nl)
    rows = _rows_for(flat.shape[-1], mult)
    return jnp.pad(flat, [(0, 0)] * nl + [(0, rows * 1024 - flat.shape[-1])]).reshape(lead + (rows, 1024))


def _unpack(buf, shapes):
    lead = buf.shape[:-2]
    flat = buf.reshape(lead + (-1,))
    out, off = [], 0
    for s in shapes:
        n = 1
        for d_ in s:
            n *= d_
        out.append(flat[..., off:off + n].reshape(lead + tuple(s)))
        off += n
    return out


def _unshard(g8, ax):
    a = jnp.moveaxis(g8, 0, ax)
    return a.reshape(a.shape[:ax] + (N_DEV * a.shape[ax + 1],) + a.shape[ax + 2:])


def _shard8(full, ax):
    s = full.shape
    return jnp.moveaxis(full.reshape(s[:ax] + (N_DEV, s[ax] // N_DEV) + s[ax + 1:]), ax, 0)


_NAMES = ["meta", "ev_ln", "ev_w_in", "ev_b_if", "ev_a_norm", "ev_conv_w", "ev_conv_b", "ev_w_ra", "ev_b_ra",
          "ev_w_rx", "ev_b_rx", "ev_lam", "ev_w_out", "od_ln", "od_w_in", "od_b_f", "od_g_qa", "od_g_kva",
          "od_w_uq", "od_w_ukv", "od_g_qn", "od_g_qr", "od_g_kn", "od_g_kr", "od_g_fq", "od_g_fk", "od_w_out",
          "mlp_ln", "w_ff1", "w_ff2"]
_SHARD_AXIS = {"meta": 1, "ev_w_in": 2, "ev_conv_w": 2, "ev_w_out": 1, "od_ln": 1, "od_w_in": 2, "od_g_qa": 1,
               "od_g_kva": 1, "od_w_uq": 2, "od_w_ukv": 2, "od_w_out": 1, "w_ff1": 2, "w_ff2": 1}
_MATMUL_WEIGHTS = ("ev_w_in", "ev_w_out", "od_w_in", "od_w_uq", "od_w_ukv", "od_w_out", "w_ff1", "w_ff2")
_BIG_REPL = ("ev_w_ra", "ev_w_rx")
_COL_SHARDED = ("ev_w_in", "od_w_in", "od_w_uq", "od_w_ukv", "w_ff1_0", "w_ff1_1")
_GATHER_ON = {"mm_z0": ("od_w_in", "od_w_uq", "od_w_ukv"), "mlstm_fwd": ("ev_w_out", "w_ff1_0", "w_ff2_0"),
              "mla_attn_f": ("od_w_out", "w_ff1_1", "w_ff2_1")}
_SCATTER_ON = {"fox_attn_b": ("w_ff2_1", "w_ff1_1", "od_w_out"),
               "mlstm_bwd": ("od_w_ukv", "od_w_uq", "od_w_in", "w_ff2_0", "w_ff1_0", "ev_w_out"),
               "mm_dxn0": ("ev_w_in",)}
_REPL_ON = "mlstm_bwd"


def _unit_of(d, n):
    return d[n[:-2]][int(n[-1])] if n.startswith("w_ff") else d[n][0]


def _unit_full(n, g8):
    return jnp.transpose(g8, (1, 0, 2)).reshape(g8.shape[1], -1) if n in _COL_SHARDED else g8.reshape(-1, g8.shape[2])


def _unit_slots(n, full, r, c):
    return full.reshape(r, N_DEV, c).transpose(1, 0, 2) if n in _COL_SHARDED else full.reshape(N_DEV, r, c)


class _Plan:
    def __init__(self, shards):
        self.shards = shards
        self.parts = {}

    def slots(self, n, g):
        r, c = self.shards[n].shape
        return _unit_slots(n, _mw_unpad(n, g[n]), r, c).astype(BF16)

    def ride(self, tag, g):
        if tag in _GATHER_ON:
            return [(self.shards[n].astype(BF16), False) for n in _GATHER_ON[tag]]
        ops = [(self.slots(n, g), True) for n in _SCATTER_ON[tag]]
        if tag == _REPL_ON:
            ops += [(g[n].reshape(-1, B_BLOCK), False) for n in _BIG_REPL]
        return ops

    def arrived(self, tag, got, w):
        if tag in _GATHER_ON:
            for n, g8 in zip(_GATHER_ON[tag], got):
                w[n] = _mw_pad(n, _unit_full(n, g8))
        else:
            self.parts.update(zip(_SCATTER_ON[tag] + (_BIG_REPL if tag == _REPL_ON else ()), got))


def kernel(x, positions, meta, ev_ln, ev_w_in, ev_b_if, ev_a_norm, ev_conv_w, ev_conv_b, ev_w_ra, ev_b_ra, ev_w_rx, ev_b_rx, ev_lam, ev_w_out, od_ln, od_w_in, od_b_f, od_g_qa, od_g_kva, od_w_uq, od_w_ukv, od_g_qn, od_g_qr, od_g_kn, od_g_kr, od_g_fq, od_g_fk, od_w_out, mlp_ln, w_ff1, w_ff2, loss_target, m_meta, m_ev_ln, m_ev_w_in, m_ev_b_if, m_ev_a_norm, m_ev_conv_w, m_ev_conv_b, m_ev_w_ra, m_ev_b_ra, m_ev_w_rx, m_ev_b_rx, m_ev_lam, m_ev_w_out, m_od_ln, m_od_w_in, m_od_b_f, m_od_g_qa, m_od_g_kva, m_od_w_uq, m_od_w_ukv, m_od_g_qn, m_od_g_qr, m_od_g_kn, m_od_g_kr, m_od_g_fq, m_od_g_fk, m_od_w_out, m_mlp_ln, m_w_ff1, m_w_ff2, v_meta, v_ev_ln, v_ev_w_in, v_ev_b_if, v_ev_a_norm, v_ev_conv_w, v_ev_conv_b, v_ev_w_ra, v_ev_b_ra, v_ev_w_rx, v_ev_b_rx, v_ev_lam, v_ev_w_out, v_od_ln, v_od_w_in, v_od_b_f, v_od_g_qa, v_od_g_kva, v_od_w_uq, v_od_w_ukv, v_od_g_qn, v_od_g_qr, v_od_g_kn, v_od_g_kr, v_od_g_fq, v_od_g_fk, v_od_w_out, v_mlp_ln, v_w_ff1, v_w_ff2):
    given = dict(locals())
    wts = {n: given[n] for n in _NAMES}
    mom = {n: given["m_" + n] for n in _NAMES}
    var = {n: given["v_" + n] for n in _NAMES}
    small_sh = [n for n in _NAMES if n in _SHARD_AXIS and n not in _MATMUL_WEIGHTS]
    small_rp = [n for n in _NAMES if n not in _SHARD_AXIS and n not in _BIG_REPL]
    shp = {n: wts[n].shape for n in _NAMES}
    plan = _Plan({n: _unit_of(wts, n) for n in _MM_UNITS})

    got = _exchange([(plan.shards["ev_w_in"].astype(BF16), False), (_pack([wts[n] for n in small_sh], 8), False)],
                    "gather_first")
    p = {n: wts[n] for n in _NAMES if n not in _SHARD_AXIS}
    for n, a in zip(small_sh, _unpack(got[1], [shp[n] for n in small_sh])):
        p[n] = _unshard(a, _SHARD_AXIS[n])
    p["ev_w_in"] = _unit_full("ev_w_in", got[0])

    loss, gx, g = _local_step(x[0], positions[0], loss_target[0], _prep_weights(p), plan)
    grads = _unprep_grads(g)

    parts = _exchange([(_pack([_shard8(grads[n], _SHARD_AXIS[n]) for n in small_sh], 8, (N_DEV,)), True),
                       (_pack([grads[n].reshape(shp[n]) for n in small_rp], 8), False)], "exchange_last")

    res = {}
    unit_res = {n: _adamw(plan.parts[n], *[_unit_of(d, n) for d in (wts, mom, var)], f"adamw_{n}")
                for n in _MM_UNITS}
    for n in _MATMUL_WEIGHTS:
        if n.startswith("w_ff"):
            res[n] = [jnp.stack([unit_res[n + "_0"][k], unit_res[n + "_1"][k]]) for k in range(4)]
        else:
            res[n] = [r[None] for r in unit_res[n]]
    r4 = _adamw(parts[0], *[_pack([d[n] for n in small_sh], 8) for d in (wts, mom, var)], "adamw_small_sharded")
    for n, *four in zip(small_sh, *[_unpack(r, [shp[n] for n in small_sh]) for r in r4]):
        res[n] = four
    for n in _BIG_REPL:
        r4 = _adamw(plan.parts[n], *[d[n].reshape(-1, B_BLOCK) for d in (wts, mom, var)], f"adamw_{n}")
        res[n] = [r.reshape(shp[n]) for r in r4]
    r4 = _adamw(parts[1], *[_pack([d[n] for n in small_rp], 8) for d in (wts, mom, var)], "adamw_small_repl")
    for n, *four in zip(small_rp, *[_unpack(r, [shp[n] for n in small_rp]) for r in r4]):
        res[n] = four

    outs = [res[n][kind] for kind in range(4) for n in _NAMES]
    loss = lax.psum(loss, ("x", "y", "c"))
    return (loss, gx[None], *outs)
```

```python
import functools

import jax
import jax.numpy as jnp
from jax import lax
from jax.experimental import pallas as pl
from jax.experimental.pallas import tpu as pltpu

F32 = jnp.float32
BF16 = jnp.bfloat16

D_MODEL = 1024
N_META = 16
PAD = 112
EPS = 1e-6
NEG = -1e30
A_HEADS, A_DQK, A_DV, A_CHUNK = 4, 128, 256, 64
B_BLOCKS, B_BLOCK, CONV_W, LRU_C = 8, 128, 4, 8.0
C_HEADS, C_Q_LORA, C_KV_LORA, C_NOPE, C_ROPE, C_V = 8, 384, 256, 64, 32, 64
ROPE_THETA = 10000.0
D_HEADS, D_HD = 8, 64
HP = 128
ZE = 5376
ZO = 4096
N_DEV = 8
ADAM_LR, ADAM_B1, ADAM_B2, ADAM_EPS, ADAM_WD, ADAM_STEP = 0.001, 0.9, 0.999, 1e-08, 0.01, 10
VMEM_LIMIT = 56 * 1024 * 1024
SCAN_BLOCK = 128


def _pick(n, prefs):
    for p in prefs:
        if n % p == 0:
            return p
    return n


def _cparams(dims):
    return pltpu.CompilerParams(dimension_semantics=dims, vmem_limit_bytes=VMEM_LIMIT)


def _full_spec(shape):
    nd = len(shape)
    return pl.BlockSpec(shape, lambda *_: (0,) * nd)


def _me_and_peers():
    mx, my, mc = lax.axis_index("x"), lax.axis_index("y"), lax.axis_index("c")
    peers = []
    for k in range(1, N_DEV):
        px, py, pc = mx ^ ((k >> 2) & 1), my ^ ((k >> 1) & 1), mc ^ (k & 1)
        peers.append(((px, py, pc), 4 * px + 2 * py + pc))
    return 4 * mx + 2 * my + mc, peers


def _comm_copies(x_refs, o_refs, flags, send_sems, recv_sems, local_sems):
    n_ops, n_peer = len(flags), N_DEV - 1
    me, peers = _me_and_peers()
    local = [pltpu.make_async_copy(x_refs[a].at[me] if flags[a] else x_refs[a], o_refs[a].at[me], local_sems.at[a])
             for a in range(n_ops)]
    remote = []
    for k, (peer, pid) in enumerate(peers):
        for a in range(n_ops):
            remote.append(pltpu.make_async_remote_copy(
                src_ref=x_refs[a].at[pid] if flags[a] else x_refs[a], dst_ref=o_refs[a].at[me],
                send_sem=send_sems.at[a * n_peer + k], recv_sem=recv_sems.at[a * n_peer + k],
                device_id=peer, device_id_type=pl.DeviceIdType.MESH))
    return local, remote


def _comm_wait(local, remote):
    for cp in remote:
        cp.wait_send()
    for cp in remote:
        cp.wait_recv()
    for cp in local:
        cp.wait()


def _comm_shapes(comm):
    n = len(comm)
    out_shape = [jax.ShapeDtypeStruct((N_DEV,) + x.shape[-2:], x.dtype) for x, _ in comm]
    sems = [pltpu.SemaphoreType.DMA((n * (N_DEV - 1),)), pltpu.SemaphoreType.DMA((n * (N_DEV - 1),)),
            pltpu.SemaphoreType.DMA((n,))]
    return out_shape, sems


def _pcall(body, name, grid, in_specs, out_specs, out_shape, scratch_shapes, dims, ins, comm=None):
    if not comm:
        return pl.pallas_call(body, name=name, grid=grid, in_specs=in_specs, out_specs=out_specs,
                              out_shape=out_shape, scratch_shapes=scratch_shapes,
                              compiler_params=_cparams(dims))(*ins)
    n_in, n_out, n = len(in_specs), len(out_specs), len(comm)
    flags = [s for _, s in comm]
    c_shape, c_sems = _comm_shapes(comm)

    def riding(*refs):
        cx = refs[n_in:n_in + n]
        co = refs[n_in + n + n_out:n_in + 2 * n + n_out]
        rest = refs[n_in + 2 * n + n_out:]
        sems = rest[len(rest) - 3:]
        ids = [pl.program_id(a) for a in range(len(grid))]
        first = functools.reduce(jnp.logical_and, [i == 0 for i in ids])
        last = functools.reduce(jnp.logical_and, [i == g - 1 for i, g in zip(ids, grid)])

        @pl.when(first)
        def _():
            local, remote = _comm_copies(cx, co, flags, *sems)
            for cp in local + remote:
                cp.start()

        body(*refs[:n_in], *refs[n_in + n:n_in + n + n_out], *rest[:len(rest) - 3])

        @pl.when(last)
        def _():
            _comm_wait(*_comm_copies(cx, co, flags, *sems))

    any_spec = pl.BlockSpec(memory_space=pl.ANY)
    res = pl.pallas_call(
        riding, name=name, grid=grid, in_specs=list(in_specs) + [any_spec] * n,
        out_specs=list(out_specs) + [any_spec] * n, out_shape=list(out_shape) + c_shape,
        scratch_shapes=list(scratch_shapes) + c_sems,
        compiler_params=_cparams(("arbitrary",) * len(grid)))(*ins, *[x for x, _ in comm])
    return list(res[:n_out]), list(res[n_out:])


def _mm(a, b, mode, name, out_dtype=None, add=None, relu2=False, relu2_of=None, comm=None):
    if out_dtype is None:
        out_dtype = BF16 if (mode == "tn" or relu2_of is not None) else F32
    if relu2_of is not None:
        add = relu2_of
    if mode == "nn":
        (m, k), n = a.shape, b.shape[1]
    elif mode == "nt":
        (m, k), n = a.shape, b.shape[0]
    else:
        (k, m), n = a.shape, b.shape[1]
    tm = _pick(m, (1408, 1024, 768, 512, 384, 256, 128))
    tn = _pick(n, (1024, 768, 512, 384, 256, 128))
    tk = _pick(k, (1408, 1024, 768, 512, 384, 256, 128))
    nk = k // tk
    if mode == "nn":
        a_spec = pl.BlockSpec((tm, tk), lambda i, j, q: (i, q))
        b_spec = pl.BlockSpec((tk, tn), lambda i, j, q: (q, j))
        dn = (((1,), (0,)), ((), ()))
    elif mode == "nt":
        a_spec = pl.BlockSpec((tm, tk), lambda i, j, q: (i, q))
        b_spec = pl.BlockSpec((tn, tk), lambda i, j, q: (j, q))
        dn = (((1,), (1,)), ((), ()))
    else:
        a_spec = pl.BlockSpec((tk, tm), lambda i, j, q: (q, i))
        b_spec = pl.BlockSpec((tk, tn), lambda i, j, q: (q, j))
        dn = (((0,), (0,)), ((), ()))
    o_spec = pl.BlockSpec((tm, tn), lambda i, j, q: (i, j))
    has_add = add is not None

    def body(*refs):
        a_ref, b_ref = refs[:2]
        add_ref = refs[2] if has_add else None
        o_refs, acc = refs[2 + has_add:-1], refs[-1]
        q = pl.program_id(2)

        @pl.when(q == 0)
        def _():
            acc[...] = jnp.zeros_like(acc)

        acc[...] += lax.dot_general(a_ref[...].astype(BF16), b_ref[...].astype(BF16), dn,
                                    preferred_element_type=F32)

        @pl.when(q == nk - 1)
        def _():
            r = acc[...]
            if relu2_of is not None:
                r = r * (2.0 * jnp.maximum(add_ref[...], 0.0))
            elif has_add:
                r = r + add_ref[...]
            o_refs[0][...] = r.astype(o_refs[0].dtype)
            if relu2:
                pos = jnp.maximum(r, 0.0)
                o_refs[1][...] = (pos * pos).astype(o_refs[1].dtype)

    ins = [a, b] + ([add] if has_add else [])
    in_specs = [a_spec, b_spec] + ([o_spec] if has_add else [])
    out_shape = [jax.ShapeDtypeStruct((m, n), out_dtype)] + ([jax.ShapeDtypeStruct((m, n), BF16)] if relu2 else [])
    res = _pcall(body, name, (m // tm, n // tn, nk), in_specs, [o_spec] * len(out_shape), out_shape,
                 [pltpu.VMEM((tm, tn), F32)], ("parallel", "parallel", "arbitrary"), ins, comm)
    outs = res[0] if comm else res
    outs = tuple(outs) if relu2 else outs[0]
    return (outs, res[1]) if comm else outs


def _row_specs(rows, tm):
    return [pl.BlockSpec((tm, w), functools.partial(lambda cb, i: (i, cb), cb)) for (_, w, cb) in rows]


def _row_fwd(name, f, rows, params, outs, tm):
    tp = rows[0][0].shape[0]
    nr, npar = len(rows), len(params)

    def body(*refs):
        i = pl.program_id(0)
        rv = [r[...] for r in refs[:nr]]
        pv = [r[...] for r in refs[nr:nr + npar]]
        res = f(i, rv, pv)
        for o_ref, r in zip(refs[nr + npar:], res):
            o_ref[...] = r.astype(o_ref.dtype)

    res = pl.pallas_call(
        body, name=name, grid=(tp // tm,),
        in_specs=_row_specs(rows, tm) + [_full_spec(p.shape) for p in params],
        out_specs=[pl.BlockSpec((tm, w), lambda i: (i, 0)) for (w, _) in outs],
        out_shape=[jax.ShapeDtypeStruct((tp, w), dt) for (w, dt) in outs],
        compiler_params=_cparams(("parallel",)),
    )(*[r[0] for r in rows], *params)
    return list(res)


def _row_bwd(name, f, rows, params, douts, tm, diff, add=None, out_dtypes=None):
    tp = rows[0][0].shape[0]
    nr, npar, nd = len(rows), len(params), len(douts)
    didx = [k for k in range(nr) if diff[k]]
    has_add = add is not None

    def body(*refs):
        i = pl.program_id(0)
        rv = [r[...] for r in refs[:nr]]
        pv = [r[...] for r in refs[nr:nr + npar]]
        dv = [r[...] for r in refs[nr + npar:nr + npar + nd]]
        pos = nr + npar + nd
        add_ref = refs[pos] if has_add else None
        pos += 1 if has_add else 0
        dr_refs = refs[pos:pos + len(didx)]
        dp_refs = refs[pos + len(didx):]

        def g(drv, pvs):
            full = list(rv)
            for k, val in zip(didx, drv):
                full[k] = val
            return tuple(f(i, full, list(pvs)))

        _, vjp = jax.vjp(g, [rv[k] for k in didx], pv)
        d_r, d_p = vjp(tuple(dv))
        for n_, (ref, val) in enumerate(zip(dr_refs, d_r)):
            if has_add and n_ == 0:
                val = val + add_ref[...]
            ref[...] = val.astype(ref.dtype)

        @pl.when(i == 0)
        def _():
            for ref in dp_refs:
                ref[...] = jnp.zeros_like(ref)

        for ref, val in zip(dp_refs, d_p):
            ref[...] += val

    in_specs = (_row_specs(rows, tm) + [_full_spec(p.shape) for p in params] + _row_specs(douts, tm))
    ins = [r[0] for r in rows] + list(params) + [d[0] for d in douts]
    if has_add:
        in_specs.append(pl.BlockSpec((tm, rows[didx[0]][1]), lambda i: (i, 0)))
        ins.append(add)
    out_specs = ([pl.BlockSpec((tm, rows[k][1]), lambda i: (i, 0)) for k in didx]
                 + [_full_spec(p.shape) for p in params])
    out_dtypes = out_dtypes or [F32] * len(didx)
    out_shape = ([jax.ShapeDtypeStruct((tp, rows[k][1]), dt) for k, dt in zip(didx, out_dtypes)]
                 + [jax.ShapeDtypeStruct(p.shape, F32) for p in params])
    res = pl.pallas_call(
        body, name=name, grid=(tp // tm,), in_specs=in_specs, out_specs=out_specs, out_shape=out_shape,
        compiler_params=_cparams(("arbitrary",)),
    )(*ins)
    return list(res[:len(didx)]), list(res[len(didx):])


def _concat_cols(pieces, dtype, name):
    arrs = [p for p in pieces if not isinstance(p, int)]
    widths = [p if isinstance(p, int) else p.shape[1] for p in pieces]
    tp = arrs[0].shape[0]
    tm = _pick(tp, (384, 256, 128))

    def body(*refs):
        o_ref, k, off = refs[-1], 0, 0
        for p, w in zip(pieces, widths):
            if isinstance(p, int):
                o_ref[:, off:off + w] = jnp.zeros((tm, w), o_ref.dtype)
            else:
                o_ref[:, off:off + w] = refs[k][...].astype(o_ref.dtype)
                k += 1
            off += w

    return pl.pallas_call(
        body, name=name, grid=(tp // tm,),
        in_specs=[pl.BlockSpec((tm, a.shape[1]), lambda i: (i, 0)) for a in arrs],
        out_specs=pl.BlockSpec((tm, sum(widths)), lambda i: (i, 0)),
        out_shape=jax.ShapeDtypeStruct((tp, sum(widths)), dtype),
        compiler_params=_cparams(("parallel",)),
    )(*arrs)


def _rowmask(i, tm):
    return (i * tm + lax.broadcasted_iota(jnp.int32, (tm, 1), 0)) >= PAD


def _lane(n=HP):
    return lax.broadcasted_iota(jnp.int32, (1, n), 1)


def _softplus(x):
    return jnp.maximum(x, 0.0) + jnp.log(1.0 + jnp.exp(-jnp.abs(x)))


def _log_sigmoid(x):
    return -_softplus(-x)


def _sigmoid(x):
    return 1.0 / (1.0 + jnp.exp(-x))


@functools.partial(jax.custom_vjp, nondiff_argnums=(1,))
def _lroll(x, s):
    return pltpu.roll(x, s % HP, 1)


def _lroll_fwd(x, s):
    return _lroll(x, s), None


def _lroll_bwd(s, _, g):
    return (pltpu.roll(g, (-s) % HP, 1),)


_lroll.defvjp(_lroll_fwd, _lroll_bwd)


def _f_norm(i, rv, pv):
    (h,), (g,) = rv, pv
    return [h * lax.rsqrt(jnp.mean(h * h, axis=-1, keepdims=True) + EPS) * g]


def _f_gate0(i, rv, pv):
    (misc,), (b,) = rv, pv
    tm = misc.shape[0]
    x = misc + b
    lane, ok = _lane(), _rowmask(i, tm)
    li = jnp.where(ok, x, NEG)
    lf = jnp.where(ok, _log_sigmoid(x), 0.0)
    return [jnp.where(lane < A_HEADS, li, jnp.where(lane < 2 * A_HEADS, lf, 0.0))]


def _f_aout(i, rv, pv):
    (ha, o), (g,) = rv, pv
    outs = []
    for h in range(A_HEADS):
        x = ha[:, h * A_DV:(h + 1) * A_DV]
        outs.append(x * lax.rsqrt(jnp.mean(x * x, axis=-1, keepdims=True) + EPS) * g)
    return [jnp.concatenate(outs, axis=1) * _sigmoid(o)]


def _f_gates(i, rv, pv):
    (xc,), (w_ra, b_ra, w_rx, b_rx, lam) = rv, pv
    tm = xc.shape[0]
    ra, rx = [], []
    for g in range(B_BLOCKS):
        xg = xc[:, g * B_BLOCK:(g + 1) * B_BLOCK].astype(BF16)
        ra.append(jnp.dot(xg, w_ra[g].astype(BF16), preferred_element_type=F32))
        rx.append(jnp.dot(xg, w_rx[g].astype(BF16), preferred_element_type=F32))
    r = _sigmoid(jnp.concatenate(ra, axis=1) + b_ra)
    ig = _sigmoid(jnp.concatenate(rx, axis=1) + b_rx)
    log_a = -LRU_C * r * _softplus(-lam)
    a = jnp.exp(log_a)
    u = jnp.sqrt(1.0 - jnp.exp(2.0 * log_a)) * (ig * xc)
    return [a, jnp.where(_rowmask(i, tm), u, 0.0)]


def _f_bout(i, rv, pv):
    hs, gb = rv
    gelu = 0.5 * gb * (1.0 + jnp.tanh(0.7978845608028654 * (gb + 0.044715 * gb * gb * gb)))
    return [hs * gelu]


def _f_cnorm(i, rv, pv):
    (cq, ckv), (gq, gkv) = rv, pv
    return [cq * lax.rsqrt(jnp.mean(cq * cq, axis=-1, keepdims=True) + EPS) * gq,
            ckv * lax.rsqrt(jnp.mean(ckv * ckv, axis=-1, keepdims=True) + EPS) * gkv]


def _rope128(x, cos, sin):
    lane = _lane()
    rot = jnp.where((lane >= C_NOPE) & (lane < C_NOPE + C_ROPE // 2), -_lroll(x, -(C_ROPE // 2)),
                    jnp.where((lane >= C_NOPE + C_ROPE // 2) & (lane < C_NOPE + C_ROPE), _lroll(x, C_ROPE // 2), 0.0))
    return x * cos + rot * sin


def _f_mlaprep(i, rv, pv):
    (q_, kk_, misc, cos, sin), (gq, gk) = rv, pv
    lane = _lane()
    m_n = lane < C_NOPE
    m_r = (lane >= C_NOPE) & (lane < C_NOPE + C_ROPE)

    def norm2(x, g):
        x2 = x * x
        sn = jnp.sum(jnp.where(m_n, x2, 0.0), axis=-1, keepdims=True) * (1.0 / C_NOPE)
        sr = jnp.sum(jnp.where(m_r, x2, 0.0), axis=-1, keepdims=True) * (1.0 / C_ROPE)
        scale = jnp.where(m_n, lax.rsqrt(sn + EPS), jnp.where(m_r, lax.rsqrt(sr + EPS), 0.0))
        return x * scale * g

    kr = _rope128(norm2(jnp.where(m_r, misc, 0.0), gk), cos, sin)
    qs, ks = [], []
    for h in range(C_HEADS):
        qs.append(_rope128(norm2(q_[:, h * HP:(h + 1) * HP], gq), cos, sin))
        ks.append(norm2(jnp.where(m_n, kk_[:, h * HP:(h + 1) * HP], 0.0), gk) + kr)
    return [jnp.concatenate(qs, axis=1), jnp.concatenate(ks, axis=1)]


def _f_foxprep(i, rv, pv):
    (fq, fk, misc), (gq, gk, bf) = rv, pv
    tm = fq.shape[0]
    lane = _lane()

    def hnorm(x, g):
        outs = []
        for h in range(D_HEADS):
            xh = x[:, h * HP:(h + 1) * HP]
            ss = jnp.sum(xh * xh, axis=-1, keepdims=True) * (1.0 / D_HD)
            outs.append(xh * lax.rsqrt(ss + EPS) * g)
        return jnp.concatenate(outs, axis=1)

    lf = jnp.where(_rowmask(i, tm) & (lane >= 96) & (lane < 96 + D_HEADS), _log_sigmoid(misc + bf), 0.0)
    return [hnorm(fq, gq), hnorm(fk, gk), lf]


def _mlstm_chunk(c, n, m, q, k, v, li, lf):
    ln = q.shape[0]
    r = lax.broadcasted_iota(jnp.int32, (ln, ln), 0)
    cc = lax.broadcasted_iota(jnp.int32, (ln, ln), 1)
    causal = cc <= r
    eye = cc == r
    li_row = jnp.sum(jnp.where(eye, li, 0.0), axis=0, keepdims=True)
    b_col = jnp.sum(jnp.where(causal, jnp.sum(jnp.where(eye, lf, 0.0), axis=0, keepdims=True), 0.0),
                    axis=1, keepdims=True)
    b_row = jnp.sum(jnp.where(r <= cc, lf, 0.0), axis=0, keepdims=True)
    k = k * (A_DQK ** -0.5)
    qb, kb, vb = q.astype(BF16), k.astype(BF16), v.astype(BF16)
    dmat = jnp.where(causal, b_col - b_row + li_row, NEG)
    inter = b_col + m
    m_row = jnp.maximum(inter, jnp.max(dmat, axis=1, keepdims=True))
    w_intra = jnp.exp(dmat - m_row)
    w_inter = jnp.exp(inter - m_row)
    s = lax.dot_general(qb, kb, (((1,), (1,)), ((), ())), preferred_element_type=F32) * w_intra
    num = (w_inter * jnp.dot(qb, c.astype(BF16), preferred_element_type=F32)
           + jnp.dot(s.astype(BF16), vb, preferred_element_type=F32))
    den = w_inter * jnp.sum(q * n, axis=1, keepdims=True) + jnp.sum(s, axis=1, keepdims=True)
    h = num / jnp.maximum(jnp.abs(den), jnp.exp(-m_row))
    g = jnp.sum(lf, axis=0, keepdims=True)
    dk = g - b_col + li
    m_new = jnp.maximum(g + m, jnp.max(dk, axis=0, keepdims=True))
    wk = jnp.exp(dk - m_new)
    sc = jnp.exp(g + m - m_new)
    kw = wk * k
    c_new = sc * c + lax.dot_general(kw.astype(BF16), vb, (((0,), (0,)), ((), ())), preferred_element_type=F32)
    n_new = sc * n + jnp.sum(kw, axis=0, keepdims=True)
    return c_new, n_new, m_new, h


def _mlstm_fwd(z, li, lf, comm=None):
    tp = z.shape[0]
    nc = tp // A_CHUNK
    ln = A_CHUNK

    def body(q_ref, k_ref, v_ref, li_ref, lf_ref, h_ref, cs_ref, ns_ref, ms_ref, c_s, n_s, m_s):
        @pl.when(pl.program_id(1) == 0)
        def _():
            c_s[...] = jnp.zeros_like(c_s)
            n_s[...] = jnp.zeros_like(n_s)
            m_s[...] = jnp.zeros_like(m_s)

        c, n, m = c_s[...], n_s[...], m_s[...]
        cs_ref[0, 0] = c
        ns_ref[0, 0] = n
        ms_ref[0, 0] = m
        c2, n2, m2, h = _mlstm_chunk(c, n, m, q_ref[...], k_ref[...], v_ref[...], li_ref[0], lf_ref[0])
        c_s[...] = c2
        n_s[...] = n2
        m_s[...] = m2
        h_ref[...] = h

    g_spec = pl.BlockSpec((1, ln, 1), lambda h, j: (h, j, 0))
    return _pcall(
        body, "mlstm_fwd", (A_HEADS, nc),
        [pl.BlockSpec((ln, A_DQK), lambda h, j: (j, h)),
         pl.BlockSpec((ln, A_DQK), lambda h, j: (j, A_HEADS + h)),
         pl.BlockSpec((ln, A_DV), lambda h, j: (j, A_HEADS + h)),
         g_spec, g_spec],
        [pl.BlockSpec((ln, A_DV), lambda h, j: (j, h)),
         pl.BlockSpec((1, 1, A_DQK, A_DV), lambda h, j: (h, j, 0, 0)),
         pl.BlockSpec((1, 1, 1, A_DQK), lambda h, j: (h, j, 0, 0)),
         pl.BlockSpec((1, 1, 1, 1), lambda h, j: (h, j, 0, 0))],
        [jax.ShapeDtypeStruct((tp, A_HEADS * A_DV), F32),
         jax.ShapeDtypeStruct((A_HEADS, nc, A_DQK, A_DV), F32),
         jax.ShapeDtypeStruct((A_HEADS, nc, 1, A_DQK), F32),
         jax.ShapeDtypeStruct((A_HEADS, nc, 1, 1), F32)],
        [pltpu.VMEM((A_DQK, A_DV), F32), pltpu.VMEM((1, A_DQK), F32), pltpu.VMEM((1, 1), F32)],
        ("parallel", "arbitrary"), (z, z, z, li, lf), comm)


def _mlstm_bwd(z, li, lf, cs, ns, ms, dh, comm=None):
    tp = z.shape[0]
    nc = tp // A_CHUNK
    ln = A_CHUNK

    def body(q_ref, k_ref, v_ref, li_ref, lf_ref, cs_ref, ns_ref, ms_ref, dh_ref,
             dq_ref, dk_ref, dv_ref, dli_ref, dlf_ref, dc_s, dn_s, dm_s):
        @pl.when(pl.program_id(1) == 0)
        def _():
            dc_s[...] = jnp.zeros_like(dc_s)
            dn_s[...] = jnp.zeros_like(dn_s)
            dm_s[...] = jnp.zeros_like(dm_s)

        _, vjp = jax.vjp(_mlstm_chunk, cs_ref[0, 0], ns_ref[0, 0], ms_ref[0, 0], q_ref[...], k_ref[...],
                         v_ref[...], li_ref[0], lf_ref[0])
        dc, dn, dm, dq, dk, dv, dli, dlf = vjp((dc_s[...], dn_s[...], dm_s[...], dh_ref[...]))
        dc_s[...] = dc
        dn_s[...] = dn
        dm_s[...] = dm
        dq_ref[...] = dq.astype(BF16)
        dk_ref[...] = dk.astype(BF16)
        dv_ref[...] = dv.astype(BF16)
        dli_ref[0] = dli
        dlf_ref[0] = dlf

    def rj(j):
        return nc - 1 - j

    g_spec = pl.BlockSpec((1, ln, 1), lambda h, j: (h, rj(j), 0))
    return _pcall(
        body, "mlstm_bwd", (A_HEADS, nc),
        [pl.BlockSpec((ln, A_DQK), lambda h, j: (rj(j), h)),
         pl.BlockSpec((ln, A_DQK), lambda h, j: (rj(j), A_HEADS + h)),
         pl.BlockSpec((ln, A_DV), lambda h, j: (rj(j), A_HEADS + h)),
         g_spec, g_spec,
         pl.BlockSpec((1, 1, A_DQK, A_DV), lambda h, j: (h, rj(j), 0, 0)),
         pl.BlockSpec((1, 1, 1, A_DQK), lambda h, j: (h, rj(j), 0, 0)),
         pl.BlockSpec((1, 1, 1, 1), lambda h, j: (h, rj(j), 0, 0)),
         pl.BlockSpec((ln, A_DV), lambda h, j: (rj(j), h))],
        [pl.BlockSpec((ln, A_DQK), lambda h, j: (rj(j), h)),
         pl.BlockSpec((ln, A_DQK), lambda h, j: (rj(j), h)),
         pl.BlockSpec((ln, A_DV), lambda h, j: (rj(j), h)),
         g_spec, g_spec],
        [jax.ShapeDtypeStruct((tp, A_HEADS * A_DQK), BF16),
         jax.ShapeDtypeStruct((tp, A_HEADS * A_DQK), BF16),
         jax.ShapeDtypeStruct((tp, A_HEADS * A_DV), BF16),
         jax.ShapeDtypeStruct((A_HEADS, tp, 1), F32),
         jax.ShapeDtypeStruct((A_HEADS, tp, 1), F32)],
        [pltpu.VMEM((A_DQK, A_DV), F32), pltpu.VMEM((1, A_DQK), F32), pltpu.VMEM((1, 1), F32)],
        ("parallel", "arbitrary"), (z, z, z, li, lf, cs, ns, ms, dh), comm)


def _shift_down(x, s, row):
    return x if s == 0 else jnp.where(row >= s, pltpu.roll(x, s, 0), 0.0)


def _shift_up(x, s, row):
    n = x.shape[0]
    return x if s == 0 else jnp.where(row < n - s, pltpu.roll(x, n - s, 0), 0.0)


def _conv_fwd(z, xcb, w, b):
    tp, c = z.shape[0], w.shape[1]
    ct = 256

    def body(x_ref, w_ref, b_ref, o_ref):
        x = x_ref[...]
        row = lax.broadcasted_iota(jnp.int32, (tp, 1), 0)
        acc = jnp.zeros_like(x) + b_ref[...]
        for k in range(CONV_W):
            acc = acc + w_ref[k:k + 1, :] * _shift_down(x, CONV_W - 1 - k, row)
        o_ref[...] = acc

    return pl.pallas_call(
        body, name="conv_fwd", grid=(c // ct,),
        in_specs=[pl.BlockSpec((tp, ct), lambda j: (0, xcb + j)), pl.BlockSpec((CONV_W, ct), lambda j: (0, j)),
                  pl.BlockSpec((1, ct), lambda j: (0, j))],
        out_specs=pl.BlockSpec((tp, ct), lambda j: (0, j)),
        out_shape=jax.ShapeDtypeStruct((tp, c), F32),
        compiler_params=_cparams(("parallel",)),
    )(z, w, b)


def _conv_bwd(z, xcb, w, dxc):
    tp, c = z.shape[0], w.shape[1]
    ct = 256

    def body(x_ref, w_ref, d_ref, dx_ref, dw_ref, db_ref):
        x, d = x_ref[...], d_ref[...]
        row = lax.broadcasted_iota(jnp.int32, (tp, 1), 0)
        acc = jnp.zeros_like(x)
        for k in range(CONV_W):
            s = CONV_W - 1 - k
            acc = acc + w_ref[k:k + 1, :] * _shift_up(d, s, row)
            dw_ref[k:k + 1, :] = jnp.sum(d * _shift_down(x, s, row), axis=0, keepdims=True)
        dx_ref[...] = acc.astype(BF16)
        db_ref[...] = jnp.sum(d, axis=0, keepdims=True)

    return pl.pallas_call(
        body, name="conv_bwd", grid=(c // ct,),
        in_specs=[pl.BlockSpec((tp, ct), lambda j: (0, xcb + j)), pl.BlockSpec((CONV_W, ct), lambda j: (0, j)),
                  pl.BlockSpec((tp, ct), lambda j: (0, j))],
        out_specs=[pl.BlockSpec((tp, ct), lambda j: (0, j)), pl.BlockSpec((CONV_W, ct), lambda j: (0, j)),
                   pl.BlockSpec((1, ct), lambda j: (0, j))],
        out_shape=[jax.ShapeDtypeStruct((tp, c), BF16), jax.ShapeDtypeStruct((CONV_W, c), F32),
                   jax.ShapeDtypeStruct((1, c), F32)],
        compiler_params=_cparams(("parallel",)),
    )(z, w, dxc)


def _scan_fwd(a, u, name):
    tp, c = a.shape
    ct = _pick(c, (256, 128))
    lb = SCAN_BLOCK
    nb = tp // lb

    def body(a_ref, u_ref, h_ref, hp_ref):
        row = lax.broadcasted_iota(jnp.int32, (lb, 1), 0)

        def blk(j, carry):
            r0 = pl.multiple_of(j * lb, lb)
            aa, uu = a_ref[pl.ds(r0, lb), :], u_ref[pl.ds(r0, lb), :]
            s = 1
            while s < lb:
                mk = row >= s
                uu = jnp.where(mk, aa * pltpu.roll(uu, s, 0) + uu, uu)
                aa = jnp.where(mk, aa * pltpu.roll(aa, s, 0), aa)
                s *= 2
            hh = uu + aa * carry
            h_ref[pl.ds(r0, lb), :] = hh
            hp_ref[pl.ds(r0, lb), :] = jnp.where(row >= 1, pltpu.roll(hh, 1, 0), carry)
            return hh[lb - 1:lb, :]

        lax.fori_loop(0, nb, blk, jnp.zeros((1, ct), F32))

    spec = pl.BlockSpec((tp, ct), lambda j: (0, j))
    return pl.pallas_call(
        body, name=name, grid=(c // ct,), in_specs=[spec, spec], out_specs=[spec, spec],
        out_shape=[jax.ShapeDtypeStruct((tp, c), F32)] * 2,
        compiler_params=_cparams(("parallel",)),
    )(a, u)


def _scan_bwd(a, hprev, dh, name):
    tp, c = a.shape
    ct = _pick(c, (256, 128))
    lb = SCAN_BLOCK
    nb = tp // lb

    def body(a_ref, hp_ref, dh_ref, du_ref, da_ref):
        row = lax.broadcasted_iota(jnp.int32, (lb, 1), 0)

        def blk(jj, carry):
            g_next, a_next = carry
            r0 = pl.multiple_of((nb - 1 - jj) * lb, lb)
            a_blk = a_ref[pl.ds(r0, lb), :]
            aa = jnp.where(row < lb - 1, pltpu.roll(a_blk, lb - 1, 0), a_next)
            gg = dh_ref[pl.ds(r0, lb), :]
            s = 1
            while s < lb:
                mk = row < lb - s
                gg = jnp.where(mk, aa * pltpu.roll(gg, lb - s, 0) + gg, gg)
                aa = jnp.where(mk, aa * pltpu.roll(aa, lb - s, 0), aa)
                s *= 2
            gg = gg + aa * g_next
            du_ref[pl.ds(r0, lb), :] = gg
            da_ref[pl.ds(r0, lb), :] = gg * hp_ref[pl.ds(r0, lb), :]
            return gg[0:1, :], a_blk[0:1, :]

        lax.fori_loop(0, nb, blk, (jnp.zeros((1, ct), F32), jnp.zeros((1, ct), F32)))

    spec = pl.BlockSpec((tp, ct), lambda j: (0, j))
    return pl.pallas_call(
        body, name=name, grid=(c // ct,), in_specs=[spec, spec, spec], out_specs=[spec, spec],
        out_shape=[jax.ShapeDtypeStruct((tp, c), F32)] * 2,
        compiler_params=_cparams(("parallel",)),
    )(a, hprev, dh)


ATT_HEADS_PER_STEP = 2
LOG2E = 1.4426950408889634
LN2 = 0.6931471805599453


def _diag_valid(i, tq):
    r = lax.broadcasted_iota(jnp.int32, (tq, tq), 0)
    c = lax.broadcasted_iota(jnp.int32, (tq, tq), 1)
    return (c <= r) & ((i * tq + c >= PAD) | (c == r))


def _key_terms(tp, tq, bk=None):
    pad_neg = jnp.where(jnp.arange(tp) < PAD, NEG, 0.0).astype(F32).reshape(1, tp // tq, 1, tq)
    if bk is None:
        return pad_neg, jnp.zeros_like(pad_neg)
    kd = -bk.reshape(bk.shape[0], tp // tq, 1, tq)
    return kd + pad_neg, kd


def _attn_fwd(q, k, v, vcb, scale, bq, kterms, name, comm=None):
    tp = q.shape[0]
    nh = q.shape[1] // HP
    tq = _pick(tp, (384, 256, 128))
    has_bq = bq is not None
    c1 = scale * LOG2E
    per_head = kterms[0].shape[0] > 1
    hg = ATT_HEADS_PER_STEP
    lanes = [slice(hh * HP, (hh + 1) * HP) for hh in range(hg)]

    def body(*refs):
        if has_bq:
            q_ref, k_ref, v_ref, kb_ref, kd_ref, bq_ref, o_ref, lse_ref = refs
            rb = [bq_ref[hh] * LOG2E for hh in range(hg)]
        else:
            q_ref, k_ref, v_ref, kb_ref, kd_ref, o_ref, lse_ref = refs
        i = pl.program_id(1)
        qb = [q_ref[:, ln].astype(BF16) for ln in lanes]

        def tile(j, carry, diag):
            r0 = pl.multiple_of(j * tq, tq)
            out = []
            for hh in range(hg):
                m, l, acc = carry[hh]
                kb = k_ref[pl.ds(r0, tq), lanes[hh]].astype(BF16)
                vb = v_ref[pl.ds(r0, tq), lanes[hh]].astype(BF16)
                kt = (kd_ref if diag else kb_ref)[hh if per_head else 0, j] * LOG2E
                x = lax.dot_general(qb[hh], kb, (((1,), (1,)), ((), ())), preferred_element_type=F32) * c1 + kt
                if has_bq:
                    x = x + rb[hh]
                if diag:
                    x = jnp.where(_diag_valid(i, tq), x, NEG)
                m2 = jnp.maximum(m, jnp.max(x, axis=1, keepdims=True))
                alpha = jnp.exp2(m - m2)
                p = jnp.exp2(x - m2)
                l2 = alpha * l + jnp.sum(p, axis=1, keepdims=True)
                acc2 = alpha * acc + jnp.dot(p.astype(BF16), vb, preferred_element_type=F32)
                out.append((m2, l2, acc2))
            return tuple(out)

        init = tuple((jnp.full((tq, 1), NEG, F32), jnp.zeros((tq, 1), F32), jnp.zeros((tq, HP), F32))
                     for _ in range(hg))
        res = tile(i, lax.fori_loop(0, i, lambda j, c: tile(j, c, False), init), True)
        for hh, (m, l, acc) in enumerate(res):
            o_ref[:, lanes[hh]] = acc / l
            lse_ref[hh] = m * LN2 + jnp.log(l)

    kt_spec = pl.BlockSpec((hg if per_head else 1, tp // tq, 1, tq),
                           (lambda h, i: (h, 0, 0, 0)) if per_head else (lambda h, i: (0, 0, 0, 0)))
    in_specs = [pl.BlockSpec((tq, hg * HP), lambda h, i: (i, h)), pl.BlockSpec((tp, hg * HP), lambda h, i: (0, h)),
                pl.BlockSpec((tp, hg * HP), lambda h, i: (0, vcb // hg + h)), kt_spec, kt_spec]
    ins = [q, k, v, *kterms]
    if has_bq:
        in_specs += [pl.BlockSpec((hg, tq, 1), lambda h, i: (h, i, 0))]
        ins += [bq]
    return _pcall(
        body, name, (nh // hg, tp // tq), in_specs,
        [pl.BlockSpec((tq, hg * HP), lambda h, i: (i, h)), pl.BlockSpec((hg, tq, 1), lambda h, i: (h, i, 0))],
        [jax.ShapeDtypeStruct((tp, nh * HP), F32), jax.ShapeDtypeStruct((nh, tp, 1), F32)],
        [], ("parallel", "parallel"), ins, comm)


def _attn_bwd(q, k, v, vcb, o, lse, do, docb, scale, bq, kterms, name, comm=None):
    tp = q.shape[0]
    nh = q.shape[1] // HP
    tq = _pick(tp, (384, 256, 128))
    has_bq = bq is not None
    c1 = scale * LOG2E
    per_head = kterms[0].shape[0] > 1
    hg = ATT_HEADS_PER_STEP
    lanes = [slice(hh * HP, (hh + 1) * HP) for hh in range(hg)]

    def body(*refs):
        if has_bq:
            (q_ref, k_ref, v_ref, o_ref, lse_ref, do_ref, kb_ref, kd_ref, bq_ref,
             dq_ref, dk_ref, dv_ref, dbq_ref, dkt_ref) = refs
            rb = [(bq_ref[hh] - lse_ref[hh]) * LOG2E for hh in range(hg)]
        else:
            q_ref, k_ref, v_ref, o_ref, lse_ref, do_ref, kb_ref, kd_ref, dq_ref, dk_ref, dv_ref = refs
            rb = [lse_ref[hh] * (-LOG2E) for hh in range(hg)]
        i = pl.program_id(1)

        @pl.when(i == 0)
        def _():
            dk_ref[...] = jnp.zeros_like(dk_ref)
            dv_ref[...] = jnp.zeros_like(dv_ref)
            if has_bq:
                dkt_ref[...] = jnp.zeros_like(dkt_ref)

        qb = [q_ref[:, ln].astype(BF16) for ln in lanes]
        do_ = [do_ref[:, ln] for ln in lanes]
        dob = [d.astype(BF16) for d in do_]
        delta = [jnp.sum(do_[hh] * o_ref[:, lanes[hh]], axis=1, keepdims=True) for hh in range(hg)]

        def tile(j, carry, diag):
            r0 = pl.multiple_of(j * tq, tq)
            out = []
            for hh in range(hg):
                dq, dbq = carry[hh]
                kb = k_ref[pl.ds(r0, tq), lanes[hh]].astype(BF16)
                vb = v_ref[pl.ds(r0, tq), lanes[hh]].astype(BF16)
                kt = (kd_ref if diag else kb_ref)[hh if per_head else 0, j] * LOG2E
                x = (lax.dot_general(qb[hh], kb, (((1,), (1,)), ((), ())), preferred_element_type=F32) * c1
                     + kt + rb[hh])
                if diag:
                    x = jnp.where(_diag_valid(i, tq), x, NEG)
                p = jnp.exp2(x)
                dp = lax.dot_general(dob[hh], vb, (((1,), (1,)), ((), ())), preferred_element_type=F32)
                ds = p * (dp - delta[hh])
                dsb = ds.astype(BF16)
                dk_ref[pl.ds(r0, tq), lanes[hh]] += lax.dot_general(dsb, qb[hh], (((0,), (0,)), ((), ())),
                                                                    preferred_element_type=F32) * scale
                dv_ref[pl.ds(r0, tq), lanes[hh]] += lax.dot_general(p.astype(BF16), dob[hh], (((0,), (0,)), ((), ())),
                                                                    preferred_element_type=F32)
                if has_bq:
                    dkt_ref[hh, j] += jnp.sum(ds, axis=0, keepdims=True)
                    dbq = dbq + jnp.sum(ds, axis=1, keepdims=True)
                out.append((dq + jnp.dot(dsb, kb, preferred_element_type=F32), dbq))
            return tuple(out)

        init = tuple((jnp.zeros((tq, HP), F32), jnp.zeros((tq, 1), F32)) for _ in range(hg))
        res = tile(i, lax.fori_loop(0, i, lambda j, c: tile(j, c, False), init), True)
        for hh, (dq, dbq) in enumerate(res):
            dq_ref[:, lanes[hh]] = dq * scale
            if has_bq:
                dbq_ref[hh] = dbq

    blk_q = pl.BlockSpec((tq, hg * HP), lambda h, i: (i, h))
    blk_k = pl.BlockSpec((tp, hg * HP), lambda h, i: (0, h))
    kt_spec = pl.BlockSpec((hg if per_head else 1, tp // tq, 1, tq),
                           (lambda h, i: (h, 0, 0, 0)) if per_head else (lambda h, i: (0, 0, 0, 0)))
    row_spec = pl.BlockSpec((hg, tq, 1), lambda h, i: (h, i, 0))
    in_specs = [blk_q, blk_k, pl.BlockSpec((tp, hg * HP), lambda h, i: (0, vcb // hg + h)), blk_q,
                row_spec, pl.BlockSpec((tq, hg * HP), lambda h, i: (i, docb // hg + h)), kt_spec, kt_spec]
    ins = [q, k, v, o, lse, do, *kterms]
    out_specs = [blk_q, blk_k, blk_k]
    out_shape = [jax.ShapeDtypeStruct((tp, nh * HP), F32)] * 3
    if has_bq:
        in_specs += [row_spec]
        ins += [bq]
        out_specs += [row_spec, kt_spec]
        out_shape += [jax.ShapeDtypeStruct((nh, tp, 1), F32), jax.ShapeDtypeStruct((nh, tp // tq, 1, tq), F32)]
    return _pcall(body, name, (nh // hg, tp // tq), in_specs, out_specs, out_shape, [], ("parallel", "arbitrary"), ins,
                  comm)


def _loss_head(h, tgt):
    tp, d = h.shape
    tm = 128
    first = (PAD + N_META) // tm

    def body(h_ref, t_ref, l_ref, d_ref):
        i = pl.program_id(0)

        @pl.when(i == 0)
        def _():
            l_ref[...] = jnp.zeros_like(l_ref)

        live = i >= first
        err = jnp.where(live, h_ref[...] - t_ref[...], 0.0)
        d_ref[...] = err * (1.0 / d)
        l_ref[...] += (0.5 / d) * jnp.sum(err * err)

    return pl.pallas_call(
        body, name="loss_head", grid=(tp // tm,),
        in_specs=[pl.BlockSpec((tm, d), lambda i: (i, 0)), pl.BlockSpec((tm, d), lambda i: (i, 0))],
        out_specs=[_full_spec((8, 128)), pl.BlockSpec((tm, d), lambda i: (i, 0))],
        out_shape=[jax.ShapeDtypeStruct((8, 128), F32), jax.ShapeDtypeStruct((tp, d), F32)],
        compiler_params=_cparams(("arbitrary",)),
    )(h, tgt)


def _rope_tables(pos_rows):
    half = C_ROPE // 2
    freqs = ROPE_THETA ** (-jnp.arange(half, dtype=F32) / half)
    ang = pos_rows[:, None].astype(F32) * freqs
    cos, sin = jnp.cos(ang), jnp.sin(ang)
    tp = pos_rows.shape[0]
    one, zero = jnp.ones((tp, C_NOPE), F32), jnp.zeros((tp, C_NOPE), F32)
    tail1, tail0 = jnp.ones((tp, HP - C_NOPE - C_ROPE), F32), jnp.zeros((tp, HP - C_NOPE - C_ROPE), F32)
    return (jnp.concatenate([one, cos, cos, tail1], axis=1), jnp.concatenate([zero, sin, sin, tail0], axis=1))


def _heads_to_cols(x, lo, n):
    t = x[:, lo:lo + n].T
    return t[:, :, None], t[:, None, :]


def _local_step(x, positions, tgt, w, plan=None):
    def hosted(tag, fn, *args):
        ops = plan.ride(tag, g) if plan is not None else None
        if not ops:
            return fn(*args)
        res, got = fn(*args, comm=ops)
        plan.arrived(tag, got, w)
        return res

    s_len = x.shape[0]
    tp = PAD + N_META + s_len
    tm = _pick(tp, (384, 256, 128))
    front = PAD + N_META
    h0 = jnp.concatenate([jnp.zeros((PAD, D_MODEL), F32), w["meta"], x], axis=0)
    tgt_p = jnp.concatenate([jnp.zeros((front, D_MODEL), F32), tgt], axis=0)
    pos_rows = jnp.concatenate([jnp.zeros((PAD,), jnp.int32), jnp.arange(N_META, dtype=jnp.int32),
                                positions + N_META])
    cos, sin = _rope_tables(pos_rows)
    g = {}

    r_h0 = [(h0, D_MODEL, 0)]
    (xn0,) = _row_fwd("norm0_f", _f_norm, r_h0, [w["ev_ln"]], [(D_MODEL, BF16)], tm)
    z0 = hosted("mm_z0", _mm, xn0, w["ev_w_in"], "nn", "mm_z0")
    r_misc0 = [(z0, HP, 5120 // HP)]
    (g0,) = _row_fwd("gate0_f", _f_gate0, r_misc0, [w["ev_b_if"]], [(HP, F32)], tm)
    li, _ = _heads_to_cols(g0, 0, A_HEADS)
    lf, _ = _heads_to_cols(g0, A_HEADS, A_HEADS)
    h_a, cs, ns, ms = hosted("mlstm_fwd", _mlstm_fwd, z0, li, lf)
    r_aout = [(h_a, 1024, 0), (z0, 1024, 2)]
    (ha,) = _row_fwd("aout_f", _f_aout, r_aout, [w["ev_a_norm"]], [(1024, BF16)], tm)
    xc = _conv_fwd(z0, 3072 // 256, w["ev_conv_w"], w["ev_conv_b"])
    p_gates = [w["ev_w_ra"], w["ev_b_ra"], w["ev_w_rx"], w["ev_b_rx"], w["ev_lam"]]
    a_g, u_g = _row_fwd("gates_f", _f_gates, [(xc, 1024, 0)], p_gates, [(1024, F32), (1024, F32)], tm)
    hs, hprev = _scan_fwd(a_g, u_g, "lru_f")
    r_bout = [(hs, 1024, 0), (z0, 1024, 4)]
    (hb,) = _row_fwd("bout_f", _f_bout, r_bout, [], [(1024, BF16)], tm)
    hab = _concat_cols([ha, hb], BF16, "cat_hab")
    h1 = _mm(hab, w["ev_w_out"], "nn", "mm_h1", add=h0)
    (xn1,) = _row_fwd("norm1_f", _f_norm, [(h1, D_MODEL, 0)], [w["mlp_ln0"]], [(D_MODEL, BF16)], tm)
    p0, act0 = _mm(xn1, w["w_ff1_0"], "nn", "mm_p0", relu2=True)
    h2 = _mm(act0, w["w_ff2_0"], "nn", "mm_h2", add=h1)
    (xn2,) = _row_fwd("norm2_f", _f_norm, [(h2, D_MODEL, 0)], [w["od_ln"]], [(D_MODEL, BF16)], tm)
    z1 = _mm(xn2, w["od_w_in"], "nn", "mm_z1")
    r_c = [(z1, C_Q_LORA, 3072 // C_Q_LORA), (z1, C_KV_LORA, 3584 // C_KV_LORA)]
    cqn, ckvn = _row_fwd("cnorm_f", _f_cnorm, r_c, [w["od_g_qa"], w["od_g_kva"]],
                         [(C_Q_LORA, BF16), (C_KV_LORA, BF16)], tm)
    q_ = _mm(cqn, w["od_w_uq"], "nn", "mm_q")
    kv_ = _mm(ckvn, w["od_w_ukv"], "nn", "mm_kv")
    r_mla = [(q_, 1024, 0), (kv_, 1024, 0), (z1, HP, 3840 // HP), (cos, HP, 0), (sin, HP, 0)]
    p_mla = [w["gq_full"], w["gk_full"]]
    qm, km = _row_fwd("mla_f", _f_mlaprep, r_mla, p_mla, [(1024, BF16), (1024, BF16)], tm)
    sc_c = (C_NOPE + C_ROPE) ** -0.5
    kt_c = _key_terms(tp, tm)
    hc, lse_c = hosted("mla_attn_f", _attn_fwd, qm, km, kv_, C_HEADS, sc_c, None, kt_c, "mla_attn_f")
    r_fox = [(z1, 1024, 0), (z1, 1024, 1), (z1, HP, 3840 // HP)]
    p_fox = [w["gfq_full"], w["gfk_full"], w["bf_full"]]
    qf, kf, lfx = _row_fwd("fox_f", _f_foxprep, r_fox, p_fox, [(1024, BF16), (1024, BF16), (HP, F32)], tm)
    ones = jnp.ones((tp, HP), F32)
    fcum, fprev = _scan_fwd(ones, lfx, "fcum_f")
    bq, bk = _heads_to_cols(fcum, 96, D_HEADS)
    kt_d = _key_terms(tp, tm, bk)
    sc_d = D_HD ** -0.5
    hd, lse_d = _attn_fwd(qf, kf, z1, 2048 // HP, sc_d, bq, kt_d, "fox_attn_f")
    hcd = _concat_cols([hc, hd], F32, "cat_hcd")
    h3 = _mm(hcd, w["od_w_out"], "nn", "mm_h3", add=h2)
    (xn3,) = _row_fwd("norm3_f", _f_norm, [(h3, D_MODEL, 0)], [w["mlp_ln1"]], [(D_MODEL, BF16)], tm)
    p1, act1 = _mm(xn3, w["w_ff1_1"], "nn", "mm_p1", relu2=True)
    h4 = _mm(act1, w["w_ff2_1"], "nn", "mm_h4", add=h3)
    lpart, dh4 = _loss_head(h4, tgt_p)
    loss = lpart[0, 0]

    def mlp_bwd(tag, dh_out, h_in, xn, p, act, ln, w1, w2):
        dp = _mm(dh_out, w2, "nt", f"mm_dp{tag}", relu2_of=p)
        g_w2 = _mm(act, dh_out, "tn", f"mm_dw2_{tag}")
        g_w1 = _mm(xn, dp, "tn", f"mm_dw1_{tag}")
        dxn = _mm(dp, w1, "nt", f"mm_dxn{tag}")
        (dh_in,), (g_ln,) = _row_bwd(f"normm{tag}_b", _f_norm, [(h_in, D_MODEL, 0)], [ln], [(dxn, D_MODEL, 0)], tm,
                                     [True], add=dh_out)
        return dh_in, g_ln, g_w1, g_w2

    dh3, g["mlp_ln1"], g["w_ff1_1"], g["w_ff2_1"] = mlp_bwd("1", dh4, h3, xn3, p1, act1, w["mlp_ln1"],
                                                            w["w_ff1_1"], w["w_ff2_1"])
    dhcd = _mm(dh3, w["od_w_out"], "nt", "mm_dhcd")
    g["od_w_out"] = _mm(hcd, dh3, "tn", "mm_dwout1")
    dqf, dkf, dvf, dbq, dkt = hosted("fox_attn_b", _attn_bwd, qf, kf, z1, 2048 // HP, hd, lse_d, dhcd, D_HEADS, sc_d,
                                     bq, kt_d, "fox_attn_b")
    dfc = dbq[:, :, 0].T - dkt.reshape(D_HEADS, tp).T
    dfcum = jnp.concatenate([jnp.zeros((tp, 96), F32), dfc, jnp.zeros((tp, HP - 96 - D_HEADS), F32)], axis=1)
    dlfx, _ = _scan_bwd(ones, fprev, dfcum, "fcum_b")
    (dfq, dfk, dmisc_f), (g["gfq_full"], g["gfk_full"], g["bf_full"]) = _row_bwd(
        "fox_b", _f_foxprep, r_fox, p_fox, [(dqf, 1024, 0), (dkf, 1024, 0), (dlfx, HP, 0)], tm, [True, True, True],
        out_dtypes=[BF16, BF16, F32])
    dqm, dkm, dvm = _attn_bwd(qm, km, kv_, C_HEADS, hc, lse_c, dhcd, 0, sc_c, None, kt_c, "mla_attn_b")
    (dq_, dkk_, dmisc_m), (g["gq_full"], g["gk_full"]) = _row_bwd(
        "mla_b", _f_mlaprep, r_mla, p_mla, [(dqm, 1024, 0), (dkm, 1024, 0)], tm, [True, True, True, False, False],
        out_dtypes=[BF16, BF16, F32])
    dkv_ = _concat_cols([dkk_, dvm], BF16, "cat_dkv")
    dckvn = _mm(dkv_, w["od_w_ukv"], "nt", "mm_dckvn")
    g["od_w_ukv"] = _mm(ckvn, dkv_, "tn", "mm_dwukv")
    dcqn = _mm(dq_, w["od_w_uq"], "nt", "mm_dcqn")
    g["od_w_uq"] = _mm(cqn, dq_, "tn", "mm_dwuq")
    (dcq, dckv), (g["od_g_qa"], g["od_g_kva"]) = _row_bwd(
        "cnorm_b", _f_cnorm, r_c, [w["od_g_qa"], w["od_g_kva"]],
        [(dcqn, C_Q_LORA, 0), (dckvn, C_KV_LORA, 0)], tm, [True, True], out_dtypes=[BF16, BF16])
    dz1 = _concat_cols([dfq, dfk, dvf, dcq, HP, dckv, dmisc_f + dmisc_m, HP], BF16, "cat_dz1")
    g["od_w_in"] = _mm(xn2, dz1, "tn", "mm_dwin1")
    dxn2 = _mm(dz1, w["od_w_in"], "nt", "mm_dxn2")
    (dh2,), (g["od_ln"],) = _row_bwd("norm2_b", _f_norm, [(h2, D_MODEL, 0)], [w["od_ln"]], [(dxn2, D_MODEL, 0)], tm,
                                     [True], add=dh3)
    dh1, g["mlp_ln0"], g["w_ff1_0"], g["w_ff2_0"] = mlp_bwd("0", dh2, h1, xn1, p0, act0, w["mlp_ln0"],
                                                            w["w_ff1_0"], w["w_ff2_0"])
    dhab = _mm(dh1, w["ev_w_out"], "nt", "mm_dhab")
    g["ev_w_out"] = _mm(hab, dh1, "tn", "mm_dwout0")
    (dhs, dgb), _ = _row_bwd("bout_b", _f_bout, r_bout, [], [(dhab, 1024, 1)], tm, [True, True],
                             out_dtypes=[F32, BF16])
    du_g, da_g = _scan_bwd(a_g, hprev, dhs, "lru_b")
    (dxc,), (g["ev_w_ra"], g["ev_b_ra"], g["ev_w_rx"], g["ev_b_rx"], g["ev_lam"]) = _row_bwd(
        "gates_b", _f_gates, [(xc, 1024, 0)], p_gates, [(da_g, 1024, 0), (du_g, 1024, 0)], tm, [True])
    dxb, g["ev_conv_w"], g["ev_conv_b"] = _conv_bwd(z0, 3072 // 256, w["ev_conv_w"], dxc)
    (dh_a, do_), (g["ev_a_norm"],) = _row_bwd("aout_b", _f_aout, r_aout, [w["ev_a_norm"]], [(dhab, 1024, 0)], tm,
                                              [True, True], out_dtypes=[F32, BF16])
    dq, dk, dv, dli, dlf = hosted("mlstm_bwd", _mlstm_bwd, z0, li, lf, cs, ns, ms, dh_a)
    dg0 = jnp.concatenate([dli[:, :, 0].T, dlf[:, :, 0].T, jnp.zeros((tp, HP - 2 * A_HEADS), F32)], axis=1)
    (dmisc0,), (g["ev_b_if"],) = _row_bwd("gate0_b", _f_gate0, r_misc0, [w["ev_b_if"]], [(dg0, HP, 0)], tm, [True],
                                          out_dtypes=[BF16])
    dz0 = _concat_cols([dq, dk, dv, do_, dxb, dgb, dmisc0, HP], BF16, "cat_dz0")
    g["ev_w_in"] = _mm(xn0, dz0, "tn", "mm_dwin0")
    dxn0 = hosted("mm_dxn0", _mm, dz0, w["ev_w_in"], "nt", "mm_dxn0")
    (dh0,), (g["ev_ln"],) = _row_bwd("norm0_b", _f_norm, r_h0, [w["ev_ln"]], [(dxn0, D_MODEL, 0)], tm, [True], add=dh1)
    g["meta"] = dh0[PAD:front]
    return loss, dh0[front:], g


def _pad_last(a, n):
    return jnp.pad(a, [(0, 0)] * (a.ndim - 1) + [(0, n - a.shape[-1])])


def _pad_heads(a, nh, d):
    return _pad_last(a.reshape(a.shape[:-1] + (nh, d)), HP).reshape(a.shape[:-1] + (nh * HP,))


def _unpad_heads(a, nh, d):
    return a.reshape(a.shape[:-1] + (nh, HP))[..., :d].reshape(a.shape[:-1] + (nh * d,))


_MM_UNITS = ("ev_w_in", "ev_w_out", "od_w_in", "od_w_uq", "od_w_ukv", "od_w_out",
             "w_ff1_0", "w_ff2_0", "w_ff1_1", "w_ff2_1")


def _mw_pad(name, a):
    if name == "ev_w_in":
        return _pad_last(jnp.concatenate([a[:, :3072], a[:, 3080:5128], a[:, 3072:3080]], axis=1), ZE)
    if name == "od_w_in":
        z = lambda n: jnp.zeros((a.shape[0], n), a.dtype)
        return jnp.concatenate(
            [_pad_heads(a[:, 672:1184], D_HEADS, D_HD), _pad_heads(a[:, 1184:1696], D_HEADS, D_HD),
             _pad_heads(a[:, 1696:2208], D_HEADS, D_HD), a[:, 0:384], z(128), a[:, 384:640],
             z(64), a[:, 640:672], a[:, 2208:2216], z(24), z(128)], axis=1)
    if name == "od_w_uq":
        return _pad_heads(a, C_HEADS, C_NOPE + C_ROPE)
    if name == "od_w_ukv":
        wkv = a.reshape(C_KV_LORA, C_HEADS, C_NOPE + C_V)
        return jnp.concatenate([_pad_last(wkv[:, :, :C_NOPE], HP).reshape(C_KV_LORA, -1),
                                _pad_last(wkv[:, :, C_NOPE:], HP).reshape(C_KV_LORA, -1)], axis=1)
    if name == "od_w_out":
        return jnp.pad(a.reshape(2 * C_HEADS, C_V, D_MODEL), ((0, 0), (0, HP - C_V), (0, 0))).reshape(-1, D_MODEL)
    return a


def _mw_unpad(name, g):
    if name == "ev_w_in":
        return jnp.concatenate([g[:, :3072], g[:, 5120:5128], g[:, 3072:5120]], axis=1)
    if name == "od_w_in":
        return jnp.concatenate(
            [g[:, 3072:3456], g[:, 3584:3840], g[:, 3904:3936], _unpad_heads(g[:, 0:1024], D_HEADS, D_HD),
             _unpad_heads(g[:, 1024:2048], D_HEADS, D_HD), _unpad_heads(g[:, 2048:3072], D_HEADS, D_HD),
             g[:, 3936:3944]], axis=1)
    if name == "od_w_uq":
        return _unpad_heads(g, C_HEADS, C_NOPE + C_ROPE)
    if name == "od_w_ukv":
        gk = g[:, :C_HEADS * HP].reshape(C_KV_LORA, C_HEADS, HP)[:, :, :C_NOPE]
        gv = g[:, C_HEADS * HP:].reshape(C_KV_LORA, C_HEADS, HP)[:, :, :C_V]
        return jnp.concatenate([gk, gv], axis=2).reshape(C_KV_LORA, -1)
    if name == "od_w_out":
        return g.reshape(2 * C_HEADS, HP, D_MODEL)[:, :C_V].reshape(-1, D_MODEL)
    return g


def _prep_weights(p):
    w = {}
    w["meta"] = p["meta"]
    for k in ("ev_ln", "ev_a_norm", "ev_conv_b", "ev_b_ra", "ev_b_rx", "ev_lam", "od_ln", "od_g_qa", "od_g_kva"):
        w[k] = p[k].reshape(1, -1)
    w["ev_b_if"] = _pad_last(p["ev_b_if"].reshape(1, -1), HP)
    w["ev_conv_w"] = p["ev_conv_w"][0]
    w["ev_w_ra"] = p["ev_w_ra"][0]
    w["ev_w_rx"] = p["ev_w_rx"][0]
    f1 = lambda a: a.reshape(1, -1)
    w["gq_full"] = _pad_last(jnp.concatenate([f1(p["od_g_qn"]), f1(p["od_g_qr"])], axis=1), HP)
    w["gk_full"] = _pad_last(jnp.concatenate([f1(p["od_g_kn"]), f1(p["od_g_kr"])], axis=1), HP)
    w["gfq_full"] = _pad_last(f1(p["od_g_fq"]), HP)
    w["gfk_full"] = _pad_last(f1(p["od_g_fk"]), HP)
    w["bf_full"] = _pad_last(jnp.concatenate([jnp.zeros((1, 96), F32), f1(p["od_b_f"])], axis=1), HP)
    for l in (0, 1):
        w[f"mlp_ln{l}"] = p["mlp_ln"][l:l + 1]
    for n in _MM_UNITS:
        if n in p:
            w[n] = _mw_pad(n, p[n])
    return w


def _unprep_grads(g):
    o = {}
    o["meta"] = g["meta"]
    for k in ("ev_ln", "ev_a_norm", "ev_conv_b", "ev_b_ra", "ev_b_rx", "ev_lam", "od_ln", "od_g_qa", "od_g_kva"):
        o[k] = g[k].reshape(1, -1)
    o["ev_b_if"] = g["ev_b_if"][:, :2 * A_HEADS]
    o["ev_conv_w"] = g["ev_conv_w"][None]
    o["ev_w_ra"] = g["ev_w_ra"][None]
    o["ev_w_rx"] = g["ev_w_rx"][None]
    o["od_g_qn"] = g["gq_full"][:, :C_NOPE]
    o["od_g_qr"] = g["gq_full"][:, C_NOPE:C_NOPE + C_ROPE]
    o["od_g_kn"] = g["gk_full"][:, :C_NOPE]
    o["od_g_kr"] = g["gk_full"][:, C_NOPE:C_NOPE + C_ROPE]
    o["od_g_fq"] = g["gfq_full"][:, :D_HD]
    o["od_g_fk"] = g["gfk_full"][:, :D_HD]
    o["od_b_f"] = g["bf_full"][:, 96:96 + D_HEADS]
    o["mlp_ln"] = jnp.concatenate([g["mlp_ln0"], g["mlp_ln1"]], axis=0)
    return o


def _exchange(ops, name):
    n_ops = len(ops)
    flags = [s for _, s in ops]

    def body(*refs):
        local, remote = _comm_copies(refs[:n_ops], refs[n_ops:2 * n_ops], flags, *refs[2 * n_ops:])
        for cp in local + remote:
            cp.start()
        _comm_wait(local, remote)

    any_spec = pl.BlockSpec(memory_space=pl.ANY)
    c_shape, c_sems = _comm_shapes(ops)
    return pl.pallas_call(body, name=name, out_shape=c_shape, in_specs=[any_spec] * n_ops,
                          out_specs=[any_spec] * n_ops, scratch_shapes=c_sems)(*[x for x, _ in ops])


def _adamw(parts, w, m, v, name):
    r, c = w.shape
    tr = r
    for cand in (512, 256, 128, 64, 32, 16):
        if r % cand == 0 and N_DEV * cand * c * 4 <= 4 * 1024 * 1024:
            tr = cand
            break
    c1 = 1.0 / (1.0 - ADAM_B1 ** ADAM_STEP)
    c2 = 1.0 / (1.0 - ADAM_B2 ** ADAM_STEP)

    def body(p_ref, w_ref, m_ref, v_ref, g_ref, d_ref, mo_ref, vo_ref):
        g = p_ref[0].astype(F32)
        for j in range(1, N_DEV):
            g = g + p_ref[j].astype(F32)
        m2 = ADAM_B1 * m_ref[...] + (1.0 - ADAM_B1) * g
        v2 = ADAM_B2 * v_ref[...] + (1.0 - ADAM_B2) * (g * g)
        g_ref[...] = g
        mo_ref[...] = m2
        vo_ref[...] = v2
        d_ref[...] = -ADAM_LR * ((m2 * c1) / (jnp.sqrt(v2 * c2) + ADAM_EPS) + ADAM_WD * w_ref[...])

    spec = pl.BlockSpec((tr, c), lambda i: (i, 0))
    return pl.pallas_call(
        body, name=name, grid=(r // tr,),
        in_specs=[pl.BlockSpec((N_DEV, tr, c), lambda i: (0, i, 0)), spec, spec, spec],
        out_specs=[spec] * 4, out_shape=[jax.ShapeDtypeStruct((r, c), F32)] * 4,
        compiler_params=_cparams(("parallel",)),
    )(parts, w, m, v)


def _rows_for(n, mult):
    return -(-n // (1024 * mult)) * mult


def _pack(arrs, mult, lead=()):
    nl = len(lead)
    flat = jnp.concatenate([a.reshape(lead + (-1,)) for a in arrs], axis=nl)
    rows = _rows_for(flat.shape[-1], mult)
    return jnp.pad(flat, [(0, 0)] * nl + [(0, rows * 1024 - flat.shape[-1])]).reshape(lead + (rows, 1024))


def _unpack(buf, shapes):
    lead = buf.shape[:-2]
    flat = buf.reshape(lead + (-1,))
    out, off = [], 0
    for s in shapes:
        n = 1
        for d_ in s:
            n *= d_
        out.append(flat[..., off:off + n].reshape(lead + tuple(s)))
        off += n
    return out


def _unshard(g8, ax):
    a = jnp.moveaxis(g8, 0, ax)
    return a.reshape(a.shape[:ax] + (N_DEV * a.shape[ax + 1],) + a.shape[ax + 2:])


def _shard8(full, ax):
    s = full.shape
    return jnp.moveaxis(full.reshape(s[:ax] + (N_DEV, s[ax] // N_DEV) + s[ax + 1:]), ax, 0)


_NAMES = ["meta", "ev_ln", "ev_w_in", "ev_b_if", "ev_a_norm", "ev_conv_w", "ev_conv_b", "ev_w_ra", "ev_b_ra",
          "ev_w_rx", "ev_b_rx", "ev_lam", "ev_w_out", "od_ln", "od_w_in", "od_b_f", "od_g_qa", "od_g_kva",
          "od_w_uq", "od_w_ukv", "od_g_qn", "od_g_qr", "od_g_kn", "od_g_kr", "od_g_fq", "od_g_fk", "od_w_out",
          "mlp_ln", "w_ff1", "w_ff2"]
_SHARD_AXIS = {"meta": 1, "ev_w_in": 2, "ev_conv_w": 2, "ev_w_out": 1, "od_ln": 1, "od_w_in": 2, "od_g_qa": 1,
               "od_g_kva": 1, "od_w_uq": 2, "od_w_ukv": 2, "od_w_out": 1, "w_ff1": 2, "w_ff2": 1}
_MATMUL_WEIGHTS = ("ev_w_in", "ev_w_out", "od_w_in", "od_w_uq", "od_w_ukv", "od_w_out", "w_ff1", "w_ff2")
_BIG_REPL = ("ev_w_ra", "ev_w_rx")
_COL_SHARDED = ("ev_w_in", "od_w_in", "od_w_uq", "od_w_ukv", "w_ff1_0", "w_ff1_1")
_GATHER_ON = {"mm_z0": ("od_w_in", "od_w_uq", "od_w_ukv"), "mlstm_fwd": ("ev_w_out", "w_ff1_0", "w_ff2_0"),
              "mla_attn_f": ("od_w_out", "w_ff1_1", "w_ff2_1")}
_SCATTER_ON = {"fox_attn_b": ("w_ff2_1", "w_ff1_1", "od_w_out"),
               "mlstm_bwd": ("od_w_ukv", "od_w_uq", "od_w_in", "w_ff2_0", "w_ff1_0", "ev_w_out"),
               "mm_dxn0": ("ev_w_in",)}
_REPL_ON = "mlstm_bwd"


def _unit_of(d, n):
    return d[n[:-2]][int(n[-1])] if n.startswith("w_ff") else d[n][0]


def _unit_full(n, g8):
    return jnp.transpose(g8, (1, 0, 2)).reshape(g8.shape[1], -1) if n in _COL_SHARDED else g8.reshape(-1, g8.shape[2])


def _unit_slots(n, full, r, c):
    return full.reshape(r, N_DEV, c).transpose(1, 0, 2) if n in _COL_SHARDED else full.reshape(N_DEV, r, c)


class _Plan:
    def __init__(self, shards):
        self.shards = shards
        self.parts = {}

    def slots(self, n, g):
        r, c = self.shards[n].shape
        return _unit_slots(n, _mw_unpad(n, g[n]), r, c).astype(BF16)

    def ride(self, tag, g):
        if tag in _GATHER_ON:
            return [(self.shards[n].astype(BF16), False) for n in _GATHER_ON[tag]]
        ops = [(self.slots(n, g), True) for n in _SCATTER_ON[tag]]
        if tag == _REPL_ON:
            ops += [(g[n].reshape(-1, B_BLOCK).astype(BF16), False) for n in _BIG_REPL]
        return ops

    def arrived(self, tag, got, w):
        if tag in _GATHER_ON:
            for n, g8 in zip(_GATHER_ON[tag], got):
                w[n] = _mw_pad(n, _unit_full(n, g8))
        else:
            self.parts.update(zip(_SCATTER_ON[tag] + (_BIG_REPL if tag == _REPL_ON else ()), got))


def kernel(x, positions, meta, ev_ln, ev_w_in, ev_b_if, ev_a_norm, ev_conv_w, ev_conv_b, ev_w_ra, ev_b_ra, ev_w_rx, ev_b_rx, ev_lam, ev_w_out, od_ln, od_w_in, od_b_f, od_g_qa, od_g_kva, od_w_uq, od_w_ukv, od_g_qn, od_g_qr, od_g_kn, od_g_kr, od_g_fq, od_g_fk, od_w_out, mlp_ln, w_ff1, w_ff2, loss_target, m_meta, m_ev_ln, m_ev_w_in, m_ev_b_if, m_ev_a_norm, m_ev_conv_w, m_ev_conv_b, m_ev_w_ra, m_ev_b_ra, m_ev_w_rx, m_ev_b_rx, m_ev_lam, m_ev_w_out, m_od_ln, m_od_w_in, m_od_b_f, m_od_g_qa, m_od_g_kva, m_od_w_uq, m_od_w_ukv, m_od_g_qn, m_od_g_qr, m_od_g_kn, m_od_g_kr, m_od_g_fq, m_od_g_fk, m_od_w_out, m_mlp_ln, m_w_ff1, m_w_ff2, v_meta, v_ev_ln, v_ev_w_in, v_ev_b_if, v_ev_a_norm, v_ev_conv_w, v_ev_conv_b, v_ev_w_ra, v_ev_b_ra, v_ev_w_rx, v_ev_b_rx, v_ev_lam, v_ev_w_out, v_od_ln, v_od_w_in, v_od_b_f, v_od_g_qa, v_od_g_kva, v_od_w_uq, v_od_w_ukv, v_od_g_qn, v_od_g_qr, v_od_g_kn, v_od_g_kr, v_od_g_fq, v_od_g_fk, v_od_w_out, v_mlp_ln, v_w_ff1, v_w_ff2):
    given = dict(locals())
    wts = {n: given[n] for n in _NAMES}
    mom = {n: given["m_" + n] for n in _NAMES}
    var = {n: given["v_" + n] for n in _NAMES}
    small_sh = [n for n in _NAMES if n in _SHARD_AXIS and n not in _MATMUL_WEIGHTS]
    small_rp = [n for n in _NAMES if n not in _SHARD_AXIS and n not in _BIG_REPL]
    shp = {n: wts[n].shape for n in _NAMES}
    plan = _Plan({n: _unit_of(wts, n) for n in _MM_UNITS})

    got = _exchange([(plan.shards["ev_w_in"].astype(BF16), False), (_pack([wts[n] for n in small_sh], 8), False)],
                    "gather_first")
    p = {n: wts[n] for n in _NAMES if n not in _SHARD_AXIS}
    for n, a in zip(small_sh, _unpack(got[1], [shp[n] for n in small_sh])):
        p[n] = _unshard(a, _SHARD_AXIS[n])
    p["ev_w_in"] = _unit_full("ev_w_in", got[0])

    loss, gx, g = _local_step(x[0], positions[0], loss_target[0], _prep_weights(p), plan)
    grads = _unprep_grads(g)

    parts = _exchange([(_pack([_shard8(grads[n], _SHARD_AXIS[n]) for n in small_sh], 8, (N_DEV,)), True),
                       (_pack([grads[n].reshape(shp[n]) for n in small_rp], 8), False)], "exchange_last")

    res = {}
    unit_res = {n: _adamw(plan.parts[n], *[_unit_of(d, n) for d in (wts, mom, var)], f"adamw_{n}")
                for n in _MM_UNITS}
    for n in _MATMUL_WEIGHTS:
        if n.startswith("w_ff"):
            res[n] = [jnp.stack([unit_res[n + "_0"][k], unit_res[n + "_1"][k]]) for k in range(4)]
        else:
            res[n] = [r[None] for r in unit_res[n]]
    r4 = _adamw(parts[0], *[_pack([d[n] for n in small_sh], 8) for d in (wts, mom, var)], "adamw_small_sharded")
    for n, *four in zip(small_sh, *[_unpack(r, [shp[n] for n in small_sh]) for r in r4]):
        res[n] = four
    for n in _BIG_REPL:
        r4 = _adamw(plan.parts[n], *[d[n].reshape(-1, B_BLOCK) for d in (wts, mom, var)], f"adamw_{n}")
        res[n] = [r.reshape(shp[n]) for r in r4]
    r4 = _adamw(parts[1], *[_pack([d[n] for n in small_rp], 8) for d in (wts, mom, var)], "adamw_small_repl")
    for n, *four in zip(small_rp, *[_unpack(r, [shp[n] for n in small_rp]) for r in r4]):
        res[n] = four

    outs = [res[n][kind] for kind in range(4) for n in _NAMES]
    loss = lax.psum(loss, ("x", "y", "c"))
    return (loss, gx[None], *outs)
```

```python
import functools

import jax
import jax.numpy as jnp
from jax import lax
from jax.experimental import pallas as pl
from jax.experimental.pallas import tpu as pltpu

F32 = jnp.float32
BF16 = jnp.bfloat16

D_MODEL = 1024
N_META = 16
PAD = 112
EPS = 1e-6
NEG = -1e30
A_HEADS, A_DQK, A_DV, A_CHUNK = 4, 128, 256, 64
B_BLOCKS, B_BLOCK, CONV_W, LRU_C = 8, 128, 4, 8.0
C_HEADS, C_Q_LORA, C_KV_LORA, C_NOPE, C_ROPE, C_V = 8, 384, 256, 64, 32, 64
ROPE_THETA = 10000.0
D_HEADS, D_HD = 8, 64
HP = 128
ZE = 5376
ZO = 4096
N_DEV = 8
ADAM_LR, ADAM_B1, ADAM_B2, ADAM_EPS, ADAM_WD, ADAM_STEP = 0.001, 0.9, 0.999, 1e-08, 0.01, 10
VMEM_LIMIT = 56 * 1024 * 1024
SCAN_BLOCK = 128


def _pick(n, prefs):
    for p in prefs:
        if n % p == 0:
            return p
    return n


def _cparams(dims):
    return pltpu.CompilerParams(dimension_semantics=dims, vmem_limit_bytes=VMEM_LIMIT)


def _full_spec(shape):
    nd = len(shape)
    return pl.BlockSpec(shape, lambda *_: (0,) * nd)


def _me_and_peers():
    mx, my, mc = lax.axis_index("x"), lax.axis_index("y"), lax.axis_index("c")
    peers = []
    for k in range(1, N_DEV):
        px, py, pc = mx ^ ((k >> 2) & 1), my ^ ((k >> 1) & 1), mc ^ (k & 1)
        peers.append(((px, py, pc), 4 * px + 2 * py + pc))
    return 4 * mx + 2 * my + mc, peers


def _comm_copies(x_refs, o_refs, flags, send_sems, recv_sems, local_sems):
    n_peer = N_DEV - 1
    mx, my, mc = lax.axis_index("x"), lax.axis_index("y"), lax.axis_index("c")
    me = 4 * mx + 2 * my + mc
    sibling = (mx, my, 1 - mc)
    chips = [(mx ^ a, my ^ b) for a, b in ((0, 1), (1, 0), (1, 1))]
    _, peers = _me_and_peers()
    out = dict(first=[], landed=[], forward=[], remote=[], local=[])

    def copy(a, k, src, dst, to):
        return pltpu.make_async_remote_copy(src_ref=src, dst_ref=dst, send_sem=send_sems.at[a * n_peer + k],
                                            recv_sem=recv_sems.at[a * n_peer + k], device_id=to,
                                            device_id_type=pl.DeviceIdType.MESH)

    for a, scatter in enumerate(flags):
        x_ref, o_ref = x_refs[a], o_refs[a]
        out["local"].append(pltpu.make_async_copy(x_ref.at[me] if scatter else x_ref, o_ref.at[me], local_sems.at[a]))
        if scatter:
            out["first"] += [copy(a, k, x_ref.at[pid], o_ref.at[me], peer) for k, (peer, pid) in enumerate(peers)]
        else:
            out["first"].append(copy(a, 0, x_ref, o_ref.at[me], sibling))
            for t, (px, py) in enumerate(chips):
                far = copy(a, 1 + t, x_ref, o_ref.at[me], (px, py, mc))
                slot = o_ref.at[4 * px + 2 * py + mc]
                out["first"].append(far)
                out["landed"].append(far)
                out["forward"].append(copy(a, 4 + t, slot, slot, sibling))
    out["remote"] = out["first"] + out["forward"]
    return out


def _comm_begin(c):
    for cp in c["local"] + c["first"]:
        cp.start()


def _comm_end(c):
    for cp in c["landed"]:
        cp.wait_recv()
    for cp in c["forward"]:
        cp.start()
    for cp in c["remote"]:
        cp.wait_send()
    for cp in c["remote"]:
        if all(cp is not d for d in c["landed"]):
            cp.wait_recv()
    for cp in c["local"]:
        cp.wait()


def _comm_shapes(comm):
    n = len(comm)
    out_shape = [jax.ShapeDtypeStruct((N_DEV,) + x.shape[-2:], x.dtype) for x, _ in comm]
    sems = [pltpu.SemaphoreType.DMA((n * (N_DEV - 1),)), pltpu.SemaphoreType.DMA((n * (N_DEV - 1),)),
            pltpu.SemaphoreType.DMA((n,))]
    return out_shape, sems


def _pcall(body, name, grid, in_specs, out_specs, out_shape, scratch_shapes, dims, ins, comm=None):
    if not comm:
        return pl.pallas_call(body, name=name, grid=grid, in_specs=in_specs, out_specs=out_specs,
                              out_shape=out_shape, scratch_shapes=scratch_shapes,
                              compiler_params=_cparams(dims))(*ins)
    n_in, n_out, n = len(in_specs), len(out_specs), len(comm)
    flags = [s for _, s in comm]
    c_shape, c_sems = _comm_shapes(comm)

    def riding(*refs):
        cx = refs[n_in:n_in + n]
        co = refs[n_in + n + n_out:n_in + 2 * n + n_out]
        rest = refs[n_in + 2 * n + n_out:]
        sems = rest[len(rest) - 3:]
        ids = [pl.program_id(a) for a in range(len(grid))]
        first = functools.reduce(jnp.logical_and, [i == 0 for i in ids])
        last = functools.reduce(jnp.logical_and, [i == g - 1 for i, g in zip(ids, grid)])

        @pl.when(first)
        def _():
            _comm_begin(_comm_copies(cx, co, flags, *sems))

        body(*refs[:n_in], *refs[n_in + n:n_in + n + n_out], *rest[:len(rest) - 3])

        @pl.when(last)
        def _():
            _comm_end(_comm_copies(cx, co, flags, *sems))

    any_spec = pl.BlockSpec(memory_space=pl.ANY)
    res = pl.pallas_call(
        riding, name=name, grid=grid, in_specs=list(in_specs) + [any_spec] * n,
        out_specs=list(out_specs) + [any_spec] * n, out_shape=list(out_shape) + c_shape,
        scratch_shapes=list(scratch_shapes) + c_sems,
        compiler_params=_cparams(("arbitrary",) * len(grid)))(*ins, *[x for x, _ in comm])
    return list(res[:n_out]), list(res[n_out:])


def _mm(a, b, mode, name, out_dtype=None, add=None, relu2=False, relu2_of=None, comm=None):
    if out_dtype is None:
        out_dtype = BF16 if (mode == "tn" or relu2_of is not None) else F32
    if relu2_of is not None:
        add = relu2_of
    if mode == "nn":
        (m, k), n = a.shape, b.shape[1]
    elif mode == "nt":
        (m, k), n = a.shape, b.shape[0]
    else:
        (k, m), n = a.shape, b.shape[1]
    tm = _pick(m, (1408, 1024, 768, 512, 384, 256, 128))
    tn = _pick(n, (1024, 768, 512, 384, 256, 128))
    tk = _pick(k, (1408, 1024, 768, 512, 384, 256, 128))
    nk = k // tk
    if mode == "nn":
        a_spec = pl.BlockSpec((tm, tk), lambda i, j, q: (i, q))
        b_spec = pl.BlockSpec((tk, tn), lambda i, j, q: (q, j))
        dn = (((1,), (0,)), ((), ()))
    elif mode == "nt":
        a_spec = pl.BlockSpec((tm, tk), lambda i, j, q: (i, q))
        b_spec = pl.BlockSpec((tn, tk), lambda i, j, q: (j, q))
        dn = (((1,), (1,)), ((), ()))
    else:
        a_spec = pl.BlockSpec((tk, tm), lambda i, j, q: (q, i))
        b_spec = pl.BlockSpec((tk, tn), lambda i, j, q: (q, j))
        dn = (((0,), (0,)), ((), ()))
    o_spec = pl.BlockSpec((tm, tn), lambda i, j, q: (i, j))
    has_add = add is not None

    def body(*refs):
        a_ref, b_ref = refs[:2]
        add_ref = refs[2] if has_add else None
        o_refs, acc = refs[2 + has_add:-1], refs[-1]
        q = pl.program_id(2)

        @pl.when(q == 0)
        def _():
            acc[...] = jnp.zeros_like(acc)

        acc[...] += lax.dot_general(a_ref[...].astype(BF16), b_ref[...].astype(BF16), dn,
                                    preferred_element_type=F32)

        @pl.when(q == nk - 1)
        def _():
            r = acc[...]
            if relu2_of is not None:
                r = r * (2.0 * jnp.maximum(add_ref[...], 0.0))
            elif has_add:
                r = r + add_ref[...]
            o_refs[0][...] = r.astype(o_refs[0].dtype)
            if relu2:
                pos = jnp.maximum(r, 0.0)
                o_refs[1][...] = (pos * pos).astype(o_refs[1].dtype)

    ins = [a, b] + ([add] if has_add else [])
    in_specs = [a_spec, b_spec] + ([o_spec] if has_add else [])
    out_shape = [jax.ShapeDtypeStruct((m, n), out_dtype)] + ([jax.ShapeDtypeStruct((m, n), BF16)] if relu2 else [])
    res = _pcall(body, name, (m // tm, n // tn, nk), in_specs, [o_spec] * len(out_shape), out_shape,
                 [pltpu.VMEM((tm, tn), F32)], ("parallel", "parallel", "arbitrary"), ins, comm)
    outs = res[0] if comm else res
    outs = tuple(outs) if relu2 else outs[0]
    return (outs, res[1]) if comm else outs


def _row_specs(rows, tm):
    return [pl.BlockSpec((tm, w), functools.partial(lambda cb, i: (i, cb), cb)) for (_, w, cb) in rows]


def _row_fwd(name, f, rows, params, outs, tm):
    tp = rows[0][0].shape[0]
    nr, npar = len(rows), len(params)

    def body(*refs):
        i = pl.program_id(0)
        rv = [r[...] for r in refs[:nr]]
        pv = [r[...] for r in refs[nr:nr + npar]]
        res = f(i, rv, pv)
        for o_ref, r in zip(refs[nr + npar:], res):
            o_ref[...] = r.astype(o_ref.dtype)

    res = pl.pallas_call(
        body, name=name, grid=(tp // tm,),
        in_specs=_row_specs(rows, tm) + [_full_spec(p.shape) for p in params],
        out_specs=[pl.BlockSpec((tm, w), lambda i: (i, 0)) for (w, _) in outs],
        out_shape=[jax.ShapeDtypeStruct((tp, w), dt) for (w, dt) in outs],
        compiler_params=_cparams(("parallel",)),
    )(*[r[0] for r in rows], *params)
    return list(res)


def _row_bwd(name, f, rows, params, douts, tm, diff, add=None, out_dtypes=None):
    tp = rows[0][0].shape[0]
    nr, npar, nd = len(rows), len(params), len(douts)
    didx = [k for k in range(nr) if diff[k]]
    has_add = add is not None

    def body(*refs):
        i = pl.program_id(0)
        rv = [r[...] for r in refs[:nr]]
        pv = [r[...] for r in refs[nr:nr + npar]]
        dv = [r[...] for r in refs[nr + npar:nr + npar + nd]]
        pos = nr + npar + nd
        add_ref = refs[pos] if has_add else None
        pos += 1 if has_add else 0
        dr_refs = refs[pos:pos + len(didx)]
        dp_refs = refs[pos + len(didx):]

        def g(drv, pvs):
            full = list(rv)
            for k, val in zip(didx, drv):
                full[k] = val
            return tuple(f(i, full, list(pvs)))

        _, vjp = jax.vjp(g, [rv[k] for k in didx], pv)
        d_r, d_p = vjp(tuple(dv))
        for n_, (ref, val) in enumerate(zip(dr_refs, d_r)):
            if has_add and n_ == 0:
                val = val + add_ref[...]
            ref[...] = val.astype(ref.dtype)

        @pl.when(i == 0)
        def _():
            for ref in dp_refs:
                ref[...] = jnp.zeros_like(ref)

        for ref, val in zip(dp_refs, d_p):
            ref[...] += val

    in_specs = (_row_specs(rows, tm) + [_full_spec(p.shape) for p in params] + _row_specs(douts, tm))
    ins = [r[0] for r in rows] + list(params) + [d[0] for d in douts]
    if has_add:
        in_specs.append(pl.BlockSpec((tm, rows[didx[0]][1]), lambda i: (i, 0)))
        ins.append(add)
    out_specs = ([pl.BlockSpec((tm, rows[k][1]), lambda i: (i, 0)) for k in didx]
                 + [_full_spec(p.shape) for p in params])
    out_dtypes = out_dtypes or [F32] * len(didx)
    out_shape = ([jax.ShapeDtypeStruct((tp, rows[k][1]), dt) for k, dt in zip(didx, out_dtypes)]
                 + [jax.ShapeDtypeStruct(p.shape, F32) for p in params])
    res = pl.pallas_call(
        body, name=name, grid=(tp // tm,), in_specs=in_specs, out_specs=out_specs, out_shape=out_shape,
        compiler_params=_cparams(("arbitrary",)),
    )(*ins)
    return list(res[:len(didx)]), list(res[len(didx):])


def _concat_cols(pieces, dtype, name):
    arrs = [p for p in pieces if not isinstance(p, int)]
    widths = [p if isinstance(p, int) else p.shape[1] for p in pieces]
    tp = arrs[0].shape[0]
    tm = _pick(tp, (384, 256, 128))

    def body(*refs):
        o_ref, k, off = refs[-1], 0, 0
        for p, w in zip(pieces, widths):
            if isinstance(p, int):
                o_ref[:, off:off + w] = jnp.zeros((tm, w), o_ref.dtype)
            else:
                o_ref[:, off:off + w] = refs[k][...].astype(o_ref.dtype)
                k += 1
            off += w

    return pl.pallas_call(
        body, name=name, grid=(tp // tm,),
        in_specs=[pl.BlockSpec((tm, a.shape[1]), lambda i: (i, 0)) for a in arrs],
        out_specs=pl.BlockSpec((tm, sum(widths)), lambda i: (i, 0)),
        out_shape=jax.ShapeDtypeStruct((tp, sum(widths)), dtype),
        compiler_params=_cparams(("parallel",)),
    )(*arrs)


def _rowmask(i, tm):
    return (i * tm + lax.broadcasted_iota(jnp.int32, (tm, 1), 0)) >= PAD


def _lane(n=HP):
    return lax.broadcasted_iota(jnp.int32, (1, n), 1)


def _softplus(x):
    return jnp.maximum(x, 0.0) + jnp.log(1.0 + jnp.exp(-jnp.abs(x)))


def _log_sigmoid(x):
    return -_softplus(-x)


def _sigmoid(x):
    return 1.0 / (1.0 + jnp.exp(-x))


@functools.partial(jax.custom_vjp, nondiff_argnums=(1,))
def _lroll(x, s):
    return pltpu.roll(x, s % HP, 1)


def _lroll_fwd(x, s):
    return _lroll(x, s), None


def _lroll_bwd(s, _, g):
    return (pltpu.roll(g, (-s) % HP, 1),)


_lroll.defvjp(_lroll_fwd, _lroll_bwd)


def _f_norm(i, rv, pv):
    (h,), (g,) = rv, pv
    return [h * lax.rsqrt(jnp.mean(h * h, axis=-1, keepdims=True) + EPS) * g]


def _f_gate0(i, rv, pv):
    (misc,), (b,) = rv, pv
    tm = misc.shape[0]
    x = misc + b
    lane, ok = _lane(), _rowmask(i, tm)
    li = jnp.where(ok, x, NEG)
    lf = jnp.where(ok, _log_sigmoid(x), 0.0)
    return [jnp.where(lane < A_HEADS, li, jnp.where(lane < 2 * A_HEADS, lf, 0.0))]


def _f_aout(i, rv, pv):
    (ha, o), (g,) = rv, pv
    outs = []
    for h in range(A_HEADS):
        x = ha[:, h * A_DV:(h + 1) * A_DV]
        outs.append(x * lax.rsqrt(jnp.mean(x * x, axis=-1, keepdims=True) + EPS) * g)
    return [jnp.concatenate(outs, axis=1) * _sigmoid(o)]


def _f_gates(i, rv, pv):
    (xc,), (w_ra, b_ra, w_rx, b_rx, lam) = rv, pv
    tm = xc.shape[0]
    ra, rx = [], []
    for g in range(B_BLOCKS):
        xg = xc[:, g * B_BLOCK:(g + 1) * B_BLOCK].astype(BF16)
        ra.append(jnp.dot(xg, w_ra[g].astype(BF16), preferred_element_type=F32))
        rx.append(jnp.dot(xg, w_rx[g].astype(BF16), preferred_element_type=F32))
    r = _sigmoid(jnp.concatenate(ra, axis=1) + b_ra)
    ig = _sigmoid(jnp.concatenate(rx, axis=1) + b_rx)
    log_a = -LRU_C * r * _softplus(-lam)
    a = jnp.exp(log_a)
    u = jnp.sqrt(1.0 - jnp.exp(2.0 * log_a)) * (ig * xc)
    return [a, jnp.where(_rowmask(i, tm), u, 0.0)]


def _f_bout(i, rv, pv):
    hs, gb = rv
    gelu = 0.5 * gb * (1.0 + jnp.tanh(0.7978845608028654 * (gb + 0.044715 * gb * gb * gb)))
    return [hs * gelu]


def _f_cnorm(i, rv, pv):
    (cq, ckv), (gq, gkv) = rv, pv
    return [cq * lax.rsqrt(jnp.mean(cq * cq, axis=-1, keepdims=True) + EPS) * gq,
            ckv * lax.rsqrt(jnp.mean(ckv * ckv, axis=-1, keepdims=True) + EPS) * gkv]


def _rope128(x, cos, sin):
    lane = _lane()
    rot = jnp.where((lane >= C_NOPE) & (lane < C_NOPE + C_ROPE // 2), -_lroll(x, -(C_ROPE // 2)),
                    jnp.where((lane >= C_NOPE + C_ROPE // 2) & (lane < C_NOPE + C_ROPE), _lroll(x, C_ROPE // 2), 0.0))
    return x * cos + rot * sin


def _f_mlaprep(i, rv, pv):
    (q_, kk_, misc, cos, sin), (gq, gk) = rv, pv
    lane = _lane()
    m_n = lane < C_NOPE
    m_r = (lane >= C_NOPE) & (lane < C_NOPE + C_ROPE)

    def norm2(x, g):
        x2 = x * x
        sn = jnp.sum(jnp.where(m_n, x2, 0.0), axis=-1, keepdims=True) * (1.0 / C_NOPE)
        sr = jnp.sum(jnp.where(m_r, x2, 0.0), axis=-1, keepdims=True) * (1.0 / C_ROPE)
        scale = jnp.where(m_n, lax.rsqrt(sn + EPS), jnp.where(m_r, lax.rsqrt(sr + EPS), 0.0))
        return x * scale * g

    kr = _rope128(norm2(jnp.where(m_r, misc, 0.0), gk), cos, sin)
    qs, ks = [], []
    for h in range(C_HEADS):
        qs.append(_rope128(norm2(q_[:, h * HP:(h + 1) * HP], gq), cos, sin))
        ks.append(norm2(jnp.where(m_n, kk_[:, h * HP:(h + 1) * HP], 0.0), gk) + kr)
    return [jnp.concatenate(qs, axis=1), jnp.concatenate(ks, axis=1)]


def _f_foxprep(i, rv, pv):
    (fq, fk, misc), (gq, gk, bf) = rv, pv
    tm = fq.shape[0]
    lane = _lane()

    def hnorm(x, g):
        outs = []
        for h in range(D_HEADS):
            xh = x[:, h * HP:(h + 1) * HP]
            ss = jnp.sum(xh * xh, axis=-1, keepdims=True) * (1.0 / D_HD)
            outs.append(xh * lax.rsqrt(ss + EPS) * g)
        return jnp.concatenate(outs, axis=1)

    lf = jnp.where(_rowmask(i, tm) & (lane >= 96) & (lane < 96 + D_HEADS), _log_sigmoid(misc + bf), 0.0)
    return [hnorm(fq, gq), hnorm(fk, gk), lf]


def _mlstm_chunk(c, n, m, q, k, v, li, lf):
    ln = q.shape[0]
    r = lax.broadcasted_iota(jnp.int32, (ln, ln), 0)
    cc = lax.broadcasted_iota(jnp.int32, (ln, ln), 1)
    causal = cc <= r
    eye = cc == r
    li_row = jnp.sum(jnp.where(eye, li, 0.0), axis=0, keepdims=True)
    b_col = jnp.sum(jnp.where(causal, jnp.sum(jnp.where(eye, lf, 0.0), axis=0, keepdims=True), 0.0),
                    axis=1, keepdims=True)
    b_row = jnp.sum(jnp.where(r <= cc, lf, 0.0), axis=0, keepdims=True)
    k = k * (A_DQK ** -0.5)
    qb, kb, vb = q.astype(BF16), k.astype(BF16), v.astype(BF16)
    dmat = jnp.where(causal, b_col - b_row + li_row, NEG)
    inter = b_col + m
    m_row = jnp.maximum(inter, jnp.max(dmat, axis=1, keepdims=True))
    w_intra = jnp.exp(dmat - m_row)
    w_inter = jnp.exp(inter - m_row)
    s = lax.dot_general(qb, kb, (((1,), (1,)), ((), ())), preferred_element_type=F32) * w_intra
    num = (w_inter * jnp.dot(qb, c.astype(BF16), preferred_element_type=F32)
           + jnp.dot(s.astype(BF16), vb, preferred_element_type=F32))
    den = w_inter * jnp.sum(q * n, axis=1, keepdims=True) + jnp.sum(s, axis=1, keepdims=True)
    h = num / jnp.maximum(jnp.abs(den), jnp.exp(-m_row))
    g = jnp.sum(lf, axis=0, keepdims=True)
    dk = g - b_col + li
    m_new = jnp.maximum(g + m, jnp.max(dk, axis=0, keepdims=True))
    wk = jnp.exp(dk - m_new)
    sc = jnp.exp(g + m - m_new)
    kw = wk * k
    c_new = sc * c + lax.dot_general(kw.astype(BF16), vb, (((0,), (0,)), ((), ())), preferred_element_type=F32)
    n_new = sc * n + jnp.sum(kw, axis=0, keepdims=True)
    return c_new, n_new, m_new, h


A_QKV = 2 * A_HEADS * A_DQK + A_HEADS * A_DV


def _mlstm_chunk_all(cs, ns, ms, zqkv, gates):
    lane = _lane()
    c2, n2, m2, hs = [], [], [], []
    for h in range(A_HEADS):
        q = zqkv[:, h * A_DQK:(h + 1) * A_DQK]
        k = zqkv[:, (A_HEADS + h) * A_DQK:(A_HEADS + h + 1) * A_DQK]
        v = zqkv[:, 2 * A_HEADS * A_DQK + h * A_DV:2 * A_HEADS * A_DQK + (h + 1) * A_DV]
        li = jnp.sum(jnp.where(lane == h, gates, 0.0), axis=1, keepdims=True)
        lf = jnp.sum(jnp.where(lane == A_HEADS + h, gates, 0.0), axis=1, keepdims=True)
        c, n, m, hh = _mlstm_chunk(cs[h], ns[h], ms[h], q, k, v, li, lf)
        c2.append(c)
        n2.append(n)
        m2.append(m)
        hs.append(hh)
    return c2, n2, m2, jnp.concatenate(hs, axis=1)


def _mlstm_state_specs(nc, index):
    return ([pl.BlockSpec((1, A_HEADS, A_DQK, A_DV), lambda j: (index(j), 0, 0, 0)),
             pl.BlockSpec((1, A_HEADS, 1, A_DQK), lambda j: (index(j), 0, 0, 0)),
             pl.BlockSpec((1, A_HEADS, 1, 1), lambda j: (index(j), 0, 0, 0))],
            [jax.ShapeDtypeStruct((nc, A_HEADS, A_DQK, A_DV), F32),
             jax.ShapeDtypeStruct((nc, A_HEADS, 1, A_DQK), F32),
             jax.ShapeDtypeStruct((nc, A_HEADS, 1, 1), F32)],
            [pltpu.VMEM((A_HEADS, A_DQK, A_DV), F32), pltpu.VMEM((A_HEADS, 1, A_DQK), F32),
             pltpu.VMEM((A_HEADS, 1, 1), F32)])


def _mlstm_fwd(z, gates, comm=None):
    tp = z.shape[0]
    nc = tp // A_CHUNK
    ln = A_CHUNK
    heads = range(A_HEADS)

    def body(z_ref, g_ref, h_ref, cs_ref, ns_ref, ms_ref, c_s, n_s, m_s):
        @pl.when(pl.program_id(0) == 0)
        def _():
            c_s[...] = jnp.zeros_like(c_s)
            n_s[...] = jnp.zeros_like(n_s)
            m_s[...] = jnp.zeros_like(m_s)

        cs_ref[0] = c_s[...]
        ns_ref[0] = n_s[...]
        ms_ref[0] = m_s[...]
        c2, n2, m2, h = _mlstm_chunk_all([c_s[i] for i in heads], [n_s[i] for i in heads], [m_s[i] for i in heads],
                                         z_ref[...], g_ref[...])
        for i in heads:
            c_s[i] = c2[i]
            n_s[i] = n2[i]
            m_s[i] = m2[i]
        h_ref[...] = h

    st_specs, st_shapes, st_scratch = _mlstm_state_specs(nc, lambda j: j)
    return _pcall(
        body, "mlstm_fwd", (nc,),
        [pl.BlockSpec((ln, A_QKV), lambda j: (j, 0)), pl.BlockSpec((ln, HP), lambda j: (j, 0))],
        [pl.BlockSpec((ln, A_HEADS * A_DV), lambda j: (j, 0))] + st_specs,
        [jax.ShapeDtypeStruct((tp, A_HEADS * A_DV), F32)] + st_shapes,
        st_scratch, ("arbitrary",), (z, gates), comm)


def _mlstm_bwd(z, gates, cs, ns, ms, dh, comm=None):
    tp = z.shape[0]
    nc = tp // A_CHUNK
    ln = A_CHUNK
    heads = range(A_HEADS)

    def body(z_ref, g_ref, cs_ref, ns_ref, ms_ref, dh_ref, dz_ref, dg_ref, dc_s, dn_s, dm_s):
        @pl.when(pl.program_id(0) == 0)
        def _():
            dc_s[...] = jnp.zeros_like(dc_s)
            dn_s[...] = jnp.zeros_like(dn_s)
            dm_s[...] = jnp.zeros_like(dm_s)

        _, vjp = jax.vjp(_mlstm_chunk_all, [cs_ref[0, i] for i in heads], [ns_ref[0, i] for i in heads],
                         [ms_ref[0, i] for i in heads], z_ref[...], g_ref[...])
        dc, dn, dm, dz, dg = vjp(([dc_s[i] for i in heads], [dn_s[i] for i in heads], [dm_s[i] for i in heads],
                                  dh_ref[...]))
        for i in heads:
            dc_s[i] = dc[i]
            dn_s[i] = dn[i]
            dm_s[i] = dm[i]
        dz_ref[...] = dz.astype(BF16)
        dg_ref[...] = dg

    def rj(j):
        return nc - 1 - j

    st_specs, _, st_scratch = _mlstm_state_specs(nc, rj)
    return _pcall(
        body, "mlstm_bwd", (nc,),
        [pl.BlockSpec((ln, A_QKV), lambda j: (rj(j), 0)), pl.BlockSpec((ln, HP), lambda j: (rj(j), 0))] + st_specs
        + [pl.BlockSpec((ln, A_HEADS * A_DV), lambda j: (rj(j), 0))],
        [pl.BlockSpec((ln, A_QKV), lambda j: (rj(j), 0)), pl.BlockSpec((ln, HP), lambda j: (rj(j), 0))],
        [jax.ShapeDtypeStruct((tp, A_QKV), BF16), jax.ShapeDtypeStruct((tp, HP), F32)],
        st_scratch, ("arbitrary",), (z, gates, cs, ns, ms, dh), comm)


def _shift_down(x, s, row):
    return x if s == 0 else jnp.where(row >= s, pltpu.roll(x, s, 0), 0.0)


def _shift_up(x, s, row):
    n = x.shape[0]
    return x if s == 0 else jnp.where(row < n - s, pltpu.roll(x, n - s, 0), 0.0)


def _conv_fwd(z, xcb, w, b):
    tp, c = z.shape[0], w.shape[1]
    ct = 256

    def body(x_ref, w_ref, b_ref, o_ref):
        x = x_ref[...]
        row = lax.broadcasted_iota(jnp.int32, (tp, 1), 0)
        acc = jnp.zeros_like(x) + b_ref[...]
        for k in range(CONV_W):
            acc = acc + w_ref[k:k + 1, :] * _shift_down(x, CONV_W - 1 - k, row)
        o_ref[...] = acc

    return pl.pallas_call(
        body, name="conv_fwd", grid=(c // ct,),
        in_specs=[pl.BlockSpec((tp, ct), lambda j: (0, xcb + j)), pl.BlockSpec((CONV_W, ct), lambda j: (0, j)),
                  pl.BlockSpec((1, ct), lambda j: (0, j))],
        out_specs=pl.BlockSpec((tp, ct), lambda j: (0, j)),
        out_shape=jax.ShapeDtypeStruct((tp, c), F32),
        compiler_params=_cparams(("parallel",)),
    )(z, w, b)


def _conv_bwd(z, xcb, w, dxc):
    tp, c = z.shape[0], w.shape[1]
    ct = 256

    def body(x_ref, w_ref, d_ref, dx_ref, dw_ref, db_ref):
        x, d = x_ref[...], d_ref[...]
        row = lax.broadcasted_iota(jnp.int32, (tp, 1), 0)
        acc = jnp.zeros_like(x)
        for k in range(CONV_W):
            s = CONV_W - 1 - k
            acc = acc + w_ref[k:k + 1, :] * _shift_up(d, s, row)
            dw_ref[k:k + 1, :] = jnp.sum(d * _shift_down(x, s, row), axis=0, keepdims=True)
        dx_ref[...] = acc.astype(BF16)
        db_ref[...] = jnp.sum(d, axis=0, keepdims=True)

    return pl.pallas_call(
        body, name="conv_bwd", grid=(c // ct,),
        in_specs=[pl.BlockSpec((tp, ct), lambda j: (0, xcb + j)), pl.BlockSpec((CONV_W, ct), lambda j: (0, j)),
                  pl.BlockSpec((tp, ct), lambda j: (0, j))],
        out_specs=[pl.BlockSpec((tp, ct), lambda j: (0, j)), pl.BlockSpec((CONV_W, ct), lambda j: (0, j)),
                   pl.BlockSpec((1, ct), lambda j: (0, j))],
        out_shape=[jax.ShapeDtypeStruct((tp, c), BF16), jax.ShapeDtypeStruct((CONV_W, c), F32),
                   jax.ShapeDtypeStruct((1, c), F32)],
        compiler_params=_cparams(("parallel",)),
    )(z, w, dxc)


def _scan_fwd(a, u, name):
    tp, c = a.shape
    ct = _pick(c, (256, 128))
    lb = SCAN_BLOCK
    nb = tp // lb

    def body(a_ref, u_ref, h_ref, hp_ref):
        row = lax.broadcasted_iota(jnp.int32, (lb, 1), 0)

        def blk(j, carry):
            r0 = pl.multiple_of(j * lb, lb)
            aa, uu = a_ref[pl.ds(r0, lb), :], u_ref[pl.ds(r0, lb), :]
            s = 1
            while s < lb:
                mk = row >= s
                uu = jnp.where(mk, aa * pltpu.roll(uu, s, 0) + uu, uu)
                aa = jnp.where(mk, aa * pltpu.roll(aa, s, 0), aa)
                s *= 2
            hh = uu + aa * carry
            h_ref[pl.ds(r0, lb), :] = hh
            hp_ref[pl.ds(r0, lb), :] = jnp.where(row >= 1, pltpu.roll(hh, 1, 0), carry)
            return hh[lb - 1:lb, :]

        lax.fori_loop(0, nb, blk, jnp.zeros((1, ct), F32))

    spec = pl.BlockSpec((tp, ct), lambda j: (0, j))
    return pl.pallas_call(
        body, name=name, grid=(c // ct,), in_specs=[spec, spec], out_specs=[spec, spec],
        out_shape=[jax.ShapeDtypeStruct((tp, c), F32)] * 2,
        compiler_params=_cparams(("parallel",)),
    )(a, u)


def _scan_bwd(a, hprev, dh, name):
    tp, c = a.shape
    ct = _pick(c, (256, 128))
    lb = SCAN_BLOCK
    nb = tp // lb

    def body(a_ref, hp_ref, dh_ref, du_ref, da_ref):
        row = lax.broadcasted_iota(jnp.int32, (lb, 1), 0)

        def blk(jj, carry):
            g_next, a_next = carry
            r0 = pl.multiple_of((nb - 1 - jj) * lb, lb)
            a_blk = a_ref[pl.ds(r0, lb), :]
            aa = jnp.where(row < lb - 1, pltpu.roll(a_blk, lb - 1, 0), a_next)
            gg = dh_ref[pl.ds(r0, lb), :]
            s = 1
            while s < lb:
                mk = row < lb - s
                gg = jnp.where(mk, aa * pltpu.roll(gg, lb - s, 0) + gg, gg)
                aa = jnp.where(mk, aa * pltpu.roll(aa, lb - s, 0), aa)
                s *= 2
            gg = gg + aa * g_next
            du_ref[pl.ds(r0, lb), :] = gg
            da_ref[pl.ds(r0, lb), :] = gg * hp_ref[pl.ds(r0, lb), :]
            return gg[0:1, :], a_blk[0:1, :]

        lax.fori_loop(0, nb, blk, (jnp.zeros((1, ct), F32), jnp.zeros((1, ct), F32)))

    spec = pl.BlockSpec((tp, ct), lambda j: (0, j))
    return pl.pallas_call(
        body, name=name, grid=(c // ct,), in_specs=[spec, spec, spec], out_specs=[spec, spec],
        out_shape=[jax.ShapeDtypeStruct((tp, c), F32)] * 2,
        compiler_params=_cparams(("parallel",)),
    )(a, hprev, dh)


ATT_HEADS_PER_STEP = 2
LOG2E = 1.4426950408889634
LN2 = 0.6931471805599453


FOX_LANE0 = 96


def _head_lane(f, head):
    return jnp.sum(jnp.where(_lane() == FOX_LANE0 + head, f, 0.0), axis=1, keepdims=True)


def _diag_valid(i, tq):
    r = lax.broadcasted_iota(jnp.int32, (tq, tq), 0)
    c = lax.broadcasted_iota(jnp.int32, (tq, tq), 1)
    return (c <= r) & ((i * tq + c >= PAD) | (c == r))


def _key_terms(tp, tq, bk=None):
    pad_neg = jnp.where(jnp.arange(tp) < PAD, NEG, 0.0).astype(F32).reshape(1, tp // tq, 1, tq)
    if bk is None:
        return pad_neg, jnp.zeros_like(pad_neg)
    kd = -bk.reshape(bk.shape[0], tp // tq, 1, tq)
    return kd + pad_neg, kd


def _attn_fwd(q, k, v, vcb, scale, bq, kterms, name, comm=None):
    tp = q.shape[0]
    nh = q.shape[1] // HP
    tq = _pick(tp, (384, 256, 128))
    has_bq = bq is not None
    c1 = scale * LOG2E
    per_head = kterms[0].shape[0] > 1
    hg = ATT_HEADS_PER_STEP
    lanes = [slice(hh * HP, (hh + 1) * HP) for hh in range(hg)]

    def body(*refs):
        if has_bq:
            q_ref, k_ref, v_ref, kb_ref, kd_ref, bq_ref, o_ref, lse_ref = refs
            rb = [_head_lane(bq_ref[...], hg * pl.program_id(0) + hh) * LOG2E for hh in range(hg)]
        else:
            q_ref, k_ref, v_ref, kb_ref, kd_ref, o_ref, lse_ref = refs
        i = pl.program_id(1)
        qb = [q_ref[:, ln].astype(BF16) for ln in lanes]

        def tile(j, carry, diag):
            r0 = pl.multiple_of(j * tq, tq)
            out = []
            for hh in range(hg):
                m, l, acc = carry[hh]
                kb = k_ref[pl.ds(r0, tq), lanes[hh]].astype(BF16)
                vb = v_ref[pl.ds(r0, tq), lanes[hh]].astype(BF16)
                kt = (kd_ref if diag else kb_ref)[hh if per_head else 0, j] * LOG2E
                x = lax.dot_general(qb[hh], kb, (((1,), (1,)), ((), ())), preferred_element_type=F32) * c1 + kt
                if has_bq:
                    x = x + rb[hh]
                if diag:
                    x = jnp.where(_diag_valid(i, tq), x, NEG)
                m2 = jnp.maximum(m, jnp.max(x, axis=1, keepdims=True))
                alpha = jnp.exp2(m - m2)
                p = jnp.exp2(x - m2)
                l2 = alpha * l + jnp.sum(p, axis=1, keepdims=True)
                acc2 = alpha * acc + jnp.dot(p.astype(BF16), vb, preferred_element_type=F32)
                out.append((m2, l2, acc2))
            return tuple(out)

        init = tuple((jnp.full((tq, 1), NEG, F32), jnp.zeros((tq, 1), F32), jnp.zeros((tq, HP), F32))
                     for _ in range(hg))
        res = tile(i, lax.fori_loop(0, i, lambda j, c: tile(j, c, False), init), True)
        for hh, (m, l, acc) in enumerate(res):
            o_ref[:, lanes[hh]] = acc / l
            lse_ref[hh] = m * LN2 + jnp.log(l)

    kt_spec = pl.BlockSpec((hg if per_head else 1, tp // tq, 1, tq),
                           (lambda h, i: (h, 0, 0, 0)) if per_head else (lambda h, i: (0, 0, 0, 0)))
    in_specs = [pl.BlockSpec((tq, hg * HP), lambda h, i: (i, h)), pl.BlockSpec((tp, hg * HP), lambda h, i: (0, h)),
                pl.BlockSpec((tp, hg * HP), lambda h, i: (0, vcb // hg + h)), kt_spec, kt_spec]
    ins = [q, k, v, *kterms]
    if has_bq:
        in_specs += [pl.BlockSpec((tq, HP), lambda h, i: (i, 0))]
        ins += [bq]
    return _pcall(
        body, name, (nh // hg, tp // tq), in_specs,
        [pl.BlockSpec((tq, hg * HP), lambda h, i: (i, h)), pl.BlockSpec((hg, tq, 1), lambda h, i: (h, i, 0))],
        [jax.ShapeDtypeStruct((tp, nh * HP), F32), jax.ShapeDtypeStruct((nh, tp, 1), F32)],
        [], ("parallel", "parallel"), ins, comm)


def _attn_bwd(q, k, v, vcb, o, lse, do, docb, scale, bq, kterms, name, comm=None):
    tp = q.shape[0]
    nh = q.shape[1] // HP
    tq = _pick(tp, (384, 256, 128))
    has_bq = bq is not None
    c1 = scale * LOG2E
    per_head = kterms[0].shape[0] > 1
    hg = ATT_HEADS_PER_STEP
    lanes = [slice(hh * HP, (hh + 1) * HP) for hh in range(hg)]

    def body(*refs):
        if has_bq:
            (q_ref, k_ref, v_ref, o_ref, lse_ref, do_ref, kb_ref, kd_ref, bq_ref,
             dq_ref, dk_ref, dv_ref, dbq_ref, dkt_ref) = refs
            rb = [(_head_lane(bq_ref[...], hg * pl.program_id(0) + hh) - lse_ref[hh]) * LOG2E for hh in range(hg)]
        else:
            q_ref, k_ref, v_ref, o_ref, lse_ref, do_ref, kb_ref, kd_ref, dq_ref, dk_ref, dv_ref = refs
            rb = [lse_ref[hh] * (-LOG2E) for hh in range(hg)]
        i = pl.program_id(1)

        @pl.when(i == 0)
        def _():
            dk_ref[...] = jnp.zeros_like(dk_ref)
            dv_ref[...] = jnp.zeros_like(dv_ref)
            if has_bq:
                dkt_ref[...] = jnp.zeros_like(dkt_ref)

        qb = [q_ref[:, ln].astype(BF16) for ln in lanes]
        do_ = [do_ref[:, ln] for ln in lanes]
        dob = [d.astype(BF16) for d in do_]
        delta = [jnp.sum(do_[hh] * o_ref[:, lanes[hh]], axis=1, keepdims=True) for hh in range(hg)]

        def tile(j, carry, diag):
            r0 = pl.multiple_of(j * tq, tq)
            out = []
            for hh in range(hg):
                dq, dbq = carry[hh]
                kb = k_ref[pl.ds(r0, tq), lanes[hh]].astype(BF16)
                vb = v_ref[pl.ds(r0, tq), lanes[hh]].astype(BF16)
                kt = (kd_ref if diag else kb_ref)[hh if per_head else 0, j] * LOG2E
                x = (lax.dot_general(qb[hh], kb, (((1,), (1,)), ((), ())), preferred_element_type=F32) * c1
                     + kt + rb[hh])
                if diag:
                    x = jnp.where(_diag_valid(i, tq), x, NEG)
                p = jnp.exp2(x)
                dp = lax.dot_general(dob[hh], vb, (((1,), (1,)), ((), ())), preferred_element_type=F32)
                ds = p * (dp - delta[hh])
                dsb = ds.astype(BF16)
                dk_ref[pl.ds(r0, tq), lanes[hh]] += lax.dot_general(dsb, qb[hh], (((0,), (0,)), ((), ())),
                                                                    preferred_element_type=F32) * scale
                dv_ref[pl.ds(r0, tq), lanes[hh]] += lax.dot_general(p.astype(BF16), dob[hh], (((0,), (0,)), ((), ())),
                                                                    preferred_element_type=F32)
                if has_bq:
                    dkt_ref[hh, j] += jnp.sum(ds, axis=0, keepdims=True)
                    dbq = dbq + jnp.sum(ds, axis=1, keepdims=True)
                out.append((dq + jnp.dot(dsb, kb, preferred_element_type=F32), dbq))
            return tuple(out)

        init = tuple((jnp.zeros((tq, HP), F32), jnp.zeros((tq, 1), F32)) for _ in range(hg))
        res = tile(i, lax.fori_loop(0, i, lambda j, c: tile(j, c, False), init), True)
        for hh, (dq, dbq) in enumerate(res):
            dq_ref[:, lanes[hh]] = dq * scale
            if has_bq:
                dbq_ref[hh] = dbq

    blk_q = pl.BlockSpec((tq, hg * HP), lambda h, i: (i, h))
    blk_k = pl.BlockSpec((tp, hg * HP), lambda h, i: (0, h))
    kt_spec = pl.BlockSpec((hg if per_head else 1, tp // tq, 1, tq),
                           (lambda h, i: (h, 0, 0, 0)) if per_head else (lambda h, i: (0, 0, 0, 0)))
    row_spec = pl.BlockSpec((hg, tq, 1), lambda h, i: (h, i, 0))
    in_specs = [blk_q, blk_k, pl.BlockSpec((tp, hg * HP), lambda h, i: (0, vcb // hg + h)), blk_q,
                row_spec, pl.BlockSpec((tq, hg * HP), lambda h, i: (i, docb // hg + h)), kt_spec, kt_spec]
    ins = [q, k, v, o, lse, do, *kterms]
    out_specs = [blk_q, blk_k, blk_k]
    out_shape = [jax.ShapeDtypeStruct((tp, nh * HP), F32)] * 3
    if has_bq:
        in_specs += [pl.BlockSpec((tq, HP), lambda h, i: (i, 0))]
        ins += [bq]
        out_specs += [row_spec, kt_spec]
        out_shape += [jax.ShapeDtypeStruct((nh, tp, 1), F32), jax.ShapeDtypeStruct((nh, tp // tq, 1, tq), F32)]
    return _pcall(body, name, (nh // hg, tp // tq), in_specs, out_specs, out_shape, [], ("parallel", "arbitrary"), ins,
                  comm)


def _loss_head(h, tgt):
    tp, d = h.shape
    tm = 128
    first = (PAD + N_META) // tm

    def body(h_ref, t_ref, l_ref, d_ref):
        i = pl.program_id(0)

        @pl.when(i == 0)
        def _():
            l_ref[...] = jnp.zeros_like(l_ref)

        live = i >= first
        err = jnp.where(live, h_ref[...] - t_ref[...], 0.0)
        d_ref[...] = err * (1.0 / d)
        l_ref[...] += (0.5 / d) * jnp.sum(err * err)

    return pl.pallas_call(
        body, name="loss_head", grid=(tp // tm,),
        in_specs=[pl.BlockSpec((tm, d), lambda i: (i, 0)), pl.BlockSpec((tm, d), lambda i: (i, 0))],
        out_specs=[_full_spec((8, 128)), pl.BlockSpec((tm, d), lambda i: (i, 0))],
        out_shape=[jax.ShapeDtypeStruct((8, 128), F32), jax.ShapeDtypeStruct((tp, d), F32)],
        compiler_params=_cparams(("arbitrary",)),
    )(h, tgt)


def _rope_tables(pos_rows):
    half = C_ROPE // 2
    freqs = ROPE_THETA ** (-jnp.arange(half, dtype=F32) / half)
    ang = pos_rows[:, None].astype(F32) * freqs
    cos, sin = jnp.cos(ang), jnp.sin(ang)
    tp = pos_rows.shape[0]
    one, zero = jnp.ones((tp, C_NOPE), F32), jnp.zeros((tp, C_NOPE), F32)
    tail1, tail0 = jnp.ones((tp, HP - C_NOPE - C_ROPE), F32), jnp.zeros((tp, HP - C_NOPE - C_ROPE), F32)
    return (jnp.concatenate([one, cos, cos, tail1], axis=1), jnp.concatenate([zero, sin, sin, tail0], axis=1))


def _heads_to_cols(x, lo, n):
    t = x[:, lo:lo + n].T
    return t[:, :, None], t[:, None, :]


def _local_step(x, positions, tgt, w, plan=None):
    def hosted(tag, fn, *args):
        ops = plan.ride(tag, g) if plan is not None else None
        if not ops:
            return fn(*args)
        res, got = fn(*args, comm=ops)
        plan.arrived(tag, got, w)
        return res

    s_len = x.shape[0]
    tp = PAD + N_META + s_len
    tm = _pick(tp, (384, 256, 128))
    front = PAD + N_META
    h0 = jnp.concatenate([jnp.zeros((PAD, D_MODEL), F32), w["meta"], x], axis=0)
    tgt_p = jnp.concatenate([jnp.zeros((front, D_MODEL), F32), tgt], axis=0)
    pos_rows = jnp.concatenate([jnp.zeros((PAD,), jnp.int32), jnp.arange(N_META, dtype=jnp.int32),
                                positions + N_META])
    cos, sin = _rope_tables(pos_rows)
    g = {}

    r_h0 = [(h0, D_MODEL, 0)]
    (xn0,) = _row_fwd("norm0_f", _f_norm, r_h0, [w["ev_ln"]], [(D_MODEL, BF16)], tm)
    z0 = hosted("mm_z0", _mm, xn0, w["ev_w_in"], "nn", "mm_z0")
    r_misc0 = [(z0, HP, 5120 // HP)]
    (g0,) = _row_fwd("gate0_f", _f_gate0, r_misc0, [w["ev_b_if"]], [(HP, F32)], tm)
    h_a, cs, ns, ms = hosted("mlstm_fwd", _mlstm_fwd, z0, g0)
    r_aout = [(h_a, 1024, 0), (z0, 1024, 2)]
    (ha,) = _row_fwd("aout_f", _f_aout, r_aout, [w["ev_a_norm"]], [(1024, BF16)], tm)
    xc = _conv_fwd(z0, 3072 // 256, w["ev_conv_w"], w["ev_conv_b"])
    p_gates = [w["ev_w_ra"], w["ev_b_ra"], w["ev_w_rx"], w["ev_b_rx"], w["ev_lam"]]
    a_g, u_g = _row_fwd("gates_f", _f_gates, [(xc, 1024, 0)], p_gates, [(1024, F32), (1024, F32)], tm)
    hs, hprev = _scan_fwd(a_g, u_g, "lru_f")
    r_bout = [(hs, 1024, 0), (z0, 1024, 4)]
    (hb,) = _row_fwd("bout_f", _f_bout, r_bout, [], [(1024, BF16)], tm)
    hab = _concat_cols([ha, hb], BF16, "cat_hab")
    h1 = _mm(hab, w["ev_w_out"], "nn", "mm_h1", add=h0)
    (xn1,) = _row_fwd("norm1_f", _f_norm, [(h1, D_MODEL, 0)], [w["mlp_ln0"]], [(D_MODEL, BF16)], tm)
    p0, act0 = _mm(xn1, w["w_ff1_0"], "nn", "mm_p0", relu2=True)
    h2 = _mm(act0, w["w_ff2_0"], "nn", "mm_h2", add=h1)
    (xn2,) = _row_fwd("norm2_f", _f_norm, [(h2, D_MODEL, 0)], [w["od_ln"]], [(D_MODEL, BF16)], tm)
    z1 = _mm(xn2, w["od_w_in"], "nn", "mm_z1")
    r_c = [(z1, C_Q_LORA, 3072 // C_Q_LORA), (z1, C_KV_LORA, 3584 // C_KV_LORA)]
    cqn, ckvn = _row_fwd("cnorm_f", _f_cnorm, r_c, [w["od_g_qa"], w["od_g_kva"]],
                         [(C_Q_LORA, BF16), (C_KV_LORA, BF16)], tm)
    q_ = _mm(cqn, w["od_w_uq"], "nn", "mm_q")
    kv_ = _mm(ckvn, w["od_w_ukv"], "nn", "mm_kv")
    r_mla = [(q_, 1024, 0), (kv_, 1024, 0), (z1, HP, 3840 // HP), (cos, HP, 0), (sin, HP, 0)]
    p_mla = [w["gq_full"], w["gk_full"]]
    qm, km = _row_fwd("mla_f", _f_mlaprep, r_mla, p_mla, [(1024, BF16), (1024, BF16)], tm)
    sc_c = (C_NOPE + C_ROPE) ** -0.5
    kt_c = _key_terms(tp, tm)
    hc, lse_c = hosted("mla_attn_f", _attn_fwd, qm, km, kv_, C_HEADS, sc_c, None, kt_c, "mla_attn_f")
    r_fox = [(z1, 1024, 0), (z1, 1024, 1), (z1, HP, 3840 // HP)]
    p_fox = [w["gfq_full"], w["gfk_full"], w["bf_full"]]
    qf, kf, lfx = _row_fwd("fox_f", _f_foxprep, r_fox, p_fox, [(1024, BF16), (1024, BF16), (HP, F32)], tm)
    ones = jnp.ones((tp, HP), F32)
    fcum, fprev = _scan_fwd(ones, lfx, "fcum_f")
    _, bk = _heads_to_cols(fcum, FOX_LANE0, D_HEADS)
    kt_d = _key_terms(tp, tm, bk)
    sc_d = D_HD ** -0.5
    bq = fcum
    hd, lse_d = _attn_fwd(qf, kf, z1, 2048 // HP, sc_d, bq, kt_d, "fox_attn_f")
    hcd = _concat_cols([hc, hd], F32, "cat_hcd")
    h3 = _mm(hcd, w["od_w_out"], "nn", "mm_h3", add=h2)
    (xn3,) = _row_fwd("norm3_f", _f_norm, [(h3, D_MODEL, 0)], [w["mlp_ln1"]], [(D_MODEL, BF16)], tm)
    p1, act1 = _mm(xn3, w["w_ff1_1"], "nn", "mm_p1", relu2=True)
    h4 = _mm(act1, w["w_ff2_1"], "nn", "mm_h4", add=h3)
    lpart, dh4 = _loss_head(h4, tgt_p)
    loss = lpart[0, 0]

    def mlp_bwd(tag, dh_out, h_in, xn, p, act, ln, w1, w2):
        dp = _mm(dh_out, w2, "nt", f"mm_dp{tag}", relu2_of=p)
        g_w2 = _mm(act, dh_out, "tn", f"mm_dw2_{tag}")
        g_w1 = _mm(xn, dp, "tn", f"mm_dw1_{tag}")
        dxn = _mm(dp, w1, "nt", f"mm_dxn{tag}")
        (dh_in,), (g_ln,) = _row_bwd(f"normm{tag}_b", _f_norm, [(h_in, D_MODEL, 0)], [ln], [(dxn, D_MODEL, 0)], tm,
                                     [True], add=dh_out)
        return dh_in, g_ln, g_w1, g_w2

    dh3, g["mlp_ln1"], g["w_ff1_1"], g["w_ff2_1"] = mlp_bwd("1", dh4, h3, xn3, p1, act1, w["mlp_ln1"],
                                                            w["w_ff1_1"], w["w_ff2_1"])
    dhcd = _mm(dh3, w["od_w_out"], "nt", "mm_dhcd")
    g["od_w_out"] = _mm(hcd, dh3, "tn", "mm_dwout1")
    dqf, dkf, dvf, dbq, dkt = hosted("fox_attn_b", _attn_bwd, qf, kf, z1, 2048 // HP, hd, lse_d, dhcd, D_HEADS, sc_d,
                                     bq, kt_d, "fox_attn_b")
    dfc = dbq[:, :, 0].T - dkt.reshape(D_HEADS, tp).T
    dfcum = jnp.concatenate([jnp.zeros((tp, 96), F32), dfc, jnp.zeros((tp, HP - 96 - D_HEADS), F32)], axis=1)
    dlfx, _ = _scan_bwd(ones, fprev, dfcum, "fcum_b")
    (dfq, dfk, dmisc_f), (g["gfq_full"], g["gfk_full"], g["bf_full"]) = _row_bwd(
        "fox_b", _f_foxprep, r_fox, p_fox, [(dqf, 1024, 0), (dkf, 1024, 0), (dlfx, HP, 0)], tm, [True, True, True],
        out_dtypes=[BF16, BF16, F32])
    dqm, dkm, dvm = _attn_bwd(qm, km, kv_, C_HEADS, hc, lse_c, dhcd, 0, sc_c, None, kt_c, "mla_attn_b")
    (dq_, dkk_, dmisc_m), (g["gq_full"], g["gk_full"]) = _row_bwd(
        "mla_b", _f_mlaprep, r_mla, p_mla, [(dqm, 1024, 0), (dkm, 1024, 0)], tm, [True, True, True, False, False],
        out_dtypes=[BF16, BF16, F32])
    dkv_ = _concat_cols([dkk_, dvm], BF16, "cat_dkv")
    dckvn = _mm(dkv_, w["od_w_ukv"], "nt", "mm_dckvn")
    g["od_w_ukv"] = _mm(ckvn, dkv_, "tn", "mm_dwukv")
    dcqn = _mm(dq_, w["od_w_uq"], "nt", "mm_dcqn")
    g["od_w_uq"] = _mm(cqn, dq_, "tn", "mm_dwuq")
    (dcq, dckv), (g["od_g_qa"], g["od_g_kva"]) = _row_bwd(
        "cnorm_b", _f_cnorm, r_c, [w["od_g_qa"], w["od_g_kva"]],
        [(dcqn, C_Q_LORA, 0), (dckvn, C_KV_LORA, 0)], tm, [True, True], out_dtypes=[BF16, BF16])
    dz1 = _concat_cols([dfq, dfk, dvf, dcq, HP, dckv, dmisc_f + dmisc_m, HP], BF16, "cat_dz1")
    g["od_w_in"] = _mm(xn2, dz1, "tn", "mm_dwin1")
    dxn2 = _mm(dz1, w["od_w_in"], "nt", "mm_dxn2")
    (dh2,), (g["od_ln"],) = _row_bwd("norm2_b", _f_norm, [(h2, D_MODEL, 0)], [w["od_ln"]], [(dxn2, D_MODEL, 0)], tm,
                                     [True], add=dh3)
    dh1, g["mlp_ln0"], g["w_ff1_0"], g["w_ff2_0"] = mlp_bwd("0", dh2, h1, xn1, p0, act0, w["mlp_ln0"],
                                                            w["w_ff1_0"], w["w_ff2_0"])
    dhab = _mm(dh1, w["ev_w_out"], "nt", "mm_dhab")
    g["ev_w_out"] = _mm(hab, dh1, "tn", "mm_dwout0")
    (dhs, dgb), _ = _row_bwd("bout_b", _f_bout, r_bout, [], [(dhab, 1024, 1)], tm, [True, True],
                             out_dtypes=[F32, BF16])
    du_g, da_g = _scan_bwd(a_g, hprev, dhs, "lru_b")
    (dxc,), (g["ev_w_ra"], g["ev_b_ra"], g["ev_w_rx"], g["ev_b_rx"], g["ev_lam"]) = _row_bwd(
        "gates_b", _f_gates, [(xc, 1024, 0)], p_gates, [(da_g, 1024, 0), (du_g, 1024, 0)], tm, [True])
    dxb, g["ev_conv_w"], g["ev_conv_b"] = _conv_bwd(z0, 3072 // 256, w["ev_conv_w"], dxc)
    (dh_a, do_), (g["ev_a_norm"],) = _row_bwd("aout_b", _f_aout, r_aout, [w["ev_a_norm"]], [(dhab, 1024, 0)], tm,
                                              [True, True], out_dtypes=[F32, BF16])
    dqkv, dg0 = hosted("mlstm_bwd", _mlstm_bwd, z0, g0, cs, ns, ms, dh_a)
    (dmisc0,), (g["ev_b_if"],) = _row_bwd("gate0_b", _f_gate0, r_misc0, [w["ev_b_if"]], [(dg0, HP, 0)], tm, [True],
                                          out_dtypes=[BF16])
    dz0 = _concat_cols([dqkv, do_, dxb, dgb, dmisc0, HP], BF16, "cat_dz0")
    g["ev_w_in"] = _mm(xn0, dz0, "tn", "mm_dwin0")
    dxn0 = hosted("mm_dxn0", _mm, dz0, w["ev_w_in"], "nt", "mm_dxn0")
    (dh0,), (g["ev_ln"],) = _row_bwd("norm0_b", _f_norm, r_h0, [w["ev_ln"]], [(dxn0, D_MODEL, 0)], tm, [True], add=dh1)
    g["meta"] = dh0[PAD:front]
    return loss, dh0[front:], g


def _pad_last(a, n):
    return jnp.pad(a, [(0, 0)] * (a.ndim - 1) + [(0, n - a.shape[-1])])


def _pad_heads(a, nh, d):
    return _pad_last(a.reshape(a.shape[:-1] + (nh, d)), HP).reshape(a.shape[:-1] + (nh * HP,))


def _unpad_heads(a, nh, d):
    return a.reshape(a.shape[:-1] + (nh, HP))[..., :d].reshape(a.shape[:-1] + (nh * d,))


_MM_UNITS = ("ev_w_in", "ev_w_out", "od_w_in", "od_w_uq", "od_w_ukv", "od_w_out",
             "w_ff1_0", "w_ff2_0", "w_ff1_1", "w_ff2_1")


def _mw_pad(name, a):
    if name == "ev_w_in":
        return _pad_last(jnp.concatenate([a[:, :3072], a[:, 3080:5128], a[:, 3072:3080]], axis=1), ZE)
    if name == "od_w_in":
        z = lambda n: jnp.zeros((a.shape[0], n), a.dtype)
        return jnp.concatenate(
            [_pad_heads(a[:, 672:1184], D_HEADS, D_HD), _pad_heads(a[:, 1184:1696], D_HEADS, D_HD),
             _pad_heads(a[:, 1696:2208], D_HEADS, D_HD), a[:, 0:384], z(128), a[:, 384:640],
             z(64), a[:, 640:672], a[:, 2208:2216], z(24), z(128)], axis=1)
    if name == "od_w_uq":
        return _pad_heads(a, C_HEADS, C_NOPE + C_ROPE)
    if name == "od_w_ukv":
        wkv = a.reshape(C_KV_LORA, C_HEADS, C_NOPE + C_V)
        return jnp.concatenate([_pad_last(wkv[:, :, :C_NOPE], HP).reshape(C_KV_LORA, -1),
                                _pad_last(wkv[:, :, C_NOPE:], HP).reshape(C_KV_LORA, -1)], axis=1)
    if name == "od_w_out":
        return jnp.pad(a.reshape(2 * C_HEADS, C_V, D_MODEL), ((0, 0), (0, HP - C_V), (0, 0))).reshape(-1, D_MODEL)
    return a


def _mw_unpad(name, g):
    if name == "ev_w_in":
        return jnp.concatenate([g[:, :3072], g[:, 5120:5128], g[:, 3072:5120]], axis=1)
    if name == "od_w_in":
        return jnp.concatenate(
            [g[:, 3072:3456], g[:, 3584:3840], g[:, 3904:3936], _unpad_heads(g[:, 0:1024], D_HEADS, D_HD),
             _unpad_heads(g[:, 1024:2048], D_HEADS, D_HD), _unpad_heads(g[:, 2048:3072], D_HEADS, D_HD),
             g[:, 3936:3944]], axis=1)
    if name == "od_w_uq":
        return _unpad_heads(g, C_HEADS, C_NOPE + C_ROPE)
    if name == "od_w_ukv":
        gk = g[:, :C_HEADS * HP].reshape(C_KV_LORA, C_HEADS, HP)[:, :, :C_NOPE]
        gv = g[:, C_HEADS * HP:].reshape(C_KV_LORA, C_HEADS, HP)[:, :, :C_V]
        return jnp.concatenate([gk, gv], axis=2).reshape(C_KV_LORA, -1)
    if name == "od_w_out":
        return g.reshape(2 * C_HEADS, HP, D_MODEL)[:, :C_V].reshape(-1, D_MODEL)
    return g


def _prep_weights(p):
    w = {}
    w["meta"] = p["meta"]
    for k in ("ev_ln", "ev_a_norm", "ev_conv_b", "ev_b_ra", "ev_b_rx", "ev_lam", "od_ln", "od_g_qa", "od_g_kva"):
        w[k] = p[k].reshape(1, -1)
    w["ev_b_if"] = _pad_last(p["ev_b_if"].reshape(1, -1), HP)
    w["ev_conv_w"] = p["ev_conv_w"][0]
    w["ev_w_ra"] = p["ev_w_ra"][0]
    w["ev_w_rx"] = p["ev_w_rx"][0]
    f1 = lambda a: a.reshape(1, -1)
    w["gq_full"] = _pad_last(jnp.concatenate([f1(p["od_g_qn"]), f1(p["od_g_qr"])], axis=1), HP)
    w["gk_full"] = _pad_last(jnp.concatenate([f1(p["od_g_kn"]), f1(p["od_g_kr"])], axis=1), HP)
    w["gfq_full"] = _pad_last(f1(p["od_g_fq"]), HP)
    w["gfk_full"] = _pad_last(f1(p["od_g_fk"]), HP)
    w["bf_full"] = _pad_last(jnp.concatenate([jnp.zeros((1, 96), F32), f1(p["od_b_f"])], axis=1), HP)
    for l in (0, 1):
        w[f"mlp_ln{l}"] = p["mlp_ln"][l:l + 1]
    for n in _MM_UNITS:
        if n in p:
            w[n] = _mw_pad(n, p[n])
    return w


def _unprep_grads(g):
    o = {}
    o["meta"] = g["meta"]
    for k in ("ev_ln", "ev_a_norm", "ev_conv_b", "ev_b_ra", "ev_b_rx", "ev_lam", "od_ln", "od_g_qa", "od_g_kva"):
        o[k] = g[k].reshape(1, -1)
    o["ev_b_if"] = g["ev_b_if"][:, :2 * A_HEADS]
    o["ev_conv_w"] = g["ev_conv_w"][None]
    o["ev_w_ra"] = g["ev_w_ra"][None]
    o["ev_w_rx"] = g["ev_w_rx"][None]
    o["od_g_qn"] = g["gq_full"][:, :C_NOPE]
    o["od_g_qr"] = g["gq_full"][:, C_NOPE:C_NOPE + C_ROPE]
    o["od_g_kn"] = g["gk_full"][:, :C_NOPE]
    o["od_g_kr"] = g["gk_full"][:, C_NOPE:C_NOPE + C_ROPE]
    o["od_g_fq"] = g["gfq_full"][:, :D_HD]
    o["od_g_fk"] = g["gfk_full"][:, :D_HD]
    o["od_b_f"] = g["bf_full"][:, 96:96 + D_HEADS]
    o["mlp_ln"] = jnp.concatenate([g["mlp_ln0"], g["mlp_ln1"]], axis=0)
    return o


def _exchange(ops, name):
    n_ops = len(ops)
    flags = [s for _, s in ops]

    def body(*refs):
        copies = _comm_copies(refs[:n_ops], refs[n_ops:2 * n_ops], flags, *refs[2 * n_ops:])
        _comm_begin(copies)
        _comm_end(copies)

    any_spec = pl.BlockSpec(memory_space=pl.ANY)
    c_shape, c_sems = _comm_shapes(ops)
    return pl.pallas_call(body, name=name, out_shape=c_shape, in_specs=[any_spec] * n_ops,
                          out_specs=[any_spec] * n_ops, scratch_shapes=c_sems)(*[x for x, _ in ops])


def _adamw(parts, w, m, v, name):
    r, c = w.shape
    tr = r
    for cand in (512, 256, 128, 64, 32, 16):
        if r % cand == 0 and N_DEV * cand * c * 4 <= 4 * 1024 * 1024:
            tr = cand
            break
    c1 = 1.0 / (1.0 - ADAM_B1 ** ADAM_STEP)
    c2 = 1.0 / (1.0 - ADAM_B2 ** ADAM_STEP)

    def body(p_ref, w_ref, m_ref, v_ref, g_ref, d_ref, mo_ref, vo_ref):
        g = p_ref[0].astype(F32)
        for j in range(1, N_DEV):
            g = g + p_ref[j].astype(F32)
        m2 = ADAM_B1 * m_ref[...] + (1.0 - ADAM_B1) * g
        v2 = ADAM_B2 * v_ref[...] + (1.0 - ADAM_B2) * (g * g)
        g_ref[...] = g
        mo_ref[...] = m2
        vo_ref[...] = v2
        d_ref[...] = -ADAM_LR * ((m2 * c1) / (jnp.sqrt(v2 * c2) + ADAM_EPS) + ADAM_WD * w_ref[...])

    spec = pl.BlockSpec((tr, c), lambda i: (i, 0))
    return pl.pallas_call(
        body, name=name, grid=(r // tr,),
        in_specs=[pl.BlockSpec((N_DEV, tr, c), lambda i: (0, i, 0)), spec, spec, spec],
        out_specs=[spec] * 4, out_shape=[jax.ShapeDtypeStruct((r, c), F32)] * 4,
        compiler_params=_cparams(("parallel",)),
    )(parts, w, m, v)


def _rows_for(n, mult):
    return -(-n // (1024 * mult)) * mult


def _pack(arrs, mult, lead=()):
    nl = len(lead)
    flat = jnp.concatenate([a.reshape(lead + (-1,)) for a in arrs], axis=nl)
    rows = _rows_for(flat.shape[-1], mult)
    return jnp.pad(flat, [(0, 0)] * nl + [(0, rows * 1024 - flat.shape[-1])]).reshape(lead + (rows, 1024))


def _unpack(buf, shapes):
    lead = buf.shape[:-2]
    flat = buf.reshape(lead + (-1,))
    out, off = [], 0
    for s in shapes:
        n = 1
        for d_ in s:
            n *= d_
        out.append(flat[..., off:off + n].reshape(lead + tuple(s)))
        off += n
    return out


def _unshard(g8, ax):
    a = jnp.moveaxis(g8, 0, ax)
    return a.reshape(a.shape[:ax] + (N_DEV * a.shape[ax + 1],) + a.shape[ax + 2:])


def _shard8(full, ax):
    s = full.shape
    return jnp.moveaxis(full.reshape(s[:ax] + (N_DEV, s[ax] // N_DEV) + s[ax + 1:]), ax, 0)


_NAMES = ["meta", "ev_ln", "ev_w_in", "ev_b_if", "ev_a_norm", "ev_conv_w", "ev_conv_b", "ev_w_ra", "ev_b_ra",
          "ev_w_rx", "ev_b_rx", "ev_lam", "ev_w_out", "od_ln", "od_w_in", "od_b_f", "od_g_qa", "od_g_kva",
          "od_w_uq", "od_w_ukv", "od_g_qn", "od_g_qr", "od_g_kn", "od_g_kr", "od_g_fq", "od_g_fk", "od_w_out",
          "mlp_ln", "w_ff1", "w_ff2"]
_SHARD_AXIS = {"meta": 1, "ev_w_in": 2, "ev_conv_w": 2, "ev_w_out": 1, "od_ln": 1, "od_w_in": 2, "od_g_qa": 1,
               "od_g_kva": 1, "od_w_uq": 2, "od_w_ukv": 2, "od_w_out": 1, "w_ff1": 2, "w_ff2": 1}
_MATMUL_WEIGHTS = ("ev_w_in", "ev_w_out", "od_w_in", "od_w_uq", "od_w_ukv", "od_w_out", "w_ff1", "w_ff2")
_BIG_REPL = ("ev_w_ra", "ev_w_rx")
_COL_SHARDED = ("ev_w_in", "od_w_in", "od_w_uq", "od_w_ukv", "w_ff1_0", "w_ff1_1")
_GATHER_ON = {"mm_z0": ("od_w_in", "od_w_uq", "od_w_ukv"), "mlstm_fwd": ("ev_w_out", "w_ff1_0", "w_ff2_0"),
              "mla_attn_f": ("od_w_out", "w_ff1_1", "w_ff2_1")}
_SCATTER_ON = {"fox_attn_b": ("w_ff2_1", "w_ff1_1", "od_w_out"),
               "mlstm_bwd": ("od_w_ukv", "od_w_uq", "od_w_in", "w_ff2_0", "w_ff1_0", "ev_w_out"),
               "mm_dxn0": ("ev_w_in",)}
_REPL_ON = "mlstm_bwd"


def _unit_of(d, n):
    return d[n[:-2]][int(n[-1])] if n.startswith("w_ff") else d[n][0]


def _unit_full(n, g8):
    return jnp.transpose(g8, (1, 0, 2)).reshape(g8.shape[1], -1) if n in _COL_SHARDED else g8.reshape(-1, g8.shape[2])


def _unit_slots(n, full, r, c):
    return full.reshape(r, N_DEV, c).transpose(1, 0, 2) if n in _COL_SHARDED else full.reshape(N_DEV, r, c)


class _Plan:
    def __init__(self, shards):
        self.shards = shards
        self.parts = {}

    def slots(self, n, g):
        r, c = self.shards[n].shape
        return _unit_slots(n, _mw_unpad(n, g[n]), r, c).astype(BF16)

    def ride(self, tag, g):
        if tag in _GATHER_ON:
            return [(self.shards[n].astype(BF16), False) for n in _GATHER_ON[tag]]
        ops = [(self.slots(n, g), True) for n in _SCATTER_ON[tag]]
        if tag == _REPL_ON:
            ops += [(g[n].reshape(-1, B_BLOCK).astype(BF16), False) for n in _BIG_REPL]
        return ops

    def arrived(self, tag, got, w):
        if tag in _GATHER_ON:
            for n, g8 in zip(_GATHER_ON[tag], got):
                w[n] = _mw_pad(n, _unit_full(n, g8))
        else:
            self.parts.update(zip(_SCATTER_ON[tag] + (_BIG_REPL if tag == _REPL_ON else ()), got))


def kernel(x, positions, meta, ev_ln, ev_w_in, ev_b_if, ev_a_norm, ev_conv_w, ev_conv_b, ev_w_ra, ev_b_ra, ev_w_rx, ev_b_rx, ev_lam, ev_w_out, od_ln, od_w_in, od_b_f, od_g_qa, od_g_kva, od_w_uq, od_w_ukv, od_g_qn, od_g_qr, od_g_kn, od_g_kr, od_g_fq, od_g_fk, od_w_out, mlp_ln, w_ff1, w_ff2, loss_target, m_meta, m_ev_ln, m_ev_w_in, m_ev_b_if, m_ev_a_norm, m_ev_conv_w, m_ev_conv_b, m_ev_w_ra, m_ev_b_ra, m_ev_w_rx, m_ev_b_rx, m_ev_lam, m_ev_w_out, m_od_ln, m_od_w_in, m_od_b_f, m_od_g_qa, m_od_g_kva, m_od_w_uq, m_od_w_ukv, m_od_g_qn, m_od_g_qr, m_od_g_kn, m_od_g_kr, m_od_g_fq, m_od_g_fk, m_od_w_out, m_mlp_ln, m_w_ff1, m_w_ff2, v_meta, v_ev_ln, v_ev_w_in, v_ev_b_if, v_ev_a_norm, v_ev_conv_w, v_ev_conv_b, v_ev_w_ra, v_ev_b_ra, v_ev_w_rx, v_ev_b_rx, v_ev_lam, v_ev_w_out, v_od_ln, v_od_w_in, v_od_b_f, v_od_g_qa, v_od_g_kva, v_od_w_uq, v_od_w_ukv, v_od_g_qn, v_od_g_qr, v_od_g_kn, v_od_g_kr, v_od_g_fq, v_od_g_fk, v_od_w_out, v_mlp_ln, v_w_ff1, v_w_ff2):
    given = dict(locals())
    wts = {n: given[n] for n in _NAMES}
    mom = {n: given["m_" + n] for n in _NAMES}
    var = {n: given["v_" + n] for n in _NAMES}
    small_sh = [n for n in _NAMES if n in _SHARD_AXIS and n not in _MATMUL_WEIGHTS]
    small_rp = [n for n in _NAMES if n not in _SHARD_AXIS and n not in _BIG_REPL]
    shp = {n: wts[n].shape for n in _NAMES}
    plan = _Plan({n: _unit_of(wts, n) for n in _MM_UNITS})

    got = _exchange([(plan.shards["ev_w_in"].astype(BF16), False), (_pack([wts[n] for n in small_sh], 8), False)],
                    "gather_first")
    p = {n: wts[n] for n in _NAMES if n not in _SHARD_AXIS}
    for n, a in zip(small_sh, _unpack(got[1], [shp[n] for n in small_sh])):
        p[n] = _unshard(a, _SHARD_AXIS[n])
    p["ev_w_in"] = _unit_full("ev_w_in", got[0])

    loss, gx, g = _local_step(x[0], positions[0], loss_target[0], _prep_weights(p), plan)
    grads = _unprep_grads(g)

    parts = _exchange([(_pack([_shard8(grads[n], _SHARD_AXIS[n]) for n in small_sh], 8, (N_DEV,)), True),
                       (_pack([grads[n].reshape(shp[n]) for n in small_rp], 8), False)], "exchange_last")

    res = {}
    unit_res = {n: _adamw(plan.parts[n], *[_unit_of(d, n) for d in (wts, mom, var)], f"adamw_{n}")
                for n in _MM_UNITS}
    for n in _MATMUL_WEIGHTS:
        if n.startswith("w_ff"):
            res[n] = [jnp.stack([unit_res[n + "_0"][k], unit_res[n + "_1"][k]]) for k in range(4)]
        else:
            res[n] = [r[None] for r in unit_res[n]]
    r4 = _adamw(parts[0], *[_pack([d[n] for n in small_sh], 8) for d in (wts, mom, var)], "adamw_small_sharded")
    for n, *four in zip(small_sh, *[_unpack(r, [shp[n] for n in small_sh]) for r in r4]):
        res[n] = four
    for n in _BIG_REPL:
        r4 = _adamw(plan.parts[n], *[d[n].reshape(-1, B_BLOCK) for d in (wts, mom, var)], f"adamw_{n}")
        res[n] = [r.reshape(shp[n]) for r in r4]
    r4 = _adamw(parts[1], *[_pack([d[n] for n in small_rp], 8) for d in (wts, mom, var)], "adamw_small_repl")
    for n, *four in zip(small_rp, *[_unpack(r, [shp[n] for n in small_rp]) for r in r4]):
        res[n] = four

    outs = [res[n][kind] for kind in range(4) for n in _NAMES]
    loss = lax.psum(loss, ("x", "y", "c"))
    return (loss, gx[None], *outs)
```

```python
import functools

import jax
import jax.numpy as jnp
from jax import lax
from jax.experimental import pallas as pl
from jax.experimental.pallas import tpu as pltpu

F32 = jnp.float32
BF16 = jnp.bfloat16

D_MODEL = 1024
N_META = 16
PAD = 112
EPS = 1e-6
NEG = -1e30
A_HEADS, A_DQK, A_DV = 4, 128, 256
A_CHUNK = 384
B_BLOCKS, B_BLOCK, CONV_W, LRU_C = 8, 128, 4, 8.0
C_HEADS, C_Q_LORA, C_KV_LORA, C_NOPE, C_ROPE, C_V = 8, 384, 256, 64, 32, 64
ROPE_THETA = 10000.0
D_HEADS, D_HD = 8, 64
HP = 128
ZE = 5376
ZO = 4096
N_DEV = 8
ADAM_LR, ADAM_B1, ADAM_B2, ADAM_EPS, ADAM_WD, ADAM_STEP = 0.001, 0.9, 0.999, 1e-08, 0.01, 10
VMEM_LIMIT = 56 * 1024 * 1024
SCAN_BLOCK = 128


def _pick(n, prefs):
    for p in prefs:
        if n % p == 0:
            return p
    return n


def _cparams(dims):
    return pltpu.CompilerParams(dimension_semantics=dims, vmem_limit_bytes=VMEM_LIMIT)


def _full_spec(shape):
    nd = len(shape)
    return pl.BlockSpec(shape, lambda *_: (0,) * nd)


def _me_and_peers():
    mx, my, mc = lax.axis_index("x"), lax.axis_index("y"), lax.axis_index("c")
    peers = []
    for k in range(1, N_DEV):
        px, py, pc = mx ^ ((k >> 2) & 1), my ^ ((k >> 1) & 1), mc ^ (k & 1)
        peers.append(((px, py, pc), 4 * px + 2 * py + pc))
    return 4 * mx + 2 * my + mc, peers


def _comm_copies(x_refs, o_refs, flags, send_sems, recv_sems, local_sems):
    n_peer = N_DEV - 1
    mx, my, mc = lax.axis_index("x"), lax.axis_index("y"), lax.axis_index("c")
    me = 4 * mx + 2 * my + mc
    sibling = (mx, my, 1 - mc)
    chips = [(mx ^ a, my ^ b) for a, b in ((0, 1), (1, 0), (1, 1))]
    _, peers = _me_and_peers()
    out = dict(first=[], landed=[], forward=[], remote=[], local=[])

    def copy(a, k, src, dst, to):
        return pltpu.make_async_remote_copy(src_ref=src, dst_ref=dst, send_sem=send_sems.at[a * n_peer + k],
                                            recv_sem=recv_sems.at[a * n_peer + k], device_id=to,
                                            device_id_type=pl.DeviceIdType.MESH)

    for a, scatter in enumerate(flags):
        x_ref, o_ref = x_refs[a], o_refs[a]
        out["local"].append(pltpu.make_async_copy(x_ref.at[me] if scatter else x_ref, o_ref.at[me], local_sems.at[a]))
        if scatter:
            out["first"] += [copy(a, k, x_ref.at[pid], o_ref.at[me], peer) for k, (peer, pid) in enumerate(peers)]
        else:
            out["first"].append(copy(a, 0, x_ref, o_ref.at[me], sibling))
            for t, (px, py) in enumerate(chips):
                far = copy(a, 1 + t, x_ref, o_ref.at[me], (px, py, mc))
                slot = o_ref.at[4 * px + 2 * py + mc]
                out["first"].append(far)
                out["landed"].append(far)
                out["forward"].append(copy(a, 4 + t, slot, slot, sibling))
    out["remote"] = out["first"] + out["forward"]
    return out


def _comm_begin(c):
    for cp in c["local"] + c["first"]:
        cp.start()


def _comm_end(c):
    for cp in c["landed"]:
        cp.wait_recv()
    for cp in c["forward"]:
        cp.start()
    for cp in c["remote"]:
        cp.wait_send()
    for cp in c["remote"]:
        if all(cp is not d for d in c["landed"]):
            cp.wait_recv()
    for cp in c["local"]:
        cp.wait()


def _comm_shapes(comm):
    n = len(comm)
    out_shape = [jax.ShapeDtypeStruct((N_DEV,) + x.shape[-2:], x.dtype) for x, _ in comm]
    sems = [pltpu.SemaphoreType.DMA((n * (N_DEV - 1),)), pltpu.SemaphoreType.DMA((n * (N_DEV - 1),)),
            pltpu.SemaphoreType.DMA((n,))]
    return out_shape, sems


def _pcall(body, name, grid, in_specs, out_specs, out_shape, scratch_shapes, dims, ins, comm=None):
    if not comm:
        return pl.pallas_call(body, name=name, grid=grid, in_specs=in_specs, out_specs=out_specs,
                              out_shape=out_shape, scratch_shapes=scratch_shapes,
                              compiler_params=_cparams(dims))(*ins)
    n_in, n_out, n = len(in_specs), len(out_specs), len(comm)
    flags = [s for _, s in comm]
    c_shape, c_sems = _comm_shapes(comm)

    def riding(*refs):
        cx = refs[n_in:n_in + n]
        co = refs[n_in + n + n_out:n_in + 2 * n + n_out]
        rest = refs[n_in + 2 * n + n_out:]
        sems = rest[len(rest) - 3:]
        ids = [pl.program_id(a) for a in range(len(grid))]
        first = functools.reduce(jnp.logical_and, [i == 0 for i in ids])
        last = functools.reduce(jnp.logical_and, [i == g - 1 for i, g in zip(ids, grid)])

        @pl.when(first)
        def _():
            _comm_begin(_comm_copies(cx, co, flags, *sems))

        body(*refs[:n_in], *refs[n_in + n:n_in + n + n_out], *rest[:len(rest) - 3])

        @pl.when(last)
        def _():
            _comm_end(_comm_copies(cx, co, flags, *sems))

    any_spec = pl.BlockSpec(memory_space=pl.ANY)
    res = pl.pallas_call(
        riding, name=name, grid=grid, in_specs=list(in_specs) + [any_spec] * n,
        out_specs=list(out_specs) + [any_spec] * n, out_shape=list(out_shape) + c_shape,
        scratch_shapes=list(scratch_shapes) + c_sems,
        compiler_params=_cparams(("arbitrary",) * len(grid)))(*ins, *[x for x, _ in comm])
    return list(res[:n_out]), list(res[n_out:])


def _mm(a, b, mode, name, out_dtype=None, add=None, relu2=False, relu2_of=None, comm=None):
    if out_dtype is None:
        out_dtype = BF16 if (mode == "tn" or relu2_of is not None) else F32
    if relu2_of is not None:
        add = relu2_of
    if mode == "nn":
        (m, k), n = a.shape, b.shape[1]
    elif mode == "nt":
        (m, k), n = a.shape, b.shape[0]
    else:
        (k, m), n = a.shape, b.shape[1]
    tm = _pick(m, (1408, 1024, 768, 512, 384, 256, 128))
    tn = _pick(n, (1024, 768, 512, 384, 256, 128))
    tk = _pick(k, (1408, 1024, 768, 512, 384, 256, 128))
    nk = k // tk
    if mode == "nn":
        a_spec = pl.BlockSpec((tm, tk), lambda i, j, q: (i, q))
        b_spec = pl.BlockSpec((tk, tn), lambda i, j, q: (q, j))
        dn = (((1,), (0,)), ((), ()))
    elif mode == "nt":
        a_spec = pl.BlockSpec((tm, tk), lambda i, j, q: (i, q))
        b_spec = pl.BlockSpec((tn, tk), lambda i, j, q: (j, q))
        dn = (((1,), (1,)), ((), ()))
    else:
        a_spec = pl.BlockSpec((tk, tm), lambda i, j, q: (q, i))
        b_spec = pl.BlockSpec((tk, tn), lambda i, j, q: (q, j))
        dn = (((0,), (0,)), ((), ()))
    o_spec = pl.BlockSpec((tm, tn), lambda i, j, q: (i, j))
    has_add = add is not None

    def body(*refs):
        a_ref, b_ref = refs[:2]
        add_ref = refs[2] if has_add else None
        o_refs, acc = refs[2 + has_add:-1], refs[-1]
        q = pl.program_id(2)

        @pl.when(q == 0)
        def _():
            acc[...] = jnp.zeros_like(acc)

        acc[...] += lax.dot_general(a_ref[...].astype(BF16), b_ref[...].astype(BF16), dn,
                                    preferred_element_type=F32)

        @pl.when(q == nk - 1)
        def _():
            r = acc[...]
            if relu2_of is not None:
                r = r * (2.0 * jnp.maximum(add_ref[...], 0.0))
            elif has_add:
                r = r + add_ref[...]
            o_refs[0][...] = r.astype(o_refs[0].dtype)
            if relu2:
                pos = jnp.maximum(r, 0.0)
                o_refs[1][...] = (pos * pos).astype(o_refs[1].dtype)

    ins = [a, b] + ([add] if has_add else [])
    in_specs = [a_spec, b_spec] + ([o_spec] if has_add else [])
    out_shape = [jax.ShapeDtypeStruct((m, n), out_dtype)] + ([jax.ShapeDtypeStruct((m, n), BF16)] if relu2 else [])
    res = _pcall(body, name, (m // tm, n // tn, nk), in_specs, [o_spec] * len(out_shape), out_shape,
                 [pltpu.VMEM((tm, tn), F32)], ("parallel", "parallel", "arbitrary"), ins, comm)
    outs = res[0] if comm else res
    outs = tuple(outs) if relu2 else outs[0]
    return (outs, res[1]) if comm else outs


def _row_specs(rows, tm):
    return [pl.BlockSpec((tm, w), functools.partial(lambda cb, i: (i, cb), cb)) for (_, w, cb) in rows]


def _row_fwd(name, f, rows, params, outs, tm):
    tp = rows[0][0].shape[0]
    nr, npar = len(rows), len(params)

    def body(*refs):
        i = pl.program_id(0)
        rv = [r[...] for r in refs[:nr]]
        pv = [r[...] for r in refs[nr:nr + npar]]
        res = f(i, rv, pv)
        for o_ref, r in zip(refs[nr + npar:], res):
            o_ref[...] = r.astype(o_ref.dtype)

    res = pl.pallas_call(
        body, name=name, grid=(tp // tm,),
        in_specs=_row_specs(rows, tm) + [_full_spec(p.shape) for p in params],
        out_specs=[pl.BlockSpec((tm, w), lambda i: (i, 0)) for (w, _) in outs],
        out_shape=[jax.ShapeDtypeStruct((tp, w), dt) for (w, dt) in outs],
        compiler_params=_cparams(("parallel",)),
    )(*[r[0] for r in rows], *params)
    return list(res)


def _row_bwd(name, f, rows, params, douts, tm, diff, add=None, out_dtypes=None, comm=None):
    tp = rows[0][0].shape[0]
    nr, npar, nd = len(rows), len(params), len(douts)
    didx = [k for k in range(nr) if diff[k]]
    has_add = add is not None

    def body(*refs):
        i = pl.program_id(0)
        rv = [r[...] for r in refs[:nr]]
        pv = [r[...] for r in refs[nr:nr + npar]]
        dv = [r[...] for r in refs[nr + npar:nr + npar + nd]]
        pos = nr + npar + nd
        add_ref = refs[pos] if has_add else None
        pos += 1 if has_add else 0
        dr_refs = refs[pos:pos + len(didx)]
        dp_refs = refs[pos + len(didx):]

        def g(drv, pvs):
            full = list(rv)
            for k, val in zip(didx, drv):
                full[k] = val
            return tuple(f(i, full, list(pvs)))

        _, vjp = jax.vjp(g, [rv[k] for k in didx], pv)
        d_r, d_p = vjp(tuple(dv))
        for n_, (ref, val) in enumerate(zip(dr_refs, d_r)):
            if has_add and n_ == 0:
                val = val + add_ref[...]
            ref[...] = val.astype(ref.dtype)

        @pl.when(i == 0)
        def _():
            for ref in dp_refs:
                ref[...] = jnp.zeros_like(ref)

        for ref, val in zip(dp_refs, d_p):
            ref[...] += val

    in_specs = (_row_specs(rows, tm) + [_full_spec(p.shape) for p in params] + _row_specs(douts, tm))
    ins = [r[0] for r in rows] + list(params) + [d[0] for d in douts]
    if has_add:
        in_specs.append(pl.BlockSpec((tm, rows[didx[0]][1]), lambda i: (i, 0)))
        ins.append(add)
    out_specs = ([pl.BlockSpec((tm, rows[k][1]), lambda i: (i, 0)) for k in didx]
                 + [_full_spec(p.shape) for p in params])
    out_dtypes = out_dtypes or [F32] * len(didx)
    out_shape = ([jax.ShapeDtypeStruct((tp, rows[k][1]), dt) for k, dt in zip(didx, out_dtypes)]
                 + [jax.ShapeDtypeStruct(p.shape, F32) for p in params])
    res = _pcall(body, name, (tp // tm,), in_specs, out_specs, out_shape, [], ("arbitrary",), ins, comm)
    outs = res[0] if comm else res
    outs = (list(outs[:len(didx)]), list(outs[len(didx):]))
    return (outs, res[1]) if comm else outs


def _concat_cols(pieces, dtype, name):
    arrs = [p for p in pieces if not isinstance(p, int)]
    widths = [p if isinstance(p, int) else p.shape[1] for p in pieces]
    tp = arrs[0].shape[0]
    tm = _pick(tp, (384, 256, 128))

    def body(*refs):
        o_ref, k, off = refs[-1], 0, 0
        for p, w in zip(pieces, widths):
            if isinstance(p, int):
                o_ref[:, off:off + w] = jnp.zeros((tm, w), o_ref.dtype)
            else:
                o_ref[:, off:off + w] = refs[k][...].astype(o_ref.dtype)
                k += 1
            off += w

    return pl.pallas_call(
        body, name=name, grid=(tp // tm,),
        in_specs=[pl.BlockSpec((tm, a.shape[1]), lambda i: (i, 0)) for a in arrs],
        out_specs=pl.BlockSpec((tm, sum(widths)), lambda i: (i, 0)),
        out_shape=jax.ShapeDtypeStruct((tp, sum(widths)), dtype),
        compiler_params=_cparams(("parallel",)),
    )(*arrs)


def _rowmask(i, tm):
    return (i * tm + lax.broadcasted_iota(jnp.int32, (tm, 1), 0)) >= PAD


def _lane(n=HP):
    return lax.broadcasted_iota(jnp.int32, (1, n), 1)


def _softplus(x):
    return jnp.maximum(x, 0.0) + jnp.log(1.0 + jnp.exp(-jnp.abs(x)))


def _log_sigmoid(x):
    return -_softplus(-x)


def _sigmoid(x):
    return 1.0 / (1.0 + jnp.exp(-x))


@functools.partial(jax.custom_vjp, nondiff_argnums=(1,))
def _lroll(x, s):
    return pltpu.roll(x, s % HP, 1)


def _lroll_fwd(x, s):
    return _lroll(x, s), None


def _lroll_bwd(s, _, g):
    return (pltpu.roll(g, (-s) % HP, 1),)


_lroll.defvjp(_lroll_fwd, _lroll_bwd)


def _f_norm(i, rv, pv):
    (h,), (g,) = rv, pv
    return [h * lax.rsqrt(jnp.mean(h * h, axis=-1, keepdims=True) + EPS) * g]


def _f_gate0(i, rv, pv):
    (misc,), (b,) = rv, pv
    tm = misc.shape[0]
    x = misc + b
    lane, ok = _lane(), _rowmask(i, tm)
    li = jnp.where(ok, x, NEG)
    lf = jnp.where(ok, _log_sigmoid(x), 0.0)
    return [jnp.where(lane < A_HEADS, li, jnp.where(lane < 2 * A_HEADS, lf, 0.0))]


def _f_aout(i, rv, pv):
    (ha, o), (g,) = rv, pv
    outs = []
    for h in range(A_HEADS):
        x = ha[:, h * A_DV:(h + 1) * A_DV]
        outs.append(x * lax.rsqrt(jnp.mean(x * x, axis=-1, keepdims=True) + EPS) * g)
    return [jnp.concatenate(outs, axis=1) * _sigmoid(o)]


def _f_gates(i, rv, pv):
    (xc,), (w_ra, b_ra, w_rx, b_rx, lam) = rv, pv
    tm = xc.shape[0]
    ra, rx = [], []
    for g in range(B_BLOCKS):
        xg = xc[:, g * B_BLOCK:(g + 1) * B_BLOCK].astype(BF16)
        ra.append(jnp.dot(xg, w_ra[g].astype(BF16), preferred_element_type=F32))
        rx.append(jnp.dot(xg, w_rx[g].astype(BF16), preferred_element_type=F32))
    r = _sigmoid(jnp.concatenate(ra, axis=1) + b_ra)
    ig = _sigmoid(jnp.concatenate(rx, axis=1) + b_rx)
    log_a = -LRU_C * r * _softplus(-lam)
    a = jnp.exp(log_a)
    u = jnp.sqrt(1.0 - jnp.exp(2.0 * log_a)) * (ig * xc)
    return [a, jnp.where(_rowmask(i, tm), u, 0.0)]


def _f_bout(i, rv, pv):
    hs, gb = rv
    gelu = 0.5 * gb * (1.0 + jnp.tanh(0.7978845608028654 * (gb + 0.044715 * gb * gb * gb)))
    return [hs * gelu]


def _f_cnorm(i, rv, pv):
    (cq, ckv), (gq, gkv) = rv, pv
    return [cq * lax.rsqrt(jnp.mean(cq * cq, axis=-1, keepdims=True) + EPS) * gq,
            ckv * lax.rsqrt(jnp.mean(ckv * ckv, axis=-1, keepdims=True) + EPS) * gkv]


def _rope128(x, cos, sin):
    lane = _lane()
    rot = jnp.where((lane >= C_NOPE) & (lane < C_NOPE + C_ROPE // 2), -_lroll(x, -(C_ROPE // 2)),
                    jnp.where((lane >= C_NOPE + C_ROPE // 2) & (lane < C_NOPE + C_ROPE), _lroll(x, C_ROPE // 2), 0.0))
    return x * cos + rot * sin


def _f_mlaprep(i, rv, pv):
    (q_, kk_, misc, cos, sin), (gq, gk) = rv, pv
    lane = _lane()
    m_n = lane < C_NOPE
    m_r = (lane >= C_NOPE) & (lane < C_NOPE + C_ROPE)

    def norm2(x, g):
        x2 = x * x
        sn = jnp.sum(jnp.where(m_n, x2, 0.0), axis=-1, keepdims=True) * (1.0 / C_NOPE)
        sr = jnp.sum(jnp.where(m_r, x2, 0.0), axis=-1, keepdims=True) * (1.0 / C_ROPE)
        scale = jnp.where(m_n, lax.rsqrt(sn + EPS), jnp.where(m_r, lax.rsqrt(sr + EPS), 0.0))
        return x * scale * g

    kr = _rope128(norm2(jnp.where(m_r, misc, 0.0), gk), cos, sin)
    qs, ks = [], []
    for h in range(C_HEADS):
        qs.append(_rope128(norm2(q_[:, h * HP:(h + 1) * HP], gq), cos, sin))
        ks.append(norm2(jnp.where(m_n, kk_[:, h * HP:(h + 1) * HP], 0.0), gk) + kr)
    return [jnp.concatenate(qs, axis=1), jnp.concatenate(ks, axis=1)]


def _f_foxprep(i, rv, pv):
    (fq, fk, misc), (gq, gk, bf) = rv, pv
    tm = fq.shape[0]
    lane = _lane()

    def hnorm(x, g):
        outs = []
        for h in range(D_HEADS):
            xh = x[:, h * HP:(h + 1) * HP]
            ss = jnp.sum(xh * xh, axis=-1, keepdims=True) * (1.0 / D_HD)
            outs.append(xh * lax.rsqrt(ss + EPS) * g)
        return jnp.concatenate(outs, axis=1)

    lf = jnp.where(_rowmask(i, tm) & (lane >= 96) & (lane < 96 + D_HEADS), _log_sigmoid(misc + bf), 0.0)
    return [hnorm(fq, gq), hnorm(fk, gk), lf]


def _mlstm_chunk(c, n, m, q, k, v, li, lf):
    ln = q.shape[0]
    r = lax.broadcasted_iota(jnp.int32, (ln, ln), 0)
    cc = lax.broadcasted_iota(jnp.int32, (ln, ln), 1)
    causal = cc <= r
    eye = cc == r
    li_row = jnp.sum(jnp.where(eye, li, 0.0), axis=0, keepdims=True)
    b_col = jnp.sum(jnp.where(causal, jnp.sum(jnp.where(eye, lf, 0.0), axis=0, keepdims=True), 0.0),
                    axis=1, keepdims=True)
    b_row = jnp.sum(jnp.where(r <= cc, lf, 0.0), axis=0, keepdims=True)
    k = k * (A_DQK ** -0.5)
    qb, kb, vb = q.astype(BF16), k.astype(BF16), v.astype(BF16)
    dmat = jnp.where(causal, b_col - b_row + li_row, NEG)
    inter = b_col + m
    m_row = jnp.maximum(inter, jnp.max(dmat, axis=1, keepdims=True))
    w_intra = jnp.exp(dmat - m_row)
    w_inter = jnp.exp(inter - m_row)
    s = lax.dot_general(qb, kb, (((1,), (1,)), ((), ())), preferred_element_type=F32) * w_intra
    num = (w_inter * jnp.dot(qb, c.astype(BF16), preferred_element_type=F32)
           + jnp.dot(s.astype(BF16), vb, preferred_element_type=F32))
    den = w_inter * jnp.sum(q * n, axis=1, keepdims=True) + jnp.sum(s, axis=1, keepdims=True)
    h = num / jnp.maximum(jnp.abs(den), jnp.exp(-m_row))
    g = jnp.sum(lf, axis=0, keepdims=True)
    dk = g - b_col + li
    m_new = jnp.maximum(g + m, jnp.max(dk, axis=0, keepdims=True))
    wk = jnp.exp(dk - m_new)
    sc = jnp.exp(g + m - m_new)
    kw = wk * k
    c_new = sc * c + lax.dot_general(kw.astype(BF16), vb, (((0,), (0,)), ((), ())), preferred_element_type=F32)
    n_new = sc * n + jnp.sum(kw, axis=0, keepdims=True)
    return c_new, n_new, m_new, h


A_QKV = 2 * A_HEADS * A_DQK + A_HEADS * A_DV


def _mlstm_chunk_all(cs, ns, ms, zqkv, gates):
    lane = _lane()
    c2, n2, m2, hs = [], [], [], []
    for h in range(A_HEADS):
        q = zqkv[:, h * A_DQK:(h + 1) * A_DQK]
        k = zqkv[:, (A_HEADS + h) * A_DQK:(A_HEADS + h + 1) * A_DQK]
        v = zqkv[:, 2 * A_HEADS * A_DQK + h * A_DV:2 * A_HEADS * A_DQK + (h + 1) * A_DV]
        li = jnp.sum(jnp.where(lane == h, gates, 0.0), axis=1, keepdims=True)
        lf = jnp.sum(jnp.where(lane == A_HEADS + h, gates, 0.0), axis=1, keepdims=True)
        c, n, m, hh = _mlstm_chunk(cs[h], ns[h], ms[h], q, k, v, li, lf)
        c2.append(c)
        n2.append(n)
        m2.append(m)
        hs.append(hh)
    return c2, n2, m2, jnp.concatenate(hs, axis=1)


def _mlstm_state_specs(nc, index):
    return ([pl.BlockSpec((1, A_HEADS, A_DQK, A_DV), lambda j: (index(j), 0, 0, 0)),
             pl.BlockSpec((1, A_HEADS, 1, A_DQK), lambda j: (index(j), 0, 0, 0)),
             pl.BlockSpec((1, A_HEADS, 1, 1), lambda j: (index(j), 0, 0, 0))],
            [jax.ShapeDtypeStruct((nc, A_HEADS, A_DQK, A_DV), F32),
             jax.ShapeDtypeStruct((nc, A_HEADS, 1, A_DQK), F32),
             jax.ShapeDtypeStruct((nc, A_HEADS, 1, 1), F32)],
            [pltpu.VMEM((A_HEADS, A_DQK, A_DV), F32), pltpu.VMEM((A_HEADS, 1, A_DQK), F32),
             pltpu.VMEM((A_HEADS, 1, 1), F32)])


def _mlstm_fwd(z, gates, comm=None):
    tp = z.shape[0]
    nc = tp // A_CHUNK
    ln = A_CHUNK
    heads = range(A_HEADS)

    def body(z_ref, g_ref, h_ref, cs_ref, ns_ref, ms_ref, c_s, n_s, m_s):
        @pl.when(pl.program_id(0) == 0)
        def _():
            c_s[...] = jnp.zeros_like(c_s)
            n_s[...] = jnp.zeros_like(n_s)
            m_s[...] = jnp.zeros_like(m_s)

        cs_ref[0] = c_s[...]
        ns_ref[0] = n_s[...]
        ms_ref[0] = m_s[...]
        c2, n2, m2, h = _mlstm_chunk_all([c_s[i] for i in heads], [n_s[i] for i in heads], [m_s[i] for i in heads],
                                         z_ref[...], g_ref[...])
        for i in heads:
            c_s[i] = c2[i]
            n_s[i] = n2[i]
            m_s[i] = m2[i]
        h_ref[...] = h

    st_specs, st_shapes, st_scratch = _mlstm_state_specs(nc, lambda j: j)
    return _pcall(
        body, "mlstm_fwd", (nc,),
        [pl.BlockSpec((ln, A_QKV), lambda j: (j, 0)), pl.BlockSpec((ln, HP), lambda j: (j, 0))],
        [pl.BlockSpec((ln, A_HEADS * A_DV), lambda j: (j, 0))] + st_specs,
        [jax.ShapeDtypeStruct((tp, A_HEADS * A_DV), F32)] + st_shapes,
        st_scratch, ("arbitrary",), (z, gates), comm)


def _mlstm_bwd(z, gates, cs, ns, ms, dh, comm=None):
    tp = z.shape[0]
    nc = tp // A_CHUNK
    ln = A_CHUNK
    heads = range(A_HEADS)

    def body(z_ref, g_ref, cs_ref, ns_ref, ms_ref, dh_ref, dz_ref, dg_ref, dc_s, dn_s, dm_s):
        @pl.when(pl.program_id(0) == 0)
        def _():
            dc_s[...] = jnp.zeros_like(dc_s)
            dn_s[...] = jnp.zeros_like(dn_s)
            dm_s[...] = jnp.zeros_like(dm_s)

        _, vjp = jax.vjp(_mlstm_chunk_all, [cs_ref[0, i] for i in heads], [ns_ref[0, i] for i in heads],
                         [ms_ref[0, i] for i in heads], z_ref[...], g_ref[...])
        dc, dn, dm, dz, dg = vjp(([dc_s[i] for i in heads], [dn_s[i] for i in heads], [dm_s[i] for i in heads],
                                  dh_ref[...]))
        for i in heads:
            dc_s[i] = dc[i]
            dn_s[i] = dn[i]
            dm_s[i] = dm[i]
        dz_ref[...] = dz.astype(BF16)
        dg_ref[...] = dg

    def rj(j):
        return nc - 1 - j

    st_specs, _, st_scratch = _mlstm_state_specs(nc, rj)
    return _pcall(
        body, "mlstm_bwd", (nc,),
        [pl.BlockSpec((ln, A_QKV), lambda j: (rj(j), 0)), pl.BlockSpec((ln, HP), lambda j: (rj(j), 0))] + st_specs
        + [pl.BlockSpec((ln, A_HEADS * A_DV), lambda j: (rj(j), 0))],
        [pl.BlockSpec((ln, A_QKV), lambda j: (rj(j), 0)), pl.BlockSpec((ln, HP), lambda j: (rj(j), 0))],
        [jax.ShapeDtypeStruct((tp, A_QKV), BF16), jax.ShapeDtypeStruct((tp, HP), F32)],
        st_scratch, ("arbitrary",), (z, gates, cs, ns, ms, dh), comm)


def _shift_down(x, s, row):
    return x if s == 0 else jnp.where(row >= s, pltpu.roll(x, s, 0), 0.0)


def _shift_up(x, s, row):
    n = x.shape[0]
    return x if s == 0 else jnp.where(row < n - s, pltpu.roll(x, n - s, 0), 0.0)


def _conv_fwd(z, xcb, w, b):
    tp, c = z.shape[0], w.shape[1]
    ct = 256

    def body(x_ref, w_ref, b_ref, o_ref):
        x = x_ref[...]
        row = lax.broadcasted_iota(jnp.int32, (tp, 1), 0)
        acc = jnp.zeros_like(x) + b_ref[...]
        for k in range(CONV_W):
            acc = acc + w_ref[k:k + 1, :] * _shift_down(x, CONV_W - 1 - k, row)
        o_ref[...] = acc

    return pl.pallas_call(
        body, name="conv_fwd", grid=(c // ct,),
        in_specs=[pl.BlockSpec((tp, ct), lambda j: (0, xcb + j)), pl.BlockSpec((CONV_W, ct), lambda j: (0, j)),
                  pl.BlockSpec((1, ct), lambda j: (0, j))],
        out_specs=pl.BlockSpec((tp, ct), lambda j: (0, j)),
        out_shape=jax.ShapeDtypeStruct((tp, c), F32),
        compiler_params=_cparams(("parallel",)),
    )(z, w, b)


def _conv_bwd(z, xcb, w, dxc):
    tp, c = z.shape[0], w.shape[1]
    ct = 256

    def body(x_ref, w_ref, d_ref, dx_ref, dw_ref, db_ref):
        x, d = x_ref[...], d_ref[...]
        row = lax.broadcasted_iota(jnp.int32, (tp, 1), 0)
        acc = jnp.zeros_like(x)
        for k in range(CONV_W):
            s = CONV_W - 1 - k
            acc = acc + w_ref[k:k + 1, :] * _shift_up(d, s, row)
            dw_ref[k:k + 1, :] = jnp.sum(d * _shift_down(x, s, row), axis=0, keepdims=True)
        dx_ref[...] = acc.astype(BF16)
        db_ref[...] = jnp.sum(d, axis=0, keepdims=True)

    return pl.pallas_call(
        body, name="conv_bwd", grid=(c // ct,),
        in_specs=[pl.BlockSpec((tp, ct), lambda j: (0, xcb + j)), pl.BlockSpec((CONV_W, ct), lambda j: (0, j)),
                  pl.BlockSpec((tp, ct), lambda j: (0, j))],
        out_specs=[pl.BlockSpec((tp, ct), lambda j: (0, j)), pl.BlockSpec((CONV_W, ct), lambda j: (0, j)),
                   pl.BlockSpec((1, ct), lambda j: (0, j))],
        out_shape=[jax.ShapeDtypeStruct((tp, c), BF16), jax.ShapeDtypeStruct((CONV_W, c), F32),
                   jax.ShapeDtypeStruct((1, c), F32)],
        compiler_params=_cparams(("parallel",)),
    )(z, w, dxc)


def _scan_fwd(a, u, name):
    tp, c = a.shape
    ct = _pick(c, (256, 128))
    lb = SCAN_BLOCK
    nb = tp // lb

    def body(a_ref, u_ref, h_ref, hp_ref):
        row = lax.broadcasted_iota(jnp.int32, (lb, 1), 0)

        def blk(j, carry):
            r0 = pl.multiple_of(j * lb, lb)
            aa, uu = a_ref[pl.ds(r0, lb), :], u_ref[pl.ds(r0, lb), :]
            s = 1
            while s < lb:
                mk = row >= s
                uu = jnp.where(mk, aa * pltpu.roll(uu, s, 0) + uu, uu)
                aa = jnp.where(mk, aa * pltpu.roll(aa, s, 0), aa)
                s *= 2
            hh = uu + aa * carry
            h_ref[pl.ds(r0, lb), :] = hh
            hp_ref[pl.ds(r0, lb), :] = jnp.where(row >= 1, pltpu.roll(hh, 1, 0), carry)
            return hh[lb - 1:lb, :]

        lax.fori_loop(0, nb, blk, jnp.zeros((1, ct), F32))

    spec = pl.BlockSpec((tp, ct), lambda j: (0, j))
    return pl.pallas_call(
        body, name=name, grid=(c // ct,), in_specs=[spec, spec], out_specs=[spec, spec],
        out_shape=[jax.ShapeDtypeStruct((tp, c), F32)] * 2,
        compiler_params=_cparams(("parallel",)),
    )(a, u)


def _scan_bwd(a, hprev, dh, name):
    tp, c = a.shape
    ct = _pick(c, (256, 128))
    lb = SCAN_BLOCK
    nb = tp // lb

    def body(a_ref, hp_ref, dh_ref, du_ref, da_ref):
        row = lax.broadcasted_iota(jnp.int32, (lb, 1), 0)

        def blk(jj, carry):
            g_next, a_next = carry
            r0 = pl.multiple_of((nb - 1 - jj) * lb, lb)
            a_blk = a_ref[pl.ds(r0, lb), :]
            aa = jnp.where(row < lb - 1, pltpu.roll(a_blk, lb - 1, 0), a_next)
            gg = dh_ref[pl.ds(r0, lb), :]
            s = 1
            while s < lb:
                mk = row < lb - s
                gg = jnp.where(mk, aa * pltpu.roll(gg, lb - s, 0) + gg, gg)
                aa = jnp.where(mk, aa * pltpu.roll(aa, lb - s, 0), aa)
                s *= 2
            gg = gg + aa * g_next
            du_ref[pl.ds(r0, lb), :] = gg
            da_ref[pl.ds(r0, lb), :] = gg * hp_ref[pl.ds(r0, lb), :]
            return gg[0:1, :], a_blk[0:1, :]

        lax.fori_loop(0, nb, blk, (jnp.zeros((1, ct), F32), jnp.zeros((1, ct), F32)))

    spec = pl.BlockSpec((tp, ct), lambda j: (0, j))
    return pl.pallas_call(
        body, name=name, grid=(c // ct,), in_specs=[spec, spec, spec], out_specs=[spec, spec],
        out_shape=[jax.ShapeDtypeStruct((tp, c), F32)] * 2,
        compiler_params=_cparams(("parallel",)),
    )(a, hprev, dh)


ATT_HEADS_PER_STEP = 2
LOG2E = 1.4426950408889634
LN2 = 0.6931471805599453


FOX_LANE0 = 96


def _head_lane(f, head):
    return jnp.sum(jnp.where(_lane() == FOX_LANE0 + head, f, 0.0), axis=1, keepdims=True)


def _diag_valid(i, tq):
    r = lax.broadcasted_iota(jnp.int32, (tq, tq), 0)
    c = lax.broadcasted_iota(jnp.int32, (tq, tq), 1)
    return (c <= r) & ((i * tq + c >= PAD) | (c == r))


def _key_terms(tp, tq, bk=None):
    pad_neg = jnp.where(jnp.arange(tp) < PAD, NEG, 0.0).astype(F32).reshape(1, tp // tq, 1, tq)
    if bk is None:
        return pad_neg, jnp.zeros_like(pad_neg)
    kd = -bk.reshape(bk.shape[0], tp // tq, 1, tq)
    return kd + pad_neg, kd


def _attn_fwd(q, k, v, vcb, scale, bq, kterms, name, comm=None):
    tp = q.shape[0]
    nh = q.shape[1] // HP
    tq = _pick(tp, (384, 256, 128))
    has_bq = bq is not None
    c1 = scale * LOG2E
    per_head = kterms[0].shape[0] > 1
    hg = ATT_HEADS_PER_STEP
    lanes = [slice(hh * HP, (hh + 1) * HP) for hh in range(hg)]

    def body(*refs):
        if has_bq:
            q_ref, k_ref, v_ref, kb_ref, kd_ref, bq_ref, o_ref, lse_ref = refs
            rb = [_head_lane(bq_ref[...], hg * pl.program_id(0) + hh) * LOG2E for hh in range(hg)]
        else:
            q_ref, k_ref, v_ref, kb_ref, kd_ref, o_ref, lse_ref = refs
        i = pl.program_id(1)
        qb = [q_ref[:, ln].astype(BF16) for ln in lanes]

        def tile(j, carry, diag):
            r0 = pl.multiple_of(j * tq, tq)
            out = []
            for hh in range(hg):
                m, l, acc = carry[hh]
                kb = k_ref[pl.ds(r0, tq), lanes[hh]].astype(BF16)
                vb = v_ref[pl.ds(r0, tq), lanes[hh]].astype(BF16)
                kt = (kd_ref if diag else kb_ref)[hh if per_head else 0, j] * LOG2E
                x = lax.dot_general(qb[hh], kb, (((1,), (1,)), ((), ())), preferred_element_type=F32) * c1 + kt
                if has_bq:
                    x = x + rb[hh]
                if diag:
                    x = jnp.where(_diag_valid(i, tq), x, NEG)
                m2 = jnp.maximum(m, jnp.max(x, axis=1, keepdims=True))
                alpha = jnp.exp2(m - m2)
                p = jnp.exp2(x - m2)
                l2 = alpha * l + jnp.sum(p, axis=1, keepdims=True)
                acc2 = alpha * acc + jnp.dot(p.astype(BF16), vb, preferred_element_type=F32)
                out.append((m2, l2, acc2))
            return tuple(out)

        init = tuple((jnp.full((tq, 1), NEG, F32), jnp.zeros((tq, 1), F32), jnp.zeros((tq, HP), F32))
                     for _ in range(hg))
        res = tile(i, lax.fori_loop(0, i, lambda j, c: tile(j, c, False), init), True)
        for hh, (m, l, acc) in enumerate(res):
            o_ref[:, lanes[hh]] = acc / l
            lse_ref[hh] = m * LN2 + jnp.log(l)

    kt_spec = pl.BlockSpec((hg if per_head else 1, tp // tq, 1, tq),
                           (lambda h, i: (h, 0, 0, 0)) if per_head else (lambda h, i: (0, 0, 0, 0)))
    in_specs = [pl.BlockSpec((tq, hg * HP), lambda h, i: (i, h)), pl.BlockSpec((tp, hg * HP), lambda h, i: (0, h)),
                pl.BlockSpec((tp, hg * HP), lambda h, i: (0, vcb // hg + h)), kt_spec, kt_spec]
    ins = [q, k, v, *kterms]
    if has_bq:
        in_specs += [pl.BlockSpec((tq, HP), lambda h, i: (i, 0))]
        ins += [bq]
    return _pcall(
        body, name, (nh // hg, tp // tq), in_specs,
        [pl.BlockSpec((tq, hg * HP), lambda h, i: (i, h)), pl.BlockSpec((hg, tq, 1), lambda h, i: (h, i, 0))],
        [jax.ShapeDtypeStruct((tp, nh * HP), F32), jax.ShapeDtypeStruct((nh, tp, 1), F32)],
        [], ("parallel", "parallel"), ins, comm)


def _attn_bwd(q, k, v, vcb, o, lse, do, docb, scale, bq, kterms, name, comm=None):
    tp = q.shape[0]
    nh = q.shape[1] // HP
    tq = _pick(tp, (384, 256, 128))
    has_bq = bq is not None
    c1 = scale * LOG2E
    per_head = kterms[0].shape[0] > 1
    hg = ATT_HEADS_PER_STEP
    lanes = [slice(hh * HP, (hh + 1) * HP) for hh in range(hg)]

    def body(*refs):
        if has_bq:
            (q_ref, k_ref, v_ref, o_ref, lse_ref, do_ref, kb_ref, kd_ref, bq_ref,
             dq_ref, dk_ref, dv_ref, dbq_ref, dkt_ref) = refs
            rb = [(_head_lane(bq_ref[...], hg * pl.program_id(0) + hh) - lse_ref[hh]) * LOG2E for hh in range(hg)]
        else:
            q_ref, k_ref, v_ref, o_ref, lse_ref, do_ref, kb_ref, kd_ref, dq_ref, dk_ref, dv_ref = refs
            rb = [lse_ref[hh] * (-LOG2E) for hh in range(hg)]
        i = pl.program_id(1)

        @pl.when(i == 0)
        def _():
            dk_ref[...] = jnp.zeros_like(dk_ref)
            dv_ref[...] = jnp.zeros_like(dv_ref)
            if has_bq:
                dkt_ref[...] = jnp.zeros_like(dkt_ref)

        qb = [q_ref[:, ln].astype(BF16) for ln in lanes]
        do_ = [do_ref[:, ln] for ln in lanes]
        dob = [d.astype(BF16) for d in do_]
        delta = [jnp.sum(do_[hh] * o_ref[:, lanes[hh]], axis=1, keepdims=True) for hh in range(hg)]

        def tile(j, carry, diag):
            r0 = pl.multiple_of(j * tq, tq)
            out = []
            for hh in range(hg):
                dq, dbq = carry[hh]
                kb = k_ref[pl.ds(r0, tq), lanes[hh]].astype(BF16)
                vb = v_ref[pl.ds(r0, tq), lanes[hh]].astype(BF16)
                kt = (kd_ref if diag else kb_ref)[hh if per_head else 0, j] * LOG2E
                x = (lax.dot_general(qb[hh], kb, (((1,), (1,)), ((), ())), preferred_element_type=F32) * c1
                     + kt + rb[hh])
                if diag:
                    x = jnp.where(_diag_valid(i, tq), x, NEG)
                p = jnp.exp2(x)
                dp = lax.dot_general(dob[hh], vb, (((1,), (1,)), ((), ())), preferred_element_type=F32)
                ds = p * (dp - delta[hh])
                dsb = ds.astype(BF16)
                dk_ref[pl.ds(r0, tq), lanes[hh]] += lax.dot_general(dsb, qb[hh], (((0,), (0,)), ((), ())),
                                                                    preferred_element_type=F32) * scale
                dv_ref[pl.ds(r0, tq), lanes[hh]] += lax.dot_general(p.astype(BF16), dob[hh], (((0,), (0,)), ((), ())),
                                                                    preferred_element_type=F32)
                if has_bq:
                    dkt_ref[hh, j] += jnp.sum(ds, axis=0, keepdims=True)
                    dbq = dbq + jnp.sum(ds, axis=1, keepdims=True)
                out.append((dq + jnp.dot(dsb, kb, preferred_element_type=F32), dbq))
            return tuple(out)

        init = tuple((jnp.zeros((tq, HP), F32), jnp.zeros((tq, 1), F32)) for _ in range(hg))
        res = tile(i, lax.fori_loop(0, i, lambda j, c: tile(j, c, False), init), True)
        for hh, (dq, dbq) in enumerate(res):
            dq_ref[:, lanes[hh]] = dq * scale
            if has_bq:
                dbq_ref[hh] = dbq

    blk_q = pl.BlockSpec((tq, hg * HP), lambda h, i: (i, h))
    blk_k = pl.BlockSpec((tp, hg * HP), lambda h, i: (0, h))
    kt_spec = pl.BlockSpec((hg if per_head else 1, tp // tq, 1, tq),
                           (lambda h, i: (h, 0, 0, 0)) if per_head else (lambda h, i: (0, 0, 0, 0)))
    row_spec = pl.BlockSpec((hg, tq, 1), lambda h, i: (h, i, 0))
    in_specs = [blk_q, blk_k, pl.BlockSpec((tp, hg * HP), lambda h, i: (0, vcb // hg + h)), blk_q,
                row_spec, pl.BlockSpec((tq, hg * HP), lambda h, i: (i, docb // hg + h)), kt_spec, kt_spec]
    ins = [q, k, v, o, lse, do, *kterms]
    out_specs = [blk_q, blk_k, blk_k]
    out_shape = [jax.ShapeDtypeStruct((tp, nh * HP), F32)] * 3
    if has_bq:
        in_specs += [pl.BlockSpec((tq, HP), lambda h, i: (i, 0))]
        ins += [bq]
        out_specs += [row_spec, kt_spec]
        out_shape += [jax.ShapeDtypeStruct((nh, tp, 1), F32), jax.ShapeDtypeStruct((nh, tp // tq, 1, tq), F32)]
    return _pcall(body, name, (nh // hg, tp // tq), in_specs, out_specs, out_shape, [], ("parallel", "arbitrary"), ins,
                  comm)


def _loss_head(h, tgt):
    tp, d = h.shape
    tm = 128
    first = (PAD + N_META) // tm

    def body(h_ref, t_ref, l_ref, d_ref):
        i = pl.program_id(0)

        @pl.when(i == 0)
        def _():
            l_ref[...] = jnp.zeros_like(l_ref)

        live = i >= first
        err = jnp.where(live, h_ref[...] - t_ref[...], 0.0)
        d_ref[...] = err * (1.0 / d)
        l_ref[...] += (0.5 / d) * jnp.sum(err * err)

    return pl.pallas_call(
        body, name="loss_head", grid=(tp // tm,),
        in_specs=[pl.BlockSpec((tm, d), lambda i: (i, 0)), pl.BlockSpec((tm, d), lambda i: (i, 0))],
        out_specs=[_full_spec((8, 128)), pl.BlockSpec((tm, d), lambda i: (i, 0))],
        out_shape=[jax.ShapeDtypeStruct((8, 128), F32), jax.ShapeDtypeStruct((tp, d), F32)],
        compiler_params=_cparams(("arbitrary",)),
    )(h, tgt)


def _rope_tables(pos_rows):
    half = C_ROPE // 2
    freqs = ROPE_THETA ** (-jnp.arange(half, dtype=F32) / half)
    ang = pos_rows[:, None].astype(F32) * freqs
    cos, sin = jnp.cos(ang), jnp.sin(ang)
    tp = pos_rows.shape[0]
    one, zero = jnp.ones((tp, C_NOPE), F32), jnp.zeros((tp, C_NOPE), F32)
    tail1, tail0 = jnp.ones((tp, HP - C_NOPE - C_ROPE), F32), jnp.zeros((tp, HP - C_NOPE - C_ROPE), F32)
    return (jnp.concatenate([one, cos, cos, tail1], axis=1), jnp.concatenate([zero, sin, sin, tail0], axis=1))


def _heads_to_cols(x, lo, n):
    t = x[:, lo:lo + n].T
    return t[:, :, None], t[:, None, :]


def _local_step(x, positions, tgt, w, plan=None):
    def hosted(tag, fn, *args, **kw):
        ops = plan.ride(tag, g) if plan is not None else None
        if not ops:
            return fn(*args, **kw)
        res, got = fn(*args, comm=ops, **kw)
        plan.arrived(tag, got, w)
        return res

    s_len = x.shape[0]
    tp = PAD + N_META + s_len
    tm = _pick(tp, (384, 256, 128))
    front = PAD + N_META
    h0 = jnp.concatenate([jnp.zeros((PAD, D_MODEL), F32), w["meta"], x], axis=0)
    tgt_p = jnp.concatenate([jnp.zeros((front, D_MODEL), F32), tgt], axis=0)
    pos_rows = jnp.concatenate([jnp.zeros((PAD,), jnp.int32), jnp.arange(N_META, dtype=jnp.int32),
                                positions + N_META])
    cos, sin = _rope_tables(pos_rows)
    g = {}

    r_h0 = [(h0, D_MODEL, 0)]
    (xn0,) = _row_fwd("norm0_f", _f_norm, r_h0, [w["ev_ln"]], [(D_MODEL, BF16)], tm)
    z0 = hosted("mm_z0", _mm, xn0, w["ev_w_in"], "nn", "mm_z0")
    r_misc0 = [(z0, HP, 5120 // HP)]
    (g0,) = _row_fwd("gate0_f", _f_gate0, r_misc0, [w["ev_b_if"]], [(HP, F32)], tm)
    h_a, cs, ns, ms = hosted("mlstm_fwd", _mlstm_fwd, z0, g0)
    r_aout = [(h_a, 1024, 0), (z0, 1024, 2)]
    (ha,) = _row_fwd("aout_f", _f_aout, r_aout, [w["ev_a_norm"]], [(1024, BF16)], tm)
    xc = _conv_fwd(z0, 3072 // 256, w["ev_conv_w"], w["ev_conv_b"])
    p_gates = [w["ev_w_ra"], w["ev_b_ra"], w["ev_w_rx"], w["ev_b_rx"], w["ev_lam"]]
    a_g, u_g = _row_fwd("gates_f", _f_gates, [(xc, 1024, 0)], p_gates, [(1024, F32), (1024, F32)], tm)
    hs, hprev = _scan_fwd(a_g, u_g, "lru_f")
    r_bout = [(hs, 1024, 0), (z0, 1024, 4)]
    (hb,) = _row_fwd("bout_f", _f_bout, r_bout, [], [(1024, BF16)], tm)
    hab = _concat_cols([ha, hb], BF16, "cat_hab")
    h1 = _mm(hab, w["ev_w_out"], "nn", "mm_h1", add=h0)
    (xn1,) = _row_fwd("norm1_f", _f_norm, [(h1, D_MODEL, 0)], [w["mlp_ln0"]], [(D_MODEL, BF16)], tm)
    p0, act0 = _mm(xn1, w["w_ff1_0"], "nn", "mm_p0", relu2=True)
    h2 = _mm(act0, w["w_ff2_0"], "nn", "mm_h2", add=h1)
    (xn2,) = _row_fwd("norm2_f", _f_norm, [(h2, D_MODEL, 0)], [w["od_ln"]], [(D_MODEL, BF16)], tm)
    z1 = _mm(xn2, w["od_w_in"], "nn", "mm_z1")
    r_c = [(z1, C_Q_LORA, 3072 // C_Q_LORA), (z1, C_KV_LORA, 3584 // C_KV_LORA)]
    cqn, ckvn = _row_fwd("cnorm_f", _f_cnorm, r_c, [w["od_g_qa"], w["od_g_kva"]],
                         [(C_Q_LORA, BF16), (C_KV_LORA, BF16)], tm)
    q_ = _mm(cqn, w["od_w_uq"], "nn", "mm_q")
    kv_ = _mm(ckvn, w["od_w_ukv"], "nn", "mm_kv")
    r_mla = [(q_, 1024, 0), (kv_, 1024, 0), (z1, HP, 3840 // HP), (cos, HP, 0), (sin, HP, 0)]
    p_mla = [w["gq_full"], w["gk_full"]]
    qm, km = _row_fwd("mla_f", _f_mlaprep, r_mla, p_mla, [(1024, BF16), (1024, BF16)], tm)
    sc_c = (C_NOPE + C_ROPE) ** -0.5
    kt_c = _key_terms(tp, tm)
    hc, lse_c = hosted("mla_attn_f", _attn_fwd, qm, km, kv_, C_HEADS, sc_c, None, kt_c, "mla_attn_f")
    r_fox = [(z1, 1024, 0), (z1, 1024, 1), (z1, HP, 3840 // HP)]
    p_fox = [w["gfq_full"], w["gfk_full"], w["bf_full"]]
    qf, kf, lfx = _row_fwd("fox_f", _f_foxprep, r_fox, p_fox, [(1024, BF16), (1024, BF16), (HP, F32)], tm)
    ones = jnp.ones((tp, HP), F32)
    fcum, fprev = _scan_fwd(ones, lfx, "fcum_f")
    _, bk = _heads_to_cols(fcum, FOX_LANE0, D_HEADS)
    kt_d = _key_terms(tp, tm, bk)
    sc_d = D_HD ** -0.5
    bq = fcum
    hd, lse_d = _attn_fwd(qf, kf, z1, 2048 // HP, sc_d, bq, kt_d, "fox_attn_f")
    hcd = _concat_cols([hc, hd], F32, "cat_hcd")
    h3 = _mm(hcd, w["od_w_out"], "nn", "mm_h3", add=h2)
    (xn3,) = _row_fwd("norm3_f", _f_norm, [(h3, D_MODEL, 0)], [w["mlp_ln1"]], [(D_MODEL, BF16)], tm)
    p1, act1 = _mm(xn3, w["w_ff1_1"], "nn", "mm_p1", relu2=True)
    h4 = _mm(act1, w["w_ff2_1"], "nn", "mm_h4", add=h3)
    lpart, dh4 = _loss_head(h4, tgt_p)
    loss = lpart[0, 0]

    def mlp_bwd(tag, dh_out, h_in, xn, p, act, ln, w1, w2):
        dp = hosted(f"mm_dp{tag}", _mm, dh_out, w2, "nt", f"mm_dp{tag}", relu2_of=p)
        g_w2 = _mm(act, dh_out, "tn", f"mm_dw2_{tag}")
        g_w1 = _mm(xn, dp, "tn", f"mm_dw1_{tag}")
        dxn = _mm(dp, w1, "nt", f"mm_dxn{tag}")
        (dh_in,), (g_ln,) = _row_bwd(f"normm{tag}_b", _f_norm, [(h_in, D_MODEL, 0)], [ln], [(dxn, D_MODEL, 0)], tm,
                                     [True], add=dh_out)
        return dh_in, g_ln, g_w1, g_w2

    dh3, g["mlp_ln1"], g["w_ff1_1"], g["w_ff2_1"] = mlp_bwd("1", dh4, h3, xn3, p1, act1, w["mlp_ln1"],
                                                            w["w_ff1_1"], w["w_ff2_1"])
    dhcd = _mm(dh3, w["od_w_out"], "nt", "mm_dhcd")
    g["od_w_out"] = _mm(hcd, dh3, "tn", "mm_dwout1")
    dqf, dkf, dvf, dbq, dkt = hosted("fox_attn_b", _attn_bwd, qf, kf, z1, 2048 // HP, hd, lse_d, dhcd, D_HEADS, sc_d,
                                     bq, kt_d, "fox_attn_b")
    dfc = dbq[:, :, 0].T - dkt.reshape(D_HEADS, tp).T
    dfcum = jnp.concatenate([jnp.zeros((tp, 96), F32), dfc, jnp.zeros((tp, HP - 96 - D_HEADS), F32)], axis=1)
    dlfx, _ = _scan_bwd(ones, fprev, dfcum, "fcum_b")
    (dfq, dfk, dmisc_f), (g["gfq_full"], g["gfk_full"], g["bf_full"]) = _row_bwd(
        "fox_b", _f_foxprep, r_fox, p_fox, [(dqf, 1024, 0), (dkf, 1024, 0), (dlfx, HP, 0)], tm, [True, True, True],
        out_dtypes=[BF16, BF16, F32])
    dqm, dkm, dvm = _attn_bwd(qm, km, kv_, C_HEADS, hc, lse_c, dhcd, 0, sc_c, None, kt_c, "mla_attn_b")
    (dq_, dkk_, dmisc_m), (g["gq_full"], g["gk_full"]) = _row_bwd(
        "mla_b", _f_mlaprep, r_mla, p_mla, [(dqm, 1024, 0), (dkm, 1024, 0)], tm, [True, True, True, False, False],
        out_dtypes=[BF16, BF16, F32])
    dkv_ = _concat_cols([dkk_, dvm], BF16, "cat_dkv")
    dckvn = _mm(dkv_, w["od_w_ukv"], "nt", "mm_dckvn")
    g["od_w_ukv"] = _mm(ckvn, dkv_, "tn", "mm_dwukv")
    dcqn = _mm(dq_, w["od_w_uq"], "nt", "mm_dcqn")
    g["od_w_uq"] = _mm(cqn, dq_, "tn", "mm_dwuq")
    (dcq, dckv), (g["od_g_qa"], g["od_g_kva"]) = _row_bwd(
        "cnorm_b", _f_cnorm, r_c, [w["od_g_qa"], w["od_g_kva"]],
        [(dcqn, C_Q_LORA, 0), (dckvn, C_KV_LORA, 0)], tm, [True, True], out_dtypes=[BF16, BF16])
    dz1 = _concat_cols([dfq, dfk, dvf, dcq, HP, dckv, dmisc_f + dmisc_m, HP], BF16, "cat_dz1")
    g["od_w_in"] = _mm(xn2, dz1, "tn", "mm_dwin1")
    dxn2 = _mm(dz1, w["od_w_in"], "nt", "mm_dxn2")
    (dh2,), (g["od_ln"],) = _row_bwd("norm2_b", _f_norm, [(h2, D_MODEL, 0)], [w["od_ln"]], [(dxn2, D_MODEL, 0)], tm,
                                     [True], add=dh3)
    dh1, g["mlp_ln0"], g["w_ff1_0"], g["w_ff2_0"] = mlp_bwd("0", dh2, h1, xn1, p0, act0, w["mlp_ln0"],
                                                            w["w_ff1_0"], w["w_ff2_0"])
    dhab = _mm(dh1, w["ev_w_out"], "nt", "mm_dhab")
    g["ev_w_out"] = _mm(hab, dh1, "tn", "mm_dwout0")
    (dhs, dgb), _ = _row_bwd("bout_b", _f_bout, r_bout, [], [(dhab, 1024, 1)], tm, [True, True],
                             out_dtypes=[F32, BF16])
    du_g, da_g = _scan_bwd(a_g, hprev, dhs, "lru_b")
    (dxc,), (g["ev_w_ra"], g["ev_b_ra"], g["ev_w_rx"], g["ev_b_rx"], g["ev_lam"]) = hosted(
        "gates_b", _row_bwd, "gates_b", _f_gates, [(xc, 1024, 0)], p_gates, [(da_g, 1024, 0), (du_g, 1024, 0)], tm,
        [True])
    dxb, g["ev_conv_w"], g["ev_conv_b"] = _conv_bwd(z0, 3072 // 256, w["ev_conv_w"], dxc)
    (dh_a, do_), (g["ev_a_norm"],) = _row_bwd("aout_b", _f_aout, r_aout, [w["ev_a_norm"]], [(dhab, 1024, 0)], tm,
                                              [True, True], out_dtypes=[F32, BF16])
    dqkv, dg0 = hosted("mlstm_bwd", _mlstm_bwd, z0, g0, cs, ns, ms, dh_a)
    (dmisc0,), (g["ev_b_if"],) = _row_bwd("gate0_b", _f_gate0, r_misc0, [w["ev_b_if"]], [(dg0, HP, 0)], tm, [True],
                                          out_dtypes=[BF16])
    dz0 = _concat_cols([dqkv, do_, dxb, dgb, dmisc0, HP], BF16, "cat_dz0")
    g["ev_w_in"] = _mm(xn0, dz0, "tn", "mm_dwin0")
    dxn0 = hosted("mm_dxn0", _mm, dz0, w["ev_w_in"], "nt", "mm_dxn0")
    (dh0,), (g["ev_ln"],) = _row_bwd("norm0_b", _f_norm, r_h0, [w["ev_ln"]], [(dxn0, D_MODEL, 0)], tm, [True], add=dh1)
    g["meta"] = dh0[PAD:front]
    return loss, dh0[front:], g


def _pad_last(a, n):
    return jnp.pad(a, [(0, 0)] * (a.ndim - 1) + [(0, n - a.shape[-1])])


def _pad_heads(a, nh, d):
    return _pad_last(a.reshape(a.shape[:-1] + (nh, d)), HP).reshape(a.shape[:-1] + (nh * HP,))


def _unpad_heads(a, nh, d):
    return a.reshape(a.shape[:-1] + (nh, HP))[..., :d].reshape(a.shape[:-1] + (nh * d,))


_MM_UNITS = ("ev_w_in", "ev_w_out", "od_w_in", "od_w_uq", "od_w_ukv", "od_w_out",
             "w_ff1_0", "w_ff2_0", "w_ff1_1", "w_ff2_1")


def _mw_pad(name, a):
    if name == "ev_w_in":
        return _pad_last(jnp.concatenate([a[:, :3072], a[:, 3080:5128], a[:, 3072:3080]], axis=1), ZE)
    if name == "od_w_in":
        z = lambda n: jnp.zeros((a.shape[0], n), a.dtype)
        return jnp.concatenate(
            [_pad_heads(a[:, 672:1184], D_HEADS, D_HD), _pad_heads(a[:, 1184:1696], D_HEADS, D_HD),
             _pad_heads(a[:, 1696:2208], D_HEADS, D_HD), a[:, 0:384], z(128), a[:, 384:640],
             z(64), a[:, 640:672], a[:, 2208:2216], z(24), z(128)], axis=1)
    if name == "od_w_uq":
        return _pad_heads(a, C_HEADS, C_NOPE + C_ROPE)
    if name == "od_w_ukv":
        wkv = a.reshape(C_KV_LORA, C_HEADS, C_NOPE + C_V)
        return jnp.concatenate([_pad_last(wkv[:, :, :C_NOPE], HP).reshape(C_KV_LORA, -1),
                                _pad_last(wkv[:, :, C_NOPE:], HP).reshape(C_KV_LORA, -1)], axis=1)
    if name == "od_w_out":
        return jnp.pad(a.reshape(2 * C_HEADS, C_V, D_MODEL), ((0, 0), (0, HP - C_V), (0, 0))).reshape(-1, D_MODEL)
    return a


def _mw_unpad(name, g):
    if name == "ev_w_in":
        return jnp.concatenate([g[:, :3072], g[:, 5120:5128], g[:, 3072:5120]], axis=1)
    if name == "od_w_in":
        return jnp.concatenate(
            [g[:, 3072:3456], g[:, 3584:3840], g[:, 3904:3936], _unpad_heads(g[:, 0:1024], D_HEADS, D_HD),
             _unpad_heads(g[:, 1024:2048], D_HEADS, D_HD), _unpad_heads(g[:, 2048:3072], D_HEADS, D_HD),
             g[:, 3936:3944]], axis=1)
    if name == "od_w_uq":
        return _unpad_heads(g, C_HEADS, C_NOPE + C_ROPE)
    if name == "od_w_ukv":
        gk = g[:, :C_HEADS * HP].reshape(C_KV_LORA, C_HEADS, HP)[:, :, :C_NOPE]
        gv = g[:, C_HEADS * HP:].reshape(C_KV_LORA, C_HEADS, HP)[:, :, :C_V]
        return jnp.concatenate([gk, gv], axis=2).reshape(C_KV_LORA, -1)
    if name == "od_w_out":
        return g.reshape(2 * C_HEADS, HP, D_MODEL)[:, :C_V].reshape(-1, D_MODEL)
    return g


def _prep_weights(p):
    w = {}
    w["meta"] = p["meta"]
    for k in ("ev_ln", "ev_a_norm", "ev_conv_b", "ev_b_ra", "ev_b_rx", "ev_lam", "od_ln", "od_g_qa", "od_g_kva"):
        w[k] = p[k].reshape(1, -1)
    w["ev_b_if"] = _pad_last(p["ev_b_if"].reshape(1, -1), HP)
    w["ev_conv_w"] = p["ev_conv_w"][0]
    w["ev_w_ra"] = p["ev_w_ra"][0]
    w["ev_w_rx"] = p["ev_w_rx"][0]
    f1 = lambda a: a.reshape(1, -1)
    w["gq_full"] = _pad_last(jnp.concatenate([f1(p["od_g_qn"]), f1(p["od_g_qr"])], axis=1), HP)
    w["gk_full"] = _pad_last(jnp.concatenate([f1(p["od_g_kn"]), f1(p["od_g_kr"])], axis=1), HP)
    w["gfq_full"] = _pad_last(f1(p["od_g_fq"]), HP)
    w["gfk_full"] = _pad_last(f1(p["od_g_fk"]), HP)
    w["bf_full"] = _pad_last(jnp.concatenate([jnp.zeros((1, 96), F32), f1(p["od_b_f"])], axis=1), HP)
    for l in (0, 1):
        w[f"mlp_ln{l}"] = p["mlp_ln"][l:l + 1]
    for n in _MM_UNITS:
        if n in p:
            w[n] = _mw_pad(n, p[n])
    return w


def _unprep_grads(g):
    o = {}
    o["meta"] = g["meta"]
    for k in ("ev_ln", "ev_a_norm", "ev_conv_b", "ev_b_ra", "ev_b_rx", "ev_lam", "od_ln", "od_g_qa", "od_g_kva"):
        o[k] = g[k].reshape(1, -1)
    o["ev_b_if"] = g["ev_b_if"][:, :2 * A_HEADS]
    o["ev_conv_w"] = g["ev_conv_w"][None]
    o["ev_w_ra"] = g["ev_w_ra"][None]
    o["ev_w_rx"] = g["ev_w_rx"][None]
    o["od_g_qn"] = g["gq_full"][:, :C_NOPE]
    o["od_g_qr"] = g["gq_full"][:, C_NOPE:C_NOPE + C_ROPE]
    o["od_g_kn"] = g["gk_full"][:, :C_NOPE]
    o["od_g_kr"] = g["gk_full"][:, C_NOPE:C_NOPE + C_ROPE]
    o["od_g_fq"] = g["gfq_full"][:, :D_HD]
    o["od_g_fk"] = g["gfk_full"][:, :D_HD]
    o["od_b_f"] = g["bf_full"][:, 96:96 + D_HEADS]
    o["mlp_ln"] = jnp.concatenate([g["mlp_ln0"], g["mlp_ln1"]], axis=0)
    return o


def _exchange(ops, name):
    n_ops = len(ops)
    flags = [s for _, s in ops]

    def body(*refs):
        copies = _comm_copies(refs[:n_ops], refs[n_ops:2 * n_ops], flags, *refs[2 * n_ops:])
        _comm_begin(copies)
        _comm_end(copies)

    any_spec = pl.BlockSpec(memory_space=pl.ANY)
    c_shape, c_sems = _comm_shapes(ops)
    return pl.pallas_call(body, name=name, out_shape=c_shape, in_specs=[any_spec] * n_ops,
                          out_specs=[any_spec] * n_ops, scratch_shapes=c_sems)(*[x for x, _ in ops])


def _adamw(parts, w, m, v, name):
    r, c = w.shape
    tr = r
    for cand in (512, 256, 128, 64, 32, 16):
        if r % cand == 0 and N_DEV * cand * c * 4 <= 4 * 1024 * 1024:
            tr = cand
            break
    c1 = 1.0 / (1.0 - ADAM_B1 ** ADAM_STEP)
    c2 = 1.0 / (1.0 - ADAM_B2 ** ADAM_STEP)

    def body(p_ref, w_ref, m_ref, v_ref, g_ref, d_ref, mo_ref, vo_ref):
        g = p_ref[0].astype(F32)
        for j in range(1, N_DEV):
            g = g + p_ref[j].astype(F32)
        m2 = ADAM_B1 * m_ref[...] + (1.0 - ADAM_B1) * g
        v2 = ADAM_B2 * v_ref[...] + (1.0 - ADAM_B2) * (g * g)
        g_ref[...] = g
        mo_ref[...] = m2
        vo_ref[...] = v2
        d_ref[...] = -ADAM_LR * ((m2 * c1) / (jnp.sqrt(v2 * c2) + ADAM_EPS) + ADAM_WD * w_ref[...])

    spec = pl.BlockSpec((tr, c), lambda i: (i, 0))
    return pl.pallas_call(
        body, name=name, grid=(r // tr,),
        in_specs=[pl.BlockSpec((N_DEV, tr, c), lambda i: (0, i, 0)), spec, spec, spec],
        out_specs=[spec] * 4, out_shape=[jax.ShapeDtypeStruct((r, c), F32)] * 4,
        compiler_params=_cparams(("parallel",)),
    )(parts, w, m, v)


def _rows_for(n, mult):
    return -(-n // (1024 * mult)) * mult


def _pack(arrs, mult, lead=()):
    nl = len(lead)
    flat = jnp.concatenate([a.reshape(lead + (-1,)) for a in arrs], axis=nl)
    rows = _rows_for(flat.shape[-1], mult)
    return jnp.pad(flat, [(0, 0)] * nl + [(0, rows * 1024 - flat.shape[-1])]).reshape(lead + (rows, 1024))


def _unpack(buf, shapes):
    lead = buf.shape[:-2]
    flat = buf.reshape(lead + (-1,))
    out, off = [], 0
    for s in shapes:
        n = 1
        for d_ in s:
            n *= d_
        out.append(flat[..., off:off + n].reshape(lead + tuple(s)))
        off += n
    return out


def _unshard(g8, ax):
    a = jnp.moveaxis(g8, 0, ax)
    return a.reshape(a.shape[:ax] + (N_DEV * a.shape[ax + 1],) + a.shape[ax + 2:])


def _shard8(full, ax):
    s = full.shape
    return jnp.moveaxis(full.reshape(s[:ax] + (N_DEV, s[ax] // N_DEV) + s[ax + 1:]), ax, 0)


_NAMES = ["meta", "ev_ln", "ev_w_in", "ev_b_if", "ev_a_norm", "ev_conv_w", "ev_conv_b", "ev_w_ra", "ev_b_ra",
          "ev_w_rx", "ev_b_rx", "ev_lam", "ev_w_out", "od_ln", "od_w_in", "od_b_f", "od_g_qa", "od_g_kva",
          "od_w_uq", "od_w_ukv", "od_g_qn", "od_g_qr", "od_g_kn", "od_g_kr", "od_g_fq", "od_g_fk", "od_w_out",
          "mlp_ln", "w_ff1", "w_ff2"]
_SHARD_AXIS = {"meta": 1, "ev_w_in": 2, "ev_conv_w": 2, "ev_w_out": 1, "od_ln": 1, "od_w_in": 2, "od_g_qa": 1,
               "od_g_kva": 1, "od_w_uq": 2, "od_w_ukv": 2, "od_w_out": 1, "w_ff1": 2, "w_ff2": 1}
_MATMUL_WEIGHTS = ("ev_w_in", "ev_w_out", "od_w_in", "od_w_uq", "od_w_ukv", "od_w_out", "w_ff1", "w_ff2")
_BIG_REPL = ("ev_w_ra", "ev_w_rx")
_COL_SHARDED = ("ev_w_in", "od_w_in", "od_w_uq", "od_w_ukv", "w_ff1_0", "w_ff1_1")
_GATHER_ON = {"mm_z0": ("od_w_in", "od_w_uq", "od_w_ukv"), "mlstm_fwd": ("ev_w_out", "w_ff1_0", "w_ff2_0"),
              "mla_attn_f": ("od_w_out", "w_ff1_1", "w_ff2_1")}
_SCATTER_ON = {"fox_attn_b": ("w_ff2_1", "w_ff1_1", "od_w_out"),
               "mm_dp0": ("od_w_ukv", "od_w_uq", "od_w_in"), "gates_b": ("ev_w_out",),
               "mlstm_bwd": ("w_ff2_0", "w_ff1_0"), "mm_dxn0": ("ev_w_in",)}
_REPL_ON = "mlstm_bwd"


def _unit_of(d, n):
    return d[n[:-2]][int(n[-1])] if n.startswith("w_ff") else d[n][0]


def _unit_full(n, g8):
    return jnp.transpose(g8, (1, 0, 2)).reshape(g8.shape[1], -1) if n in _COL_SHARDED else g8.reshape(-1, g8.shape[2])


def _unit_slots(n, full, r, c):
    return full.reshape(r, N_DEV, c).transpose(1, 0, 2) if n in _COL_SHARDED else full.reshape(N_DEV, r, c)


class _Plan:
    def __init__(self, shards):
        self.shards = shards
        self.parts = {}

    def slots(self, n, g):
        r, c = self.shards[n].shape
        return _unit_slots(n, _mw_unpad(n, g[n]), r, c).astype(BF16)

    def ride(self, tag, g):
        if tag in _GATHER_ON:
            return [(self.shards[n].astype(BF16), False) for n in _GATHER_ON[tag]]
        ops = [(self.slots(n, g), True) for n in _SCATTER_ON.get(tag, ())]
        if tag == _REPL_ON:
            ops += [(g[n].reshape(-1, B_BLOCK).astype(BF16), False) for n in _BIG_REPL]
        return ops

    def arrived(self, tag, got, w):
        if tag in _GATHER_ON:
            for n, g8 in zip(_GATHER_ON[tag], got):
                w[n] = _mw_pad(n, _unit_full(n, g8))
        else:
            self.parts.update(zip(_SCATTER_ON[tag] + (_BIG_REPL if tag == _REPL_ON else ()), got))


def kernel(x, positions, meta, ev_ln, ev_w_in, ev_b_if, ev_a_norm, ev_conv_w, ev_conv_b, ev_w_ra, ev_b_ra, ev_w_rx, ev_b_rx, ev_lam, ev_w_out, od_ln, od_w_in, od_b_f, od_g_qa, od_g_kva, od_w_uq, od_w_ukv, od_g_qn, od_g_qr, od_g_kn, od_g_kr, od_g_fq, od_g_fk, od_w_out, mlp_ln, w_ff1, w_ff2, loss_target, m_meta, m_ev_ln, m_ev_w_in, m_ev_b_if, m_ev_a_norm, m_ev_conv_w, m_ev_conv_b, m_ev_w_ra, m_ev_b_ra, m_ev_w_rx, m_ev_b_rx, m_ev_lam, m_ev_w_out, m_od_ln, m_od_w_in, m_od_b_f, m_od_g_qa, m_od_g_kva, m_od_w_uq, m_od_w_ukv, m_od_g_qn, m_od_g_qr, m_od_g_kn, m_od_g_kr, m_od_g_fq, m_od_g_fk, m_od_w_out, m_mlp_ln, m_w_ff1, m_w_ff2, v_meta, v_ev_ln, v_ev_w_in, v_ev_b_if, v_ev_a_norm, v_ev_conv_w, v_ev_conv_b, v_ev_w_ra, v_ev_b_ra, v_ev_w_rx, v_ev_b_rx, v_ev_lam, v_ev_w_out, v_od_ln, v_od_w_in, v_od_b_f, v_od_g_qa, v_od_g_kva, v_od_w_uq, v_od_w_ukv, v_od_g_qn, v_od_g_qr, v_od_g_kn, v_od_g_kr, v_od_g_fq, v_od_g_fk, v_od_w_out, v_mlp_ln, v_w_ff1, v_w_ff2):
    given = dict(locals())
    wts = {n: given[n] for n in _NAMES}
    mom = {n: given["m_" + n] for n in _NAMES}
    var = {n: given["v_" + n] for n in _NAMES}
    small_sh = [n for n in _NAMES if n in _SHARD_AXIS and n not in _MATMUL_WEIGHTS]
    small_rp = [n for n in _NAMES if n not in _SHARD_AXIS and n not in _BIG_REPL]
    shp = {n: wts[n].shape for n in _NAMES}
    plan = _Plan({n: _unit_of(wts, n) for n in _MM_UNITS})

    got = _exchange([(plan.shards["ev_w_in"].astype(BF16), False), (_pack([wts[n] for n in small_sh], 8), False)],
                    "gather_first")
    p = {n: wts[n] for n in _NAMES if n not in _SHARD_AXIS}
    for n, a in zip(small_sh, _unpack(got[1], [shp[n] for n in small_sh])):
        p[n] = _unshard(a, _SHARD_AXIS[n])
    p["ev_w_in"] = _unit_full("ev_w_in", got[0])

    loss, gx, g = _local_step(x[0], positions[0], loss_target[0], _prep_weights(p), plan)
    grads = _unprep_grads(g)

    parts = _exchange([(_pack([_shard8(grads[n], _SHARD_AXIS[n]) for n in small_sh], 8, (N_DEV,)), True),
                       (_pack([grads[n].reshape(shp[n]) for n in small_rp], 8), False)], "exchange_last")

    res = {}
    unit_res = {n: _adamw(plan.parts[n], *[_unit_of(d, n) for d in (wts, mom, var)], f"adamw_{n}")
                for n in _MM_UNITS}
    for n in _MATMUL_WEIGHTS:
        if n.startswith("w_ff"):
            res[n] = [jnp.stack([unit_res[n + "_0"][k], unit_res[n + "_1"][k]]) for k in range(4)]
        else:
            res[n] = [r[None] for r in unit_res[n]]
    r4 = _adamw(parts[0], *[_pack([d[n] for n in small_sh], 8) for d in (wts, mom, var)], "adamw_small_sharded")
    for n, *four in zip(small_sh, *[_unpack(r, [shp[n] for n in small_sh]) for r in r4]):
        res[n] = four
    for n in _BIG_REPL:
        r4 = _adamw(plan.parts[n], *[d[n].reshape(-1, B_BLOCK) for d in (wts, mom, var)], f"adamw_{n}")
        res[n] = [r.reshape(shp[n]) for r in r4]
    r4 = _adamw(parts[1], *[_pack([d[n] for n in small_rp], 8) for d in (wts, mom, var)], "adamw_small_repl")
    for n, *four in zip(small_rp, *[_unpack(r, [shp[n] for n in small_rp]) for r in r4]):
        res[n] = four

    outs = [res[n][kind] for kind in range(4) for n in _NAMES]
    loss = lax.psum(loss, ("x", "y", "c"))
    return (loss, gx[None], *outs)
```

```python
import functools

import jax
import jax.numpy as jnp
from jax import lax
from jax.experimental import pallas as pl
from jax.experimental.pallas import tpu as pltpu

F32 = jnp.float32
BF16 = jnp.bfloat16

D_MODEL = 1024
N_META = 16
PAD = 112
EPS = 1e-6
NEG = -1e30
A_HEADS, A_DQK, A_DV = 4, 128, 256
A_CHUNK = 384
B_BLOCKS, B_BLOCK, CONV_W, LRU_C = 8, 128, 4, 8.0
C_HEADS, C_Q_LORA, C_KV_LORA, C_NOPE, C_ROPE, C_V = 8, 384, 256, 64, 32, 64
ROPE_THETA = 10000.0
D_HEADS, D_HD = 8, 64
HP = 128
ZE = 5376
ZO = 4096
N_DEV = 8
ADAM_LR, ADAM_B1, ADAM_B2, ADAM_EPS, ADAM_WD, ADAM_STEP = 0.001, 0.9, 0.999, 1e-08, 0.01, 10
VMEM_LIMIT = 56 * 1024 * 1024
SCAN_BLOCK = 128


def _pick(n, prefs):
    for p in prefs:
        if n % p == 0:
            return p
    return n


def _cparams(dims):
    return pltpu.CompilerParams(dimension_semantics=dims, vmem_limit_bytes=VMEM_LIMIT)


def _full_spec(shape):
    nd = len(shape)
    return pl.BlockSpec(shape, lambda *_: (0,) * nd)


def _me_and_peers():
    mx, my, mc = lax.axis_index("x"), lax.axis_index("y"), lax.axis_index("c")
    peers = []
    for k in range(1, N_DEV):
        px, py, pc = mx ^ ((k >> 2) & 1), my ^ ((k >> 1) & 1), mc ^ (k & 1)
        peers.append(((px, py, pc), 4 * px + 2 * py + pc))
    return 4 * mx + 2 * my + mc, peers


def _comm_copies(x_refs, o_refs, flags, send_sems, recv_sems, local_sems):
    n_peer = N_DEV - 1
    mx, my, mc = lax.axis_index("x"), lax.axis_index("y"), lax.axis_index("c")
    me = 4 * mx + 2 * my + mc
    sibling = (mx, my, 1 - mc)
    chips = [(mx ^ a, my ^ b) for a, b in ((0, 1), (1, 0), (1, 1))]
    _, peers = _me_and_peers()
    out = dict(first=[], landed=[], forward=[], remote=[], local=[])

    def copy(a, k, src, dst, to):
        return pltpu.make_async_remote_copy(src_ref=src, dst_ref=dst, send_sem=send_sems.at[a * n_peer + k],
                                            recv_sem=recv_sems.at[a * n_peer + k], device_id=to,
                                            device_id_type=pl.DeviceIdType.MESH)

    for a, scatter in enumerate(flags):
        x_ref, o_ref = x_refs[a], o_refs[a]
        out["local"].append(pltpu.make_async_copy(x_ref.at[me] if scatter else x_ref, o_ref.at[me], local_sems.at[a]))
        if scatter:
            out["first"] += [copy(a, k, x_ref.at[pid], o_ref.at[me], peer) for k, (peer, pid) in enumerate(peers)]
        else:
            out["first"].append(copy(a, 0, x_ref, o_ref.at[me], sibling))
            for t, (px, py) in enumerate(chips):
                far = copy(a, 1 + t, x_ref, o_ref.at[me], (px, py, mc))
                slot = o_ref.at[4 * px + 2 * py + mc]
                out["first"].append(far)
                out["landed"].append(far)
                out["forward"].append(copy(a, 4 + t, slot, slot, sibling))
    out["remote"] = out["first"] + out["forward"]
    return out


def _comm_begin(c):
    for cp in c["local"] + c["first"]:
        cp.start()


def _comm_end(c):
    for cp in c["landed"]:
        cp.wait_recv()
    for cp in c["forward"]:
        cp.start()
    for cp in c["remote"]:
        cp.wait_send()
    for cp in c["remote"]:
        if all(cp is not d for d in c["landed"]):
            cp.wait_recv()
    for cp in c["local"]:
        cp.wait()


def _comm_shapes(comm):
    n = len(comm)
    out_shape = [jax.ShapeDtypeStruct((N_DEV,) + x.shape[-2:], x.dtype) for x, _ in comm]
    sems = [pltpu.SemaphoreType.DMA((n * (N_DEV - 1),)), pltpu.SemaphoreType.DMA((n * (N_DEV - 1),)),
            pltpu.SemaphoreType.DMA((n,))]
    return out_shape, sems


def _pcall(body, name, grid, in_specs, out_specs, out_shape, scratch_shapes, dims, ins, comm=None):
    if not comm:
        return pl.pallas_call(body, name=name, grid=grid, in_specs=in_specs, out_specs=out_specs,
                              out_shape=out_shape, scratch_shapes=scratch_shapes,
                              compiler_params=_cparams(dims))(*ins)
    n_in, n_out, n = len(in_specs), len(out_specs), len(comm)
    flags = [s for _, s in comm]
    c_shape, c_sems = _comm_shapes(comm)

    def riding(*refs):
        cx = refs[n_in:n_in + n]
        co = refs[n_in + n + n_out:n_in + 2 * n + n_out]
        rest = refs[n_in + 2 * n + n_out:]
        sems = rest[len(rest) - 3:]
        ids = [pl.program_id(a) for a in range(len(grid))]
        first = functools.reduce(jnp.logical_and, [i == 0 for i in ids])
        last = functools.reduce(jnp.logical_and, [i == g - 1 for i, g in zip(ids, grid)])

        @pl.when(first)
        def _():
            _comm_begin(_comm_copies(cx, co, flags, *sems))

        body(*refs[:n_in], *refs[n_in + n:n_in + n + n_out], *rest[:len(rest) - 3])

        @pl.when(last)
        def _():
            _comm_end(_comm_copies(cx, co, flags, *sems))

    any_spec = pl.BlockSpec(memory_space=pl.ANY)
    res = pl.pallas_call(
        riding, name=name, grid=grid, in_specs=list(in_specs) + [any_spec] * n,
        out_specs=list(out_specs) + [any_spec] * n, out_shape=list(out_shape) + c_shape,
        scratch_shapes=list(scratch_shapes) + c_sems,
        compiler_params=_cparams(("arbitrary",) * len(grid)))(*ins, *[x for x, _ in comm])
    return list(res[:n_out]), list(res[n_out:])


def _mm(a, b, mode, name, out_dtype=None, add=None, relu2=False, relu2_of=None, comm=None):
    if out_dtype is None:
        out_dtype = BF16 if (mode == "tn" or relu2_of is not None) else F32
    if relu2_of is not None:
        add = relu2_of
    if mode == "nn":
        (m, k), n = a.shape, b.shape[1]
    elif mode == "nt":
        (m, k), n = a.shape, b.shape[0]
    else:
        (k, m), n = a.shape, b.shape[1]
    tm = _pick(m, (1408, 1024, 768, 512, 384, 256, 128))
    tn = _pick(n, (1024, 768, 512, 384, 256, 128))
    tk = _pick(k, (1408, 1024, 768, 512, 384, 256, 128))
    nk = k // tk
    if mode == "nn":
        a_spec = pl.BlockSpec((tm, tk), lambda i, j, q: (i, q))
        b_spec = pl.BlockSpec((tk, tn), lambda i, j, q: (q, j))
        dn = (((1,), (0,)), ((), ()))
    elif mode == "nt":
        a_spec = pl.BlockSpec((tm, tk), lambda i, j, q: (i, q))
        b_spec = pl.BlockSpec((tn, tk), lambda i, j, q: (j, q))
        dn = (((1,), (1,)), ((), ()))
    else:
        a_spec = pl.BlockSpec((tk, tm), lambda i, j, q: (q, i))
        b_spec = pl.BlockSpec((tk, tn), lambda i, j, q: (q, j))
        dn = (((0,), (0,)), ((), ()))
    o_spec = pl.BlockSpec((tm, tn), lambda i, j, q: (i, j))
    has_add = add is not None

    def body(*refs):
        a_ref, b_ref = refs[:2]
        add_ref = refs[2] if has_add else None
        o_refs, acc = refs[2 + has_add:-1], refs[-1]
        q = pl.program_id(2)

        @pl.when(q == 0)
        def _():
            acc[...] = jnp.zeros_like(acc)

        acc[...] += lax.dot_general(a_ref[...].astype(BF16), b_ref[...].astype(BF16), dn,
                                    preferred_element_type=F32)

        @pl.when(q == nk - 1)
        def _():
            r = acc[...]
            if relu2_of is not None:
                r = r * (2.0 * jnp.maximum(add_ref[...], 0.0))
            elif has_add:
                r = r + add_ref[...]
            o_refs[0][...] = r.astype(o_refs[0].dtype)
            if relu2:
                pos = jnp.maximum(r, 0.0)
                o_refs[1][...] = (pos * pos).astype(o_refs[1].dtype)

    ins = [a, b] + ([add] if has_add else [])
    in_specs = [a_spec, b_spec] + ([o_spec] if has_add else [])
    out_shape = [jax.ShapeDtypeStruct((m, n), out_dtype)] + ([jax.ShapeDtypeStruct((m, n), BF16)] if relu2 else [])
    res = _pcall(body, name, (m // tm, n // tn, nk), in_specs, [o_spec] * len(out_shape), out_shape,
                 [pltpu.VMEM((tm, tn), F32)], ("parallel", "parallel", "arbitrary"), ins, comm)
    outs = res[0] if comm else res
    outs = tuple(outs) if relu2 else outs[0]
    return (outs, res[1]) if comm else outs


def _row_specs(rows, tm):
    return [pl.BlockSpec((tm, w), functools.partial(lambda cb, i: (i, cb), cb)) for (_, w, cb) in rows]


def _row_fwd(name, f, rows, params, outs, tm):
    tp = rows[0][0].shape[0]
    nr, npar = len(rows), len(params)

    def body(*refs):
        i = pl.program_id(0)
        rv = [r[...] for r in refs[:nr]]
        pv = [r[...] for r in refs[nr:nr + npar]]
        res = f(i, rv, pv)
        for o_ref, r in zip(refs[nr + npar:], res):
            o_ref[...] = r.astype(o_ref.dtype)

    res = pl.pallas_call(
        body, name=name, grid=(tp // tm,),
        in_specs=_row_specs(rows, tm) + [_full_spec(p.shape) for p in params],
        out_specs=[pl.BlockSpec((tm, w), lambda i: (i, 0)) for (w, _) in outs],
        out_shape=[jax.ShapeDtypeStruct((tp, w), dt) for (w, dt) in outs],
        compiler_params=_cparams(("parallel",)),
    )(*[r[0] for r in rows], *params)
    return list(res)


def _row_bwd(name, f, rows, params, douts, tm, diff, add=None, out_dtypes=None, comm=None):
    tp = rows[0][0].shape[0]
    nr, npar, nd = len(rows), len(params), len(douts)
    didx = [k for k in range(nr) if diff[k]]
    has_add = add is not None

    def body(*refs):
        i = pl.program_id(0)
        rv = [r[...] for r in refs[:nr]]
        pv = [r[...] for r in refs[nr:nr + npar]]
        dv = [r[...] for r in refs[nr + npar:nr + npar + nd]]
        pos = nr + npar + nd
        add_ref = refs[pos] if has_add else None
        pos += 1 if has_add else 0
        dr_refs = refs[pos:pos + len(didx)]
        dp_refs = refs[pos + len(didx):]

        def g(drv, pvs):
            full = list(rv)
            for k, val in zip(didx, drv):
                full[k] = val
            return tuple(f(i, full, list(pvs)))

        _, vjp = jax.vjp(g, [rv[k] for k in didx], pv)
        d_r, d_p = vjp(tuple(dv))
        for n_, (ref, val) in enumerate(zip(dr_refs, d_r)):
            if has_add and n_ == 0:
                val = val + add_ref[...]
            ref[...] = val.astype(ref.dtype)

        @pl.when(i == 0)
        def _():
            for ref in dp_refs:
                ref[...] = jnp.zeros_like(ref)

        for ref, val in zip(dp_refs, d_p):
            ref[...] += val

    in_specs = (_row_specs(rows, tm) + [_full_spec(p.shape) for p in params] + _row_specs(douts, tm))
    ins = [r[0] for r in rows] + list(params) + [d[0] for d in douts]
    if has_add:
        in_specs.append(pl.BlockSpec((tm, rows[didx[0]][1]), lambda i: (i, 0)))
        ins.append(add)
    out_specs = ([pl.BlockSpec((tm, rows[k][1]), lambda i: (i, 0)) for k in didx]
                 + [_full_spec(p.shape) for p in params])
    out_dtypes = out_dtypes or [F32] * len(didx)
    out_shape = ([jax.ShapeDtypeStruct((tp, rows[k][1]), dt) for k, dt in zip(didx, out_dtypes)]
                 + [jax.ShapeDtypeStruct(p.shape, F32) for p in params])
    res = _pcall(body, name, (tp // tm,), in_specs, out_specs, out_shape, [], ("arbitrary",), ins, comm)
    outs = res[0] if comm else res
    outs = (list(outs[:len(didx)]), list(outs[len(didx):]))
    return (outs, res[1]) if comm else outs


def _concat_cols(pieces, dtype, name):
    arrs = [p for p in pieces if not isinstance(p, int)]
    widths = [p if isinstance(p, int) else p.shape[1] for p in pieces]
    tp = arrs[0].shape[0]
    tm = _pick(tp, (384, 256, 128))

    def body(*refs):
        o_ref, k, off = refs[-1], 0, 0
        for p, w in zip(pieces, widths):
            if isinstance(p, int):
                o_ref[:, off:off + w] = jnp.zeros((tm, w), o_ref.dtype)
            else:
                o_ref[:, off:off + w] = refs[k][...].astype(o_ref.dtype)
                k += 1
            off += w

    return pl.pallas_call(
        body, name=name, grid=(tp // tm,),
        in_specs=[pl.BlockSpec((tm, a.shape[1]), lambda i: (i, 0)) for a in arrs],
        out_specs=pl.BlockSpec((tm, sum(widths)), lambda i: (i, 0)),
        out_shape=jax.ShapeDtypeStruct((tp, sum(widths)), dtype),
        compiler_params=_cparams(("parallel",)),
    )(*arrs)


def _rowmask(i, tm):
    return (i * tm + lax.broadcasted_iota(jnp.int32, (tm, 1), 0)) >= PAD


def _lane(n=HP):
    return lax.broadcasted_iota(jnp.int32, (1, n), 1)


def _softplus(x):
    return jnp.maximum(x, 0.0) + jnp.log(1.0 + jnp.exp(-jnp.abs(x)))


def _log_sigmoid(x):
    return -_softplus(-x)


def _sigmoid(x):
    return 1.0 / (1.0 + jnp.exp(-x))


@functools.partial(jax.custom_vjp, nondiff_argnums=(1,))
def _lroll(x, s):
    return pltpu.roll(x, s % HP, 1)


def _lroll_fwd(x, s):
    return _lroll(x, s), None


def _lroll_bwd(s, _, g):
    return (pltpu.roll(g, (-s) % HP, 1),)


_lroll.defvjp(_lroll_fwd, _lroll_bwd)


def _f_norm(i, rv, pv):
    (h,), (g,) = rv, pv
    return [h * lax.rsqrt(jnp.mean(h * h, axis=-1, keepdims=True) + EPS) * g]


def _f_gate0(i, rv, pv):
    (misc,), (b,) = rv, pv
    tm = misc.shape[0]
    x = misc + b
    lane, ok = _lane(), _rowmask(i, tm)
    li = jnp.where(ok, x, NEG)
    lf = jnp.where(ok, _log_sigmoid(x), 0.0)
    return [jnp.where(lane < A_HEADS, li, jnp.where(lane < 2 * A_HEADS, lf, 0.0))]


def _f_aout(i, rv, pv):
    (ha, o), (g,) = rv, pv
    outs = []
    for h in range(A_HEADS):
        x = ha[:, h * A_DV:(h + 1) * A_DV]
        outs.append(x * lax.rsqrt(jnp.mean(x * x, axis=-1, keepdims=True) + EPS) * g)
    return [jnp.concatenate(outs, axis=1) * _sigmoid(o)]


def _f_gates(i, rv, pv):
    (xc,), (w_ra, b_ra, w_rx, b_rx, lam) = rv, pv
    tm = xc.shape[0]
    ra, rx = [], []
    for g in range(B_BLOCKS):
        xg = xc[:, g * B_BLOCK:(g + 1) * B_BLOCK].astype(BF16)
        ra.append(jnp.dot(xg, w_ra[g].astype(BF16), preferred_element_type=F32))
        rx.append(jnp.dot(xg, w_rx[g].astype(BF16), preferred_element_type=F32))
    r = _sigmoid(jnp.concatenate(ra, axis=1) + b_ra)
    ig = _sigmoid(jnp.concatenate(rx, axis=1) + b_rx)
    log_a = -LRU_C * r * _softplus(-lam)
    a = jnp.exp(log_a)
    u = jnp.sqrt(1.0 - jnp.exp(2.0 * log_a)) * (ig * xc)
    return [a, jnp.where(_rowmask(i, tm), u, 0.0)]


def _f_bout(i, rv, pv):
    hs, gb = rv
    gelu = 0.5 * gb * (1.0 + jnp.tanh(0.7978845608028654 * (gb + 0.044715 * gb * gb * gb)))
    return [hs * gelu]


def _f_cnorm(i, rv, pv):
    (cq, ckv), (gq, gkv) = rv, pv
    return [cq * lax.rsqrt(jnp.mean(cq * cq, axis=-1, keepdims=True) + EPS) * gq,
            ckv * lax.rsqrt(jnp.mean(ckv * ckv, axis=-1, keepdims=True) + EPS) * gkv]


def _rope128(x, cos, sin):
    lane = _lane()
    rot = jnp.where((lane >= C_NOPE) & (lane < C_NOPE + C_ROPE // 2), -_lroll(x, -(C_ROPE // 2)),
                    jnp.where((lane >= C_NOPE + C_ROPE // 2) & (lane < C_NOPE + C_ROPE), _lroll(x, C_ROPE // 2), 0.0))
    return x * cos + rot * sin


def _f_mlaprep(i, rv, pv):
    (q_, kk_, misc, cos, sin), (gq, gk) = rv, pv
    lane = _lane()
    m_n = lane < C_NOPE
    m_r = (lane >= C_NOPE) & (lane < C_NOPE + C_ROPE)

    def norm2(x, g):
        x2 = x * x
        sn = jnp.sum(jnp.where(m_n, x2, 0.0), axis=-1, keepdims=True) * (1.0 / C_NOPE)
        sr = jnp.sum(jnp.where(m_r, x2, 0.0), axis=-1, keepdims=True) * (1.0 / C_ROPE)
        scale = jnp.where(m_n, lax.rsqrt(sn + EPS), jnp.where(m_r, lax.rsqrt(sr + EPS), 0.0))
        return x * scale * g

    kr = _rope128(norm2(jnp.where(m_r, misc, 0.0), gk), cos, sin)
    qs, ks = [], []
    for h in range(C_HEADS):
        qs.append(_rope128(norm2(q_[:, h * HP:(h + 1) * HP], gq), cos, sin))
        ks.append(norm2(jnp.where(m_n, kk_[:, h * HP:(h + 1) * HP], 0.0), gk) + kr)
    return [jnp.concatenate(qs, axis=1), jnp.concatenate(ks, axis=1)]


def _f_foxprep(i, rv, pv):
    (fq, fk, misc), (gq, gk, bf) = rv, pv
    tm = fq.shape[0]
    lane = _lane()

    def hnorm(x, g):
        outs = []
        for h in range(D_HEADS):
            xh = x[:, h * HP:(h + 1) * HP]
            ss = jnp.sum(xh * xh, axis=-1, keepdims=True) * (1.0 / D_HD)
            outs.append(xh * lax.rsqrt(ss + EPS) * g)
        return jnp.concatenate(outs, axis=1)

    lf = jnp.where(_rowmask(i, tm) & (lane >= 96) & (lane < 96 + D_HEADS), _log_sigmoid(misc + bf), 0.0)
    return [hnorm(fq, gq), hnorm(fk, gk), lf]


def _mlstm_chunk(c, n, m, q, k, v, li, lf):
    ln = q.shape[0]
    r = lax.broadcasted_iota(jnp.int32, (ln, ln), 0)
    cc = lax.broadcasted_iota(jnp.int32, (ln, ln), 1)
    causal = cc <= r
    eye = cc == r
    li_row = jnp.sum(jnp.where(eye, li, 0.0), axis=0, keepdims=True)
    b_col = jnp.sum(jnp.where(causal, jnp.sum(jnp.where(eye, lf, 0.0), axis=0, keepdims=True), 0.0),
                    axis=1, keepdims=True)
    b_row = jnp.sum(jnp.where(r <= cc, lf, 0.0), axis=0, keepdims=True)
    k = k * (A_DQK ** -0.5)
    qb, kb, vb = q.astype(BF16), k.astype(BF16), v.astype(BF16)
    dmat = jnp.where(causal, b_col - b_row + li_row, NEG)
    inter = b_col + m
    m_row = jnp.maximum(inter, jnp.max(dmat, axis=1, keepdims=True))
    w_intra = jnp.exp(dmat - m_row)
    w_inter = jnp.exp(inter - m_row)
    s = lax.dot_general(qb, kb, (((1,), (1,)), ((), ())), preferred_element_type=F32) * w_intra
    num = (w_inter * jnp.dot(qb, c.astype(BF16), preferred_element_type=F32)
           + jnp.dot(s.astype(BF16), vb, preferred_element_type=F32))
    den = w_inter * jnp.sum(q * n, axis=1, keepdims=True) + jnp.sum(s, axis=1, keepdims=True)
    h = num / jnp.maximum(jnp.abs(den), jnp.exp(-m_row))
    g = jnp.sum(lf, axis=0, keepdims=True)
    dk = g - b_col + li
    m_new = jnp.maximum(g + m, jnp.max(dk, axis=0, keepdims=True))
    wk = jnp.exp(dk - m_new)
    sc = jnp.exp(g + m - m_new)
    kw = wk * k
    c_new = sc * c + lax.dot_general(kw.astype(BF16), vb, (((0,), (0,)), ((), ())), preferred_element_type=F32)
    n_new = sc * n + jnp.sum(kw, axis=0, keepdims=True)
    return c_new, n_new, m_new, h


A_QKV = 2 * A_HEADS * A_DQK + A_HEADS * A_DV


def _mlstm_chunk_all(cs, ns, ms, zqkv, gates):
    lane = _lane()
    c2, n2, m2, hs = [], [], [], []
    for h in range(A_HEADS):
        q = zqkv[:, h * A_DQK:(h + 1) * A_DQK]
        k = zqkv[:, (A_HEADS + h) * A_DQK:(A_HEADS + h + 1) * A_DQK]
        v = zqkv[:, 2 * A_HEADS * A_DQK + h * A_DV:2 * A_HEADS * A_DQK + (h + 1) * A_DV]
        li = jnp.sum(jnp.where(lane == h, gates, 0.0), axis=1, keepdims=True)
        lf = jnp.sum(jnp.where(lane == A_HEADS + h, gates, 0.0), axis=1, keepdims=True)
        c, n, m, hh = _mlstm_chunk(cs[h], ns[h], ms[h], q, k, v, li, lf)
        c2.append(c)
        n2.append(n)
        m2.append(m)
        hs.append(hh)
    return c2, n2, m2, jnp.concatenate(hs, axis=1)


def _mlstm_state_specs(nc, index):
    return ([pl.BlockSpec((1, A_HEADS, A_DQK, A_DV), lambda j: (index(j), 0, 0, 0)),
             pl.BlockSpec((1, A_HEADS, 1, A_DQK), lambda j: (index(j), 0, 0, 0)),
             pl.BlockSpec((1, A_HEADS, 1, 1), lambda j: (index(j), 0, 0, 0))],
            [jax.ShapeDtypeStruct((nc, A_HEADS, A_DQK, A_DV), F32),
             jax.ShapeDtypeStruct((nc, A_HEADS, 1, A_DQK), F32),
             jax.ShapeDtypeStruct((nc, A_HEADS, 1, 1), F32)],
            [pltpu.VMEM((A_HEADS, A_DQK, A_DV), F32), pltpu.VMEM((A_HEADS, 1, A_DQK), F32),
             pltpu.VMEM((A_HEADS, 1, 1), F32)])


def _mlstm_fwd(z, gates, comm=None):
    tp = z.shape[0]
    nc = tp // A_CHUNK
    ln = A_CHUNK
    heads = range(A_HEADS)

    def body(z_ref, g_ref, h_ref, cs_ref, ns_ref, ms_ref, c_s, n_s, m_s):
        @pl.when(pl.program_id(0) == 0)
        def _():
            c_s[...] = jnp.zeros_like(c_s)
            n_s[...] = jnp.zeros_like(n_s)
            m_s[...] = jnp.zeros_like(m_s)

        cs_ref[0] = c_s[...]
        ns_ref[0] = n_s[...]
        ms_ref[0] = m_s[...]
        c2, n2, m2, h = _mlstm_chunk_all([c_s[i] for i in heads], [n_s[i] for i in heads], [m_s[i] for i in heads],
                                         z_ref[...], g_ref[...])
        for i in heads:
            c_s[i] = c2[i]
            n_s[i] = n2[i]
            m_s[i] = m2[i]
        h_ref[...] = h

    st_specs, st_shapes, st_scratch = _mlstm_state_specs(nc, lambda j: j)
    return _pcall(
        body, "mlstm_fwd", (nc,),
        [pl.BlockSpec((ln, A_QKV), lambda j: (j, 0)), pl.BlockSpec((ln, HP), lambda j: (j, 0))],
        [pl.BlockSpec((ln, A_HEADS * A_DV), lambda j: (j, 0))] + st_specs,
        [jax.ShapeDtypeStruct((tp, A_HEADS * A_DV), F32)] + st_shapes,
        st_scratch, ("arbitrary",), (z, gates), comm)


def _mlstm_bwd(z, gates, cs, ns, ms, dh, comm=None):
    tp = z.shape[0]
    nc = tp // A_CHUNK
    ln = A_CHUNK
    heads = range(A_HEADS)

    def body(z_ref, g_ref, cs_ref, ns_ref, ms_ref, dh_ref, dz_ref, dg_ref, dc_s, dn_s, dm_s):
        @pl.when(pl.program_id(0) == 0)
        def _():
            dc_s[...] = jnp.zeros_like(dc_s)
            dn_s[...] = jnp.zeros_like(dn_s)
            dm_s[...] = jnp.zeros_like(dm_s)

        _, vjp = jax.vjp(_mlstm_chunk_all, [cs_ref[0, i] for i in heads], [ns_ref[0, i] for i in heads],
                         [ms_ref[0, i] for i in heads], z_ref[...], g_ref[...])
        dc, dn, dm, dz, dg = vjp(([dc_s[i] for i in heads], [dn_s[i] for i in heads], [dm_s[i] for i in heads],
                                  dh_ref[...]))
        for i in heads:
            dc_s[i] = dc[i]
            dn_s[i] = dn[i]
            dm_s[i] = dm[i]
        dz_ref[...] = dz.astype(BF16)
        dg_ref[...] = dg

    def rj(j):
        return nc - 1 - j

    st_specs, _, st_scratch = _mlstm_state_specs(nc, rj)
    return _pcall(
        body, "mlstm_bwd", (nc,),
        [pl.BlockSpec((ln, A_QKV), lambda j: (rj(j), 0)), pl.BlockSpec((ln, HP), lambda j: (rj(j), 0))] + st_specs
        + [pl.BlockSpec((ln, A_HEADS * A_DV), lambda j: (rj(j), 0))],
        [pl.BlockSpec((ln, A_QKV), lambda j: (rj(j), 0)), pl.BlockSpec((ln, HP), lambda j: (rj(j), 0))],
        [jax.ShapeDtypeStruct((tp, A_QKV), BF16), jax.ShapeDtypeStruct((tp, HP), F32)],
        st_scratch, ("arbitrary",), (z, gates, cs, ns, ms, dh), comm)


def _shift_down(x, s, row):
    return x if s == 0 else jnp.where(row >= s, pltpu.roll(x, s, 0), 0.0)


def _shift_up(x, s, row):
    n = x.shape[0]
    return x if s == 0 else jnp.where(row < n - s, pltpu.roll(x, n - s, 0), 0.0)


def _conv_fwd(z, xcb, w, b):
    tp, c = z.shape[0], w.shape[1]
    ct = 256

    def body(x_ref, w_ref, b_ref, o_ref):
        x = x_ref[...]
        row = lax.broadcasted_iota(jnp.int32, (tp, 1), 0)
        acc = jnp.zeros_like(x) + b_ref[...]
        for k in range(CONV_W):
            acc = acc + w_ref[k:k + 1, :] * _shift_down(x, CONV_W - 1 - k, row)
        o_ref[...] = acc

    return pl.pallas_call(
        body, name="conv_fwd", grid=(c // ct,),
        in_specs=[pl.BlockSpec((tp, ct), lambda j: (0, xcb + j)), pl.BlockSpec((CONV_W, ct), lambda j: (0, j)),
                  pl.BlockSpec((1, ct), lambda j: (0, j))],
        out_specs=pl.BlockSpec((tp, ct), lambda j: (0, j)),
        out_shape=jax.ShapeDtypeStruct((tp, c), F32),
        compiler_params=_cparams(("parallel",)),
    )(z, w, b)


def _conv_bwd(z, xcb, w, dxc):
    tp, c = z.shape[0], w.shape[1]
    ct = 256

    def body(x_ref, w_ref, d_ref, dx_ref, dw_ref, db_ref):
        x, d = x_ref[...], d_ref[...]
        row = lax.broadcasted_iota(jnp.int32, (tp, 1), 0)
        acc = jnp.zeros_like(x)
        for k in range(CONV_W):
            s = CONV_W - 1 - k
            acc = acc + w_ref[k:k + 1, :] * _shift_up(d, s, row)
            dw_ref[k:k + 1, :] = jnp.sum(d * _shift_down(x, s, row), axis=0, keepdims=True)
        dx_ref[...] = acc.astype(BF16)
        db_ref[...] = jnp.sum(d, axis=0, keepdims=True)

    return pl.pallas_call(
        body, name="conv_bwd", grid=(c // ct,),
        in_specs=[pl.BlockSpec((tp, ct), lambda j: (0, xcb + j)), pl.BlockSpec((CONV_W, ct), lambda j: (0, j)),
                  pl.BlockSpec((tp, ct), lambda j: (0, j))],
        out_specs=[pl.BlockSpec((tp, ct), lambda j: (0, j)), pl.BlockSpec((CONV_W, ct), lambda j: (0, j)),
                   pl.BlockSpec((1, ct), lambda j: (0, j))],
        out_shape=[jax.ShapeDtypeStruct((tp, c), BF16), jax.ShapeDtypeStruct((CONV_W, c), F32),
                   jax.ShapeDtypeStruct((1, c), F32)],
        compiler_params=_cparams(("parallel",)),
    )(z, w, dxc)


def _scan_fwd(a, u, name):
    tp, c = a.shape
    ct = _pick(c, (256, 128))
    lb = SCAN_BLOCK
    nb = tp // lb

    def body(a_ref, u_ref, h_ref, hp_ref):
        row = lax.broadcasted_iota(jnp.int32, (lb, 1), 0)

        def blk(j, carry):
            r0 = pl.multiple_of(j * lb, lb)
            aa, uu = a_ref[pl.ds(r0, lb), :], u_ref[pl.ds(r0, lb), :]
            s = 1
            while s < lb:
                mk = row >= s
                uu = jnp.where(mk, aa * pltpu.roll(uu, s, 0) + uu, uu)
                aa = jnp.where(mk, aa * pltpu.roll(aa, s, 0), aa)
                s *= 2
            hh = uu + aa * carry
            h_ref[pl.ds(r0, lb), :] = hh
            hp_ref[pl.ds(r0, lb), :] = jnp.where(row >= 1, pltpu.roll(hh, 1, 0), carry)
            return hh[lb - 1:lb, :]

        lax.fori_loop(0, nb, blk, jnp.zeros((1, ct), F32))

    spec = pl.BlockSpec((tp, ct), lambda j: (0, j))
    return pl.pallas_call(
        body, name=name, grid=(c // ct,), in_specs=[spec, spec], out_specs=[spec, spec],
        out_shape=[jax.ShapeDtypeStruct((tp, c), F32)] * 2,
        compiler_params=_cparams(("parallel",)),
    )(a, u)


def _scan_bwd(a, hprev, dh, name):
    tp, c = a.shape
    ct = _pick(c, (256, 128))
    lb = SCAN_BLOCK
    nb = tp // lb

    def body(a_ref, hp_ref, dh_ref, du_ref, da_ref):
        row = lax.broadcasted_iota(jnp.int32, (lb, 1), 0)

        def blk(jj, carry):
            g_next, a_next = carry
            r0 = pl.multiple_of((nb - 1 - jj) * lb, lb)
            a_blk = a_ref[pl.ds(r0, lb), :]
            aa = jnp.where(row < lb - 1, pltpu.roll(a_blk, lb - 1, 0), a_next)
            gg = dh_ref[pl.ds(r0, lb), :]
            s = 1
            while s < lb:
                mk = row < lb - s
                gg = jnp.where(mk, aa * pltpu.roll(gg, lb - s, 0) + gg, gg)
                aa = jnp.where(mk, aa * pltpu.roll(aa, lb - s, 0), aa)
                s *= 2
            gg = gg + aa * g_next
            du_ref[pl.ds(r0, lb), :] = gg
            da_ref[pl.ds(r0, lb), :] = gg * hp_ref[pl.ds(r0, lb), :]
            return gg[0:1, :], a_blk[0:1, :]

        lax.fori_loop(0, nb, blk, (jnp.zeros((1, ct), F32), jnp.zeros((1, ct), F32)))

    spec = pl.BlockSpec((tp, ct), lambda j: (0, j))
    return pl.pallas_call(
        body, name=name, grid=(c // ct,), in_specs=[spec, spec, spec], out_specs=[spec, spec],
        out_shape=[jax.ShapeDtypeStruct((tp, c), F32)] * 2,
        compiler_params=_cparams(("parallel",)),
    )(a, hprev, dh)


ATT_HEADS_PER_STEP = 2
LOG2E = 1.4426950408889634
LN2 = 0.6931471805599453


FOX_LANE0 = 96


def _head_lane(f, head):
    return jnp.sum(jnp.where(_lane() == FOX_LANE0 + head, f, 0.0), axis=1, keepdims=True)


def _diag_valid(i, tq):
    r = lax.broadcasted_iota(jnp.int32, (tq, tq), 0)
    c = lax.broadcasted_iota(jnp.int32, (tq, tq), 1)
    return (c <= r) & ((i * tq + c >= PAD) | (c == r))


def _key_terms(tp, tq, bk=None):
    pad_neg = jnp.where(jnp.arange(tp) < PAD, NEG, 0.0).astype(F32).reshape(1, tp // tq, 1, tq)
    if bk is None:
        return pad_neg, jnp.zeros_like(pad_neg)
    kd = -bk.reshape(bk.shape[0], tp // tq, 1, tq)
    return kd + pad_neg, kd


def _attn_fwd(q, k, v, vcb, scale, bq, kterms, name, comm=None):
    tp = q.shape[0]
    nh = q.shape[1] // HP
    tq = _pick(tp, (384, 256, 128))
    has_bq = bq is not None
    c1 = scale * LOG2E
    per_head = kterms[0].shape[0] > 1
    hg = ATT_HEADS_PER_STEP
    lanes = [slice(hh * HP, (hh + 1) * HP) for hh in range(hg)]

    def body(*refs):
        if has_bq:
            q_ref, k_ref, v_ref, kb_ref, kd_ref, bq_ref, o_ref, lse_ref = refs
            rb = [_head_lane(bq_ref[...], hg * pl.program_id(0) + hh) * LOG2E for hh in range(hg)]
        else:
            q_ref, k_ref, v_ref, kb_ref, kd_ref, o_ref, lse_ref = refs
        i = pl.program_id(1)
        qb = [q_ref[:, ln].astype(BF16) for ln in lanes]

        def tile(j, carry, diag):
            r0 = pl.multiple_of(j * tq, tq)
            qk = [lax.dot_general(qb[hh], k_ref[pl.ds(r0, tq), lanes[hh]].astype(BF16), (((1,), (1,)), ((), ())),
                                  preferred_element_type=F32) for hh in range(hg)]
            out = []
            for hh in range(hg):
                m, l, acc = carry[hh]
                vb = v_ref[pl.ds(r0, tq), lanes[hh]].astype(BF16)
                kt = (kd_ref if diag else kb_ref)[hh if per_head else 0, j] * LOG2E
                x = qk[hh] * c1 + kt
                if has_bq:
                    x = x + rb[hh]
                if diag:
                    x = jnp.where(_diag_valid(i, tq), x, NEG)
                m2 = jnp.maximum(m, jnp.max(x, axis=1, keepdims=True))
                alpha = jnp.exp2(m - m2)
                p = jnp.exp2(x - m2)
                l2 = alpha * l + jnp.sum(p, axis=1, keepdims=True)
                acc2 = alpha * acc + jnp.dot(p.astype(BF16), vb, preferred_element_type=F32)
                out.append((m2, l2, acc2))
            return tuple(out)

        init = tuple((jnp.full((tq, 1), NEG, F32), jnp.zeros((tq, 1), F32), jnp.zeros((tq, HP), F32))
                     for _ in range(hg))
        res = tile(i, lax.fori_loop(0, i, lambda j, c: tile(j, c, False), init), True)
        for hh, (m, l, acc) in enumerate(res):
            o_ref[:, lanes[hh]] = acc / l
            lse_ref[hh] = m * LN2 + jnp.log(l)

    kt_spec = pl.BlockSpec((hg if per_head else 1, tp // tq, 1, tq),
                           (lambda h, i: (h, 0, 0, 0)) if per_head else (lambda h, i: (0, 0, 0, 0)))
    in_specs = [pl.BlockSpec((tq, hg * HP), lambda h, i: (i, h)), pl.BlockSpec((tp, hg * HP), lambda h, i: (0, h)),
                pl.BlockSpec((tp, hg * HP), lambda h, i: (0, vcb // hg + h)), kt_spec, kt_spec]
    ins = [q, k, v, *kterms]
    if has_bq:
        in_specs += [pl.BlockSpec((tq, HP), lambda h, i: (i, 0))]
        ins += [bq]
    return _pcall(
        body, name, (nh // hg, tp // tq), in_specs,
        [pl.BlockSpec((tq, hg * HP), lambda h, i: (i, h)), pl.BlockSpec((hg, tq, 1), lambda h, i: (h, i, 0))],
        [jax.ShapeDtypeStruct((tp, nh * HP), F32), jax.ShapeDtypeStruct((nh, tp, 1), F32)],
        [], ("parallel", "parallel"), ins, comm)


def _attn_bwd(q, k, v, vcb, o, lse, do, docb, scale, bq, kterms, name, comm=None):
    tp = q.shape[0]
    nh = q.shape[1] // HP
    tq = _pick(tp, (384, 256, 128))
    has_bq = bq is not None
    c1 = scale * LOG2E
    per_head = kterms[0].shape[0] > 1
    hg = ATT_HEADS_PER_STEP
    lanes = [slice(hh * HP, (hh + 1) * HP) for hh in range(hg)]

    def body(*refs):
        if has_bq:
            (q_ref, k_ref, v_ref, o_ref, lse_ref, do_ref, kb_ref, kd_ref, bq_ref,
             dq_ref, dk_ref, dv_ref, dbq_ref, dkt_ref) = refs
            rb = [(_head_lane(bq_ref[...], hg * pl.program_id(0) + hh) - lse_ref[hh]) * LOG2E for hh in range(hg)]
        else:
            q_ref, k_ref, v_ref, o_ref, lse_ref, do_ref, kb_ref, kd_ref, dq_ref, dk_ref, dv_ref = refs
            rb = [lse_ref[hh] * (-LOG2E) for hh in range(hg)]
        i = pl.program_id(1)

        @pl.when(i == 0)
        def _():
            dk_ref[...] = jnp.zeros_like(dk_ref)
            dv_ref[...] = jnp.zeros_like(dv_ref)
            if has_bq:
                dkt_ref[...] = jnp.zeros_like(dkt_ref)

        qb = [q_ref[:, ln].astype(BF16) for ln in lanes]
        do_ = [do_ref[:, ln] for ln in lanes]
        dob = [d.astype(BF16) for d in do_]
        delta = [jnp.sum(do_[hh] * o_ref[:, lanes[hh]], axis=1, keepdims=True) for hh in range(hg)]

        def tile(j, carry, diag):
            r0 = pl.multiple_of(j * tq, tq)
            kbs = [k_ref[pl.ds(r0, tq), lanes[hh]].astype(BF16) for hh in range(hg)]
            qk = [lax.dot_general(qb[hh], kbs[hh], (((1,), (1,)), ((), ())), preferred_element_type=F32)
                  for hh in range(hg)]
            dps = [lax.dot_general(dob[hh], v_ref[pl.ds(r0, tq), lanes[hh]].astype(BF16), (((1,), (1,)), ((), ())),
                                   preferred_element_type=F32) for hh in range(hg)]
            out = []
            for hh in range(hg):
                dq, dbq = carry[hh]
                kb = kbs[hh]
                kt = (kd_ref if diag else kb_ref)[hh if per_head else 0, j] * LOG2E
                x = qk[hh] * c1 + kt + rb[hh]
                if diag:
                    x = jnp.where(_diag_valid(i, tq), x, NEG)
                p = jnp.exp2(x)
                ds = p * (dps[hh] - delta[hh])
                dsb = ds.astype(BF16)
                dk_ref[pl.ds(r0, tq), lanes[hh]] += lax.dot_general(dsb, qb[hh], (((0,), (0,)), ((), ())),
                                                                    preferred_element_type=F32) * scale
                dv_ref[pl.ds(r0, tq), lanes[hh]] += lax.dot_general(p.astype(BF16), dob[hh], (((0,), (0,)), ((), ())),
                                                                    preferred_element_type=F32)
                if has_bq:
                    dkt_ref[hh, j] += jnp.sum(ds, axis=0, keepdims=True)
                    dbq = dbq + jnp.sum(ds, axis=1, keepdims=True)
                out.append((dq + jnp.dot(dsb, kb, preferred_element_type=F32), dbq))
            return tuple(out)

        init = tuple((jnp.zeros((tq, HP), F32), jnp.zeros((tq, 1), F32)) for _ in range(hg))
        res = tile(i, lax.fori_loop(0, i, lambda j, c: tile(j, c, False), init), True)
        for hh, (dq, dbq) in enumerate(res):
            dq_ref[:, lanes[hh]] = dq * scale
            if has_bq:
                dbq_ref[hh] = dbq

    blk_q = pl.BlockSpec((tq, hg * HP), lambda h, i: (i, h))
    blk_k = pl.BlockSpec((tp, hg * HP), lambda h, i: (0, h))
    kt_spec = pl.BlockSpec((hg if per_head else 1, tp // tq, 1, tq),
                           (lambda h, i: (h, 0, 0, 0)) if per_head else (lambda h, i: (0, 0, 0, 0)))
    row_spec = pl.BlockSpec((hg, tq, 1), lambda h, i: (h, i, 0))
    in_specs = [blk_q, blk_k, pl.BlockSpec((tp, hg * HP), lambda h, i: (0, vcb // hg + h)), blk_q,
                row_spec, pl.BlockSpec((tq, hg * HP), lambda h, i: (i, docb // hg + h)), kt_spec, kt_spec]
    ins = [q, k, v, o, lse, do, *kterms]
    out_specs = [blk_q, blk_k, blk_k]
    out_shape = [jax.ShapeDtypeStruct((tp, nh * HP), F32)] * 3
    if has_bq:
        in_specs += [pl.BlockSpec((tq, HP), lambda h, i: (i, 0))]
        ins += [bq]
        out_specs += [row_spec, kt_spec]
        out_shape += [jax.ShapeDtypeStruct((nh, tp, 1), F32), jax.ShapeDtypeStruct((nh, tp // tq, 1, tq), F32)]
    return _pcall(body, name, (nh // hg, tp // tq), in_specs, out_specs, out_shape, [], ("parallel", "arbitrary"), ins,
                  comm)


def _loss_head(h, tgt):
    tp, d = h.shape
    tm = 128
    first = (PAD + N_META) // tm

    def body(h_ref, t_ref, l_ref, d_ref):
        i = pl.program_id(0)

        @pl.when(i == 0)
        def _():
            l_ref[...] = jnp.zeros_like(l_ref)

        live = i >= first
        err = jnp.where(live, h_ref[...] - t_ref[...], 0.0)
        d_ref[...] = err * (1.0 / d)
        l_ref[...] += (0.5 / d) * jnp.sum(err * err)

    return pl.pallas_call(
        body, name="loss_head", grid=(tp // tm,),
        in_specs=[pl.BlockSpec((tm, d), lambda i: (i, 0)), pl.BlockSpec((tm, d), lambda i: (i, 0))],
        out_specs=[_full_spec((8, 128)), pl.BlockSpec((tm, d), lambda i: (i, 0))],
        out_shape=[jax.ShapeDtypeStruct((8, 128), F32), jax.ShapeDtypeStruct((tp, d), F32)],
        compiler_params=_cparams(("arbitrary",)),
    )(h, tgt)


def _rope_tables(pos_rows):
    half = C_ROPE // 2
    freqs = ROPE_THETA ** (-jnp.arange(half, dtype=F32) / half)
    ang = pos_rows[:, None].astype(F32) * freqs
    cos, sin = jnp.cos(ang), jnp.sin(ang)
    tp = pos_rows.shape[0]
    one, zero = jnp.ones((tp, C_NOPE), F32), jnp.zeros((tp, C_NOPE), F32)
    tail1, tail0 = jnp.ones((tp, HP - C_NOPE - C_ROPE), F32), jnp.zeros((tp, HP - C_NOPE - C_ROPE), F32)
    return (jnp.concatenate([one, cos, cos, tail1], axis=1), jnp.concatenate([zero, sin, sin, tail0], axis=1))


def _heads_to_cols(x, lo, n):
    t = x[:, lo:lo + n].T
    return t[:, :, None], t[:, None, :]


def _local_step(x, positions, tgt, w, plan=None):
    def hosted(tag, fn, *args, **kw):
        ops = plan.ride(tag, g) if plan is not None else None
        if not ops:
            return fn(*args, **kw)
        res, got = fn(*args, comm=ops, **kw)
        plan.arrived(tag, got, w)
        return res

    s_len = x.shape[0]
    tp = PAD + N_META + s_len
    tm = _pick(tp, (384, 256, 128))
    front = PAD + N_META
    h0 = jnp.concatenate([jnp.zeros((PAD, D_MODEL), F32), w["meta"], x], axis=0)
    tgt_p = jnp.concatenate([jnp.zeros((front, D_MODEL), F32), tgt], axis=0)
    pos_rows = jnp.concatenate([jnp.zeros((PAD,), jnp.int32), jnp.arange(N_META, dtype=jnp.int32),
                                positions + N_META])
    cos, sin = _rope_tables(pos_rows)
    g = {}

    r_h0 = [(h0, D_MODEL, 0)]
    (xn0,) = _row_fwd("norm0_f", _f_norm, r_h0, [w["ev_ln"]], [(D_MODEL, BF16)], tm)
    z0 = hosted("mm_z0", _mm, xn0, w["ev_w_in"], "nn", "mm_z0")
    r_misc0 = [(z0, HP, 5120 // HP)]
    (g0,) = _row_fwd("gate0_f", _f_gate0, r_misc0, [w["ev_b_if"]], [(HP, F32)], tm)
    h_a, cs, ns, ms = hosted("mlstm_fwd", _mlstm_fwd, z0, g0)
    r_aout = [(h_a, 1024, 0), (z0, 1024, 2)]
    (ha,) = _row_fwd("aout_f", _f_aout, r_aout, [w["ev_a_norm"]], [(1024, BF16)], tm)
    xc = _conv_fwd(z0, 3072 // 256, w["ev_conv_w"], w["ev_conv_b"])
    p_gates = [w["ev_w_ra"], w["ev_b_ra"], w["ev_w_rx"], w["ev_b_rx"], w["ev_lam"]]
    a_g, u_g = _row_fwd("gates_f", _f_gates, [(xc, 1024, 0)], p_gates, [(1024, F32), (1024, F32)], tm)
    hs, hprev = _scan_fwd(a_g, u_g, "lru_f")
    r_bout = [(hs, 1024, 0), (z0, 1024, 4)]
    (hb,) = _row_fwd("bout_f", _f_bout, r_bout, [], [(1024, BF16)], tm)
    hab = _concat_cols([ha, hb], BF16, "cat_hab")
    h1 = _mm(hab, w["ev_w_out"], "nn", "mm_h1", add=h0)
    (xn1,) = _row_fwd("norm1_f", _f_norm, [(h1, D_MODEL, 0)], [w["mlp_ln0"]], [(D_MODEL, BF16)], tm)
    p0, act0 = _mm(xn1, w["w_ff1_0"], "nn", "mm_p0", relu2=True)
    h2 = _mm(act0, w["w_ff2_0"], "nn", "mm_h2", add=h1)
    (xn2,) = _row_fwd("norm2_f", _f_norm, [(h2, D_MODEL, 0)], [w["od_ln"]], [(D_MODEL, BF16)], tm)
    z1 = _mm(xn2, w["od_w_in"], "nn", "mm_z1")
    r_c = [(z1, C_Q_LORA, 3072 // C_Q_LORA), (z1, C_KV_LORA, 3584 // C_KV_LORA)]
    cqn, ckvn = _row_fwd("cnorm_f", _f_cnorm, r_c, [w["od_g_qa"], w["od_g_kva"]],
                         [(C_Q_LORA, BF16), (C_KV_LORA, BF16)], tm)
    q_ = _mm(cqn, w["od_w_uq"], "nn", "mm_q")
    kv_ = _mm(ckvn, w["od_w_ukv"], "nn", "mm_kv")
    r_mla = [(q_, 1024, 0), (kv_, 1024, 0), (z1, HP, 3840 // HP), (cos, HP, 0), (sin, HP, 0)]
    p_mla = [w["gq_full"], w["gk_full"]]
    qm, km = _row_fwd("mla_f", _f_mlaprep, r_mla, p_mla, [(1024, BF16), (1024, BF16)], tm)
    sc_c = (C_NOPE + C_ROPE) ** -0.5
    kt_c = _key_terms(tp, tm)
    hc, lse_c = hosted("mla_attn_f", _attn_fwd, qm, km, kv_, C_HEADS, sc_c, None, kt_c, "mla_attn_f")
    r_fox = [(z1, 1024, 0), (z1, 1024, 1), (z1, HP, 3840 // HP)]
    p_fox = [w["gfq_full"], w["gfk_full"], w["bf_full"]]
    qf, kf, lfx = _row_fwd("fox_f", _f_foxprep, r_fox, p_fox, [(1024, BF16), (1024, BF16), (HP, F32)], tm)
    ones = jnp.ones((tp, HP), F32)
    fcum, fprev = _scan_fwd(ones, lfx, "fcum_f")
    _, bk = _heads_to_cols(fcum, FOX_LANE0, D_HEADS)
    kt_d = _key_terms(tp, tm, bk)
    sc_d = D_HD ** -0.5
    bq = fcum
    hd, lse_d = _attn_fwd(qf, kf, z1, 2048 // HP, sc_d, bq, kt_d, "fox_attn_f")
    hcd = _concat_cols([hc, hd], F32, "cat_hcd")
    h3 = _mm(hcd, w["od_w_out"], "nn", "mm_h3", add=h2)
    (xn3,) = _row_fwd("norm3_f", _f_norm, [(h3, D_MODEL, 0)], [w["mlp_ln1"]], [(D_MODEL, BF16)], tm)
    p1, act1 = _mm(xn3, w["w_ff1_1"], "nn", "mm_p1", relu2=True)
    h4 = _mm(act1, w["w_ff2_1"], "nn", "mm_h4", add=h3)
    lpart, dh4 = _loss_head(h4, tgt_p)
    loss = lpart[0, 0]

    def mlp_bwd(tag, dh_out, h_in, xn, p, act, ln, w1, w2):
        dp = hosted(f"mm_dp{tag}", _mm, dh_out, w2, "nt", f"mm_dp{tag}", relu2_of=p)
        g_w2 = _mm(act, dh_out, "tn", f"mm_dw2_{tag}")
        g_w1 = _mm(xn, dp, "tn", f"mm_dw1_{tag}")
        dxn = _mm(dp, w1, "nt", f"mm_dxn{tag}")
        (dh_in,), (g_ln,) = _row_bwd(f"normm{tag}_b", _f_norm, [(h_in, D_MODEL, 0)], [ln], [(dxn, D_MODEL, 0)], tm,
                                     [True], add=dh_out)
        return dh_in, g_ln, g_w1, g_w2

    dh3, g["mlp_ln1"], g["w_ff1_1"], g["w_ff2_1"] = mlp_bwd("1", dh4, h3, xn3, p1, act1, w["mlp_ln1"],
                                                            w["w_ff1_1"], w["w_ff2_1"])
    dhcd = _mm(dh3, w["od_w_out"], "nt", "mm_dhcd")
    g["od_w_out"] = _mm(hcd, dh3, "tn", "mm_dwout1")
    dqf, dkf, dvf, dbq, dkt = hosted("fox_attn_b", _attn_bwd, qf, kf, z1, 2048 // HP, hd, lse_d, dhcd, D_HEADS, sc_d,
                                     bq, kt_d, "fox_attn_b")
    dfc = dbq[:, :, 0].T - dkt.reshape(D_HEADS, tp).T
    dfcum = jnp.concatenate([jnp.zeros((tp, 96), F32), dfc, jnp.zeros((tp, HP - 96 - D_HEADS), F32)], axis=1)
    dlfx, _ = _scan_bwd(ones, fprev, dfcum, "fcum_b")
    (dfq, dfk, dmisc_f), (g["gfq_full"], g["gfk_full"], g["bf_full"]) = _row_bwd(
        "fox_b", _f_foxprep, r_fox, p_fox, [(dqf, 1024, 0), (dkf, 1024, 0), (dlfx, HP, 0)], tm, [True, True, True],
        out_dtypes=[BF16, BF16, F32])
    dqm, dkm, dvm = _attn_bwd(qm, km, kv_, C_HEADS, hc, lse_c, dhcd, 0, sc_c, None, kt_c, "mla_attn_b")
    (dq_, dkk_, dmisc_m), (g["gq_full"], g["gk_full"]) = _row_bwd(
        "mla_b", _f_mlaprep, r_mla, p_mla, [(dqm, 1024, 0), (dkm, 1024, 0)], tm, [True, True, True, False, False],
        out_dtypes=[BF16, BF16, F32])
    dkv_ = _concat_cols([dkk_, dvm], BF16, "cat_dkv")
    dckvn = _mm(dkv_, w["od_w_ukv"], "nt", "mm_dckvn")
    g["od_w_ukv"] = _mm(ckvn, dkv_, "tn", "mm_dwukv")
    dcqn = _mm(dq_, w["od_w_uq"], "nt", "mm_dcqn")
    g["od_w_uq"] = _mm(cqn, dq_, "tn", "mm_dwuq")
    (dcq, dckv), (g["od_g_qa"], g["od_g_kva"]) = _row_bwd(
        "cnorm_b", _f_cnorm, r_c, [w["od_g_qa"], w["od_g_kva"]],
        [(dcqn, C_Q_LORA, 0), (dckvn, C_KV_LORA, 0)], tm, [True, True], out_dtypes=[BF16, BF16])
    dz1 = _concat_cols([dfq, dfk, dvf, dcq, HP, dckv, dmisc_f + dmisc_m, HP], BF16, "cat_dz1")
    g["od_w_in"] = _mm(xn2, dz1, "tn", "mm_dwin1")
    dxn2 = _mm(dz1, w["od_w_in"], "nt", "mm_dxn2")
    (dh2,), (g["od_ln"],) = _row_bwd("norm2_b", _f_norm, [(h2, D_MODEL, 0)], [w["od_ln"]], [(dxn2, D_MODEL, 0)], tm,
                                     [True], add=dh3)
    dh1, g["mlp_ln0"], g["w_ff1_0"], g["w_ff2_0"] = mlp_bwd("0", dh2, h1, xn1, p0, act0, w["mlp_ln0"],
                                                            w["w_ff1_0"], w["w_ff2_0"])
    dhab = _mm(dh1, w["ev_w_out"], "nt", "mm_dhab")
    g["ev_w_out"] = _mm(hab, dh1, "tn", "mm_dwout0")
    (dhs, dgb), _ = _row_bwd("bout_b", _f_bout, r_bout, [], [(dhab, 1024, 1)], tm, [True, True],
                             out_dtypes=[F32, BF16])
    du_g, da_g = _scan_bwd(a_g, hprev, dhs, "lru_b")
    (dxc,), (g["ev_w_ra"], g["ev_b_ra"], g["ev_w_rx"], g["ev_b_rx"], g["ev_lam"]) = hosted(
        "gates_b", _row_bwd, "gates_b", _f_gates, [(xc, 1024, 0)], p_gates, [(da_g, 1024, 0), (du_g, 1024, 0)], tm,
        [True])
    dxb, g["ev_conv_w"], g["ev_conv_b"] = _conv_bwd(z0, 3072 // 256, w["ev_conv_w"], dxc)
    (dh_a, do_), (g["ev_a_norm"],) = _row_bwd("aout_b", _f_aout, r_aout, [w["ev_a_norm"]], [(dhab, 1024, 0)], tm,
                                              [True, True], out_dtypes=[F32, BF16])
    dqkv, dg0 = hosted("mlstm_bwd", _mlstm_bwd, z0, g0, cs, ns, ms, dh_a)
    (dmisc0,), (g["ev_b_if"],) = _row_bwd("gate0_b", _f_gate0, r_misc0, [w["ev_b_if"]], [(dg0, HP, 0)], tm, [True],
                                          out_dtypes=[BF16])
    dz0 = _concat_cols([dqkv, do_, dxb, dgb, dmisc0, HP], BF16, "cat_dz0")
    g["ev_w_in"] = _mm(xn0, dz0, "tn", "mm_dwin0")
    dxn0 = hosted("mm_dxn0", _mm, dz0, w["ev_w_in"], "nt", "mm_dxn0")
    (dh0,), (g["ev_ln"],) = _row_bwd("norm0_b", _f_norm, r_h0, [w["ev_ln"]], [(dxn0, D_MODEL, 0)], tm, [True], add=dh1)
    g["meta"] = dh0[PAD:front]
    return loss, dh0[front:], g


def _pad_last(a, n):
    return jnp.pad(a, [(0, 0)] * (a.ndim - 1) + [(0, n - a.shape[-1])])


def _pad_heads(a, nh, d):
    return _pad_last(a.reshape(a.shape[:-1] + (nh, d)), HP).reshape(a.shape[:-1] + (nh * HP,))


def _unpad_heads(a, nh, d):
    return a.reshape(a.shape[:-1] + (nh, HP))[..., :d].reshape(a.shape[:-1] + (nh * d,))


_MM_UNITS = ("ev_w_in", "ev_w_out", "od_w_in", "od_w_uq", "od_w_ukv", "od_w_out",
             "w_ff1_0", "w_ff2_0", "w_ff1_1", "w_ff2_1")


def _mw_pad(name, a):
    if name == "ev_w_in":
        return _pad_last(jnp.concatenate([a[:, :3072], a[:, 3080:5128], a[:, 3072:3080]], axis=1), ZE)
    if name == "od_w_in":
        z = lambda n: jnp.zeros((a.shape[0], n), a.dtype)
        return jnp.concatenate(
            [_pad_heads(a[:, 672:1184], D_HEADS, D_HD), _pad_heads(a[:, 1184:1696], D_HEADS, D_HD),
             _pad_heads(a[:, 1696:2208], D_HEADS, D_HD), a[:, 0:384], z(128), a[:, 384:640],
             z(64), a[:, 640:672], a[:, 2208:2216], z(24), z(128)], axis=1)
    if name == "od_w_uq":
        return _pad_heads(a, C_HEADS, C_NOPE + C_ROPE)
    if name == "od_w_ukv":
        wkv = a.reshape(C_KV_LORA, C_HEADS, C_NOPE + C_V)
        return jnp.concatenate([_pad_last(wkv[:, :, :C_NOPE], HP).reshape(C_KV_LORA, -1),
                                _pad_last(wkv[:, :, C_NOPE:], HP).reshape(C_KV_LORA, -1)], axis=1)
    if name == "od_w_out":
        return jnp.pad(a.reshape(2 * C_HEADS, C_V, D_MODEL), ((0, 0), (0, HP - C_V), (0, 0))).reshape(-1, D_MODEL)
    return a


def _mw_unpad(name, g):
    if name == "ev_w_in":
        return jnp.concatenate([g[:, :3072], g[:, 5120:5128], g[:, 3072:5120]], axis=1)
    if name == "od_w_in":
        return jnp.concatenate(
            [g[:, 3072:3456], g[:, 3584:3840], g[:, 3904:3936], _unpad_heads(g[:, 0:1024], D_HEADS, D_HD),
             _unpad_heads(g[:, 1024:2048], D_HEADS, D_HD), _unpad_heads(g[:, 2048:3072], D_HEADS, D_HD),
             g[:, 3936:3944]], axis=1)
    if name == "od_w_uq":
        return _unpad_heads(g, C_HEADS, C_NOPE + C_ROPE)
    if name == "od_w_ukv":
        gk = g[:, :C_HEADS * HP].reshape(C_KV_LORA, C_HEADS, HP)[:, :, :C_NOPE]
        gv = g[:, C_HEADS * HP:].reshape(C_KV_LORA, C_HEADS, HP)[:, :, :C_V]
        return jnp.concatenate([gk, gv], axis=2).reshape(C_KV_LORA, -1)
    if name == "od_w_out":
        return g.reshape(2 * C_HEADS, HP, D_MODEL)[:, :C_V].reshape(-1, D_MODEL)
    return g


def _prep_weights(p):
    w = {}
    w["meta"] = p["meta"]
    for k in ("ev_ln", "ev_a_norm", "ev_conv_b", "ev_b_ra", "ev_b_rx", "ev_lam", "od_ln", "od_g_qa", "od_g_kva"):
        w[k] = p[k].reshape(1, -1)
    w["ev_b_if"] = _pad_last(p["ev_b_if"].reshape(1, -1), HP)
    w["ev_conv_w"] = p["ev_conv_w"][0]
    w["ev_w_ra"] = p["ev_w_ra"][0]
    w["ev_w_rx"] = p["ev_w_rx"][0]
    f1 = lambda a: a.reshape(1, -1)
    w["gq_full"] = _pad_last(jnp.concatenate([f1(p["od_g_qn"]), f1(p["od_g_qr"])], axis=1), HP)
    w["gk_full"] = _pad_last(jnp.concatenate([f1(p["od_g_kn"]), f1(p["od_g_kr"])], axis=1), HP)
    w["gfq_full"] = _pad_last(f1(p["od_g_fq"]), HP)
    w["gfk_full"] = _pad_last(f1(p["od_g_fk"]), HP)
    w["bf_full"] = _pad_last(jnp.concatenate([jnp.zeros((1, 96), F32), f1(p["od_b_f"])], axis=1), HP)
    for l in (0, 1):
        w[f"mlp_ln{l}"] = p["mlp_ln"][l:l + 1]
    for n in _MM_UNITS:
        if n in p:
            w[n] = _mw_pad(n, p[n])
    return w


def _unprep_grads(g):
    o = {}
    o["meta"] = g["meta"]
    for k in ("ev_ln", "ev_a_norm", "ev_conv_b", "ev_b_ra", "ev_b_rx", "ev_lam", "od_ln", "od_g_qa", "od_g_kva"):
        o[k] = g[k].reshape(1, -1)
    o["ev_b_if"] = g["ev_b_if"][:, :2 * A_HEADS]
    o["ev_conv_w"] = g["ev_conv_w"][None]
    o["ev_w_ra"] = g["ev_w_ra"][None]
    o["ev_w_rx"] = g["ev_w_rx"][None]
    o["od_g_qn"] = g["gq_full"][:, :C_NOPE]
    o["od_g_qr"] = g["gq_full"][:, C_NOPE:C_NOPE + C_ROPE]
    o["od_g_kn"] = g["gk_full"][:, :C_NOPE]
    o["od_g_kr"] = g["gk_full"][:, C_NOPE:C_NOPE + C_ROPE]
    o["od_g_fq"] = g["gfq_full"][:, :D_HD]
    o["od_g_fk"] = g["gfk_full"][:, :D_HD]
    o["od_b_f"] = g["bf_full"][:, 96:96 + D_HEADS]
    o["mlp_ln"] = jnp.concatenate([g["mlp_ln0"], g["mlp_ln1"]], axis=0)
    return o


def _exchange(ops, name):
    n_ops = len(ops)
    flags = [s for _, s in ops]

    def body(*refs):
        copies = _comm_copies(refs[:n_ops], refs[n_ops:2 * n_ops], flags, *refs[2 * n_ops:])
        _comm_begin(copies)
        _comm_end(copies)

    any_spec = pl.BlockSpec(memory_space=pl.ANY)
    c_shape, c_sems = _comm_shapes(ops)
    return pl.pallas_call(body, name=name, out_shape=c_shape, in_specs=[any_spec] * n_ops,
                          out_specs=[any_spec] * n_ops, scratch_shapes=c_sems)(*[x for x, _ in ops])


def _adamw(parts, w, m, v, name):
    r, c = w.shape
    tr = r
    for cand in (512, 256, 128, 64, 32, 16):
        if r % cand == 0 and N_DEV * cand * c * 4 <= 4 * 1024 * 1024:
            tr = cand
            break
    c1 = 1.0 / (1.0 - ADAM_B1 ** ADAM_STEP)
    c2 = 1.0 / (1.0 - ADAM_B2 ** ADAM_STEP)

    def body(p_ref, w_ref, m_ref, v_ref, g_ref, d_ref, mo_ref, vo_ref):
        g = p_ref[0].astype(F32)
        for j in range(1, N_DEV):
            g = g + p_ref[j].astype(F32)
        m2 = ADAM_B1 * m_ref[...] + (1.0 - ADAM_B1) * g
        v2 = ADAM_B2 * v_ref[...] + (1.0 - ADAM_B2) * (g * g)
        g_ref[...] = g
        mo_ref[...] = m2
        vo_ref[...] = v2
        d_ref[...] = -ADAM_LR * ((m2 * c1) / (jnp.sqrt(v2 * c2) + ADAM_EPS) + ADAM_WD * w_ref[...])

    spec = pl.BlockSpec((tr, c), lambda i: (i, 0))
    return pl.pallas_call(
        body, name=name, grid=(r // tr,),
        in_specs=[pl.BlockSpec((N_DEV, tr, c), lambda i: (0, i, 0)), spec, spec, spec],
        out_specs=[spec] * 4, out_shape=[jax.ShapeDtypeStruct((r, c), F32)] * 4,
        compiler_params=_cparams(("parallel",)),
    )(parts, w, m, v)


def _rows_for(n, mult):
    return -(-n // (1024 * mult)) * mult


def _pack(arrs, mult, lead=()):
    nl = len(lead)
    flat = jnp.concatenate([a.reshape(lead + (-1,)) for a in arrs], axis=nl)
    rows = _rows_for(flat.shape[-1], mult)
    return jnp.pad(flat, [(0, 0)] * nl + [(0, rows * 1024 - flat.shape[-1])]).reshape(lead + (rows, 1024))


def _unpack(buf, shapes):
    lead = buf.shape[:-2]
    flat = buf.reshape(lead + (-1,))
    out, off = [], 0
    for s in shapes:
        n = 1
        for d_ in s:
            n *= d_
        out.append(flat[..., off:off + n].reshape(lead + tuple(s)))
        off += n
    return out


def _unshard(g8, ax):
    a = jnp.moveaxis(g8, 0, ax)
    return a.reshape(a.shape[:ax] + (N_DEV * a.shape[ax + 1],) + a.shape[ax + 2:])


def _shard8(full, ax):
    s = full.shape
    return jnp.moveaxis(full.reshape(s[:ax] + (N_DEV, s[ax] // N_DEV) + s[ax + 1:]), ax, 0)


_NAMES = ["meta", "ev_ln", "ev_w_in", "ev_b_if", "ev_a_norm", "ev_conv_w", "ev_conv_b", "ev_w_ra", "ev_b_ra",
          "ev_w_rx", "ev_b_rx", "ev_lam", "ev_w_out", "od_ln", "od_w_in", "od_b_f", "od_g_qa", "od_g_kva",
          "od_w_uq", "od_w_ukv", "od_g_qn", "od_g_qr", "od_g_kn", "od_g_kr", "od_g_fq", "od_g_fk", "od_w_out",
          "mlp_ln", "w_ff1", "w_ff2"]
_SHARD_AXIS = {"meta": 1, "ev_w_in": 2, "ev_conv_w": 2, "ev_w_out": 1, "od_ln": 1, "od_w_in": 2, "od_g_qa": 1,
               "od_g_kva": 1, "od_w_uq": 2, "od_w_ukv": 2, "od_w_out": 1, "w_ff1": 2, "w_ff2": 1}
_MATMUL_WEIGHTS = ("ev_w_in", "ev_w_out", "od_w_in", "od_w_uq", "od_w_ukv", "od_w_out", "w_ff1", "w_ff2")
_BIG_REPL = ("ev_w_ra", "ev_w_rx")
_COL_SHARDED = ("ev_w_in", "od_w_in", "od_w_uq", "od_w_ukv", "w_ff1_0", "w_ff1_1")
_GATHER_ON = {"mm_z0": ("od_w_in", "od_w_uq", "od_w_ukv"), "mlstm_fwd": ("ev_w_out", "w_ff1_0", "w_ff2_0"),
              "mla_attn_f": ("od_w_out", "w_ff1_1", "w_ff2_1")}
_SCATTER_ON = {"fox_attn_b": ("w_ff2_1", "w_ff1_1", "od_w_out"),
               "mm_dp0": ("od_w_ukv", "od_w_uq", "od_w_in"), "gates_b": ("ev_w_out",),
               "mlstm_bwd": ("w_ff2_0", "w_ff1_0"), "mm_dxn0": ("ev_w_in",)}
_REPL_ON = "mlstm_bwd"


def _unit_of(d, n):
    return d[n[:-2]][int(n[-1])] if n.startswith("w_ff") else d[n][0]


def _unit_full(n, g8):
    return jnp.transpose(g8, (1, 0, 2)).reshape(g8.shape[1], -1) if n in _COL_SHARDED else g8.reshape(-1, g8.shape[2])


def _unit_slots(n, full, r, c):
    return full.reshape(r, N_DEV, c).transpose(1, 0, 2) if n in _COL_SHARDED else full.reshape(N_DEV, r, c)


class _Plan:
    def __init__(self, shards):
        self.shards = shards
        self.parts = {}

    def slots(self, n, g):
        r, c = self.shards[n].shape
        return _unit_slots(n, _mw_unpad(n, g[n]), r, c).astype(BF16)

    def ride(self, tag, g):
        if tag in _GATHER_ON:
            return [(self.shards[n].astype(BF16), False) for n in _GATHER_ON[tag]]
        ops = [(self.slots(n, g), True) for n in _SCATTER_ON.get(tag, ())]
        if tag == _REPL_ON:
            ops += [(g[n].reshape(-1, B_BLOCK).astype(BF16), False) for n in _BIG_REPL]
        return ops

    def arrived(self, tag, got, w):
        if tag in _GATHER_ON:
            for n, g8 in zip(_GATHER_ON[tag], got):
                w[n] = _mw_pad(n, _unit_full(n, g8))
        else:
            self.parts.update(zip(_SCATTER_ON[tag] + (_BIG_REPL if tag == _REPL_ON else ()), got))


def kernel(x, positions, meta, ev_ln, ev_w_in, ev_b_if, ev_a_norm, ev_conv_w, ev_conv_b, ev_w_ra, ev_b_ra, ev_w_rx, ev_b_rx, ev_lam, ev_w_out, od_ln, od_w_in, od_b_f, od_g_qa, od_g_kva, od_w_uq, od_w_ukv, od_g_qn, od_g_qr, od_g_kn, od_g_kr, od_g_fq, od_g_fk, od_w_out, mlp_ln, w_ff1, w_ff2, loss_target, m_meta, m_ev_ln, m_ev_w_in, m_ev_b_if, m_ev_a_norm, m_ev_conv_w, m_ev_conv_b, m_ev_w_ra, m_ev_b_ra, m_ev_w_rx, m_ev_b_rx, m_ev_lam, m_ev_w_out, m_od_ln, m_od_w_in, m_od_b_f, m_od_g_qa, m_od_g_kva, m_od_w_uq, m_od_w_ukv, m_od_g_qn, m_od_g_qr, m_od_g_kn, m_od_g_kr, m_od_g_fq, m_od_g_fk, m_od_w_out, m_mlp_ln, m_w_ff1, m_w_ff2, v_meta, v_ev_ln, v_ev_w_in, v_ev_b_if, v_ev_a_norm, v_ev_conv_w, v_ev_conv_b, v_ev_w_ra, v_ev_b_ra, v_ev_w_rx, v_ev_b_rx, v_ev_lam, v_ev_w_out, v_od_ln, v_od_w_in, v_od_b_f, v_od_g_qa, v_od_g_kva, v_od_w_uq, v_od_w_ukv, v_od_g_qn, v_od_g_qr, v_od_g_kn, v_od_g_kr, v_od_g_fq, v_od_g_fk, v_od_w_out, v_mlp_ln, v_w_ff1, v_w_ff2):
    given = dict(locals())
    wts = {n: given[n] for n in _NAMES}
    mom = {n: given["m_" + n] for n in _NAMES}
    var = {n: given["v_" + n] for n in _NAMES}
    small_sh = [n for n in _NAMES if n in _SHARD_AXIS and n not in _MATMUL_WEIGHTS]
    small_rp = [n for n in _NAMES if n not in _SHARD_AXIS and n not in _BIG_REPL]
    shp = {n: wts[n].shape for n in _NAMES}
    plan = _Plan({n: _unit_of(wts, n) for n in _MM_UNITS})

    got = _exchange([(plan.shards["ev_w_in"].astype(BF16), False), (_pack([wts[n] for n in small_sh], 8), False)],
                    "gather_first")
    p = {n: wts[n] for n in _NAMES if n not in _SHARD_AXIS}
    for n, a in zip(small_sh, _unpack(got[1], [shp[n] for n in small_sh])):
        p[n] = _unshard(a, _SHARD_AXIS[n])
    p["ev_w_in"] = _unit_full("ev_w_in", got[0])

    loss, gx, g = _local_step(x[0], positions[0], loss_target[0], _prep_weights(p), plan)
    grads = _unprep_grads(g)

    parts = _exchange([(_pack([_shard8(grads[n], _SHARD_AXIS[n]) for n in small_sh], 8, (N_DEV,)), True),
                       (_pack([grads[n].reshape(shp[n]) for n in small_rp], 8), False)], "exchange_last")

    res = {}
    unit_res = {n: _adamw(plan.parts[n], *[_unit_of(d, n) for d in (wts, mom, var)], f"adamw_{n}")
                for n in _MM_UNITS}
    for n in _MATMUL_WEIGHTS:
        if n.startswith("w_ff"):
            res[n] = [jnp.stack([unit_res[n + "_0"][k], unit_res[n + "_1"][k]]) for k in range(4)]
        else:
            res[n] = [r[None] for r in unit_res[n]]
    r4 = _adamw(parts[0], *[_pack([d[n] for n in small_sh], 8) for d in (wts, mom, var)], "adamw_small_sharded")
    for n, *four in zip(small_sh, *[_unpack(r, [shp[n] for n in small_sh]) for r in r4]):
        res[n] = four
    for n in _BIG_REPL:
        r4 = _adamw(plan.parts[n], *[d[n].reshape(-1, B_BLOCK) for d in (wts, mom, var)], f"adamw_{n}")
        res[n] = [r.reshape(shp[n]) for r in r4]
    r4 = _adamw(parts[1], *[_pack([d[n] for n in small_rp], 8) for d in (wts, mom, var)], "adamw_small_repl")
    for n, *four in zip(small_rp, *[_unpack(r, [shp[n] for n in small_rp]) for r in r4]):
        res[n] = four

    outs = [res[n][kind] for kind in range(4) for n in _NAMES]
    loss = lax.psum(loss, ("x", "y", "c"))
    return (loss, gx[None], *outs)
```

```python
import functools

import jax
import jax.numpy as jnp
from jax import lax
from jax.experimental import pallas as pl
from jax.experimental.pallas import tpu as pltpu

F32 = jnp.float32
BF16 = jnp.bfloat16

D_MODEL = 1024
N_META = 16
PAD = 112
EPS = 1e-6
NEG = -1e30
A_HEADS, A_DQK, A_DV = 4, 128, 256
A_CHUNK = 384
B_BLOCKS, B_BLOCK, CONV_W, LRU_C = 8, 128, 4, 8.0
C_HEADS, C_Q_LORA, C_KV_LORA, C_NOPE, C_ROPE, C_V = 8, 384, 256, 64, 32, 64
ROPE_THETA = 10000.0
D_HEADS, D_HD = 8, 64
HP = 128
ZE = 5376
ZO = 4096
N_DEV = 8
ADAM_LR, ADAM_B1, ADAM_B2, ADAM_EPS, ADAM_WD, ADAM_STEP = 0.001, 0.9, 0.999, 1e-08, 0.01, 10
VMEM_LIMIT = 56 * 1024 * 1024
SCAN_BLOCK = 128


def _pick(n, prefs):
    for p in prefs:
        if n % p == 0:
            return p
    return n


def _cparams(dims):
    return pltpu.CompilerParams(dimension_semantics=dims, vmem_limit_bytes=VMEM_LIMIT)


def _full_spec(shape):
    nd = len(shape)
    return pl.BlockSpec(shape, lambda *_: (0,) * nd)


def _me_and_peers():
    mx, my, mc = lax.axis_index("x"), lax.axis_index("y"), lax.axis_index("c")
    peers = []
    for k in range(1, N_DEV):
        px, py, pc = mx ^ ((k >> 2) & 1), my ^ ((k >> 1) & 1), mc ^ (k & 1)
        peers.append(((px, py, pc), 4 * px + 2 * py + pc))
    return 4 * mx + 2 * my + mc, peers


def _comm_copies(x_refs, o_refs, flags, send_sems, recv_sems, local_sems):
    n_peer = N_DEV - 1
    mx, my, mc = lax.axis_index("x"), lax.axis_index("y"), lax.axis_index("c")
    me = 4 * mx + 2 * my + mc
    sibling = (mx, my, 1 - mc)
    chips = [(mx ^ a, my ^ b) for a, b in ((0, 1), (1, 0), (1, 1))]
    _, peers = _me_and_peers()
    out = dict(first=[], landed=[], forward=[], remote=[], local=[])

    def copy(a, k, src, dst, to):
        return pltpu.make_async_remote_copy(src_ref=src, dst_ref=dst, send_sem=send_sems.at[a * n_peer + k],
                                            recv_sem=recv_sems.at[a * n_peer + k], device_id=to,
                                            device_id_type=pl.DeviceIdType.MESH)

    for a, scatter in enumerate(flags):
        x_ref, o_ref = x_refs[a], o_refs[a]
        out["local"].append(pltpu.make_async_copy(x_ref.at[me] if scatter else x_ref, o_ref.at[me], local_sems.at[a]))
        if scatter:
            out["first"] += [copy(a, k, x_ref.at[pid], o_ref.at[me], peer) for k, (peer, pid) in enumerate(peers)]
        else:
            out["first"].append(copy(a, 0, x_ref, o_ref.at[me], sibling))
            for t, (px, py) in enumerate(chips):
                far = copy(a, 1 + t, x_ref, o_ref.at[me], (px, py, mc))
                slot = o_ref.at[4 * px + 2 * py + mc]
                out["first"].append(far)
                out["landed"].append(far)
                out["forward"].append(copy(a, 4 + t, slot, slot, sibling))
    out["remote"] = out["first"] + out["forward"]
    return out


def _comm_begin(c):
    for cp in c["local"] + c["first"]:
        cp.start()


def _comm_end(c):
    for cp in c["landed"]:
        cp.wait_recv()
    for cp in c["forward"]:
        cp.start()
    for cp in c["remote"]:
        cp.wait_send()
    for cp in c["remote"]:
        if all(cp is not d for d in c["landed"]):
            cp.wait_recv()
    for cp in c["local"]:
        cp.wait()


def _comm_shapes(comm):
    n = len(comm)
    out_shape = [jax.ShapeDtypeStruct((N_DEV,) + x.shape[-2:], x.dtype) for x, _ in comm]
    sems = [pltpu.SemaphoreType.DMA((n * (N_DEV - 1),)), pltpu.SemaphoreType.DMA((n * (N_DEV - 1),)),
            pltpu.SemaphoreType.DMA((n,))]
    return out_shape, sems


def _pcall(body, name, grid, in_specs, out_specs, out_shape, scratch_shapes, dims, ins, comm=None):
    if not comm:
        return pl.pallas_call(body, name=name, grid=grid, in_specs=in_specs, out_specs=out_specs,
                              out_shape=out_shape, scratch_shapes=scratch_shapes,
                              compiler_params=_cparams(dims))(*ins)
    n_in, n_out, n = len(in_specs), len(out_specs), len(comm)
    flags = [s for _, s in comm]
    c_shape, c_sems = _comm_shapes(comm)

    def riding(*refs):
        cx = refs[n_in:n_in + n]
        co = refs[n_in + n + n_out:n_in + 2 * n + n_out]
        rest = refs[n_in + 2 * n + n_out:]
        sems = rest[len(rest) - 3:]
        ids = [pl.program_id(a) for a in range(len(grid))]
        first = functools.reduce(jnp.logical_and, [i == 0 for i in ids])
        last = functools.reduce(jnp.logical_and, [i == g - 1 for i, g in zip(ids, grid)])

        @pl.when(first)
        def _():
            _comm_begin(_comm_copies(cx, co, flags, *sems))

        body(*refs[:n_in], *refs[n_in + n:n_in + n + n_out], *rest[:len(rest) - 3])

        @pl.when(last)
        def _():
            _comm_end(_comm_copies(cx, co, flags, *sems))

    any_spec = pl.BlockSpec(memory_space=pl.ANY)
    res = pl.pallas_call(
        riding, name=name, grid=grid, in_specs=list(in_specs) + [any_spec] * n,
        out_specs=list(out_specs) + [any_spec] * n, out_shape=list(out_shape) + c_shape,
        scratch_shapes=list(scratch_shapes) + c_sems,
        compiler_params=_cparams(("arbitrary",) * len(grid)))(*ins, *[x for x, _ in comm])
    return list(res[:n_out]), list(res[n_out:])


def _mm(a, b, mode, name, out_dtype=None, add=None, relu2=False, relu2_of=None, comm=None):
    if out_dtype is None:
        out_dtype = BF16 if (mode == "tn" or relu2_of is not None) else F32
    if relu2_of is not None:
        add = relu2_of
    if mode == "nn":
        (m, k), n = a.shape, b.shape[1]
    elif mode == "nt":
        (m, k), n = a.shape, b.shape[0]
    else:
        (k, m), n = a.shape, b.shape[1]
    tm = _pick(m, (1408, 1024, 768, 512, 384, 256, 128))
    tn = _pick(n, (1024, 768, 512, 384, 256, 128))
    tk = _pick(k, (1408, 1024, 768, 512, 384, 256, 128))
    nk = k // tk
    if mode == "nn":
        a_spec = pl.BlockSpec((tm, tk), lambda i, j, q: (i, q))
        b_spec = pl.BlockSpec((tk, tn), lambda i, j, q: (q, j))
        dn = (((1,), (0,)), ((), ()))
    elif mode == "nt":
        a_spec = pl.BlockSpec((tm, tk), lambda i, j, q: (i, q))
        b_spec = pl.BlockSpec((tn, tk), lambda i, j, q: (j, q))
        dn = (((1,), (1,)), ((), ()))
    else:
        a_spec = pl.BlockSpec((tk, tm), lambda i, j, q: (q, i))
        b_spec = pl.BlockSpec((tk, tn), lambda i, j, q: (q, j))
        dn = (((0,), (0,)), ((), ()))
    o_spec = pl.BlockSpec((tm, tn), lambda i, j, q: (i, j))
    has_add = add is not None

    def body(*refs):
        a_ref, b_ref = refs[:2]
        add_ref = refs[2] if has_add else None
        o_refs, acc = refs[2 + has_add:-1], refs[-1]
        q = pl.program_id(2)

        @pl.when(q == 0)
        def _():
            acc[...] = jnp.zeros_like(acc)

        acc[...] += lax.dot_general(a_ref[...].astype(BF16), b_ref[...].astype(BF16), dn,
                                    preferred_element_type=F32)

        @pl.when(q == nk - 1)
        def _():
            r = acc[...]
            if relu2_of is not None:
                r = r * (2.0 * jnp.maximum(add_ref[...], 0.0))
            elif has_add:
                r = r + add_ref[...]
            o_refs[0][...] = r.astype(o_refs[0].dtype)
            if relu2:
                pos = jnp.maximum(r, 0.0)
                o_refs[1][...] = (pos * pos).astype(o_refs[1].dtype)

    ins = [a, b] + ([add] if has_add else [])
    in_specs = [a_spec, b_spec] + ([o_spec] if has_add else [])
    out_shape = [jax.ShapeDtypeStruct((m, n), out_dtype)] + ([jax.ShapeDtypeStruct((m, n), BF16)] if relu2 else [])
    res = _pcall(body, name, (m // tm, n // tn, nk), in_specs, [o_spec] * len(out_shape), out_shape,
                 [pltpu.VMEM((tm, tn), F32)], ("parallel", "parallel", "arbitrary"), ins, comm)
    outs = res[0] if comm else res
    outs = tuple(outs) if relu2 else outs[0]
    return (outs, res[1]) if comm else outs


def _row_specs(rows, tm):
    return [pl.BlockSpec((tm, w), functools.partial(lambda cb, i: (i, cb), cb)) for (_, w, cb) in rows]


def _row_fwd(name, f, rows, params, outs, tm):
    tp = rows[0][0].shape[0]
    nr, npar = len(rows), len(params)

    def body(*refs):
        i = pl.program_id(0)
        rv = [r[...] for r in refs[:nr]]
        pv = [r[...] for r in refs[nr:nr + npar]]
        res = f(i, rv, pv)
        for o_ref, r in zip(refs[nr + npar:], res):
            o_ref[...] = r.astype(o_ref.dtype)

    res = pl.pallas_call(
        body, name=name, grid=(tp // tm,),
        in_specs=_row_specs(rows, tm) + [_full_spec(p.shape) for p in params],
        out_specs=[pl.BlockSpec((tm, w), lambda i: (i, 0)) for (w, _) in outs],
        out_shape=[jax.ShapeDtypeStruct((tp, w), dt) for (w, dt) in outs],
        compiler_params=_cparams(("parallel",)),
    )(*[r[0] for r in rows], *params)
    return list(res)


def _row_bwd(name, f, rows, params, douts, tm, diff, add=None, out_dtypes=None, comm=None):
    tp = rows[0][0].shape[0]
    nr, npar, nd = len(rows), len(params), len(douts)
    didx = [k for k in range(nr) if diff[k]]
    has_add = add is not None

    def body(*refs):
        i = pl.program_id(0)
        rv = [r[...] for r in refs[:nr]]
        pv = [r[...] for r in refs[nr:nr + npar]]
        dv = [r[...] for r in refs[nr + npar:nr + npar + nd]]
        pos = nr + npar + nd
        add_ref = refs[pos] if has_add else None
        pos += 1 if has_add else 0
        dr_refs = refs[pos:pos + len(didx)]
        dp_refs = refs[pos + len(didx):]

        def g(drv, pvs):
            full = list(rv)
            for k, val in zip(didx, drv):
                full[k] = val
            return tuple(f(i, full, list(pvs)))

        _, vjp = jax.vjp(g, [rv[k] for k in didx], pv)
        d_r, d_p = vjp(tuple(dv))
        for n_, (ref, val) in enumerate(zip(dr_refs, d_r)):
            if has_add and n_ == 0:
                val = val + add_ref[...]
            ref[...] = val.astype(ref.dtype)

        @pl.when(i == 0)
        def _():
            for ref in dp_refs:
                ref[...] = jnp.zeros_like(ref)

        for ref, val in zip(dp_refs, d_p):
            ref[...] += val

    in_specs = (_row_specs(rows, tm) + [_full_spec(p.shape) for p in params] + _row_specs(douts, tm))
    ins = [r[0] for r in rows] + list(params) + [d[0] for d in douts]
    if has_add:
        in_specs.append(pl.BlockSpec((tm, rows[didx[0]][1]), lambda i: (i, 0)))
        ins.append(add)
    out_specs = ([pl.BlockSpec((tm, rows[k][1]), lambda i: (i, 0)) for k in didx]
                 + [_full_spec(p.shape) for p in params])
    out_dtypes = out_dtypes or [F32] * len(didx)
    out_shape = ([jax.ShapeDtypeStruct((tp, rows[k][1]), dt) for k, dt in zip(didx, out_dtypes)]
                 + [jax.ShapeDtypeStruct(p.shape, F32) for p in params])
    res = _pcall(body, name, (tp // tm,), in_specs, out_specs, out_shape, [], ("arbitrary",), ins, comm)
    outs = res[0] if comm else res
    outs = (list(outs[:len(didx)]), list(outs[len(didx):]))
    return (outs, res[1]) if comm else outs


def _concat_cols(pieces, dtype, name):
    arrs = [p for p in pieces if not isinstance(p, int)]
    widths = [p if isinstance(p, int) else p.shape[1] for p in pieces]
    tp = arrs[0].shape[0]
    tm = _pick(tp, (384, 256, 128))

    def body(*refs):
        o_ref, k, off = refs[-1], 0, 0
        for p, w in zip(pieces, widths):
            if isinstance(p, int):
                o_ref[:, off:off + w] = jnp.zeros((tm, w), o_ref.dtype)
            else:
                o_ref[:, off:off + w] = refs[k][...].astype(o_ref.dtype)
                k += 1
            off += w

    return pl.pallas_call(
        body, name=name, grid=(tp // tm,),
        in_specs=[pl.BlockSpec((tm, a.shape[1]), lambda i: (i, 0)) for a in arrs],
        out_specs=pl.BlockSpec((tm, sum(widths)), lambda i: (i, 0)),
        out_shape=jax.ShapeDtypeStruct((tp, sum(widths)), dtype),
        compiler_params=_cparams(("parallel",)),
    )(*arrs)


def _rowmask(i, tm):
    return (i * tm + lax.broadcasted_iota(jnp.int32, (tm, 1), 0)) >= PAD


def _lane(n=HP):
    return lax.broadcasted_iota(jnp.int32, (1, n), 1)


def _softplus(x):
    return jnp.maximum(x, 0.0) + jnp.log(1.0 + jnp.exp(-jnp.abs(x)))


def _log_sigmoid(x):
    return -_softplus(-x)


def _sigmoid(x):
    return 1.0 / (1.0 + jnp.exp(-x))


@functools.partial(jax.custom_vjp, nondiff_argnums=(1,))
def _lroll(x, s):
    return pltpu.roll(x, s % HP, 1)


def _lroll_fwd(x, s):
    return _lroll(x, s), None


def _lroll_bwd(s, _, g):
    return (pltpu.roll(g, (-s) % HP, 1),)


_lroll.defvjp(_lroll_fwd, _lroll_bwd)


def _f_norm(i, rv, pv):
    (h,), (g,) = rv, pv
    return [h * lax.rsqrt(jnp.mean(h * h, axis=-1, keepdims=True) + EPS) * g]


def _f_gate0(i, rv, pv):
    (misc,), (b,) = rv, pv
    tm = misc.shape[0]
    x = misc + b
    lane, ok = _lane(), _rowmask(i, tm)
    li = jnp.where(ok, x, NEG)
    lf = jnp.where(ok, _log_sigmoid(x), 0.0)
    return [jnp.where(lane < A_HEADS, li, jnp.where(lane < 2 * A_HEADS, lf, 0.0))]


def _f_aout(i, rv, pv):
    (ha, o), (g,) = rv, pv
    outs = []
    for h in range(A_HEADS):
        x = ha[:, h * A_DV:(h + 1) * A_DV]
        outs.append(x * lax.rsqrt(jnp.mean(x * x, axis=-1, keepdims=True) + EPS) * g)
    return [jnp.concatenate(outs, axis=1) * _sigmoid(o)]


def _f_gates(i, rv, pv):
    (xc,), (w_ra, b_ra, w_rx, b_rx, lam) = rv, pv
    tm = xc.shape[0]
    ra, rx = [], []
    for g in range(B_BLOCKS):
        xg = xc[:, g * B_BLOCK:(g + 1) * B_BLOCK].astype(BF16)
        ra.append(jnp.dot(xg, w_ra[g].astype(BF16), preferred_element_type=F32))
        rx.append(jnp.dot(xg, w_rx[g].astype(BF16), preferred_element_type=F32))
    r = _sigmoid(jnp.concatenate(ra, axis=1) + b_ra)
    ig = _sigmoid(jnp.concatenate(rx, axis=1) + b_rx)
    log_a = -LRU_C * r * _softplus(-lam)
    a = jnp.exp(log_a)
    u = jnp.sqrt(1.0 - jnp.exp(2.0 * log_a)) * (ig * xc)
    return [a, jnp.where(_rowmask(i, tm), u, 0.0)]


def _f_bout(i, rv, pv):
    hs, gb = rv
    gelu = 0.5 * gb * (1.0 + jnp.tanh(0.7978845608028654 * (gb + 0.044715 * gb * gb * gb)))
    return [hs * gelu]


def _f_cnorm(i, rv, pv):
    (cq, ckv), (gq, gkv) = rv, pv
    return [cq * lax.rsqrt(jnp.mean(cq * cq, axis=-1, keepdims=True) + EPS) * gq,
            ckv * lax.rsqrt(jnp.mean(ckv * ckv, axis=-1, keepdims=True) + EPS) * gkv]


def _rope128(x, cos, sin):
    lane = _lane()
    rot = jnp.where((lane >= C_NOPE) & (lane < C_NOPE + C_ROPE // 2), -_lroll(x, -(C_ROPE // 2)),
                    jnp.where((lane >= C_NOPE + C_ROPE // 2) & (lane < C_NOPE + C_ROPE), _lroll(x, C_ROPE // 2), 0.0))
    return x * cos + rot * sin


def _f_mlaprep(i, rv, pv):
    (q_, kk_, misc, cos, sin), (gq, gk) = rv, pv
    lane = _lane()
    m_n = lane < C_NOPE
    m_r = (lane >= C_NOPE) & (lane < C_NOPE + C_ROPE)

    def norm2(x, g):
        x2 = x * x
        sn = jnp.sum(jnp.where(m_n, x2, 0.0), axis=-1, keepdims=True) * (1.0 / C_NOPE)
        sr = jnp.sum(jnp.where(m_r, x2, 0.0), axis=-1, keepdims=True) * (1.0 / C_ROPE)
        scale = jnp.where(m_n, lax.rsqrt(sn + EPS), jnp.where(m_r, lax.rsqrt(sr + EPS), 0.0))
        return x * scale * g

    kr = _rope128(norm2(jnp.where(m_r, misc, 0.0), gk), cos, sin)
    qs, ks = [], []
    for h in range(C_HEADS):
        qs.append(_rope128(norm2(q_[:, h * HP:(h + 1) * HP], gq), cos, sin))
        ks.append(norm2(jnp.where(m_n, kk_[:, h * HP:(h + 1) * HP], 0.0), gk) + kr)
    return [jnp.concatenate(qs, axis=1), jnp.concatenate(ks, axis=1)]


def _f_foxprep(i, rv, pv):
    (fq, fk, misc), (gq, gk, bf) = rv, pv
    tm = fq.shape[0]
    lane = _lane()

    def hnorm(x, g):
        outs = []
        for h in range(D_HEADS):
            xh = x[:, h * HP:(h + 1) * HP]
            ss = jnp.sum(xh * xh, axis=-1, keepdims=True) * (1.0 / D_HD)
            outs.append(xh * lax.rsqrt(ss + EPS) * g)
        return jnp.concatenate(outs, axis=1)

    lf = jnp.where(_rowmask(i, tm) & (lane >= 96) & (lane < 96 + D_HEADS), _log_sigmoid(misc + bf), 0.0)
    return [hnorm(fq, gq), hnorm(fk, gk), lf]


def _mlstm_chunk(c, n, m, q, k, v, li, lf):
    ln = q.shape[0]
    r = lax.broadcasted_iota(jnp.int32, (ln, ln), 0)
    cc = lax.broadcasted_iota(jnp.int32, (ln, ln), 1)
    causal = cc <= r
    eye = cc == r
    li_row = jnp.sum(jnp.where(eye, li, 0.0), axis=0, keepdims=True)
    b_col = jnp.sum(jnp.where(causal, jnp.sum(jnp.where(eye, lf, 0.0), axis=0, keepdims=True), 0.0),
                    axis=1, keepdims=True)
    b_row = jnp.sum(jnp.where(r <= cc, lf, 0.0), axis=0, keepdims=True)
    k = k * (A_DQK ** -0.5)
    qb, kb, vb = q.astype(BF16), k.astype(BF16), v.astype(BF16)
    dmat = jnp.where(causal, b_col - b_row + li_row, NEG)
    inter = b_col + m
    m_row = jnp.maximum(inter, jnp.max(dmat, axis=1, keepdims=True))
    w_intra = jnp.exp(dmat - m_row)
    w_inter = jnp.exp(inter - m_row)
    s = lax.dot_general(qb, kb, (((1,), (1,)), ((), ())), preferred_element_type=F32) * w_intra
    num = (w_inter * jnp.dot(qb, c.astype(BF16), preferred_element_type=F32)
           + jnp.dot(s.astype(BF16), vb, preferred_element_type=F32))
    den = w_inter * jnp.sum(q * n, axis=1, keepdims=True) + jnp.sum(s, axis=1, keepdims=True)
    h = num / jnp.maximum(jnp.abs(den), jnp.exp(-m_row))
    g = jnp.sum(lf, axis=0, keepdims=True)
    dk = g - b_col + li
    m_new = jnp.maximum(g + m, jnp.max(dk, axis=0, keepdims=True))
    wk = jnp.exp(dk - m_new)
    sc = jnp.exp(g + m - m_new)
    kw = wk * k
    c_new = sc * c + lax.dot_general(kw.astype(BF16), vb, (((0,), (0,)), ((), ())), preferred_element_type=F32)
    n_new = sc * n + jnp.sum(kw, axis=0, keepdims=True)
    return c_new, n_new, m_new, h


A_QKV = 2 * A_HEADS * A_DQK + A_HEADS * A_DV


def _mlstm_chunk_all(cs, ns, ms, zqkv, gates):
    lane = _lane()
    c2, n2, m2, hs = [], [], [], []
    for h in range(A_HEADS):
        q = zqkv[:, h * A_DQK:(h + 1) * A_DQK]
        k = zqkv[:, (A_HEADS + h) * A_DQK:(A_HEADS + h + 1) * A_DQK]
        v = zqkv[:, 2 * A_HEADS * A_DQK + h * A_DV:2 * A_HEADS * A_DQK + (h + 1) * A_DV]
        li = jnp.sum(jnp.where(lane == h, gates, 0.0), axis=1, keepdims=True)
        lf = jnp.sum(jnp.where(lane == A_HEADS + h, gates, 0.0), axis=1, keepdims=True)
        c, n, m, hh = _mlstm_chunk(cs[h], ns[h], ms[h], q, k, v, li, lf)
        c2.append(c)
        n2.append(n)
        m2.append(m)
        hs.append(hh)
    return c2, n2, m2, jnp.concatenate(hs, axis=1)


def _mlstm_state_specs(nc, index):
    return ([pl.BlockSpec((1, A_HEADS, A_DQK, A_DV), lambda j: (index(j), 0, 0, 0)),
             pl.BlockSpec((1, A_HEADS, 1, A_DQK), lambda j: (index(j), 0, 0, 0)),
             pl.BlockSpec((1, A_HEADS, 1, 1), lambda j: (index(j), 0, 0, 0))],
            [jax.ShapeDtypeStruct((nc, A_HEADS, A_DQK, A_DV), F32),
             jax.ShapeDtypeStruct((nc, A_HEADS, 1, A_DQK), F32),
             jax.ShapeDtypeStruct((nc, A_HEADS, 1, 1), F32)],
            [pltpu.VMEM((A_HEADS, A_DQK, A_DV), F32), pltpu.VMEM((A_HEADS, 1, A_DQK), F32),
             pltpu.VMEM((A_HEADS, 1, 1), F32)])


def _mlstm_fwd(z, gates, comm=None):
    tp = z.shape[0]
    nc = tp // A_CHUNK
    ln = A_CHUNK
    heads = range(A_HEADS)

    def body(z_ref, g_ref, h_ref, cs_ref, ns_ref, ms_ref, c_s, n_s, m_s):
        @pl.when(pl.program_id(0) == 0)
        def _():
            c_s[...] = jnp.zeros_like(c_s)
            n_s[...] = jnp.zeros_like(n_s)
            m_s[...] = jnp.zeros_like(m_s)

        cs_ref[0] = c_s[...]
        ns_ref[0] = n_s[...]
        ms_ref[0] = m_s[...]
        c2, n2, m2, h = _mlstm_chunk_all([c_s[i] for i in heads], [n_s[i] for i in heads], [m_s[i] for i in heads],
                                         z_ref[...], g_ref[...])
        for i in heads:
            c_s[i] = c2[i]
            n_s[i] = n2[i]
            m_s[i] = m2[i]
        h_ref[...] = h

    st_specs, st_shapes, st_scratch = _mlstm_state_specs(nc, lambda j: j)
    return _pcall(
        body, "mlstm_fwd", (nc,),
        [pl.BlockSpec((ln, A_QKV), lambda j: (j, 0)), pl.BlockSpec((ln, HP), lambda j: (j, 0))],
        [pl.BlockSpec((ln, A_HEADS * A_DV), lambda j: (j, 0))] + st_specs,
        [jax.ShapeDtypeStruct((tp, A_HEADS * A_DV), F32)] + st_shapes,
        st_scratch, ("arbitrary",), (z, gates), comm)


def _mlstm_bwd(z, gates, cs, ns, ms, dh, comm=None):
    tp = z.shape[0]
    nc = tp // A_CHUNK
    ln = A_CHUNK
    heads = range(A_HEADS)

    def body(z_ref, g_ref, cs_ref, ns_ref, ms_ref, dh_ref, dz_ref, dg_ref, dc_s, dn_s, dm_s):
        @pl.when(pl.program_id(0) == 0)
        def _():
            dc_s[...] = jnp.zeros_like(dc_s)
            dn_s[...] = jnp.zeros_like(dn_s)
            dm_s[...] = jnp.zeros_like(dm_s)

        _, vjp = jax.vjp(_mlstm_chunk_all, [cs_ref[0, i] for i in heads], [ns_ref[0, i] for i in heads],
                         [ms_ref[0, i] for i in heads], z_ref[...], g_ref[...])
        dc, dn, dm, dz, dg = vjp(([dc_s[i] for i in heads], [dn_s[i] for i in heads], [dm_s[i] for i in heads],
                                  dh_ref[...]))
        for i in heads:
            dc_s[i] = dc[i]
            dn_s[i] = dn[i]
            dm_s[i] = dm[i]
        dz_ref[...] = dz.astype(BF16)
        dg_ref[...] = dg

    def rj(j):
        return nc - 1 - j

    st_specs, _, st_scratch = _mlstm_state_specs(nc, rj)
    return _pcall(
        body, "mlstm_bwd", (nc,),
        [pl.BlockSpec((ln, A_QKV), lambda j: (rj(j), 0)), pl.BlockSpec((ln, HP), lambda j: (rj(j), 0))] + st_specs
        + [pl.BlockSpec((ln, A_HEADS * A_DV), lambda j: (rj(j), 0))],
        [pl.BlockSpec((ln, A_QKV), lambda j: (rj(j), 0)), pl.BlockSpec((ln, HP), lambda j: (rj(j), 0))],
        [jax.ShapeDtypeStruct((tp, A_QKV), BF16), jax.ShapeDtypeStruct((tp, HP), F32)],
        st_scratch, ("arbitrary",), (z, gates, cs, ns, ms, dh), comm)


def _shift_down(x, s, row):
    return x if s == 0 else jnp.where(row >= s, pltpu.roll(x, s, 0), 0.0)


def _shift_up(x, s, row):
    n = x.shape[0]
    return x if s == 0 else jnp.where(row < n - s, pltpu.roll(x, n - s, 0), 0.0)


def _conv_fwd(z, xcb, w, b):
    tp, c = z.shape[0], w.shape[1]
    ct = 256

    def body(x_ref, w_ref, b_ref, o_ref):
        x = x_ref[...]
        row = lax.broadcasted_iota(jnp.int32, (tp, 1), 0)
        acc = jnp.zeros_like(x) + b_ref[...]
        for k in range(CONV_W):
            acc = acc + w_ref[k:k + 1, :] * _shift_down(x, CONV_W - 1 - k, row)
        o_ref[...] = acc

    return pl.pallas_call(
        body, name="conv_fwd", grid=(c // ct,),
        in_specs=[pl.BlockSpec((tp, ct), lambda j: (0, xcb + j)), pl.BlockSpec((CONV_W, ct), lambda j: (0, j)),
                  pl.BlockSpec((1, ct), lambda j: (0, j))],
        out_specs=pl.BlockSpec((tp, ct), lambda j: (0, j)),
        out_shape=jax.ShapeDtypeStruct((tp, c), F32),
        compiler_params=_cparams(("parallel",)),
    )(z, w, b)


def _conv_bwd(z, xcb, w, dxc):
    tp, c = z.shape[0], w.shape[1]
    ct = 256

    def body(x_ref, w_ref, d_ref, dx_ref, dw_ref, db_ref):
        x, d = x_ref[...], d_ref[...]
        row = lax.broadcasted_iota(jnp.int32, (tp, 1), 0)
        acc = jnp.zeros_like(x)
        for k in range(CONV_W):
            s = CONV_W - 1 - k
            acc = acc + w_ref[k:k + 1, :] * _shift_up(d, s, row)
            dw_ref[k:k + 1, :] = jnp.sum(d * _shift_down(x, s, row), axis=0, keepdims=True)
        dx_ref[...] = acc.astype(BF16)
        db_ref[...] = jnp.sum(d, axis=0, keepdims=True)

    return pl.pallas_call(
        body, name="conv_bwd", grid=(c // ct,),
        in_specs=[pl.BlockSpec((tp, ct), lambda j: (0, xcb + j)), pl.BlockSpec((CONV_W, ct), lambda j: (0, j)),
                  pl.BlockSpec((tp, ct), lambda j: (0, j))],
        out_specs=[pl.BlockSpec((tp, ct), lambda j: (0, j)), pl.BlockSpec((CONV_W, ct), lambda j: (0, j)),
                   pl.BlockSpec((1, ct), lambda j: (0, j))],
        out_shape=[jax.ShapeDtypeStruct((tp, c), BF16), jax.ShapeDtypeStruct((CONV_W, c), F32),
                   jax.ShapeDtypeStruct((1, c), F32)],
        compiler_params=_cparams(("parallel",)),
    )(z, w, dxc)


def _scan_fwd(a, u, name):
    tp, c = a.shape
    ct = _pick(c, (256, 128))
    lb = SCAN_BLOCK
    nb = tp // lb

    def body(a_ref, u_ref, h_ref, hp_ref):
        row = lax.broadcasted_iota(jnp.int32, (lb, 1), 0)

        def blk(j, carry):
            r0 = pl.multiple_of(j * lb, lb)
            aa, uu = a_ref[pl.ds(r0, lb), :], u_ref[pl.ds(r0, lb), :]
            s = 1
            while s < lb:
                mk = row >= s
                uu = jnp.where(mk, aa * pltpu.roll(uu, s, 0) + uu, uu)
                aa = jnp.where(mk, aa * pltpu.roll(aa, s, 0), aa)
                s *= 2
            hh = uu + aa * carry
            h_ref[pl.ds(r0, lb), :] = hh
            hp_ref[pl.ds(r0, lb), :] = jnp.where(row >= 1, pltpu.roll(hh, 1, 0), carry)
            return hh[lb - 1:lb, :]

        lax.fori_loop(0, nb, blk, jnp.zeros((1, ct), F32))

    spec = pl.BlockSpec((tp, ct), lambda j: (0, j))
    return pl.pallas_call(
        body, name=name, grid=(c // ct,), in_specs=[spec, spec], out_specs=[spec, spec],
        out_shape=[jax.ShapeDtypeStruct((tp, c), F32)] * 2,
        compiler_params=_cparams(("parallel",)),
    )(a, u)


def _scan_bwd(a, hprev, dh, name):
    tp, c = a.shape
    ct = _pick(c, (256, 128))
    lb = SCAN_BLOCK
    nb = tp // lb

    def body(a_ref, hp_ref, dh_ref, du_ref, da_ref):
        row = lax.broadcasted_iota(jnp.int32, (lb, 1), 0)

        def blk(jj, carry):
            g_next, a_next = carry
            r0 = pl.multiple_of((nb - 1 - jj) * lb, lb)
            a_blk = a_ref[pl.ds(r0, lb), :]
            aa = jnp.where(row < lb - 1, pltpu.roll(a_blk, lb - 1, 0), a_next)
            gg = dh_ref[pl.ds(r0, lb), :]
            s = 1
            while s < lb:
                mk = row < lb - s
                gg = jnp.where(mk, aa * pltpu.roll(gg, lb - s, 0) + gg, gg)
                aa = jnp.where(mk, aa * pltpu.roll(aa, lb - s, 0), aa)
                s *= 2
            gg = gg + aa * g_next
            du_ref[pl.ds(r0, lb), :] = gg
            da_ref[pl.ds(r0, lb), :] = gg * hp_ref[pl.ds(r0, lb), :]
            return gg[0:1, :], a_blk[0:1, :]

        lax.fori_loop(0, nb, blk, (jnp.zeros((1, ct), F32), jnp.zeros((1, ct), F32)))

    spec = pl.BlockSpec((tp, ct), lambda j: (0, j))
    return pl.pallas_call(
        body, name=name, grid=(c // ct,), in_specs=[spec, spec, spec], out_specs=[spec, spec],
        out_shape=[jax.ShapeDtypeStruct((tp, c), F32)] * 2,
        compiler_params=_cparams(("parallel",)),
    )(a, hprev, dh)


ATT_HEADS_PER_STEP = 2
LOG2E = 1.4426950408889634
LN2 = 0.6931471805599453


FOX_LANE0 = 96


def _head_lane(f, head):
    return jnp.sum(jnp.where(_lane() == FOX_LANE0 + head, f, 0.0), axis=1, keepdims=True)


def _key_terms(tp, tq, bk=None):
    pad_neg = jnp.where(jnp.arange(tp) < PAD, NEG, 0.0).astype(F32).reshape(1, tp // tq, 1, tq)
    r = lax.broadcasted_iota(jnp.int32, (tq, tq), 0)
    c = lax.broadcasted_iota(jnp.int32, (tq, tq), 1)
    diag = jnp.stack([jnp.where((c <= r) & ((c >= PAD) | (c == r)), 0.0, NEG),
                      jnp.where(c <= r, 0.0, NEG)]).astype(F32)
    if bk is None:
        return pad_neg, jnp.zeros_like(pad_neg), diag
    kd = -bk.reshape(bk.shape[0], tp // tq, 1, tq)
    return kd + pad_neg, kd, diag


def _attn_fwd(q, k, v, vcb, scale, bq, kterms, name, comm=None):
    tp = q.shape[0]
    nh = q.shape[1] // HP
    tq = _pick(tp, (384, 256, 128))
    has_bq = bq is not None
    c1 = scale * LOG2E
    per_head = kterms[0].shape[0] > 1
    hg = ATT_HEADS_PER_STEP
    lanes = [slice(hh * HP, (hh + 1) * HP) for hh in range(hg)]

    def body(*refs):
        if has_bq:
            q_ref, k_ref, v_ref, kb_ref, kd_ref, dm_ref, bq_ref, o_ref, lse_ref = refs
            rb = [_head_lane(bq_ref[...], hg * pl.program_id(0) + hh) * LOG2E for hh in range(hg)]
        else:
            q_ref, k_ref, v_ref, kb_ref, kd_ref, dm_ref, o_ref, lse_ref = refs
        i = pl.program_id(1)
        qb = [q_ref[:, ln].astype(BF16) for ln in lanes]

        def tile(j, carry, diag):
            r0 = pl.multiple_of(j * tq, tq)
            qk = [lax.dot_general(qb[hh], k_ref[pl.ds(r0, tq), lanes[hh]].astype(BF16), (((1,), (1,)), ((), ())),
                                  preferred_element_type=F32) for hh in range(hg)]
            out = []
            for hh in range(hg):
                m, l, acc = carry[hh]
                vb = v_ref[pl.ds(r0, tq), lanes[hh]].astype(BF16)
                kt = (kd_ref if diag else kb_ref)[hh if per_head else 0, j] * LOG2E
                x = qk[hh] * c1 + kt
                if has_bq:
                    x = x + rb[hh]
                if diag:
                    x = x + dm_ref[0]
                m2 = jnp.maximum(m, jnp.max(x, axis=1, keepdims=True))
                alpha = jnp.exp2(m - m2)
                p = jnp.exp2(x - m2)
                l2 = alpha * l + jnp.sum(p, axis=1, keepdims=True)
                acc2 = alpha * acc + jnp.dot(p.astype(BF16), vb, preferred_element_type=F32)
                out.append((m2, l2, acc2))
            return tuple(out)

        init = tuple((jnp.full((tq, 1), NEG, F32), jnp.zeros((tq, 1), F32), jnp.zeros((tq, HP), F32))
                     for _ in range(hg))
        res = tile(i, lax.fori_loop(0, i, lambda j, c: tile(j, c, False), init), True)
        for hh, (m, l, acc) in enumerate(res):
            o_ref[:, lanes[hh]] = acc / l
            lse_ref[hh] = m * LN2 + jnp.log(l)

    kt_spec = pl.BlockSpec((hg if per_head else 1, tp // tq, 1, tq),
                           (lambda h, i: (h, 0, 0, 0)) if per_head else (lambda h, i: (0, 0, 0, 0)))
    in_specs = [pl.BlockSpec((tq, hg * HP), lambda h, i: (i, h)), pl.BlockSpec((tp, hg * HP), lambda h, i: (0, h)),
                pl.BlockSpec((tp, hg * HP), lambda h, i: (0, vcb // hg + h)), kt_spec, kt_spec,
                pl.BlockSpec((1, tq, tq), lambda h, i: (jnp.minimum(i, 1), 0, 0))]
    ins = [q, k, v, *kterms]
    if has_bq:
        in_specs += [pl.BlockSpec((tq, HP), lambda h, i: (i, 0))]
        ins += [bq]
    return _pcall(
        body, name, (nh // hg, tp // tq), in_specs,
        [pl.BlockSpec((tq, hg * HP), lambda h, i: (i, h)), pl.BlockSpec((hg, tq, 1), lambda h, i: (h, i, 0))],
        [jax.ShapeDtypeStruct((tp, nh * HP), F32), jax.ShapeDtypeStruct((nh, tp, 1), F32)],
        [], ("parallel", "parallel"), ins, comm)


def _attn_bwd(q, k, v, vcb, o, lse, do, docb, scale, bq, kterms, name, comm=None):
    tp = q.shape[0]
    nh = q.shape[1] // HP
    tq = _pick(tp, (384, 256, 128))
    has_bq = bq is not None
    c1 = scale * LOG2E
    per_head = kterms[0].shape[0] > 1
    hg = ATT_HEADS_PER_STEP
    lanes = [slice(hh * HP, (hh + 1) * HP) for hh in range(hg)]

    def body(*refs):
        if has_bq:
            (q_ref, k_ref, v_ref, o_ref, lse_ref, do_ref, kb_ref, kd_ref, dm_ref, bq_ref,
             dq_ref, dk_ref, dv_ref, dbq_ref, dkt_ref) = refs
            rb = [(_head_lane(bq_ref[...], hg * pl.program_id(0) + hh) - lse_ref[hh]) * LOG2E for hh in range(hg)]
        else:
            q_ref, k_ref, v_ref, o_ref, lse_ref, do_ref, kb_ref, kd_ref, dm_ref, dq_ref, dk_ref, dv_ref = refs
            rb = [lse_ref[hh] * (-LOG2E) for hh in range(hg)]
        i = pl.program_id(1)

        @pl.when(i == 0)
        def _():
            dk_ref[...] = jnp.zeros_like(dk_ref)
            dv_ref[...] = jnp.zeros_like(dv_ref)
            if has_bq:
                dkt_ref[...] = jnp.zeros_like(dkt_ref)

        qb = [q_ref[:, ln].astype(BF16) for ln in lanes]
        do_ = [do_ref[:, ln] for ln in lanes]
        dob = [d.astype(BF16) for d in do_]
        delta = [jnp.sum(do_[hh] * o_ref[:, lanes[hh]], axis=1, keepdims=True) for hh in range(hg)]

        def tile(j, carry, diag):
            r0 = pl.multiple_of(j * tq, tq)
            kbs = [k_ref[pl.ds(r0, tq), lanes[hh]].astype(BF16) for hh in range(hg)]
            qk = [lax.dot_general(qb[hh], kbs[hh], (((1,), (1,)), ((), ())), preferred_element_type=F32)
                  for hh in range(hg)]
            dps = [lax.dot_general(dob[hh], v_ref[pl.ds(r0, tq), lanes[hh]].astype(BF16), (((1,), (1,)), ((), ())),
                                   preferred_element_type=F32) for hh in range(hg)]
            out = []
            for hh in range(hg):
                dq, dbq = carry[hh]
                kb = kbs[hh]
                kt = (kd_ref if diag else kb_ref)[hh if per_head else 0, j] * LOG2E
                x = qk[hh] * c1 + kt + rb[hh]
                if diag:
                    x = x + dm_ref[0]
                p = jnp.exp2(x)
                ds = p * (dps[hh] - delta[hh])
                dsb = ds.astype(BF16)
                dk_ref[pl.ds(r0, tq), lanes[hh]] += lax.dot_general(dsb, qb[hh], (((0,), (0,)), ((), ())),
                                                                    preferred_element_type=F32) * scale
                dv_ref[pl.ds(r0, tq), lanes[hh]] += lax.dot_general(p.astype(BF16), dob[hh], (((0,), (0,)), ((), ())),
                                                                    preferred_element_type=F32)
                if has_bq:
                    dkt_ref[hh, j] += jnp.sum(ds, axis=0, keepdims=True)
                    dbq = dbq + jnp.sum(ds, axis=1, keepdims=True)
                out.append((dq + jnp.dot(dsb, kb, preferred_element_type=F32), dbq))
            return tuple(out)

        init = tuple((jnp.zeros((tq, HP), F32), jnp.zeros((tq, 1), F32)) for _ in range(hg))
        res = tile(i, lax.fori_loop(0, i, lambda j, c: tile(j, c, False), init), True)
        for hh, (dq, dbq) in enumerate(res):
            dq_ref[:, lanes[hh]] = dq * scale
            if has_bq:
                dbq_ref[hh] = dbq

    blk_q = pl.BlockSpec((tq, hg * HP), lambda h, i: (i, h))
    blk_k = pl.BlockSpec((tp, hg * HP), lambda h, i: (0, h))
    kt_spec = pl.BlockSpec((hg if per_head else 1, tp // tq, 1, tq),
                           (lambda h, i: (h, 0, 0, 0)) if per_head else (lambda h, i: (0, 0, 0, 0)))
    row_spec = pl.BlockSpec((hg, tq, 1), lambda h, i: (h, i, 0))
    in_specs = [blk_q, blk_k, pl.BlockSpec((tp, hg * HP), lambda h, i: (0, vcb // hg + h)), blk_q,
                row_spec, pl.BlockSpec((tq, hg * HP), lambda h, i: (i, docb // hg + h)), kt_spec, kt_spec,
                pl.BlockSpec((1, tq, tq), lambda h, i: (jnp.minimum(i, 1), 0, 0))]
    ins = [q, k, v, o, lse, do, *kterms]
    out_specs = [blk_q, blk_k, blk_k]
    out_shape = [jax.ShapeDtypeStruct((tp, nh * HP), F32)] * 3
    if has_bq:
        in_specs += [pl.BlockSpec((tq, HP), lambda h, i: (i, 0))]
        ins += [bq]
        out_specs += [row_spec, kt_spec]
        out_shape += [jax.ShapeDtypeStruct((nh, tp, 1), F32), jax.ShapeDtypeStruct((nh, tp // tq, 1, tq), F32)]
    return _pcall(body, name, (nh // hg, tp // tq), in_specs, out_specs, out_shape, [], ("parallel", "arbitrary"), ins,
                  comm)


def _loss_head(h, tgt):
    tp, d = h.shape
    tm = 128
    first = (PAD + N_META) // tm

    def body(h_ref, t_ref, l_ref, d_ref):
        i = pl.program_id(0)

        @pl.when(i == 0)
        def _():
            l_ref[...] = jnp.zeros_like(l_ref)

        live = i >= first
        err = jnp.where(live, h_ref[...] - t_ref[...], 0.0)
        d_ref[...] = err * (1.0 / d)
        l_ref[...] += (0.5 / d) * jnp.sum(err * err)

    return pl.pallas_call(
        body, name="loss_head", grid=(tp // tm,),
        in_specs=[pl.BlockSpec((tm, d), lambda i: (i, 0)), pl.BlockSpec((tm, d), lambda i: (i, 0))],
        out_specs=[_full_spec((8, 128)), pl.BlockSpec((tm, d), lambda i: (i, 0))],
        out_shape=[jax.ShapeDtypeStruct((8, 128), F32), jax.ShapeDtypeStruct((tp, d), F32)],
        compiler_params=_cparams(("arbitrary",)),
    )(h, tgt)


def _rope_tables(pos_rows):
    half = C_ROPE // 2
    freqs = ROPE_THETA ** (-jnp.arange(half, dtype=F32) / half)
    ang = pos_rows[:, None].astype(F32) * freqs
    cos, sin = jnp.cos(ang), jnp.sin(ang)
    tp = pos_rows.shape[0]
    one, zero = jnp.ones((tp, C_NOPE), F32), jnp.zeros((tp, C_NOPE), F32)
    tail1, tail0 = jnp.ones((tp, HP - C_NOPE - C_ROPE), F32), jnp.zeros((tp, HP - C_NOPE - C_ROPE), F32)
    return (jnp.concatenate([one, cos, cos, tail1], axis=1), jnp.concatenate([zero, sin, sin, tail0], axis=1))


def _heads_to_cols(x, lo, n):
    t = x[:, lo:lo + n].T
    return t[:, :, None], t[:, None, :]


def _local_step(x, positions, tgt, w, plan=None):
    def hosted(tag, fn, *args, **kw):
        ops = plan.ride(tag, g) if plan is not None else None
        if not ops:
            return fn(*args, **kw)
        res, got = fn(*args, comm=ops, **kw)
        plan.arrived(tag, got, w)
        return res

    s_len = x.shape[0]
    tp = PAD + N_META + s_len
    tm = _pick(tp, (384, 256, 128))
    front = PAD + N_META
    h0 = jnp.concatenate([jnp.zeros((PAD, D_MODEL), F32), w["meta"], x], axis=0)
    tgt_p = jnp.concatenate([jnp.zeros((front, D_MODEL), F32), tgt], axis=0)
    pos_rows = jnp.concatenate([jnp.zeros((PAD,), jnp.int32), jnp.arange(N_META, dtype=jnp.int32),
                                positions + N_META])
    cos, sin = _rope_tables(pos_rows)
    g = {}

    r_h0 = [(h0, D_MODEL, 0)]
    (xn0,) = _row_fwd("norm0_f", _f_norm, r_h0, [w["ev_ln"]], [(D_MODEL, BF16)], tm)
    z0 = hosted("mm_z0", _mm, xn0, w["ev_w_in"], "nn", "mm_z0")
    r_misc0 = [(z0, HP, 5120 // HP)]
    (g0,) = _row_fwd("gate0_f", _f_gate0, r_misc0, [w["ev_b_if"]], [(HP, F32)], tm)
    h_a, cs, ns, ms = hosted("mlstm_fwd", _mlstm_fwd, z0, g0)
    r_aout = [(h_a, 1024, 0), (z0, 1024, 2)]
    (ha,) = _row_fwd("aout_f", _f_aout, r_aout, [w["ev_a_norm"]], [(1024, BF16)], tm)
    xc = _conv_fwd(z0, 3072 // 256, w["ev_conv_w"], w["ev_conv_b"])
    p_gates = [w["ev_w_ra"], w["ev_b_ra"], w["ev_w_rx"], w["ev_b_rx"], w["ev_lam"]]
    a_g, u_g = _row_fwd("gates_f", _f_gates, [(xc, 1024, 0)], p_gates, [(1024, F32), (1024, F32)], tm)
    hs, hprev = _scan_fwd(a_g, u_g, "lru_f")
    r_bout = [(hs, 1024, 0), (z0, 1024, 4)]
    (hb,) = _row_fwd("bout_f", _f_bout, r_bout, [], [(1024, BF16)], tm)
    hab = _concat_cols([ha, hb], BF16, "cat_hab")
    h1 = _mm(hab, w["ev_w_out"], "nn", "mm_h1", add=h0)
    (xn1,) = _row_fwd("norm1_f", _f_norm, [(h1, D_MODEL, 0)], [w["mlp_ln0"]], [(D_MODEL, BF16)], tm)
    p0, act0 = _mm(xn1, w["w_ff1_0"], "nn", "mm_p0", relu2=True)
    h2 = _mm(act0, w["w_ff2_0"], "nn", "mm_h2", add=h1)
    (xn2,) = _row_fwd("norm2_f", _f_norm, [(h2, D_MODEL, 0)], [w["od_ln"]], [(D_MODEL, BF16)], tm)
    z1 = _mm(xn2, w["od_w_in"], "nn", "mm_z1")
    r_c = [(z1, C_Q_LORA, 3072 // C_Q_LORA), (z1, C_KV_LORA, 3584 // C_KV_LORA)]
    cqn, ckvn = _row_fwd("cnorm_f", _f_cnorm, r_c, [w["od_g_qa"], w["od_g_kva"]],
                         [(C_Q_LORA, BF16), (C_KV_LORA, BF16)], tm)
    q_ = _mm(cqn, w["od_w_uq"], "nn", "mm_q")
    kv_ = _mm(ckvn, w["od_w_ukv"], "nn", "mm_kv")
    r_mla = [(q_, 1024, 0), (kv_, 1024, 0), (z1, HP, 3840 // HP), (cos, HP, 0), (sin, HP, 0)]
    p_mla = [w["gq_full"], w["gk_full"]]
    qm, km = _row_fwd("mla_f", _f_mlaprep, r_mla, p_mla, [(1024, BF16), (1024, BF16)], tm)
    sc_c = (C_NOPE + C_ROPE) ** -0.5
    kt_c = _key_terms(tp, tm)
    hc, lse_c = hosted("mla_attn_f", _attn_fwd, qm, km, kv_, C_HEADS, sc_c, None, kt_c, "mla_attn_f")
    r_fox = [(z1, 1024, 0), (z1, 1024, 1), (z1, HP, 3840 // HP)]
    p_fox = [w["gfq_full"], w["gfk_full"], w["bf_full"]]
    qf, kf, lfx = _row_fwd("fox_f", _f_foxprep, r_fox, p_fox, [(1024, BF16), (1024, BF16), (HP, F32)], tm)
    ones = jnp.ones((tp, HP), F32)
    fcum, fprev = _scan_fwd(ones, lfx, "fcum_f")
    _, bk = _heads_to_cols(fcum, FOX_LANE0, D_HEADS)
    kt_d = _key_terms(tp, tm, bk)
    sc_d = D_HD ** -0.5
    bq = fcum
    hd, lse_d = _attn_fwd(qf, kf, z1, 2048 // HP, sc_d, bq, kt_d, "fox_attn_f")
    hcd = _concat_cols([hc, hd], F32, "cat_hcd")
    h3 = _mm(hcd, w["od_w_out"], "nn", "mm_h3", add=h2)
    (xn3,) = _row_fwd("norm3_f", _f_norm, [(h3, D_MODEL, 0)], [w["mlp_ln1"]], [(D_MODEL, BF16)], tm)
    p1, act1 = _mm(xn3, w["w_ff1_1"], "nn", "mm_p1", relu2=True)
    h4 = _mm(act1, w["w_ff2_1"], "nn", "mm_h4", add=h3)
    lpart, dh4 = _loss_head(h4, tgt_p)
    loss = lpart[0, 0]

    def mlp_bwd(tag, dh_out, h_in, xn, p, act, ln, w1, w2):
        dp = hosted(f"mm_dp{tag}", _mm, dh_out, w2, "nt", f"mm_dp{tag}", relu2_of=p)
        g_w2 = _mm(act, dh_out, "tn", f"mm_dw2_{tag}")
        g_w1 = _mm(xn, dp, "tn", f"mm_dw1_{tag}")
        dxn = _mm(dp, w1, "nt", f"mm_dxn{tag}")
        (dh_in,), (g_ln,) = _row_bwd(f"normm{tag}_b", _f_norm, [(h_in, D_MODEL, 0)], [ln], [(dxn, D_MODEL, 0)], tm,
                                     [True], add=dh_out)
        return dh_in, g_ln, g_w1, g_w2

    dh3, g["mlp_ln1"], g["w_ff1_1"], g["w_ff2_1"] = mlp_bwd("1", dh4, h3, xn3, p1, act1, w["mlp_ln1"],
                                                            w["w_ff1_1"], w["w_ff2_1"])
    dhcd = _mm(dh3, w["od_w_out"], "nt", "mm_dhcd")
    g["od_w_out"] = _mm(hcd, dh3, "tn", "mm_dwout1")
    dqf, dkf, dvf, dbq, dkt = hosted("fox_attn_b", _attn_bwd, qf, kf, z1, 2048 // HP, hd, lse_d, dhcd, D_HEADS, sc_d,
                                     bq, kt_d, "fox_attn_b")
    dfc = dbq[:, :, 0].T - dkt.reshape(D_HEADS, tp).T
    dfcum = jnp.concatenate([jnp.zeros((tp, 96), F32), dfc, jnp.zeros((tp, HP - 96 - D_HEADS), F32)], axis=1)
    dlfx, _ = _scan_bwd(ones, fprev, dfcum, "fcum_b")
    (dfq, dfk, dmisc_f), (g["gfq_full"], g["gfk_full"], g["bf_full"]) = _row_bwd(
        "fox_b", _f_foxprep, r_fox, p_fox, [(dqf, 1024, 0), (dkf, 1024, 0), (dlfx, HP, 0)], tm, [True, True, True],
        out_dtypes=[BF16, BF16, F32])
    dqm, dkm, dvm = _attn_bwd(qm, km, kv_, C_HEADS, hc, lse_c, dhcd, 0, sc_c, None, kt_c, "mla_attn_b")
    (dq_, dkk_, dmisc_m), (g["gq_full"], g["gk_full"]) = _row_bwd(
        "mla_b", _f_mlaprep, r_mla, p_mla, [(dqm, 1024, 0), (dkm, 1024, 0)], tm, [True, True, True, False, False],
        out_dtypes=[BF16, BF16, F32])
    dkv_ = _concat_cols([dkk_, dvm], BF16, "cat_dkv")
    dckvn = _mm(dkv_, w["od_w_ukv"], "nt", "mm_dckvn")
    g["od_w_ukv"] = _mm(ckvn, dkv_, "tn", "mm_dwukv")
    dcqn = _mm(dq_, w["od_w_uq"], "nt", "mm_dcqn")
    g["od_w_uq"] = _mm(cqn, dq_, "tn", "mm_dwuq")
    (dcq, dckv), (g["od_g_qa"], g["od_g_kva"]) = _row_bwd(
        "cnorm_b", _f_cnorm, r_c, [w["od_g_qa"], w["od_g_kva"]],
        [(dcqn, C_Q_LORA, 0), (dckvn, C_KV_LORA, 0)], tm, [True, True], out_dtypes=[BF16, BF16])
    dz1 = _concat_cols([dfq, dfk, dvf, dcq, HP, dckv, dmisc_f + dmisc_m, HP], BF16, "cat_dz1")
    g["od_w_in"] = _mm(xn2, dz1, "tn", "mm_dwin1")
    dxn2 = _mm(dz1, w["od_w_in"], "nt", "mm_dxn2")
    (dh2,), (g["od_ln"],) = _row_bwd("norm2_b", _f_norm, [(h2, D_MODEL, 0)], [w["od_ln"]], [(dxn2, D_MODEL, 0)], tm,
                                     [True], add=dh3)
    dh1, g["mlp_ln0"], g["w_ff1_0"], g["w_ff2_0"] = mlp_bwd("0", dh2, h1, xn1, p0, act0, w["mlp_ln0"],
                                                            w["w_ff1_0"], w["w_ff2_0"])
    dhab = _mm(dh1, w["ev_w_out"], "nt", "mm_dhab")
    g["ev_w_out"] = _mm(hab, dh1, "tn", "mm_dwout0")
    (dhs, dgb), _ = _row_bwd("bout_b", _f_bout, r_bout, [], [(dhab, 1024, 1)], tm, [True, True],
                             out_dtypes=[F32, BF16])
    du_g, da_g = _scan_bwd(a_g, hprev, dhs, "lru_b")
    (dxc,), (g["ev_w_ra"], g["ev_b_ra"], g["ev_w_rx"], g["ev_b_rx"], g["ev_lam"]) = hosted(
        "gates_b", _row_bwd, "gates_b", _f_gates, [(xc, 1024, 0)], p_gates, [(da_g, 1024, 0), (du_g, 1024, 0)], tm,
        [True])
    dxb, g["ev_conv_w"], g["ev_conv_b"] = _conv_bwd(z0, 3072 // 256, w["ev_conv_w"], dxc)
    (dh_a, do_), (g["ev_a_norm"],) = _row_bwd("aout_b", _f_aout, r_aout, [w["ev_a_norm"]], [(dhab, 1024, 0)], tm,
                                              [True, True], out_dtypes=[F32, BF16])
    dqkv, dg0 = hosted("mlstm_bwd", _mlstm_bwd, z0, g0, cs, ns, ms, dh_a)
    (dmisc0,), (g["ev_b_if"],) = _row_bwd("gate0_b", _f_gate0, r_misc0, [w["ev_b_if"]], [(dg0, HP, 0)], tm, [True],
                                          out_dtypes=[BF16])
    dz0 = _concat_cols([dqkv, do_, dxb, dgb, dmisc0, HP], BF16, "cat_dz0")
    g["ev_w_in"] = _mm(xn0, dz0, "tn", "mm_dwin0")
    dxn0 = hosted("mm_dxn0", _mm, dz0, w["ev_w_in"], "nt", "mm_dxn0")
    (dh0,), (g["ev_ln"],) = _row_bwd("norm0_b", _f_norm, r_h0, [w["ev_ln"]], [(dxn0, D_MODEL, 0)], tm, [True], add=dh1)
    g["meta"] = dh0[PAD:front]
    return loss, dh0[front:], g


def _pad_last(a, n):
    return jnp.pad(a, [(0, 0)] * (a.ndim - 1) + [(0, n - a.shape[-1])])


def _pad_heads(a, nh, d):
    return _pad_last(a.reshape(a.shape[:-1] + (nh, d)), HP).reshape(a.shape[:-1] + (nh * HP,))


def _unpad_heads(a, nh, d):
    return a.reshape(a.shape[:-1] + (nh, HP))[..., :d].reshape(a.shape[:-1] + (nh * d,))


_MM_UNITS = ("ev_w_in", "ev_w_out", "od_w_in", "od_w_uq", "od_w_ukv", "od_w_out",
             "w_ff1_0", "w_ff2_0", "w_ff1_1", "w_ff2_1")


def _mw_pad(name, a):
    if name == "ev_w_in":
        return _pad_last(jnp.concatenate([a[:, :3072], a[:, 3080:5128], a[:, 3072:3080]], axis=1), ZE)
    if name == "od_w_in":
        z = lambda n: jnp.zeros((a.shape[0], n), a.dtype)
        return jnp.concatenate(
            [_pad_heads(a[:, 672:1184], D_HEADS, D_HD), _pad_heads(a[:, 1184:1696], D_HEADS, D_HD),
             _pad_heads(a[:, 1696:2208], D_HEADS, D_HD), a[:, 0:384], z(128), a[:, 384:640],
             z(64), a[:, 640:672], a[:, 2208:2216], z(24), z(128)], axis=1)
    if name == "od_w_uq":
        return _pad_heads(a, C_HEADS, C_NOPE + C_ROPE)
    if name == "od_w_ukv":
        wkv = a.reshape(C_KV_LORA, C_HEADS, C_NOPE + C_V)
        return jnp.concatenate([_pad_last(wkv[:, :, :C_NOPE], HP).reshape(C_KV_LORA, -1),
                                _pad_last(wkv[:, :, C_NOPE:], HP).reshape(C_KV_LORA, -1)], axis=1)
    if name == "od_w_out":
        return jnp.pad(a.reshape(2 * C_HEADS, C_V, D_MODEL), ((0, 0), (0, HP - C_V), (0, 0))).reshape(-1, D_MODEL)
    return a


def _mw_unpad(name, g):
    if name == "ev_w_in":
        return jnp.concatenate([g[:, :3072], g[:, 5120:5128], g[:, 3072:5120]], axis=1)
    if name == "od_w_in":
        return jnp.concatenate(
            [g[:, 3072:3456], g[:, 3584:3840], g[:, 3904:3936], _unpad_heads(g[:, 0:1024], D_HEADS, D_HD),
             _unpad_heads(g[:, 1024:2048], D_HEADS, D_HD), _unpad_heads(g[:, 2048:3072], D_HEADS, D_HD),
             g[:, 3936:3944]], axis=1)
    if name == "od_w_uq":
        return _unpad_heads(g, C_HEADS, C_NOPE + C_ROPE)
    if name == "od_w_ukv":
        gk = g[:, :C_HEADS * HP].reshape(C_KV_LORA, C_HEADS, HP)[:, :, :C_NOPE]
        gv = g[:, C_HEADS * HP:].reshape(C_KV_LORA, C_HEADS, HP)[:, :, :C_V]
        return jnp.concatenate([gk, gv], axis=2).reshape(C_KV_LORA, -1)
    if name == "od_w_out":
        return g.reshape(2 * C_HEADS, HP, D_MODEL)[:, :C_V].reshape(-1, D_MODEL)
    return g


def _prep_weights(p):
    w = {}
    w["meta"] = p["meta"]
    for k in ("ev_ln", "ev_a_norm", "ev_conv_b", "ev_b_ra", "ev_b_rx", "ev_lam", "od_ln", "od_g_qa", "od_g_kva"):
        w[k] = p[k].reshape(1, -1)
    w["ev_b_if"] = _pad_last(p["ev_b_if"].reshape(1, -1), HP)
    w["ev_conv_w"] = p["ev_conv_w"][0]
    w["ev_w_ra"] = p["ev_w_ra"][0]
    w["ev_w_rx"] = p["ev_w_rx"][0]
    f1 = lambda a: a.reshape(1, -1)
    w["gq_full"] = _pad_last(jnp.concatenate([f1(p["od_g_qn"]), f1(p["od_g_qr"])], axis=1), HP)
    w["gk_full"] = _pad_last(jnp.concatenate([f1(p["od_g_kn"]), f1(p["od_g_kr"])], axis=1), HP)
    w["gfq_full"] = _pad_last(f1(p["od_g_fq"]), HP)
    w["gfk_full"] = _pad_last(f1(p["od_g_fk"]), HP)
    w["bf_full"] = _pad_last(jnp.concatenate([jnp.zeros((1, 96), F32), f1(p["od_b_f"])], axis=1), HP)
    for l in (0, 1):
        w[f"mlp_ln{l}"] = p["mlp_ln"][l:l + 1]
    for n in _MM_UNITS:
        if n in p:
            w[n] = _mw_pad(n, p[n])
    return w


def _unprep_grads(g):
    o = {}
    o["meta"] = g["meta"]
    for k in ("ev_ln", "ev_a_norm", "ev_conv_b", "ev_b_ra", "ev_b_rx", "ev_lam", "od_ln", "od_g_qa", "od_g_kva"):
        o[k] = g[k].reshape(1, -1)
    o["ev_b_if"] = g["ev_b_if"][:, :2 * A_HEADS]
    o["ev_conv_w"] = g["ev_conv_w"][None]
    o["ev_w_ra"] = g["ev_w_ra"][None]
    o["ev_w_rx"] = g["ev_w_rx"][None]
    o["od_g_qn"] = g["gq_full"][:, :C_NOPE]
    o["od_g_qr"] = g["gq_full"][:, C_NOPE:C_NOPE + C_ROPE]
    o["od_g_kn"] = g["gk_full"][:, :C_NOPE]
    o["od_g_kr"] = g["gk_full"][:, C_NOPE:C_NOPE + C_ROPE]
    o["od_g_fq"] = g["gfq_full"][:, :D_HD]
    o["od_g_fk"] = g["gfk_full"][:, :D_HD]
    o["od_b_f"] = g["bf_full"][:, 96:96 + D_HEADS]
    o["mlp_ln"] = jnp.concatenate([g["mlp_ln0"], g["mlp_ln1"]], axis=0)
    return o


def _exchange(ops, name):
    n_ops = len(ops)
    flags = [s for _, s in ops]

    def body(*refs):
        copies = _comm_copies(refs[:n_ops], refs[n_ops:2 * n_ops], flags, *refs[2 * n_ops:])
        _comm_begin(copies)
        _comm_end(copies)

    any_spec = pl.BlockSpec(memory_space=pl.ANY)
    c_shape, c_sems = _comm_shapes(ops)
    return pl.pallas_call(body, name=name, out_shape=c_shape, in_specs=[any_spec] * n_ops,
                          out_specs=[any_spec] * n_ops, scratch_shapes=c_sems)(*[x for x, _ in ops])


def _adamw(parts, w, m, v, name):
    r, c = w.shape
    tr = r
    for cand in (512, 256, 128, 64, 32, 16):
        if r % cand == 0 and N_DEV * cand * c * 4 <= 4 * 1024 * 1024:
            tr = cand
            break
    c1 = 1.0 / (1.0 - ADAM_B1 ** ADAM_STEP)
    c2 = 1.0 / (1.0 - ADAM_B2 ** ADAM_STEP)

    def body(p_ref, w_ref, m_ref, v_ref, g_ref, d_ref, mo_ref, vo_ref):
        g = p_ref[0].astype(F32)
        for j in range(1, N_DEV):
            g = g + p_ref[j].astype(F32)
        m2 = ADAM_B1 * m_ref[...] + (1.0 - ADAM_B1) * g
        v2 = ADAM_B2 * v_ref[...] + (1.0 - ADAM_B2) * (g * g)
        g_ref[...] = g
        mo_ref[...] = m2
        vo_ref[...] = v2
        d_ref[...] = -ADAM_LR * ((m2 * c1) / (jnp.sqrt(v2 * c2) + ADAM_EPS) + ADAM_WD * w_ref[...])

    spec = pl.BlockSpec((tr, c), lambda i: (i, 0))
    return pl.pallas_call(
        body, name=name, grid=(r // tr,),
        in_specs=[pl.BlockSpec((N_DEV, tr, c), lambda i: (0, i, 0)), spec, spec, spec],
        out_specs=[spec] * 4, out_shape=[jax.ShapeDtypeStruct((r, c), F32)] * 4,
        compiler_params=_cparams(("parallel",)),
    )(parts, w, m, v)


def _rows_for(n, mult):
    return -(-n // (1024 * mult)) * mult


def _pack(arrs, mult, lead=()):
    nl = len(lead)
    flat = jnp.concatenate([a.reshape(lead + (-1,)) for a in arrs], axis=nl)
    rows = _rows_for(flat.shape[-1], mult)
    return jnp.pad(flat, [(0, 0)] * nl + [(0, rows * 1024 - flat.shape[-1])]).reshape(lead + (rows, 1024))


def _unpack(buf, shapes):
    lead = buf.shape[:-2]
    flat = buf.reshape(lead + (-1,))
    out, off = [], 0
    for s in shapes:
        n = 1
        for d_ in s:
            n *= d_
        out.append(flat[..., off:off + n].reshape(lead + tuple(s)))
        off += n
    return out


def _unshard(g8, ax):
    a = jnp.moveaxis(g8, 0, ax)
    return a.reshape(a.shape[:ax] + (N_DEV * a.shape[ax + 1],) + a.shape[ax + 2:])


def _shard8(full, ax):
    s = full.shape
    return jnp.moveaxis(full.reshape(s[:ax] + (N_DEV, s[ax] // N_DEV) + s[ax + 1:]), ax, 0)


_NAMES = ["meta", "ev_ln", "ev_w_in", "ev_b_if", "ev_a_norm", "ev_conv_w", "ev_conv_b", "ev_w_ra", "ev_b_ra",
          "ev_w_rx", "ev_b_rx", "ev_lam", "ev_w_out", "od_ln", "od_w_in", "od_b_f", "od_g_qa", "od_g_kva",
          "od_w_uq", "od_w_ukv", "od_g_qn", "od_g_qr", "od_g_kn", "od_g_kr", "od_g_fq", "od_g_fk", "od_w_out",
          "mlp_ln", "w_ff1", "w_ff2"]
_SHARD_AXIS = {"meta": 1, "ev_w_in": 2, "ev_conv_w": 2, "ev_w_out": 1, "od_ln": 1, "od_w_in": 2, "od_g_qa": 1,
               "od_g_kva": 1, "od_w_uq": 2, "od_w_ukv": 2, "od_w_out": 1, "w_ff1": 2, "w_ff2": 1}
_MATMUL_WEIGHTS = ("ev_w_in", "ev_w_out", "od_w_in", "od_w_uq", "od_w_ukv", "od_w_out", "w_ff1", "w_ff2")
_BIG_REPL = ("ev_w_ra", "ev_w_rx")
_COL_SHARDED = ("ev_w_in", "od_w_in", "od_w_uq", "od_w_ukv", "w_ff1_0", "w_ff1_1")
_GATHER_ON = {"mm_z0": ("od_w_in", "od_w_uq", "od_w_ukv"), "mlstm_fwd": ("ev_w_out", "w_ff1_0", "w_ff2_0"),
              "mla_attn_f": ("od_w_out", "w_ff1_1", "w_ff2_1")}
_SCATTER_ON = {"fox_attn_b": ("w_ff2_1", "w_ff1_1", "od_w_out"),
               "mm_dp0": ("od_w_ukv", "od_w_uq", "od_w_in"), "gates_b": ("ev_w_out",),
               "mlstm_bwd": ("w_ff2_0", "w_ff1_0"), "mm_dxn0": ("ev_w_in",)}
_REPL_ON = "mlstm_bwd"


def _unit_of(d, n):
    return d[n[:-2]][int(n[-1])] if n.startswith("w_ff") else d[n][0]


def _unit_full(n, g8):
    return jnp.transpose(g8, (1, 0, 2)).reshape(g8.shape[1], -1) if n in _COL_SHARDED else g8.reshape(-1, g8.shape[2])


def _unit_slots(n, full, r, c):
    return full.reshape(r, N_DEV, c).transpose(1, 0, 2) if n in _COL_SHARDED else full.reshape(N_DEV, r, c)


class _Plan:
    def __init__(self, shards):
        self.shards = shards
        self.parts = {}

    def slots(self, n, g):
        r, c = self.shards[n].shape
        return _unit_slots(n, _mw_unpad(n, g[n]), r, c).astype(BF16)

    def ride(self, tag, g):
        if tag in _GATHER_ON:
            return [(self.shards[n].astype(BF16), False) for n in _GATHER_ON[tag]]
        ops = [(self.slots(n, g), True) for n in _SCATTER_ON.get(tag, ())]
        if tag == _REPL_ON:
            ops += [(g[n].reshape(-1, B_BLOCK).astype(BF16), False) for n in _BIG_REPL]
        return ops

    def arrived(self, tag, got, w):
        if tag in _GATHER_ON:
            for n, g8 in zip(_GATHER_ON[tag], got):
                w[n] = _mw_pad(n, _unit_full(n, g8))
        else:
            self.parts.update(zip(_SCATTER_ON[tag] + (_BIG_REPL if tag == _REPL_ON else ()), got))


def kernel(x, positions, meta, ev_ln, ev_w_in, ev_b_if, ev_a_norm, ev_conv_w, ev_conv_b, ev_w_ra, ev_b_ra, ev_w_rx, ev_b_rx, ev_lam, ev_w_out, od_ln, od_w_in, od_b_f, od_g_qa, od_g_kva, od_w_uq, od_w_ukv, od_g_qn, od_g_qr, od_g_kn, od_g_kr, od_g_fq, od_g_fk, od_w_out, mlp_ln, w_ff1, w_ff2, loss_target, m_meta, m_ev_ln, m_ev_w_in, m_ev_b_if, m_ev_a_norm, m_ev_conv_w, m_ev_conv_b, m_ev_w_ra, m_ev_b_ra, m_ev_w_rx, m_ev_b_rx, m_ev_lam, m_ev_w_out, m_od_ln, m_od_w_in, m_od_b_f, m_od_g_qa, m_od_g_kva, m_od_w_uq, m_od_w_ukv, m_od_g_qn, m_od_g_qr, m_od_g_kn, m_od_g_kr, m_od_g_fq, m_od_g_fk, m_od_w_out, m_mlp_ln, m_w_ff1, m_w_ff2, v_meta, v_ev_ln, v_ev_w_in, v_ev_b_if, v_ev_a_norm, v_ev_conv_w, v_ev_conv_b, v_ev_w_ra, v_ev_b_ra, v_ev_w_rx, v_ev_b_rx, v_ev_lam, v_ev_w_out, v_od_ln, v_od_w_in, v_od_b_f, v_od_g_qa, v_od_g_kva, v_od_w_uq, v_od_w_ukv, v_od_g_qn, v_od_g_qr, v_od_g_kn, v_od_g_kr, v_od_g_fq, v_od_g_fk, v_od_w_out, v_mlp_ln, v_w_ff1, v_w_ff2):
    given = dict(locals())
    wts = {n: given[n] for n in _NAMES}
    mom = {n: given["m_" + n] for n in _NAMES}
    var = {n: given["v_" + n] for n in _NAMES}
    small_sh = [n for n in _NAMES if n in _SHARD_AXIS and n not in _MATMUL_WEIGHTS]
    small_rp = [n for n in _NAMES if n not in _SHARD_AXIS and n not in _BIG_REPL]
    shp = {n: wts[n].shape for n in _NAMES}
    plan = _Plan({n: _unit_of(wts, n) for n in _MM_UNITS})

    got = _exchange([(plan.shards["ev_w_in"].astype(BF16), False), (_pack([wts[n] for n in small_sh], 8), False)],
                    "gather_first")
    p = {n: wts[n] for n in _NAMES if n not in _SHARD_AXIS}
    for n, a in zip(small_sh, _unpack(got[1], [shp[n] for n in small_sh])):
        p[n] = _unshard(a, _SHARD_AXIS[n])
    p["ev_w_in"] = _unit_full("ev_w_in", got[0])

    loss, gx, g = _local_step(x[0], positions[0], loss_target[0], _prep_weights(p), plan)
    grads = _unprep_grads(g)

    parts = _exchange([(_pack([_shard8(grads[n], _SHARD_AXIS[n]) for n in small_sh], 8, (N_DEV,)), True),
                       (_pack([grads[n].reshape(shp[n]) for n in small_rp], 8), False)], "exchange_last")

    res = {}
    unit_res = {n: _adamw(plan.parts[n], *[_unit_of(d, n) for d in (wts, mom, var)], f"adamw_{n}")
                for n in _MM_UNITS}
    for n in _MATMUL_WEIGHTS:
        if n.startswith("w_ff"):
            res[n] = [jnp.stack([unit_res[n + "_0"][k], unit_res[n + "_1"][k]]) for k in range(4)]
        else:
            res[n] = [r[None] for r in unit_res[n]]
    r4 = _adamw(parts[0], *[_pack([d[n] for n in small_sh], 8) for d in (wts, mom, var)], "adamw_small_sharded")
    for n, *four in zip(small_sh, *[_unpack(r, [shp[n] for n in small_sh]) for r in r4]):
        res[n] = four
    for n in _BIG_REPL:
        r4 = _adamw(plan.parts[n], *[d[n].reshape(-1, B_BLOCK) for d in (wts, mom, var)], f"adamw_{n}")
        res[n] = [r.reshape(shp[n]) for r in r4]
    r4 = _adamw(parts[1], *[_pack([d[n] for n in small_rp], 8) for d in (wts, mom, var)], "adamw_small_repl")
    for n, *four in zip(small_rp, *[_unpack(r, [shp[n] for n in small_rp]) for r in r4]):
        res[n] = four

    outs = [res[n][kind] for kind in range(4) for n in _NAMES]
    loss = lax.psum(loss, ("x", "y", "c"))
    return (loss, gx[None], *outs)
```

```python
import functools

import jax
import jax.numpy as jnp
from jax import lax
from jax.experimental import pallas as pl
from jax.experimental.pallas import tpu as pltpu

F32 = jnp.float32
BF16 = jnp.bfloat16

D_MODEL = 1024
N_META = 16
PAD = 112
EPS = 1e-6
NEG = -1e30
A_HEADS, A_DQK, A_DV = 4, 128, 256
A_CHUNK = 384
B_BLOCKS, B_BLOCK, CONV_W, LRU_C = 8, 128, 4, 8.0
C_HEADS, C_Q_LORA, C_KV_LORA, C_NOPE, C_ROPE, C_V = 8, 384, 256, 64, 32, 64
ROPE_THETA = 10000.0
D_HEADS, D_HD = 8, 64
HP = 128
ZE = 5376
ZO = 4096
N_DEV = 8
ADAM_LR, ADAM_B1, ADAM_B2, ADAM_EPS, ADAM_WD, ADAM_STEP = 0.001, 0.9, 0.999, 1e-08, 0.01, 10
VMEM_LIMIT = 56 * 1024 * 1024
SCAN_BLOCK = 128


def _pick(n, prefs):
    for p in prefs:
        if n % p == 0:
            return p
    return n


def _cparams(dims):
    return pltpu.CompilerParams(dimension_semantics=dims, vmem_limit_bytes=VMEM_LIMIT)


def _full_spec(shape):
    nd = len(shape)
    return pl.BlockSpec(shape, lambda *_: (0,) * nd)


def _me_and_peers():
    mx, my, mc = lax.axis_index("x"), lax.axis_index("y"), lax.axis_index("c")
    peers = []
    for k in range(1, N_DEV):
        px, py, pc = mx ^ ((k >> 2) & 1), my ^ ((k >> 1) & 1), mc ^ (k & 1)
        peers.append(((px, py, pc), 4 * px + 2 * py + pc))
    return 4 * mx + 2 * my + mc, peers


def _comm_copies(x_refs, o_refs, flags, send_sems, recv_sems, local_sems):
    n_peer = N_DEV - 1
    mx, my, mc = lax.axis_index("x"), lax.axis_index("y"), lax.axis_index("c")
    me = 4 * mx + 2 * my + mc
    sibling = (mx, my, 1 - mc)
    chips = [(mx ^ a, my ^ b) for a, b in ((0, 1), (1, 0), (1, 1))]
    _, peers = _me_and_peers()
    out = dict(first=[], landed=[], forward=[], remote=[], local=[])

    def copy(a, k, src, dst, to):
        return pltpu.make_async_remote_copy(src_ref=src, dst_ref=dst, send_sem=send_sems.at[a * n_peer + k],
                                            recv_sem=recv_sems.at[a * n_peer + k], device_id=to,
                                            device_id_type=pl.DeviceIdType.MESH)

    for a, scatter in enumerate(flags):
        x_ref, o_ref = x_refs[a], o_refs[a]
        out["local"].append(pltpu.make_async_copy(x_ref.at[me] if scatter else x_ref, o_ref.at[me], local_sems.at[a]))
        if scatter:
            out["first"] += [copy(a, k, x_ref.at[pid], o_ref.at[me], peer) for k, (peer, pid) in enumerate(peers)]
        else:
            out["first"].append(copy(a, 0, x_ref, o_ref.at[me], sibling))
            for t, (px, py) in enumerate(chips):
                far = copy(a, 1 + t, x_ref, o_ref.at[me], (px, py, mc))
                slot = o_ref.at[4 * px + 2 * py + mc]
                out["first"].append(far)
                out["landed"].append(far)
                out["forward"].append(copy(a, 4 + t, slot, slot, sibling))
    out["remote"] = out["first"] + out["forward"]
    return out


def _comm_begin(c):
    for cp in c["local"] + c["first"]:
        cp.start()


def _comm_end(c):
    for cp in c["landed"]:
        cp.wait_recv()
    for cp in c["forward"]:
        cp.start()
    for cp in c["remote"]:
        cp.wait_send()
    for cp in c["remote"]:
        if all(cp is not d for d in c["landed"]):
            cp.wait_recv()
    for cp in c["local"]:
        cp.wait()


def _comm_shapes(comm):
    n = len(comm)
    out_shape = [jax.ShapeDtypeStruct((N_DEV,) + x.shape[-2:], x.dtype) for x, _ in comm]
    sems = [pltpu.SemaphoreType.DMA((n * (N_DEV - 1),)), pltpu.SemaphoreType.DMA((n * (N_DEV - 1),)),
            pltpu.SemaphoreType.DMA((n,))]
    return out_shape, sems


def _pcall(body, name, grid, in_specs, out_specs, out_shape, scratch_shapes, dims, ins, comm=None):
    if not comm:
        return pl.pallas_call(body, name=name, grid=grid, in_specs=in_specs, out_specs=out_specs,
                              out_shape=out_shape, scratch_shapes=scratch_shapes,
                              compiler_params=_cparams(dims))(*ins)
    n_in, n_out, n = len(in_specs), len(out_specs), len(comm)
    flags = [s for _, s in comm]
    c_shape, c_sems = _comm_shapes(comm)

    def riding(*refs):
        cx = refs[n_in:n_in + n]
        co = refs[n_in + n + n_out:n_in + 2 * n + n_out]
        rest = refs[n_in + 2 * n + n_out:]
        sems = rest[len(rest) - 3:]
        ids = [pl.program_id(a) for a in range(len(grid))]
        first = functools.reduce(jnp.logical_and, [i == 0 for i in ids])
        last = functools.reduce(jnp.logical_and, [i == g - 1 for i, g in zip(ids, grid)])

        @pl.when(first)
        def _():
            _comm_begin(_comm_copies(cx, co, flags, *sems))

        body(*refs[:n_in], *refs[n_in + n:n_in + n + n_out], *rest[:len(rest) - 3])

        @pl.when(last)
        def _():
            _comm_end(_comm_copies(cx, co, flags, *sems))

    any_spec = pl.BlockSpec(memory_space=pl.ANY)
    res = pl.pallas_call(
        riding, name=name, grid=grid, in_specs=list(in_specs) + [any_spec] * n,
        out_specs=list(out_specs) + [any_spec] * n, out_shape=list(out_shape) + c_shape,
        scratch_shapes=list(scratch_shapes) + c_sems,
        compiler_params=_cparams(("arbitrary",) * len(grid)))(*ins, *[x for x, _ in comm])
    return list(res[:n_out]), list(res[n_out:])


def _mm(a, b, mode, name, out_dtype=None, add=None, relu2=False, relu2_of=None, comm=None):
    if out_dtype is None:
        out_dtype = BF16 if (mode == "tn" or relu2_of is not None) else F32
    if relu2_of is not None:
        add = relu2_of
    if mode == "nn":
        (m, k), n = a.shape, b.shape[1]
    elif mode == "nt":
        (m, k), n = a.shape, b.shape[0]
    else:
        (k, m), n = a.shape, b.shape[1]
    tm = _pick(m, (1408, 1024, 768, 512, 384, 256, 128))
    tn = _pick(n, (1024, 768, 512, 384, 256, 128))
    tk = _pick(k, (1408, 1024, 768, 512, 384, 256, 128))
    nk = k // tk
    if mode == "nn":
        a_spec = pl.BlockSpec((tm, tk), lambda i, j, q: (i, q))
        b_spec = pl.BlockSpec((tk, tn), lambda i, j, q: (q, j))
        dn = (((1,), (0,)), ((), ()))
    elif mode == "nt":
        a_spec = pl.BlockSpec((tm, tk), lambda i, j, q: (i, q))
        b_spec = pl.BlockSpec((tn, tk), lambda i, j, q: (j, q))
        dn = (((1,), (1,)), ((), ()))
    else:
        a_spec = pl.BlockSpec((tk, tm), lambda i, j, q: (q, i))
        b_spec = pl.BlockSpec((tk, tn), lambda i, j, q: (q, j))
        dn = (((0,), (0,)), ((), ()))
    o_spec = pl.BlockSpec((tm, tn), lambda i, j, q: (i, j))
    has_add = add is not None

    def body(*refs):
        a_ref, b_ref = refs[:2]
        add_ref = refs[2] if has_add else None
        o_refs, acc = refs[2 + has_add:-1], refs[-1]
        q = pl.program_id(2)

        @pl.when(q == 0)
        def _():
            acc[...] = jnp.zeros_like(acc)

        acc[...] += lax.dot_general(a_ref[...].astype(BF16), b_ref[...].astype(BF16), dn,
                                    preferred_element_type=F32)

        @pl.when(q == nk - 1)
        def _():
            r = acc[...]
            if relu2_of is not None:
                r = r * (2.0 * jnp.maximum(add_ref[...], 0.0))
            elif has_add:
                r = r + add_ref[...]
            o_refs[0][...] = r.astype(o_refs[0].dtype)
            if relu2:
                pos = jnp.maximum(r, 0.0)
                o_refs[1][...] = (pos * pos).astype(o_refs[1].dtype)

    ins = [a, b] + ([add] if has_add else [])
    in_specs = [a_spec, b_spec] + ([o_spec] if has_add else [])
    out_shape = [jax.ShapeDtypeStruct((m, n), out_dtype)] + ([jax.ShapeDtypeStruct((m, n), BF16)] if relu2 else [])
    res = _pcall(body, name, (m // tm, n // tn, nk), in_specs, [o_spec] * len(out_shape), out_shape,
                 [pltpu.VMEM((tm, tn), F32)], ("parallel", "parallel", "arbitrary"), ins, comm)
    outs = res[0] if comm else res
    outs = tuple(outs) if relu2 else outs[0]
    return (outs, res[1]) if comm else outs


def _row_specs(rows, tm):
    return [pl.BlockSpec((tm, w), functools.partial(lambda cb, i: (i, cb), cb)) for (_, w, cb) in rows]


def _row_fwd(name, f, rows, params, outs, tm):
    tp = rows[0][0].shape[0]
    nr, npar = len(rows), len(params)

    def body(*refs):
        i = pl.program_id(0)
        rv = [r[...] for r in refs[:nr]]
        pv = [r[...] for r in refs[nr:nr + npar]]
        res = f(i, rv, pv)
        for o_ref, r in zip(refs[nr + npar:], res):
            o_ref[...] = r.astype(o_ref.dtype)

    res = pl.pallas_call(
        body, name=name, grid=(tp // tm,),
        in_specs=_row_specs(rows, tm) + [_full_spec(p.shape) for p in params],
        out_specs=[pl.BlockSpec((tm, w), lambda i: (i, 0)) for (w, _) in outs],
        out_shape=[jax.ShapeDtypeStruct((tp, w), dt) for (w, dt) in outs],
        compiler_params=_cparams(("parallel",)),
    )(*[r[0] for r in rows], *params)
    return list(res)


def _row_bwd(name, f, rows, params, douts, tm, diff, add=None, out_dtypes=None, comm=None):
    tp = rows[0][0].shape[0]
    nr, npar, nd = len(rows), len(params), len(douts)
    didx = [k for k in range(nr) if diff[k]]
    has_add = add is not None

    def body(*refs):
        i = pl.program_id(0)
        rv = [r[...] for r in refs[:nr]]
        pv = [r[...] for r in refs[nr:nr + npar]]
        dv = [r[...] for r in refs[nr + npar:nr + npar + nd]]
        pos = nr + npar + nd
        add_ref = refs[pos] if has_add else None
        pos += 1 if has_add else 0
        dr_refs = refs[pos:pos + len(didx)]
        dp_refs = refs[pos + len(didx):]

        def g(drv, pvs):
            full = list(rv)
            for k, val in zip(didx, drv):
                full[k] = val
            return tuple(f(i, full, list(pvs)))

        _, vjp = jax.vjp(g, [rv[k] for k in didx], pv)
        d_r, d_p = vjp(tuple(dv))
        for n_, (ref, val) in enumerate(zip(dr_refs, d_r)):
            if has_add and n_ == 0:
                val = val + add_ref[...]
            ref[...] = val.astype(ref.dtype)

        @pl.when(i == 0)
        def _():
            for ref in dp_refs:
                ref[...] = jnp.zeros_like(ref)

        for ref, val in zip(dp_refs, d_p):
            ref[...] += val

    in_specs = (_row_specs(rows, tm) + [_full_spec(p.shape) for p in params] + _row_specs(douts, tm))
    ins = [r[0] for r in rows] + list(params) + [d[0] for d in douts]
    if has_add:
        in_specs.append(pl.BlockSpec((tm, rows[didx[0]][1]), lambda i: (i, 0)))
        ins.append(add)
    out_specs = ([pl.BlockSpec((tm, rows[k][1]), lambda i: (i, 0)) for k in didx]
                 + [_full_spec(p.shape) for p in params])
    out_dtypes = out_dtypes or [F32] * len(didx)
    out_shape = ([jax.ShapeDtypeStruct((tp, rows[k][1]), dt) for k, dt in zip(didx, out_dtypes)]
                 + [jax.ShapeDtypeStruct(p.shape, F32) for p in params])
    res = _pcall(body, name, (tp // tm,), in_specs, out_specs, out_shape, [], ("arbitrary",), ins, comm)
    outs = res[0] if comm else res
    outs = (list(outs[:len(didx)]), list(outs[len(didx):]))
    return (outs, res[1]) if comm else outs


def _concat_cols(pieces, dtype, name):
    arrs = [p for p in pieces if not isinstance(p, int)]
    widths = [p if isinstance(p, int) else p.shape[1] for p in pieces]
    tp = arrs[0].shape[0]
    tm = _pick(tp, (384, 256, 128))

    def body(*refs):
        o_ref, k, off = refs[-1], 0, 0
        for p, w in zip(pieces, widths):
            if isinstance(p, int):
                o_ref[:, off:off + w] = jnp.zeros((tm, w), o_ref.dtype)
            else:
                o_ref[:, off:off + w] = refs[k][...].astype(o_ref.dtype)
                k += 1
            off += w

    return pl.pallas_call(
        body, name=name, grid=(tp // tm,),
        in_specs=[pl.BlockSpec((tm, a.shape[1]), lambda i: (i, 0)) for a in arrs],
        out_specs=pl.BlockSpec((tm, sum(widths)), lambda i: (i, 0)),
        out_shape=jax.ShapeDtypeStruct((tp, sum(widths)), dtype),
        compiler_params=_cparams(("parallel",)),
    )(*arrs)


def _rowmask(i, tm):
    return (i * tm + lax.broadcasted_iota(jnp.int32, (tm, 1), 0)) >= PAD


def _lane(n=HP):
    return lax.broadcasted_iota(jnp.int32, (1, n), 1)


def _softplus(x):
    return jnp.maximum(x, 0.0) + jnp.log(1.0 + jnp.exp(-jnp.abs(x)))


def _log_sigmoid(x):
    return -_softplus(-x)


def _sigmoid(x):
    return 1.0 / (1.0 + jnp.exp(-x))


@functools.partial(jax.custom_vjp, nondiff_argnums=(1,))
def _lroll(x, s):
    return pltpu.roll(x, s % HP, 1)


def _lroll_fwd(x, s):
    return _lroll(x, s), None


def _lroll_bwd(s, _, g):
    return (pltpu.roll(g, (-s) % HP, 1),)


_lroll.defvjp(_lroll_fwd, _lroll_bwd)


def _f_norm(i, rv, pv):
    (h,), (g,) = rv, pv
    return [h * lax.rsqrt(jnp.mean(h * h, axis=-1, keepdims=True) + EPS) * g]


def _f_gate0(i, rv, pv):
    (misc,), (b,) = rv, pv
    tm = misc.shape[0]
    x = misc + b
    lane, ok = _lane(), _rowmask(i, tm)
    li = jnp.where(ok, x, NEG)
    lf = jnp.where(ok, _log_sigmoid(x), 0.0)
    return [jnp.where(lane < A_HEADS, li, jnp.where(lane < 2 * A_HEADS, lf, 0.0))]


def _f_aout(i, rv, pv):
    (ha, o), (g,) = rv, pv
    outs = []
    for h in range(A_HEADS):
        x = ha[:, h * A_DV:(h + 1) * A_DV]
        outs.append(x * lax.rsqrt(jnp.mean(x * x, axis=-1, keepdims=True) + EPS) * g)
    return [jnp.concatenate(outs, axis=1) * _sigmoid(o)]


def _f_gates(i, rv, pv):
    (xc,), (w_ra, b_ra, w_rx, b_rx, lam) = rv, pv
    tm = xc.shape[0]
    ra, rx = [], []
    for g in range(B_BLOCKS):
        xg = xc[:, g * B_BLOCK:(g + 1) * B_BLOCK].astype(BF16)
        ra.append(jnp.dot(xg, w_ra[g].astype(BF16), preferred_element_type=F32))
        rx.append(jnp.dot(xg, w_rx[g].astype(BF16), preferred_element_type=F32))
    r = _sigmoid(jnp.concatenate(ra, axis=1) + b_ra)
    ig = _sigmoid(jnp.concatenate(rx, axis=1) + b_rx)
    log_a = -LRU_C * r * _softplus(-lam)
    a = jnp.exp(log_a)
    u = jnp.sqrt(1.0 - jnp.exp(2.0 * log_a)) * (ig * xc)
    return [a, jnp.where(_rowmask(i, tm), u, 0.0)]


def _f_bout(i, rv, pv):
    hs, gb = rv
    gelu = 0.5 * gb * (1.0 + jnp.tanh(0.7978845608028654 * (gb + 0.044715 * gb * gb * gb)))
    return [hs * gelu]


def _f_cnorm(i, rv, pv):
    (cq, ckv), (gq, gkv) = rv, pv
    return [cq * lax.rsqrt(jnp.mean(cq * cq, axis=-1, keepdims=True) + EPS) * gq,
            ckv * lax.rsqrt(jnp.mean(ckv * ckv, axis=-1, keepdims=True) + EPS) * gkv]


def _rope128(x, cos, sin):
    lane = _lane()
    rot = jnp.where((lane >= C_NOPE) & (lane < C_NOPE + C_ROPE // 2), -_lroll(x, -(C_ROPE // 2)),
                    jnp.where((lane >= C_NOPE + C_ROPE // 2) & (lane < C_NOPE + C_ROPE), _lroll(x, C_ROPE // 2), 0.0))
    return x * cos + rot * sin


def _f_mlaprep(i, rv, pv):
    (q_, kk_, misc, cos, sin), (gq, gk) = rv, pv
    lane = _lane()
    m_n = lane < C_NOPE
    m_r = (lane >= C_NOPE) & (lane < C_NOPE + C_ROPE)

    def norm2(x, g):
        x2 = x * x
        sn = jnp.sum(jnp.where(m_n, x2, 0.0), axis=-1, keepdims=True) * (1.0 / C_NOPE)
        sr = jnp.sum(jnp.where(m_r, x2, 0.0), axis=-1, keepdims=True) * (1.0 / C_ROPE)
        scale = jnp.where(m_n, lax.rsqrt(sn + EPS), jnp.where(m_r, lax.rsqrt(sr + EPS), 0.0))
        return x * scale * g

    kr = _rope128(norm2(jnp.where(m_r, misc, 0.0), gk), cos, sin)
    qs, ks = [], []
    for h in range(C_HEADS):
        qs.append(_rope128(norm2(q_[:, h * HP:(h + 1) * HP], gq), cos, sin))
        ks.append(norm2(jnp.where(m_n, kk_[:, h * HP:(h + 1) * HP], 0.0), gk) + kr)
    return [jnp.concatenate(qs, axis=1), jnp.concatenate(ks, axis=1)]


def _f_foxprep(i, rv, pv):
    (fq, fk, misc), (gq, gk, bf) = rv, pv
    tm = fq.shape[0]
    lane = _lane()

    def hnorm(x, g):
        outs = []
        for h in range(D_HEADS):
            xh = x[:, h * HP:(h + 1) * HP]
            ss = jnp.sum(xh * xh, axis=-1, keepdims=True) * (1.0 / D_HD)
            outs.append(xh * lax.rsqrt(ss + EPS) * g)
        return jnp.concatenate(outs, axis=1)

    lf = jnp.where(_rowmask(i, tm) & (lane >= 96) & (lane < 96 + D_HEADS), _log_sigmoid(misc + bf), 0.0)
    return [hnorm(fq, gq), hnorm(fk, gk), lf]


def _mlstm_chunk(c, n, m, q, k, v, li, lf):
    ln = q.shape[0]
    r = lax.broadcasted_iota(jnp.int32, (ln, ln), 0)
    cc = lax.broadcasted_iota(jnp.int32, (ln, ln), 1)
    causal = cc <= r
    eye = cc == r
    li_row = jnp.sum(jnp.where(eye, li, 0.0), axis=0, keepdims=True)
    b_col = jnp.sum(jnp.where(causal, jnp.sum(jnp.where(eye, lf, 0.0), axis=0, keepdims=True), 0.0),
                    axis=1, keepdims=True)
    b_row = jnp.sum(jnp.where(r <= cc, lf, 0.0), axis=0, keepdims=True)
    k = k * (A_DQK ** -0.5)
    qb, kb, vb = q.astype(BF16), k.astype(BF16), v.astype(BF16)
    dmat = jnp.where(causal, b_col - b_row + li_row, NEG)
    inter = b_col + m
    m_row = jnp.maximum(inter, jnp.max(dmat, axis=1, keepdims=True))
    w_intra = jnp.exp(dmat - m_row)
    w_inter = jnp.exp(inter - m_row)
    s = lax.dot_general(qb, kb, (((1,), (1,)), ((), ())), preferred_element_type=F32) * w_intra
    num = (w_inter * jnp.dot(qb, c.astype(BF16), preferred_element_type=F32)
           + jnp.dot(s.astype(BF16), vb, preferred_element_type=F32))
    den = w_inter * jnp.sum(q * n, axis=1, keepdims=True) + jnp.sum(s, axis=1, keepdims=True)
    h = num / jnp.maximum(jnp.abs(den), jnp.exp(-m_row))
    g = jnp.sum(lf, axis=0, keepdims=True)
    dk = g - b_col + li
    m_new = jnp.maximum(g + m, jnp.max(dk, axis=0, keepdims=True))
    wk = jnp.exp(dk - m_new)
    sc = jnp.exp(g + m - m_new)
    kw = wk * k
    c_new = sc * c + lax.dot_general(kw.astype(BF16), vb, (((0,), (0,)), ((), ())), preferred_element_type=F32)
    n_new = sc * n + jnp.sum(kw, axis=0, keepdims=True)
    return c_new, n_new, m_new, h


A_QKV = 2 * A_HEADS * A_DQK + A_HEADS * A_DV


def _mlstm_chunk_all(cs, ns, ms, zqkv, gates):
    lane = _lane()
    c2, n2, m2, hs = [], [], [], []
    for h in range(A_HEADS):
        q = zqkv[:, h * A_DQK:(h + 1) * A_DQK]
        k = zqkv[:, (A_HEADS + h) * A_DQK:(A_HEADS + h + 1) * A_DQK]
        v = zqkv[:, 2 * A_HEADS * A_DQK + h * A_DV:2 * A_HEADS * A_DQK + (h + 1) * A_DV]
        li = jnp.sum(jnp.where(lane == h, gates, 0.0), axis=1, keepdims=True)
        lf = jnp.sum(jnp.where(lane == A_HEADS + h, gates, 0.0), axis=1, keepdims=True)
        c, n, m, hh = _mlstm_chunk(cs[h], ns[h], ms[h], q, k, v, li, lf)
        c2.append(c)
        n2.append(n)
        m2.append(m)
        hs.append(hh)
    return c2, n2, m2, jnp.concatenate(hs, axis=1)


def _mlstm_state_specs(nc, index):
    return ([pl.BlockSpec((1, A_HEADS, A_DQK, A_DV), lambda j: (index(j), 0, 0, 0)),
             pl.BlockSpec((1, A_HEADS, 1, A_DQK), lambda j: (index(j), 0, 0, 0)),
             pl.BlockSpec((1, A_HEADS, 1, 1), lambda j: (index(j), 0, 0, 0))],
            [jax.ShapeDtypeStruct((nc, A_HEADS, A_DQK, A_DV), F32),
             jax.ShapeDtypeStruct((nc, A_HEADS, 1, A_DQK), F32),
             jax.ShapeDtypeStruct((nc, A_HEADS, 1, 1), F32)],
            [pltpu.VMEM((A_HEADS, A_DQK, A_DV), F32), pltpu.VMEM((A_HEADS, 1, A_DQK), F32),
             pltpu.VMEM((A_HEADS, 1, 1), F32)])


def _mlstm_fwd(z, gates, comm=None):
    tp = z.shape[0]
    nc = tp // A_CHUNK
    ln = A_CHUNK
    heads = range(A_HEADS)

    def body(z_ref, g_ref, h_ref, cs_ref, ns_ref, ms_ref, c_s, n_s, m_s):
        @pl.when(pl.program_id(0) == 0)
        def _():
            c_s[...] = jnp.zeros_like(c_s)
            n_s[...] = jnp.zeros_like(n_s)
            m_s[...] = jnp.zeros_like(m_s)

        cs_ref[0] = c_s[...]
        ns_ref[0] = n_s[...]
        ms_ref[0] = m_s[...]
        c2, n2, m2, h = _mlstm_chunk_all([c_s[i] for i in heads], [n_s[i] for i in heads], [m_s[i] for i in heads],
                                         z_ref[...], g_ref[...])
        for i in heads:
            c_s[i] = c2[i]
            n_s[i] = n2[i]
            m_s[i] = m2[i]
        h_ref[...] = h

    st_specs, st_shapes, st_scratch = _mlstm_state_specs(nc, lambda j: j)
    return _pcall(
        body, "mlstm_fwd", (nc,),
        [pl.BlockSpec((ln, A_QKV), lambda j: (j, 0)), pl.BlockSpec((ln, HP), lambda j: (j, 0))],
        [pl.BlockSpec((ln, A_HEADS * A_DV), lambda j: (j, 0))] + st_specs,
        [jax.ShapeDtypeStruct((tp, A_HEADS * A_DV), F32)] + st_shapes,
        st_scratch, ("arbitrary",), (z, gates), comm)


def _mlstm_bwd(z, gates, cs, ns, ms, dh, comm=None):
    tp = z.shape[0]
    nc = tp // A_CHUNK
    ln = A_CHUNK
    heads = range(A_HEADS)

    def body(z_ref, g_ref, cs_ref, ns_ref, ms_ref, dh_ref, dz_ref, dg_ref, dc_s, dn_s, dm_s):
        @pl.when(pl.program_id(0) == 0)
        def _():
            dc_s[...] = jnp.zeros_like(dc_s)
            dn_s[...] = jnp.zeros_like(dn_s)
            dm_s[...] = jnp.zeros_like(dm_s)

        _, vjp = jax.vjp(_mlstm_chunk_all, [cs_ref[0, i] for i in heads], [ns_ref[0, i] for i in heads],
                         [ms_ref[0, i] for i in heads], z_ref[...], g_ref[...])
        dc, dn, dm, dz, dg = vjp(([dc_s[i] for i in heads], [dn_s[i] for i in heads], [dm_s[i] for i in heads],
                                  dh_ref[...]))
        for i in heads:
            dc_s[i] = dc[i]
            dn_s[i] = dn[i]
            dm_s[i] = dm[i]
        dz_ref[...] = dz.astype(BF16)
        dg_ref[...] = dg

    def rj(j):
        return nc - 1 - j

    st_specs, _, st_scratch = _mlstm_state_specs(nc, rj)
    return _pcall(
        body, "mlstm_bwd", (nc,),
        [pl.BlockSpec((ln, A_QKV), lambda j: (rj(j), 0)), pl.BlockSpec((ln, HP), lambda j: (rj(j), 0))] + st_specs
        + [pl.BlockSpec((ln, A_HEADS * A_DV), lambda j: (rj(j), 0))],
        [pl.BlockSpec((ln, A_QKV), lambda j: (rj(j), 0)), pl.BlockSpec((ln, HP), lambda j: (rj(j), 0))],
        [jax.ShapeDtypeStruct((tp, A_QKV), BF16), jax.ShapeDtypeStruct((tp, HP), F32)],
        st_scratch, ("arbitrary",), (z, gates, cs, ns, ms, dh), comm)


def _shift_down(x, s, row):
    return x if s == 0 else jnp.where(row >= s, pltpu.roll(x, s, 0), 0.0)


def _shift_up(x, s, row):
    n = x.shape[0]
    return x if s == 0 else jnp.where(row < n - s, pltpu.roll(x, n - s, 0), 0.0)


def _conv_fwd(z, xcb, w, b):
    tp, c = z.shape[0], w.shape[1]
    ct = 256

    def body(x_ref, w_ref, b_ref, o_ref):
        x = x_ref[...]
        row = lax.broadcasted_iota(jnp.int32, (tp, 1), 0)
        acc = jnp.zeros_like(x) + b_ref[...]
        for k in range(CONV_W):
            acc = acc + w_ref[k:k + 1, :] * _shift_down(x, CONV_W - 1 - k, row)
        o_ref[...] = acc

    return pl.pallas_call(
        body, name="conv_fwd", grid=(c // ct,),
        in_specs=[pl.BlockSpec((tp, ct), lambda j: (0, xcb + j)), pl.BlockSpec((CONV_W, ct), lambda j: (0, j)),
                  pl.BlockSpec((1, ct), lambda j: (0, j))],
        out_specs=pl.BlockSpec((tp, ct), lambda j: (0, j)),
        out_shape=jax.ShapeDtypeStruct((tp, c), F32),
        compiler_params=_cparams(("parallel",)),
    )(z, w, b)


def _conv_bwd(z, xcb, w, dxc):
    tp, c = z.shape[0], w.shape[1]
    ct = 256

    def body(x_ref, w_ref, d_ref, dx_ref, dw_ref, db_ref):
        x, d = x_ref[...], d_ref[...]
        row = lax.broadcasted_iota(jnp.int32, (tp, 1), 0)
        acc = jnp.zeros_like(x)
        for k in range(CONV_W):
            s = CONV_W - 1 - k
            acc = acc + w_ref[k:k + 1, :] * _shift_up(d, s, row)
            dw_ref[k:k + 1, :] = jnp.sum(d * _shift_down(x, s, row), axis=0, keepdims=True)
        dx_ref[...] = acc.astype(BF16)
        db_ref[...] = jnp.sum(d, axis=0, keepdims=True)

    return pl.pallas_call(
        body, name="conv_bwd", grid=(c // ct,),
        in_specs=[pl.BlockSpec((tp, ct), lambda j: (0, xcb + j)), pl.BlockSpec((CONV_W, ct), lambda j: (0, j)),
                  pl.BlockSpec((tp, ct), lambda j: (0, j))],
        out_specs=[pl.BlockSpec((tp, ct), lambda j: (0, j)), pl.BlockSpec((CONV_W, ct), lambda j: (0, j)),
                   pl.BlockSpec((1, ct), lambda j: (0, j))],
        out_shape=[jax.ShapeDtypeStruct((tp, c), BF16), jax.ShapeDtypeStruct((CONV_W, c), F32),
                   jax.ShapeDtypeStruct((1, c), F32)],
        compiler_params=_cparams(("parallel",)),
    )(z, w, dxc)


def _scan_fwd(a, u, name):
    tp, c = a.shape
    ct = _pick(c, (256, 128))
    lb = SCAN_BLOCK
    nb = tp // lb

    def body(a_ref, u_ref, h_ref, hp_ref):
        row = lax.broadcasted_iota(jnp.int32, (lb, 1), 0)

        def blk(j, carry):
            r0 = pl.multiple_of(j * lb, lb)
            aa, uu = a_ref[pl.ds(r0, lb), :], u_ref[pl.ds(r0, lb), :]
            s = 1
            while s < lb:
                mk = row >= s
                uu = jnp.where(mk, aa * pltpu.roll(uu, s, 0) + uu, uu)
                aa = jnp.where(mk, aa * pltpu.roll(aa, s, 0), aa)
                s *= 2
            hh = uu + aa * carry
            h_ref[pl.ds(r0, lb), :] = hh
            hp_ref[pl.ds(r0, lb), :] = jnp.where(row >= 1, pltpu.roll(hh, 1, 0), carry)
            return hh[lb - 1:lb, :]

        lax.fori_loop(0, nb, blk, jnp.zeros((1, ct), F32))

    spec = pl.BlockSpec((tp, ct), lambda j: (0, j))
    return pl.pallas_call(
        body, name=name, grid=(c // ct,), in_specs=[spec, spec], out_specs=[spec, spec],
        out_shape=[jax.ShapeDtypeStruct((tp, c), F32)] * 2,
        compiler_params=_cparams(("parallel",)),
    )(a, u)


def _scan_bwd(a, hprev, dh, name):
    tp, c = a.shape
    ct = _pick(c, (256, 128))
    lb = SCAN_BLOCK
    nb = tp // lb

    def body(a_ref, hp_ref, dh_ref, du_ref, da_ref):
        row = lax.broadcasted_iota(jnp.int32, (lb, 1), 0)

        def blk(jj, carry):
            g_next, a_next = carry
            r0 = pl.multiple_of((nb - 1 - jj) * lb, lb)
            a_blk = a_ref[pl.ds(r0, lb), :]
            aa = jnp.where(row < lb - 1, pltpu.roll(a_blk, lb - 1, 0), a_next)
            gg = dh_ref[pl.ds(r0, lb), :]
            s = 1
            while s < lb:
                mk = row < lb - s
                gg = jnp.where(mk, aa * pltpu.roll(gg, lb - s, 0) + gg, gg)
                aa = jnp.where(mk, aa * pltpu.roll(aa, lb - s, 0), aa)
                s *= 2
            gg = gg + aa * g_next
            du_ref[pl.ds(r0, lb), :] = gg
            da_ref[pl.ds(r0, lb), :] = gg * hp_ref[pl.ds(r0, lb), :]
            return gg[0:1, :], a_blk[0:1, :]

        lax.fori_loop(0, nb, blk, (jnp.zeros((1, ct), F32), jnp.zeros((1, ct), F32)))

    spec = pl.BlockSpec((tp, ct), lambda j: (0, j))
    return pl.pallas_call(
        body, name=name, grid=(c // ct,), in_specs=[spec, spec, spec], out_specs=[spec, spec],
        out_shape=[jax.ShapeDtypeStruct((tp, c), F32)] * 2,
        compiler_params=_cparams(("parallel",)),
    )(a, hprev, dh)


ATT_HEADS_PER_STEP = 2
LOG2E = 1.4426950408889634
LN2 = 0.6931471805599453


FOX_LANE0 = 96


def _head_lane(f, head):
    return jnp.sum(jnp.where(_lane() == FOX_LANE0 + head, f, 0.0), axis=1, keepdims=True)


def _key_terms(tp, tq, bk=None):
    pad_neg = jnp.where(jnp.arange(tp) < PAD, NEG, 0.0).astype(F32).reshape(1, tp // tq, 1, tq)
    r = lax.broadcasted_iota(jnp.int32, (tq, tq), 0)
    c = lax.broadcasted_iota(jnp.int32, (tq, tq), 1)
    diag = jnp.stack([jnp.where((c <= r) & ((c >= PAD) | (c == r)), 0.0, NEG),
                      jnp.where(c <= r, 0.0, NEG)]).astype(F32)
    if bk is None:
        return pad_neg, jnp.zeros_like(pad_neg), diag
    kd = -bk.reshape(bk.shape[0], tp // tq, 1, tq)
    return kd + pad_neg, kd, diag


def _attn_fwd(q, k, v, vcb, scale, bq, kterms, name, comm=None):
    tp = q.shape[0]
    nh = q.shape[1] // HP
    tq = _pick(tp, (384, 256, 128))
    has_bq = bq is not None
    c1 = scale * LOG2E
    per_head = kterms[0].shape[0] > 1
    hg = 2 * ATT_HEADS_PER_STEP
    lanes = [slice(hh * HP, (hh + 1) * HP) for hh in range(hg)]

    def body(*refs):
        if has_bq:
            q_ref, k_ref, v_ref, kb_ref, kd_ref, dm_ref, bq_ref, o_ref, lse_ref = refs
            rb = [_head_lane(bq_ref[...], hg * pl.program_id(0) + hh) * LOG2E for hh in range(hg)]
        else:
            q_ref, k_ref, v_ref, kb_ref, kd_ref, dm_ref, o_ref, lse_ref = refs
        i = pl.program_id(1)
        qb = [q_ref[:, ln].astype(BF16) for ln in lanes]

        def tile(j, carry, diag):
            r0 = pl.multiple_of(j * tq, tq)
            qk = [lax.dot_general(qb[hh], k_ref[pl.ds(r0, tq), lanes[hh]].astype(BF16), (((1,), (1,)), ((), ())),
                                  preferred_element_type=F32) for hh in range(hg)]
            out = []
            for hh in range(hg):
                m, l, acc = carry[hh]
                vb = v_ref[pl.ds(r0, tq), lanes[hh]].astype(BF16)
                kt = (kd_ref if diag else kb_ref)[hh if per_head else 0, j] * LOG2E
                x = qk[hh] * c1 + kt
                if has_bq:
                    x = x + rb[hh]
                if diag:
                    x = x + dm_ref[0]
                m2 = jnp.maximum(m, jnp.max(x, axis=1, keepdims=True))
                alpha = jnp.exp2(m - m2)
                p = jnp.exp2(x - m2)
                l2 = alpha * l + jnp.sum(p, axis=1, keepdims=True)
                acc2 = alpha * acc + jnp.dot(p.astype(BF16), vb, preferred_element_type=F32)
                out.append((m2, l2, acc2))
            return tuple(out)

        init = tuple((jnp.full((tq, 1), NEG, F32), jnp.zeros((tq, 1), F32), jnp.zeros((tq, HP), F32))
                     for _ in range(hg))
        res = tile(i, lax.fori_loop(0, i, lambda j, c: tile(j, c, False), init), True)
        for hh, (m, l, acc) in enumerate(res):
            o_ref[:, lanes[hh]] = acc / l
            lse_ref[hh] = m * LN2 + jnp.log(l)

    kt_spec = pl.BlockSpec((hg if per_head else 1, tp // tq, 1, tq),
                           (lambda h, i: (h, 0, 0, 0)) if per_head else (lambda h, i: (0, 0, 0, 0)))
    in_specs = [pl.BlockSpec((tq, hg * HP), lambda h, i: (i, h)), pl.BlockSpec((tp, hg * HP), lambda h, i: (0, h)),
                pl.BlockSpec((tp, hg * HP), lambda h, i: (0, vcb // hg + h)), kt_spec, kt_spec,
                pl.BlockSpec((1, tq, tq), lambda h, i: (jnp.minimum(i, 1), 0, 0))]
    ins = [q, k, v, *kterms]
    if has_bq:
        in_specs += [pl.BlockSpec((tq, HP), lambda h, i: (i, 0))]
        ins += [bq]
    return _pcall(
        body, name, (nh // hg, tp // tq), in_specs,
        [pl.BlockSpec((tq, hg * HP), lambda h, i: (i, h)), pl.BlockSpec((hg, tq, 1), lambda h, i: (h, i, 0))],
        [jax.ShapeDtypeStruct((tp, nh * HP), F32), jax.ShapeDtypeStruct((nh, tp, 1), F32)],
        [], ("parallel", "parallel"), ins, comm)


def _attn_bwd(q, k, v, vcb, o, lse, do, docb, scale, bq, kterms, name, comm=None):
    tp = q.shape[0]
    nh = q.shape[1] // HP
    tq = _pick(tp, (384, 256, 128))
    has_bq = bq is not None
    c1 = scale * LOG2E
    per_head = kterms[0].shape[0] > 1
    hg = ATT_HEADS_PER_STEP
    lanes = [slice(hh * HP, (hh + 1) * HP) for hh in range(hg)]

    def body(*refs):
        if has_bq:
            (q_ref, k_ref, v_ref, o_ref, lse_ref, do_ref, kb_ref, kd_ref, dm_ref, bq_ref,
             dq_ref, dk_ref, dv_ref, dbq_ref, dkt_ref) = refs
            rb = [(_head_lane(bq_ref[...], hg * pl.program_id(0) + hh) - lse_ref[hh]) * LOG2E for hh in range(hg)]
        else:
            q_ref, k_ref, v_ref, o_ref, lse_ref, do_ref, kb_ref, kd_ref, dm_ref, dq_ref, dk_ref, dv_ref = refs
            rb = [lse_ref[hh] * (-LOG2E) for hh in range(hg)]
        i = pl.program_id(1)

        @pl.when(i == 0)
        def _():
            dk_ref[...] = jnp.zeros_like(dk_ref)
            dv_ref[...] = jnp.zeros_like(dv_ref)
            if has_bq:
                dkt_ref[...] = jnp.zeros_like(dkt_ref)

        qb = [q_ref[:, ln].astype(BF16) for ln in lanes]
        do_ = [do_ref[:, ln] for ln in lanes]
        dob = [d.astype(BF16) for d in do_]
        delta = [jnp.sum(do_[hh] * o_ref[:, lanes[hh]], axis=1, keepdims=True) for hh in range(hg)]

        def tile(j, carry, diag):
            r0 = pl.multiple_of(j * tq, tq)
            kbs = [k_ref[pl.ds(r0, tq), lanes[hh]].astype(BF16) for hh in range(hg)]
            qk = [lax.dot_general(qb[hh], kbs[hh], (((1,), (1,)), ((), ())), preferred_element_type=F32)
                  for hh in range(hg)]
            dps = [lax.dot_general(dob[hh], v_ref[pl.ds(r0, tq), lanes[hh]].astype(BF16), (((1,), (1,)), ((), ())),
                                   preferred_element_type=F32) for hh in range(hg)]
            out = []
            for hh in range(hg):
                dq, dbq = carry[hh]
                kb = kbs[hh]
                kt = (kd_ref if diag else kb_ref)[hh if per_head else 0, j] * LOG2E
                x = qk[hh] * c1 + kt + rb[hh]
                if diag:
                    x = x + dm_ref[0]
                p = jnp.exp2(x)
                ds = p * (dps[hh] - delta[hh])
                dsb = ds.astype(BF16)
                dk_ref[pl.ds(r0, tq), lanes[hh]] += lax.dot_general(dsb, qb[hh], (((0,), (0,)), ((), ())),
                                                                    preferred_element_type=F32) * scale
                dv_ref[pl.ds(r0, tq), lanes[hh]] += lax.dot_general(p.astype(BF16), dob[hh], (((0,), (0,)), ((), ())),
                                                                    preferred_element_type=F32)
                if has_bq:
                    dkt_ref[hh, j] += jnp.sum(ds, axis=0, keepdims=True)
                    dbq = dbq + jnp.sum(ds, axis=1, keepdims=True)
                out.append((dq + jnp.dot(dsb, kb, preferred_element_type=F32), dbq))
            return tuple(out)

        init = tuple((jnp.zeros((tq, HP), F32), jnp.zeros((tq, 1), F32)) for _ in range(hg))
        res = tile(i, lax.fori_loop(0, i, lambda j, c: tile(j, c, False), init), True)
        for hh, (dq, dbq) in enumerate(res):
            dq_ref[:, lanes[hh]] = dq * scale
            if has_bq:
                dbq_ref[hh] = dbq

    blk_q = pl.BlockSpec((tq, hg * HP), lambda h, i: (i, h))
    blk_k = pl.BlockSpec((tp, hg * HP), lambda h, i: (0, h))
    kt_spec = pl.BlockSpec((hg if per_head else 1, tp // tq, 1, tq),
                           (lambda h, i: (h, 0, 0, 0)) if per_head else (lambda h, i: (0, 0, 0, 0)))
    row_spec = pl.BlockSpec((hg, tq, 1), lambda h, i: (h, i, 0))
    in_specs = [blk_q, blk_k, pl.BlockSpec((tp, hg * HP), lambda h, i: (0, vcb // hg + h)), blk_q,
                row_spec, pl.BlockSpec((tq, hg * HP), lambda h, i: (i, docb // hg + h)), kt_spec, kt_spec,
                pl.BlockSpec((1, tq, tq), lambda h, i: (jnp.minimum(i, 1), 0, 0))]
    ins = [q, k, v, o, lse, do, *kterms]
    out_specs = [blk_q, blk_k, blk_k]
    out_shape = [jax.ShapeDtypeStruct((tp, nh * HP), F32)] * 3
    if has_bq:
        in_specs += [pl.BlockSpec((tq, HP), lambda h, i: (i, 0))]
        ins += [bq]
        out_specs += [row_spec, kt_spec]
        out_shape += [jax.ShapeDtypeStruct((nh, tp, 1), F32), jax.ShapeDtypeStruct((nh, tp // tq, 1, tq), F32)]
    return _pcall(body, name, (nh // hg, tp // tq), in_specs, out_specs, out_shape, [], ("parallel", "arbitrary"), ins,
                  comm)


def _loss_head(h, tgt):
    tp, d = h.shape
    tm = 128
    first = (PAD + N_META) // tm

    def body(h_ref, t_ref, l_ref, d_ref):
        i = pl.program_id(0)

        @pl.when(i == 0)
        def _():
            l_ref[...] = jnp.zeros_like(l_ref)

        live = i >= first
        err = jnp.where(live, h_ref[...] - t_ref[...], 0.0)
        d_ref[...] = err * (1.0 / d)
        l_ref[...] += (0.5 / d) * jnp.sum(err * err)

    return pl.pallas_call(
        body, name="loss_head", grid=(tp // tm,),
        in_specs=[pl.BlockSpec((tm, d), lambda i: (i, 0)), pl.BlockSpec((tm, d), lambda i: (i, 0))],
        out_specs=[_full_spec((8, 128)), pl.BlockSpec((tm, d), lambda i: (i, 0))],
        out_shape=[jax.ShapeDtypeStruct((8, 128), F32), jax.ShapeDtypeStruct((tp, d), F32)],
        compiler_params=_cparams(("arbitrary",)),
    )(h, tgt)


def _rope_tables(pos_rows):
    half = C_ROPE // 2
    freqs = ROPE_THETA ** (-jnp.arange(half, dtype=F32) / half)
    ang = pos_rows[:, None].astype(F32) * freqs
    cos, sin = jnp.cos(ang), jnp.sin(ang)
    tp = pos_rows.shape[0]
    one, zero = jnp.ones((tp, C_NOPE), F32), jnp.zeros((tp, C_NOPE), F32)
    tail1, tail0 = jnp.ones((tp, HP - C_NOPE - C_ROPE), F32), jnp.zeros((tp, HP - C_NOPE - C_ROPE), F32)
    return (jnp.concatenate([one, cos, cos, tail1], axis=1), jnp.concatenate([zero, sin, sin, tail0], axis=1))


def _heads_to_cols(x, lo, n):
    t = x[:, lo:lo + n].T
    return t[:, :, None], t[:, None, :]


def _local_step(x, positions, tgt, w, plan=None):
    def hosted(tag, fn, *args, **kw):
        ops = plan.ride(tag, g) if plan is not None else None
        if not ops:
            return fn(*args, **kw)
        res, got = fn(*args, comm=ops, **kw)
        plan.arrived(tag, got, w)
        return res

    s_len = x.shape[0]
    tp = PAD + N_META + s_len
    tm = _pick(tp, (384, 256, 128))
    front = PAD + N_META
    h0 = jnp.concatenate([jnp.zeros((PAD, D_MODEL), F32), w["meta"], x], axis=0)
    tgt_p = jnp.concatenate([jnp.zeros((front, D_MODEL), F32), tgt], axis=0)
    pos_rows = jnp.concatenate([jnp.zeros((PAD,), jnp.int32), jnp.arange(N_META, dtype=jnp.int32),
                                positions + N_META])
    cos, sin = _rope_tables(pos_rows)
    g = {}

    r_h0 = [(h0, D_MODEL, 0)]
    (xn0,) = _row_fwd("norm0_f", _f_norm, r_h0, [w["ev_ln"]], [(D_MODEL, BF16)], tm)
    z0 = hosted("mm_z0", _mm, xn0, w["ev_w_in"], "nn", "mm_z0")
    r_misc0 = [(z0, HP, 5120 // HP)]
    (g0,) = _row_fwd("gate0_f", _f_gate0, r_misc0, [w["ev_b_if"]], [(HP, F32)], tm)
    h_a, cs, ns, ms = hosted("mlstm_fwd", _mlstm_fwd, z0, g0)
    r_aout = [(h_a, 1024, 0), (z0, 1024, 2)]
    (ha,) = _row_fwd("aout_f", _f_aout, r_aout, [w["ev_a_norm"]], [(1024, BF16)], tm)
    xc = _conv_fwd(z0, 3072 // 256, w["ev_conv_w"], w["ev_conv_b"])
    p_gates = [w["ev_w_ra"], w["ev_b_ra"], w["ev_w_rx"], w["ev_b_rx"], w["ev_lam"]]
    a_g, u_g = _row_fwd("gates_f", _f_gates, [(xc, 1024, 0)], p_gates, [(1024, F32), (1024, F32)], tm)
    hs, hprev = _scan_fwd(a_g, u_g, "lru_f")
    r_bout = [(hs, 1024, 0), (z0, 1024, 4)]
    (hb,) = _row_fwd("bout_f", _f_bout, r_bout, [], [(1024, BF16)], tm)
    hab = _concat_cols([ha, hb], BF16, "cat_hab")
    h1 = _mm(hab, w["ev_w_out"], "nn", "mm_h1", add=h0)
    (xn1,) = _row_fwd("norm1_f", _f_norm, [(h1, D_MODEL, 0)], [w["mlp_ln0"]], [(D_MODEL, BF16)], tm)
    p0, act0 = _mm(xn1, w["w_ff1_0"], "nn", "mm_p0", relu2=True)
    h2 = _mm(act0, w["w_ff2_0"], "nn", "mm_h2", add=h1)
    (xn2,) = _row_fwd("norm2_f", _f_norm, [(h2, D_MODEL, 0)], [w["od_ln"]], [(D_MODEL, BF16)], tm)
    z1 = _mm(xn2, w["od_w_in"], "nn", "mm_z1")
    r_c = [(z1, C_Q_LORA, 3072 // C_Q_LORA), (z1, C_KV_LORA, 3584 // C_KV_LORA)]
    cqn, ckvn = _row_fwd("cnorm_f", _f_cnorm, r_c, [w["od_g_qa"], w["od_g_kva"]],
                         [(C_Q_LORA, BF16), (C_KV_LORA, BF16)], tm)
    q_ = _mm(cqn, w["od_w_uq"], "nn", "mm_q")
    kv_ = _mm(ckvn, w["od_w_ukv"], "nn", "mm_kv")
    r_mla = [(q_, 1024, 0), (kv_, 1024, 0), (z1, HP, 3840 // HP), (cos, HP, 0), (sin, HP, 0)]
    p_mla = [w["gq_full"], w["gk_full"]]
    qm, km = _row_fwd("mla_f", _f_mlaprep, r_mla, p_mla, [(1024, BF16), (1024, BF16)], tm)
    sc_c = (C_NOPE + C_ROPE) ** -0.5
    kt_c = _key_terms(tp, tm)
    hc, lse_c = hosted("mla_attn_f", _attn_fwd, qm, km, kv_, C_HEADS, sc_c, None, kt_c, "mla_attn_f")
    r_fox = [(z1, 1024, 0), (z1, 1024, 1), (z1, HP, 3840 // HP)]
    p_fox = [w["gfq_full"], w["gfk_full"], w["bf_full"]]
    qf, kf, lfx = _row_fwd("fox_f", _f_foxprep, r_fox, p_fox, [(1024, BF16), (1024, BF16), (HP, F32)], tm)
    ones = jnp.ones((tp, HP), F32)
    fcum, fprev = _scan_fwd(ones, lfx, "fcum_f")
    _, bk = _heads_to_cols(fcum, FOX_LANE0, D_HEADS)
    kt_d = _key_terms(tp, tm, bk)
    sc_d = D_HD ** -0.5
    bq = fcum
    hd, lse_d = _attn_fwd(qf, kf, z1, 2048 // HP, sc_d, bq, kt_d, "fox_attn_f")
    hcd = _concat_cols([hc, hd], F32, "cat_hcd")
    h3 = _mm(hcd, w["od_w_out"], "nn", "mm_h3", add=h2)
    (xn3,) = _row_fwd("norm3_f", _f_norm, [(h3, D_MODEL, 0)], [w["mlp_ln1"]], [(D_MODEL, BF16)], tm)
    p1, act1 = _mm(xn3, w["w_ff1_1"], "nn", "mm_p1", relu2=True)
    h4 = _mm(act1, w["w_ff2_1"], "nn", "mm_h4", add=h3)
    lpart, dh4 = _loss_head(h4, tgt_p)
    loss = lpart[0, 0]

    def mlp_bwd(tag, dh_out, h_in, xn, p, act, ln, w1, w2):
        dp = hosted(f"mm_dp{tag}", _mm, dh_out, w2, "nt", f"mm_dp{tag}", relu2_of=p)
        g_w2 = _mm(act, dh_out, "tn", f"mm_dw2_{tag}")
        g_w1 = _mm(xn, dp, "tn", f"mm_dw1_{tag}")
        dxn = _mm(dp, w1, "nt", f"mm_dxn{tag}")
        (dh_in,), (g_ln,) = _row_bwd(f"normm{tag}_b", _f_norm, [(h_in, D_MODEL, 0)], [ln], [(dxn, D_MODEL, 0)], tm,
                                     [True], add=dh_out)
        return dh_in, g_ln, g_w1, g_w2

    dh3, g["mlp_ln1"], g["w_ff1_1"], g["w_ff2_1"] = mlp_bwd("1", dh4, h3, xn3, p1, act1, w["mlp_ln1"],
                                                            w["w_ff1_1"], w["w_ff2_1"])
    dhcd = _mm(dh3, w["od_w_out"], "nt", "mm_dhcd")
    g["od_w_out"] = _mm(hcd, dh3, "tn", "mm_dwout1")
    dqf, dkf, dvf, dbq, dkt = hosted("fox_attn_b", _attn_bwd, qf, kf, z1, 2048 // HP, hd, lse_d, dhcd, D_HEADS, sc_d,
                                     bq, kt_d, "fox_attn_b")
    dfc = dbq[:, :, 0].T - dkt.reshape(D_HEADS, tp).T
    dfcum = jnp.concatenate([jnp.zeros((tp, 96), F32), dfc, jnp.zeros((tp, HP - 96 - D_HEADS), F32)], axis=1)
    dlfx, _ = _scan_bwd(ones, fprev, dfcum, "fcum_b")
    (dfq, dfk, dmisc_f), (g["gfq_full"], g["gfk_full"], g["bf_full"]) = _row_bwd(
        "fox_b", _f_foxprep, r_fox, p_fox, [(dqf, 1024, 0), (dkf, 1024, 0), (dlfx, HP, 0)], tm, [True, True, True],
        out_dtypes=[BF16, BF16, F32])
    dqm, dkm, dvm = _attn_bwd(qm, km, kv_, C_HEADS, hc, lse_c, dhcd, 0, sc_c, None, kt_c, "mla_attn_b")
    (dq_, dkk_, dmisc_m), (g["gq_full"], g["gk_full"]) = _row_bwd(
        "mla_b", _f_mlaprep, r_mla, p_mla, [(dqm, 1024, 0), (dkm, 1024, 0)], tm, [True, True, True, False, False],
        out_dtypes=[BF16, BF16, F32])
    dkv_ = _concat_cols([dkk_, dvm], BF16, "cat_dkv")
    dckvn = _mm(dkv_, w["od_w_ukv"], "nt", "mm_dckvn")
    g["od_w_ukv"] = _mm(ckvn, dkv_, "tn", "mm_dwukv")
    dcqn = _mm(dq_, w["od_w_uq"], "nt", "mm_dcqn")
    g["od_w_uq"] = _mm(cqn, dq_, "tn", "mm_dwuq")
    (dcq, dckv), (g["od_g_qa"], g["od_g_kva"]) = _row_bwd(
        "cnorm_b", _f_cnorm, r_c, [w["od_g_qa"], w["od_g_kva"]],
        [(dcqn, C_Q_LORA, 0), (dckvn, C_KV_LORA, 0)], tm, [True, True], out_dtypes=[BF16, BF16])
    dz1 = _concat_cols([dfq, dfk, dvf, dcq, HP, dckv, dmisc_f + dmisc_m, HP], BF16, "cat_dz1")
    g["od_w_in"] = _mm(xn2, dz1, "tn", "mm_dwin1")
    dxn2 = _mm(dz1, w["od_w_in"], "nt", "mm_dxn2")
    (dh2,), (g["od_ln"],) = _row_bwd("norm2_b", _f_norm, [(h2, D_MODEL, 0)], [w["od_ln"]], [(dxn2, D_MODEL, 0)], tm,
                                     [True], add=dh3)
    dh1, g["mlp_ln0"], g["w_ff1_0"], g["w_ff2_0"] = mlp_bwd("0", dh2, h1, xn1, p0, act0, w["mlp_ln0"],
                                                            w["w_ff1_0"], w["w_ff2_0"])
    dhab = _mm(dh1, w["ev_w_out"], "nt", "mm_dhab")
    g["ev_w_out"] = _mm(hab, dh1, "tn", "mm_dwout0")
    (dhs, dgb), _ = _row_bwd("bout_b", _f_bout, r_bout, [], [(dhab, 1024, 1)], tm, [True, True],
                             out_dtypes=[F32, BF16])
    du_g, da_g = _scan_bwd(a_g, hprev, dhs, "lru_b")
    (dxc,), (g["ev_w_ra"], g["ev_b_ra"], g["ev_w_rx"], g["ev_b_rx"], g["ev_lam"]) = hosted(
        "gates_b", _row_bwd, "gates_b", _f_gates, [(xc, 1024, 0)], p_gates, [(da_g, 1024, 0), (du_g, 1024, 0)], tm,
        [True])
    dxb, g["ev_conv_w"], g["ev_conv_b"] = _conv_bwd(z0, 3072 // 256, w["ev_conv_w"], dxc)
    (dh_a, do_), (g["ev_a_norm"],) = _row_bwd("aout_b", _f_aout, r_aout, [w["ev_a_norm"]], [(dhab, 1024, 0)], tm,
                                              [True, True], out_dtypes=[F32, BF16])
    dqkv, dg0 = hosted("mlstm_bwd", _mlstm_bwd, z0, g0, cs, ns, ms, dh_a)
    (dmisc0,), (g["ev_b_if"],) = _row_bwd("gate0_b", _f_gate0, r_misc0, [w["ev_b_if"]], [(dg0, HP, 0)], tm, [True],
                                          out_dtypes=[BF16])
    dz0 = _concat_cols([dqkv, do_, dxb, dgb, dmisc0, HP], BF16, "cat_dz0")
    g["ev_w_in"] = _mm(xn0, dz0, "tn", "mm_dwin0")
    dxn0 = hosted("mm_dxn0", _mm, dz0, w["ev_w_in"], "nt", "mm_dxn0")
    (dh0,), (g["ev_ln"],) = _row_bwd("norm0_b", _f_norm, r_h0, [w["ev_ln"]], [(dxn0, D_MODEL, 0)], tm, [True], add=dh1)
    g["meta"] = dh0[PAD:front]
    return loss, dh0[front:], g


def _pad_last(a, n):
    return jnp.pad(a, [(0, 0)] * (a.ndim - 1) + [(0, n - a.shape[-1])])


def _pad_heads(a, nh, d):
    return _pad_last(a.reshape(a.shape[:-1] + (nh, d)), HP).reshape(a.shape[:-1] + (nh * HP,))


def _unpad_heads(a, nh, d):
    return a.reshape(a.shape[:-1] + (nh, HP))[..., :d].reshape(a.shape[:-1] + (nh * d,))


_MM_UNITS = ("ev_w_in", "ev_w_out", "od_w_in", "od_w_uq", "od_w_ukv", "od_w_out",
             "w_ff1_0", "w_ff2_0", "w_ff1_1", "w_ff2_1")


def _mw_pad(name, a):
    if name == "ev_w_in":
        return _pad_last(jnp.concatenate([a[:, :3072], a[:, 3080:5128], a[:, 3072:3080]], axis=1), ZE)
    if name == "od_w_in":
        z = lambda n: jnp.zeros((a.shape[0], n), a.dtype)
        return jnp.concatenate(
            [_pad_heads(a[:, 672:1184], D_HEADS, D_HD), _pad_heads(a[:, 1184:1696], D_HEADS, D_HD),
             _pad_heads(a[:, 1696:2208], D_HEADS, D_HD), a[:, 0:384], z(128), a[:, 384:640],
             z(64), a[:, 640:672], a[:, 2208:2216], z(24), z(128)], axis=1)
    if name == "od_w_uq":
        return _pad_heads(a, C_HEADS, C_NOPE + C_ROPE)
    if name == "od_w_ukv":
        wkv = a.reshape(C_KV_LORA, C_HEADS, C_NOPE + C_V)
        return jnp.concatenate([_pad_last(wkv[:, :, :C_NOPE], HP).reshape(C_KV_LORA, -1),
                                _pad_last(wkv[:, :, C_NOPE:], HP).reshape(C_KV_LORA, -1)], axis=1)
    if name == "od_w_out":
        return jnp.pad(a.reshape(2 * C_HEADS, C_V, D_MODEL), ((0, 0), (0, HP - C_V), (0, 0))).reshape(-1, D_MODEL)
    return a


def _mw_unpad(name, g):
    if name == "ev_w_in":
        return jnp.concatenate([g[:, :3072], g[:, 5120:5128], g[:, 3072:5120]], axis=1)
    if name == "od_w_in":
        return jnp.concatenate(
            [g[:, 3072:3456], g[:, 3584:3840], g[:, 3904:3936], _unpad_heads(g[:, 0:1024], D_HEADS, D_HD),
             _unpad_heads(g[:, 1024:2048], D_HEADS, D_HD), _unpad_heads(g[:, 2048:3072], D_HEADS, D_HD),
             g[:, 3936:3944]], axis=1)
    if name == "od_w_uq":
        return _unpad_heads(g, C_HEADS, C_NOPE + C_ROPE)
    if name == "od_w_ukv":
        gk = g[:, :C_HEADS * HP].reshape(C_KV_LORA, C_HEADS, HP)[:, :, :C_NOPE]
        gv = g[:, C_HEADS * HP:].reshape(C_KV_LORA, C_HEADS, HP)[:, :, :C_V]
        return jnp.concatenate([gk, gv], axis=2).reshape(C_KV_LORA, -1)
    if name == "od_w_out":
        return g.reshape(2 * C_HEADS, HP, D_MODEL)[:, :C_V].reshape(-1, D_MODEL)
    return g


def _prep_weights(p):
    w = {}
    w["meta"] = p["meta"]
    for k in ("ev_ln", "ev_a_norm", "ev_conv_b", "ev_b_ra", "ev_b_rx", "ev_lam", "od_ln", "od_g_qa", "od_g_kva"):
        w[k] = p[k].reshape(1, -1)
    w["ev_b_if"] = _pad_last(p["ev_b_if"].reshape(1, -1), HP)
    w["ev_conv_w"] = p["ev_conv_w"][0]
    w["ev_w_ra"] = p["ev_w_ra"][0]
    w["ev_w_rx"] = p["ev_w_rx"][0]
    f1 = lambda a: a.reshape(1, -1)
    w["gq_full"] = _pad_last(jnp.concatenate([f1(p["od_g_qn"]), f1(p["od_g_qr"])], axis=1), HP)
    w["gk_full"] = _pad_last(jnp.concatenate([f1(p["od_g_kn"]), f1(p["od_g_kr"])], axis=1), HP)
    w["gfq_full"] = _pad_last(f1(p["od_g_fq"]), HP)
    w["gfk_full"] = _pad_last(f1(p["od_g_fk"]), HP)
    w["bf_full"] = _pad_last(jnp.concatenate([jnp.zeros((1, 96), F32), f1(p["od_b_f"])], axis=1), HP)
    for l in (0, 1):
        w[f"mlp_ln{l}"] = p["mlp_ln"][l:l + 1]
    for n in _MM_UNITS:
        if n in p:
            w[n] = _mw_pad(n, p[n])
    return w


def _unprep_grads(g):
    o = {}
    o["meta"] = g["meta"]
    for k in ("ev_ln", "ev_a_norm", "ev_conv_b", "ev_b_ra", "ev_b_rx", "ev_lam", "od_ln", "od_g_qa", "od_g_kva"):
        o[k] = g[k].reshape(1, -1)
    o["ev_b_if"] = g["ev_b_if"][:, :2 * A_HEADS]
    o["ev_conv_w"] = g["ev_conv_w"][None]
    o["ev_w_ra"] = g["ev_w_ra"][None]
    o["ev_w_rx"] = g["ev_w_rx"][None]
    o["od_g_qn"] = g["gq_full"][:, :C_NOPE]
    o["od_g_qr"] = g["gq_full"][:, C_NOPE:C_NOPE + C_ROPE]
    o["od_g_kn"] = g["gk_full"][:, :C_NOPE]
    o["od_g_kr"] = g["gk_full"][:, C_NOPE:C_NOPE + C_ROPE]
    o["od_g_fq"] = g["gfq_full"][:, :D_HD]
    o["od_g_fk"] = g["gfk_full"][:, :D_HD]
    o["od_b_f"] = g["bf_full"][:, 96:96 + D_HEADS]
    o["mlp_ln"] = jnp.concatenate([g["mlp_ln0"], g["mlp_ln1"]], axis=0)
    return o


def _exchange(ops, name):
    n_ops = len(ops)
    flags = [s for _, s in ops]

    def body(*refs):
        copies = _comm_copies(refs[:n_ops], refs[n_ops:2 * n_ops], flags, *refs[2 * n_ops:])
        _comm_begin(copies)
        _comm_end(copies)

    any_spec = pl.BlockSpec(memory_space=pl.ANY)
    c_shape, c_sems = _comm_shapes(ops)
    return pl.pallas_call(body, name=name, out_shape=c_shape, in_specs=[any_spec] * n_ops,
                          out_specs=[any_spec] * n_ops, scratch_shapes=c_sems)(*[x for x, _ in ops])


def _adamw(parts, w, m, v, name):
    r, c = w.shape
    tr = r
    for cand in (512, 256, 128, 64, 32, 16):
        if r % cand == 0 and N_DEV * cand * c * 4 <= 4 * 1024 * 1024:
            tr = cand
            break
    c1 = 1.0 / (1.0 - ADAM_B1 ** ADAM_STEP)
    c2 = 1.0 / (1.0 - ADAM_B2 ** ADAM_STEP)

    def body(p_ref, w_ref, m_ref, v_ref, g_ref, d_ref, mo_ref, vo_ref):
        g = p_ref[0].astype(F32)
        for j in range(1, N_DEV):
            g = g + p_ref[j].astype(F32)
        m2 = ADAM_B1 * m_ref[...] + (1.0 - ADAM_B1) * g
        v2 = ADAM_B2 * v_ref[...] + (1.0 - ADAM_B2) * (g * g)
        g_ref[...] = g
        mo_ref[...] = m2
        vo_ref[...] = v2
        d_ref[...] = -ADAM_LR * ((m2 * c1) / (jnp.sqrt(v2 * c2) + ADAM_EPS) + ADAM_WD * w_ref[...])

    spec = pl.BlockSpec((tr, c), lambda i: (i, 0))
    return pl.pallas_call(
        body, name=name, grid=(r // tr,),
        in_specs=[pl.BlockSpec((N_DEV, tr, c), lambda i: (0, i, 0)), spec, spec, spec],
        out_specs=[spec] * 4, out_shape=[jax.ShapeDtypeStruct((r, c), F32)] * 4,
        compiler_params=_cparams(("parallel",)),
    )(parts, w, m, v)


def _rows_for(n, mult):
    return -(-n // (1024 * mult)) * mult


def _pack(arrs, mult, lead=()):
    nl = len(lead)
    flat = jnp.concatenate([a.reshape(lead + (-1,)) for a in arrs], axis=nl)
    rows = _rows_for(flat.shape[-1], mult)
    return jnp.pad(flat, [(0, 0)] * nl + [(0, rows * 1024 - flat.shape[-1])]).reshape(lead + (rows, 1024))


def _unpack(buf, shapes):
    lead = buf.shape[:-2]
    flat = buf.reshape(lead + (-1,))
    out, off = [], 0
    for s in shapes:
        n = 1
        for d_ in s:
            n *= d_
        out.append(flat[..., off:off + n].reshape(lead + tuple(s)))
        off += n
    return out


def _unshard(g8, ax):
    a = jnp.moveaxis(g8, 0, ax)
    return a.reshape(a.shape[:ax] + (N_DEV * a.shape[ax + 1],) + a.shape[ax + 2:])


def _shard8(full, ax):
    s = full.shape
    return jnp.moveaxis(full.reshape(s[:ax] + (N_DEV, s[ax] // N_DEV) + s[ax + 1:]), ax, 0)


_NAMES = ["meta", "ev_ln", "ev_w_in", "ev_b_if", "ev_a_norm", "ev_conv_w", "ev_conv_b", "ev_w_ra", "ev_b_ra",
          "ev_w_rx", "ev_b_rx", "ev_lam", "ev_w_out", "od_ln", "od_w_in", "od_b_f", "od_g_qa", "od_g_kva",
          "od_w_uq", "od_w_ukv", "od_g_qn", "od_g_qr", "od_g_kn", "od_g_kr", "od_g_fq", "od_g_fk", "od_w_out",
          "mlp_ln", "w_ff1", "w_ff2"]
_SHARD_AXIS = {"meta": 1, "ev_w_in": 2, "ev_conv_w": 2, "ev_w_out": 1, "od_ln": 1, "od_w_in": 2, "od_g_qa": 1,
               "od_g_kva": 1, "od_w_uq": 2, "od_w_ukv": 2, "od_w_out": 1, "w_ff1": 2, "w_ff2": 1}
_MATMUL_WEIGHTS = ("ev_w_in", "ev_w_out", "od_w_in", "od_w_uq", "od_w_ukv", "od_w_out", "w_ff1", "w_ff2")
_BIG_REPL = ("ev_w_ra", "ev_w_rx")
_COL_SHARDED = ("ev_w_in", "od_w_in", "od_w_uq", "od_w_ukv", "w_ff1_0", "w_ff1_1")
_GATHER_ON = {"mm_z0": ("od_w_in", "od_w_uq", "od_w_ukv"), "mlstm_fwd": ("ev_w_out", "w_ff1_0", "w_ff2_0"),
              "mla_attn_f": ("od_w_out", "w_ff1_1", "w_ff2_1")}
_SCATTER_ON = {"fox_attn_b": ("w_ff2_1", "w_ff1_1", "od_w_out"),
               "mm_dp0": ("od_w_ukv", "od_w_uq", "od_w_in"), "gates_b": ("ev_w_out",),
               "mlstm_bwd": ("w_ff2_0", "w_ff1_0"), "mm_dxn0": ("ev_w_in",)}
_REPL_ON = "mlstm_bwd"


def _unit_of(d, n):
    return d[n[:-2]][int(n[-1])] if n.startswith("w_ff") else d[n][0]


def _unit_full(n, g8):
    return jnp.transpose(g8, (1, 0, 2)).reshape(g8.shape[1], -1) if n in _COL_SHARDED else g8.reshape(-1, g8.shape[2])


def _unit_slots(n, full, r, c):
    return full.reshape(r, N_DEV, c).transpose(1, 0, 2) if n in _COL_SHARDED else full.reshape(N_DEV, r, c)


class _Plan:
    def __init__(self, shards):
        self.shards = shards
        self.parts = {}

    def slots(self, n, g):
        r, c = self.shards[n].shape
        return _unit_slots(n, _mw_unpad(n, g[n]), r, c).astype(BF16)

    def ride(self, tag, g):
        if tag in _GATHER_ON:
            return [(self.shards[n].astype(BF16), False) for n in _GATHER_ON[tag]]
        ops = [(self.slots(n, g), True) for n in _SCATTER_ON.get(tag, ())]
        if tag == _REPL_ON:
            ops += [(g[n].reshape(-1, B_BLOCK).astype(BF16), False) for n in _BIG_REPL]
        return ops

    def arrived(self, tag, got, w):
        if tag in _GATHER_ON:
            for n, g8 in zip(_GATHER_ON[tag], got):
                w[n] = _mw_pad(n, _unit_full(n, g8))
        else:
            self.parts.update(zip(_SCATTER_ON[tag] + (_BIG_REPL if tag == _REPL_ON else ()), got))


def kernel(x, positions, meta, ev_ln, ev_w_in, ev_b_if, ev_a_norm, ev_conv_w, ev_conv_b, ev_w_ra, ev_b_ra, ev_w_rx, ev_b_rx, ev_lam, ev_w_out, od_ln, od_w_in, od_b_f, od_g_qa, od_g_kva, od_w_uq, od_w_ukv, od_g_qn, od_g_qr, od_g_kn, od_g_kr, od_g_fq, od_g_fk, od_w_out, mlp_ln, w_ff1, w_ff2, loss_target, m_meta, m_ev_ln, m_ev_w_in, m_ev_b_if, m_ev_a_norm, m_ev_conv_w, m_ev_conv_b, m_ev_w_ra, m_ev_b_ra, m_ev_w_rx, m_ev_b_rx, m_ev_lam, m_ev_w_out, m_od_ln, m_od_w_in, m_od_b_f, m_od_g_qa, m_od_g_kva, m_od_w_uq, m_od_w_ukv, m_od_g_qn, m_od_g_qr, m_od_g_kn, m_od_g_kr, m_od_g_fq, m_od_g_fk, m_od_w_out, m_mlp_ln, m_w_ff1, m_w_ff2, v_meta, v_ev_ln, v_ev_w_in, v_ev_b_if, v_ev_a_norm, v_ev_conv_w, v_ev_conv_b, v_ev_w_ra, v_ev_b_ra, v_ev_w_rx, v_ev_b_rx, v_ev_lam, v_ev_w_out, v_od_ln, v_od_w_in, v_od_b_f, v_od_g_qa, v_od_g_kva, v_od_w_uq, v_od_w_ukv, v_od_g_qn, v_od_g_qr, v_od_g_kn, v_od_g_kr, v_od_g_fq, v_od_g_fk, v_od_w_out, v_mlp_ln, v_w_ff1, v_w_ff2):
    given = dict(locals())
    wts = {n: given[n] for n in _NAMES}
    mom = {n: given["m_" + n] for n in _NAMES}
    var = {n: given["v_" + n] for n in _NAMES}
    small_sh = [n for n in _NAMES if n in _SHARD_AXIS and n not in _MATMUL_WEIGHTS]
    small_rp = [n for n in _NAMES if n not in _SHARD_AXIS and n not in _BIG_REPL]
    shp = {n: wts[n].shape for n in _NAMES}
    plan = _Plan({n: _unit_of(wts, n) for n in _MM_UNITS})

    got = _exchange([(plan.shards["ev_w_in"].astype(BF16), False), (_pack([wts[n] for n in small_sh], 8), False)],
                    "gather_first")
    p = {n: wts[n] for n in _NAMES if n not in _SHARD_AXIS}
    for n, a in zip(small_sh, _unpack(got[1], [shp[n] for n in small_sh])):
        p[n] = _unshard(a, _SHARD_AXIS[n])
    p["ev_w_in"] = _unit_full("ev_w_in", got[0])

    loss, gx, g = _local_step(x[0], positions[0], loss_target[0], _prep_weights(p), plan)
    grads = _unprep_grads(g)

    parts = _exchange([(_pack([_shard8(grads[n], _SHARD_AXIS[n]) for n in small_sh], 8, (N_DEV,)), True),
                       (_pack([grads[n].reshape(shp[n]) for n in small_rp], 8), False)], "exchange_last")

    res = {}
    unit_res = {n: _adamw(plan.parts[n], *[_unit_of(d, n) for d in (wts, mom, var)], f"adamw_{n}")
                for n in _MM_UNITS}
    for n in _MATMUL_WEIGHTS:
        if n.startswith("w_ff"):
            res[n] = [jnp.stack([unit_res[n + "_0"][k], unit_res[n + "_1"][k]]) for k in range(4)]
        else:
            res[n] = [r[None] for r in unit_res[n]]
    r4 = _adamw(parts[0], *[_pack([d[n] for n in small_sh], 8) for d in (wts, mom, var)], "adamw_small_sharded")
    for n, *four in zip(small_sh, *[_unpack(r, [shp[n] for n in small_sh]) for r in r4]):
        res[n] = four
    for n in _BIG_REPL:
        r4 = _adamw(plan.parts[n], *[d[n].reshape(-1, B_BLOCK) for d in (wts, mom, var)], f"adamw_{n}")
        res[n] = [r.reshape(shp[n]) for r in r4]
    r4 = _adamw(parts[1], *[_pack([d[n] for n in small_rp], 8) for d in (wts, mom, var)], "adamw_small_repl")
    for n, *four in zip(small_rp, *[_unpack(r, [shp[n] for n in small_rp]) for r in r4]):
        res[n] = four

    outs = [res[n][kind] for kind in range(4) for n in _NAMES]
    loss = lax.psum(loss, ("x", "y", "c"))
    return (loss, gx[None], *outs)
```
